```python
import math
import numpy as np
import jax
import jax.numpy as jnp
from jax import lax

D_MODEL = 4096
BATCH = 4
SEQ = 2048
DEPTH = 1

HEAD_DIM = 128
NSA_WIDTH = D_MODEL // 2
N_NSA_HEADS = NSA_WIDTH // HEAD_DIM
N_KV_GROUPS = 4
GROUP_SIZE = N_NSA_HEADS // N_KV_GROUPS
N_BRANCH = 3
CMP_BLOCK = 32
CMP_STRIDE = 16
SEL_BLOCK = 64
TOP_N = 16
WINDOW = 512
WIN_BLOCK = 128
SLC_Q_CHUNK = 32
RET_KEY_DIM = 256
RET_VAL_DIM = 256
RET_WIDTH = D_MODEL - NSA_WIDTH
N_RET_HEADS = RET_WIDTH // RET_VAL_DIM
RET_CHUNK = 128
ROPE_BASE = 10000.0
MIX_WIDTH = NSA_WIDTH + RET_WIDTH
D_FF = -(-8 * D_MODEL // (3 * 256)) * 256
NUM_BUCKETS = 32
MAX_DISTANCE = 128
EPS = 1e-6
NEG_INF = -1e30
FORCE_SCORE = 1e9

KV_COLS = N_BRANCH * 2 * N_KV_GROUPS * HEAD_DIM
GATE_COLS = N_NSA_HEADS * N_BRANCH
RET_QK_COLS = N_RET_HEADS * RET_KEY_DIM
SPLIT_SIZES = (NSA_WIDTH, KV_COLS, GATE_COLS, RET_QK_COLS, RET_QK_COLS, RET_WIDTH, RET_WIDTH)
IN_COLS = sum(SPLIT_SIZES)

kernel_name = 'hybrid_nsa_retention_block'


def rms_norm(x, w):
    xf = x.astype(jnp.float32)
    y = xf * lax.rsqrt(jnp.mean(xf * xf, axis=-1, keepdims=True) + EPS)
    return (y * w.astype(jnp.float32)).astype(x.dtype)


def t5_bucket(rel):
    n = jnp.maximum(rel, 0)
    max_exact = NUM_BUCKETS // 2
    nf = jnp.maximum(n, 1).astype(jnp.float32)
    large = max_exact + (jnp.log(nf / max_exact) / math.log(MAX_DISTANCE / max_exact)
                         * (NUM_BUCKETS - max_exact)).astype(jnp.int32)
    large = jnp.minimum(large, NUM_BUCKETS - 1)
    return jnp.where(n < max_exact, n, large)


def masked_softmax(logits, mask):
    p = jax.nn.softmax(jnp.where(mask, logits.astype(jnp.float32), NEG_INF), axis=-1)
    return p * mask


def compress_blocks(kv, pe, w):
    n_cmp = (kv.shape[2] - CMP_BLOCK) // CMP_STRIDE + 1
    idx = np.arange(n_cmp)[:, None] * CMP_STRIDE + np.arange(CMP_BLOCK)[None, :]
    blocks = kv[:, :, idx] + pe
    flat = blocks.reshape(blocks.shape[:3] + (CMP_BLOCK * HEAD_DIM,))
    return flat @ w


def nsa_mixer(q, kv, gates, k_norm_w, cmp_pe_k, cmp_pe_v, w_cmp_k, w_cmp_v, rel_bias):
    B, G, R, T, dh = q.shape
    scale = HEAD_DIM ** -0.5
    pos = jnp.arange(T)

    k_c = rms_norm(compress_blocks(kv[0, 0], cmp_pe_k, w_cmp_k), k_norm_w[0])
    v_c = compress_blocks(kv[0, 1], cmp_pe_v, w_cmp_v)
    n_cmp = k_c.shape[2]
    cmp_end = jnp.arange(n_cmp) * CMP_STRIDE + CMP_BLOCK - 1
    rel_c = pos[:, None] - cmp_end[None, :]
    bias_c = jnp.transpose(rel_bias[t5_bucket(rel_c)], (2, 0, 1)).reshape(G, R, T, n_cmp)
    logits_c = jnp.einsum('bgrtd,bgnd->bgrtn', q, k_c) * scale + bias_c
    p_c = masked_softmax(logits_c, rel_c >= 0)
    o_c = jnp.einsum('bgrtn,bgnd->bgrtd', p_c.astype(v_c.dtype), v_c)

    n_sel = T // SEL_BLOCK
    cs = np.arange(n_cmp)[:, None] * CMP_STRIDE
    ss = np.arange(n_sel)[None, :] * SEL_BLOCK
    overlap = np.clip(np.minimum(cs + CMP_BLOCK, ss + SEL_BLOCK) - np.maximum(cs, ss), 0, None)
    cmp_to_sel = jnp.asarray(overlap.astype(np.float32) / np.float32(CMP_BLOCK))
    imp = jnp.einsum('bgrtn,ns->bgts', p_c, cmp_to_sel)
    cur = pos // SEL_BLOCK
    blk = jnp.arange(n_sel)
    forced = (blk[None, :] == 0) | (blk[None, :] == cur[:, None]) | (blk[None, :] == cur[:, None] - 1)
    causal = blk[None, :] <= cur[:, None]
    score = jnp.where(forced, FORCE_SCORE, jnp.where(causal, imp, NEG_INF))
    _, sel_idx = lax.top_k(score, min(TOP_N, n_sel))
    n_top = sel_idx.shape[-1]

    k_s = rms_norm(kv[1, 0], k_norm_w[1]).reshape(B, G, n_sel, SEL_BLOCK, dh)
    v_s = kv[1, 1].reshape(B, G, n_sel, SEL_BLOCK, dh)
    n_qc = T // SLC_Q_CHUNK
    q_ch = jnp.moveaxis(q.reshape(B, G, R, n_qc, SLC_Q_CHUNK, dh), 3, 0)
    idx_ch = jnp.moveaxis(sel_idx.reshape(B, G, n_qc, SLC_Q_CHUNK, n_top), 2, 0)
    pos_ch = pos.reshape(n_qc, SLC_Q_CHUNK)
    heads = (jnp.arange(G)[:, None] * R + jnp.arange(R)[None, :]).reshape(1, G, R, 1, 1, 1)
    gather = jax.vmap(jax.vmap(lambda blocks, ix: blocks[ix]))

    def slc_chunk(args):
        qc, ic, pc = args
        kg = gather(k_s, ic)
        vg = gather(v_s, ic)
        rel = pc[None, None, :, None, None] - (ic[..., None] * SEL_BLOCK + jnp.arange(SEL_BLOCK))
        bias = rel_bias[t5_bucket(rel)[:, :, None], heads]
        logits = jnp.einsum('bgrcd,bgcksd->bgrcks', qc, kg) * scale + bias
        shp = logits.shape
        mask = (rel >= 0)[:, :, None].reshape(B, G, 1, SLC_Q_CHUNK, -1)
        p = masked_softmax(logits.reshape(shp[:4] + (-1,)), mask)
        return jnp.einsum('bgrcks,bgcksd->bgrcd', p.reshape(shp).astype(vg.dtype), vg)

    o_s = jnp.moveaxis(lax.map(slc_chunk, (q_ch, idx_ch, pos_ch)), 0, 3).reshape(B, G, R, T, dh)

    k_w = rms_norm(kv[2, 0], k_norm_w[2])
    v_w = kv[2, 1]
    n_blk = T // WIN_BLOCK
    n_back = WINDOW // WIN_BLOCK
    band_len = (n_back + 1) * WIN_BLOCK

    def band(a):
        ap = jnp.pad(a, ((0, 0), (0, 0), (WINDOW, 0), (0, 0))).reshape(B, G, n_blk + n_back, WIN_BLOCK, dh)
        return jnp.concatenate([ap[:, :, j:j + n_blk] for j in range(n_back + 1)], axis=3)

    k_band, v_band = band(k_w), band(v_w)
    q_blk = q.reshape(B, G, R, n_blk, WIN_BLOCK, dh)
    q_pos = pos.reshape(n_blk, WIN_BLOCK)
    k_pos = jnp.arange(n_blk)[:, None] * WIN_BLOCK - WINDOW + jnp.arange(band_len)[None, :]
    rel_w = q_pos[:, :, None] - k_pos[:, None, :]
    mask_w = (rel_w >= 0) & (rel_w < WINDOW) & (k_pos[:, None, :] >= 0)
    bias_w = jnp.transpose(rel_bias[t5_bucket(rel_w)], (3, 0, 1, 2)).reshape(G, R, n_blk, WIN_BLOCK, band_len)
    logits_w = jnp.einsum('bgrnqd,bgnkd->bgrnqk', q_blk, k_band) * scale + bias_w
    p_w = masked_softmax(logits_w, mask_w)
    o_w = jnp.einsum('bgrnqk,bgnkd->bgrnqd', p_w.astype(v_band.dtype), v_band).reshape(B, G, R, T, dh)

    return gates[..., 0:1] * o_c + gates[..., 1:2] * o_s + gates[..., 2:3] * o_w


def rotary(x, pos):
    half = x.shape[-1] // 2
    inv = jnp.exp(-jnp.linspace(0.0, 1.0, half) * math.log(ROPE_BASE))
    ang = pos.astype(jnp.float32)[:, None] * inv[None, :]
    cos = jnp.cos(ang)[None, :, None, :]
    sin = jnp.sin(ang)[None, :, None, :]
    x1, x2 = x[..., :half], x[..., half:]
    return jnp.concatenate([x1 * cos - x2 * sin, x2 * cos + x1 * sin], axis=-1).astype(x.dtype)


def retention_mixer(q, k, v, g, gn_w):
    B, T, H, dk = q.shape
    dv = v.shape[-1]
    C = RET_CHUNK
    n_ch = T // C
    pos = jnp.arange(T)
    q = rotary(q, pos)
    k = rotary(k, pos) * (dk ** -0.5)

    def chunks(a):
        return jnp.transpose(a, (0, 2, 1, 3)).reshape(B, H, n_ch, C, a.shape[-1])

    qc, kc, vc = chunks(q), chunks(k), chunks(v)
    log_gamma = jnp.log(1.0 - 2.0 ** (-5.0 - jnp.arange(H, dtype=jnp.float32)))
    i = jnp.arange(C, dtype=jnp.float32)
    diff = i[:, None] - i[None, :]
    causal = diff >= 0
    decay_in = jnp.where(causal, jnp.exp(jnp.where(causal, diff, 0.0)[None] * log_gamma[:, None, None]), 0.0)
    zeta = jnp.exp((C - 1 - i)[None, :] * log_gamma[:, None])
    xi = jnp.exp((i + 1)[None, :] * log_gamma[:, None])
    chunk_decay = jnp.exp(C * log_gamma)[None, :, None, None]

    scores = jnp.einsum('bhncd,bhnmd->bhncm', qc, kc) * decay_in[None, :, None]
    inner = jnp.einsum('bhncm,bhnme->bhnce', scores, vc)
    kv_ch = jnp.einsum('bhnmd,bhnme->nbhde', kc * zeta[None, :, None, :, None], vc).astype(jnp.float32)

    def step(state, kv_i):
        return chunk_decay * state + kv_i, state

    _, prev = lax.scan(step, jnp.zeros((B, H, dk, dv), jnp.float32), kv_ch)
    cross = jnp.einsum('bhncd,nbhde->bhnce', qc * xi[None, :, None, :, None], prev)
    y = jnp.transpose((inner + cross).reshape(B, H, T, dv), (0, 2, 1, 3))
    y = rms_norm(y, gn_w.reshape(H, dv)).reshape(B, T, H * dv)
    return jax.nn.silu(g) * y


def hybrid_layer(x, norm1_w, w_in, nsa_q_norm_w, nsa_k_norm_w, cmp_pe_k, cmp_pe_v, w_cmp_k, w_cmp_v,
                 rel_bias, ret_gn_w, w_out, norm2_w, w_gate, w_up, w_down):
    B, T, _ = x.shape
    G, R = N_KV_GROUPS, GROUP_SIZE
    h = rms_norm(x, norm1_w)
    proj = h @ w_in
    q_n, kv_n, gate_n, q_r, k_r, v_r, g_r = jnp.split(proj, np.cumsum(SPLIT_SIZES)[:-1].tolist(), axis=-1)

    q_n = rms_norm(q_n.reshape(B, T, N_NSA_HEADS, HEAD_DIM), nsa_q_norm_w)
    q_n = jnp.transpose(q_n, (0, 2, 1, 3)).reshape(B, G, R, T, HEAD_DIM)
    kv_n = jnp.transpose(kv_n.reshape(B, T, N_BRANCH, 2, G, HEAD_DIM), (2, 3, 0, 4, 1, 5))
    gates = jax.nn.sigmoid(gate_n.astype(jnp.float32)).reshape(B, T, N_NSA_HEADS, N_BRANCH)
    gates = jnp.transpose(gates, (0, 2, 1, 3)).reshape(B, G, R, T, N_BRANCH)
    o_nsa = nsa_mixer(q_n, kv_n, gates, nsa_k_norm_w, cmp_pe_k, cmp_pe_v, w_cmp_k, w_cmp_v, rel_bias)
    o_nsa = jnp.transpose(o_nsa.reshape(B, N_NSA_HEADS, T, HEAD_DIM), (0, 2, 1, 3)).reshape(B, T, NSA_WIDTH)

    o_ret = retention_mixer(q_r.reshape(B, T, N_RET_HEADS, RET_KEY_DIM),
                            k_r.reshape(B, T, N_RET_HEADS, RET_KEY_DIM),
                            v_r.reshape(B, T, N_RET_HEADS, RET_VAL_DIM), g_r, ret_gn_w)

    mixed = jnp.concatenate([o_nsa, o_ret], axis=-1).astype(x.dtype) @ w_out
    x = x + mixed
    hf = rms_norm(x, norm2_w)
    x = x + ((jax.nn.silu(hf @ w_gate) * (hf @ w_up)) @ w_down).astype(x.dtype)
    return x


def setup_inputs(seed: int = 0) -> dict:
    key = jax.random.key(seed)
    ks = jax.random.split(key, 16)
    f32 = jnp.float32
    L = DEPTH

    def normal(k, shape, scale):
        return jax.random.normal(k, shape, f32) * scale

    def gain(k, shape):
        return 1.0 + normal(k, shape, 0.02)

    return {
        'x': normal(ks[0], (BATCH, SEQ, D_MODEL), 1.0),
        'norm1_w': gain(ks[1], (L, D_MODEL)),
        'w_in': normal(ks[2], (L, D_MODEL, IN_COLS), D_MODEL ** -0.5),
        'nsa_q_norm_w': gain(ks[3], (L, HEAD_DIM)),
        'nsa_k_norm_w': gain(ks[4], (L, N_BRANCH, HEAD_DIM)),
        'cmp_pe_k': normal(ks[5], (L, CMP_BLOCK, HEAD_DIM), 0.1),
        'cmp_pe_v': normal(ks[6], (L, CMP_BLOCK, HEAD_DIM), 0.1),
        'w_cmp_k': normal(ks[7], (L, CMP_BLOCK * HEAD_DIM, HEAD_DIM), (CMP_BLOCK * HEAD_DIM) ** -0.5),
        'w_cmp_v': normal(ks[8], (L, CMP_BLOCK * HEAD_DIM, HEAD_DIM), (CMP_BLOCK * HEAD_DIM) ** -0.5),
        'rel_bias': normal(ks[9], (NUM_BUCKETS, N_NSA_HEADS), 0.1),
        'ret_gn_w': gain(ks[10], (L, RET_WIDTH)),
        'w_out': normal(ks[11], (L, MIX_WIDTH, D_MODEL), MIX_WIDTH ** -0.5),
        'norm2_w': gain(ks[12], (L, D_MODEL)),
        'w_gate': normal(ks[13], (L, D_MODEL, D_FF), D_MODEL ** -0.5),
        'w_up': normal(ks[14], (L, D_MODEL, D_FF), D_MODEL ** -0.5),
        'w_down': normal(ks[15], (L, D_FF, D_MODEL), D_FF ** -0.5),
    }


def reference(x, norm1_w, w_in, nsa_q_norm_w, nsa_k_norm_w, cmp_pe_k, cmp_pe_v, w_cmp_k, w_cmp_v,
              rel_bias, ret_gn_w, w_out, norm2_w, w_gate, w_up, w_down):
    for l in range(DEPTH):
        x = hybrid_layer(x, norm1_w[l], w_in[l], nsa_q_norm_w[l], nsa_k_norm_w[l], cmp_pe_k[l], cmp_pe_v[l],
                         w_cmp_k[l], w_cmp_v[l], rel_bias, ret_gn_w[l], w_out[l], norm2_w[l],
                         w_gate[l], w_up[l], w_down[l])
    return x
```

```python
import functools
import math

import numpy as np
import jax
import jax.numpy as jnp
from jax import lax
from jax.experimental import pallas as pl
from jax.experimental.pallas import tpu as pltpu

F32 = jnp.float32
BF16 = jnp.bfloat16

D_MODEL = 4096
SEQ = 2048
HEAD_DIM = 128
N_NSA_HEADS = 16
N_KV_GROUPS = 4
GROUP_SIZE = 4
N_BRANCH = 3
CMP_BLOCK = 32
CMP_STRIDE = 16
SEL_BLOCK = 64
TOP_N = 16
WINDOW = 512
RET_KEY_DIM = 256
RET_VAL_DIM = 256
N_RET_HEADS = 8
RET_CHUNK = 128
ROPE_BASE = 10000.0
NSA_WIDTH = 2048
RET_WIDTH = 2048
D_FF = 11008
NUM_BUCKETS = 32
MAX_DISTANCE = 128
EPS = 1e-6
NEG_INF = -1e30
FORCE_SCORE = 1e9
N_SEL = SEQ // SEL_BLOCK
N_CMP = (SEQ - CMP_BLOCK) // CMP_STRIDE + 1
SCALE = HEAD_DIM ** -0.5

LANES = 128
VMEM_LIMIT = 56 * 1024 * 1024

TQ = 256
TK = 256
NQ = SEQ // TQ

COL_Q = 0
COL_KV = 2048
COL_QR = 5120
COL_KR = 7168
COL_VR = 9216
COL_GR = 11264
COL_GATE = 13312
IN_COLS_P = COL_GATE + N_KV_GROUPS * LANES


def _bucket_thresholds():
    rel = np.arange(0, 4 * MAX_DISTANCE)
    max_exact = NUM_BUCKETS // 2
    nf = np.maximum(rel, 1).astype(np.float64)
    large = max_exact + np.floor(np.log(nf / max_exact) / math.log(MAX_DISTANCE / max_exact)
                                 * (NUM_BUCKETS - max_exact)).astype(np.int64)
    large = np.minimum(large, NUM_BUCKETS - 1)
    b = np.where(rel < max_exact, rel, large)
    return [int(np.argmax(b >= k)) for k in range(NUM_BUCKETS)]


_THR = _bucket_thresholds()


def _dot(a, b):
    return jnp.dot(a, b, preferred_element_type=F32)


def _dot_nt(a, b):
    return lax.dot_general(a, b, (((1,), (1,)), ((), ())), preferred_element_type=F32)


def _rms_kernel(x_ref, w_ref, o_ref):
    x = x_ref[...]
    y = x * lax.rsqrt(jnp.mean(x * x, axis=-1, keepdims=True) + EPS)
    o_ref[...] = (y * w_ref[...]).astype(o_ref.dtype)


def _rmsnorm(x, w, tm=256):
    m, d = x.shape
    return pl.pallas_call(
        _rms_kernel,
        out_shape=jax.ShapeDtypeStruct((m, d), BF16),
        grid=(m // tm,),
        in_specs=[pl.BlockSpec((tm, d), lambda i: (i, 0)),
                  pl.BlockSpec((1, d), lambda i: (0, 0))],
        out_specs=pl.BlockSpec((tm, d), lambda i: (i, 0)),
        compiler_params=pltpu.CompilerParams(dimension_semantics=("arbitrary",),
                                             vmem_limit_bytes=VMEM_LIMIT),
    )(x, w.reshape(1, d))


def _mm_kernel(a_ref, b_ref, o_ref):
    o_ref[...] = _dot(a_ref[...], b_ref[...]).astype(o_ref.dtype)


def _matmul(a, b, out_dtype, tm, tn):
    m, k = a.shape
    _, n = b.shape
    return pl.pallas_call(
        _mm_kernel,
        out_shape=jax.ShapeDtypeStruct((m, n), out_dtype),
        grid=(m // tm, n // tn),
        in_specs=[pl.BlockSpec((tm, k), lambda i, j: (i, 0)),
                  pl.BlockSpec((k, tn), lambda i, j: (0, j))],
        out_specs=pl.BlockSpec((tm, tn), lambda i, j: (i, j)),
        compiler_params=pltpu.CompilerParams(dimension_semantics=("arbitrary", "arbitrary"),
                                             vmem_limit_bytes=VMEM_LIMIT),
    )(a, b)


def _outproj_kernel(a1_ref, a2_ref, w1_ref, w2_ref, x_ref, o_ref):
    acc = _dot(a1_ref[...], w1_ref[...]) + _dot(a2_ref[...], w2_ref[...])
    o_ref[...] = x_ref[...] + acc


def _out_projection(o_nsa, o_ret, w_out, x, tm=1024, tn=512):
    m = x.shape[0]
    kh = NSA_WIDTH
    return pl.pallas_call(
        _outproj_kernel,
        out_shape=jax.ShapeDtypeStruct((m, D_MODEL), F32),
        grid=(m // tm, D_MODEL // tn),
        in_specs=[pl.BlockSpec((tm, kh), lambda i, j: (i, 0)),
                  pl.BlockSpec((tm, kh), lambda i, j: (i, 0)),
                  pl.BlockSpec((kh, tn), lambda i, j: (0, j)),
                  pl.BlockSpec((kh, tn), lambda i, j: (1, j)),
                  pl.BlockSpec((tm, tn), lambda i, j: (i, j))],
        out_specs=pl.BlockSpec((tm, tn), lambda i, j: (i, j)),
        compiler_params=pltpu.CompilerParams(dimension_semantics=("arbitrary", "arbitrary"),
                                             vmem_limit_bytes=VMEM_LIMIT),
    )(o_nsa, o_ret, w_out, w_out, x)


def _ffn_up_kernel(h_ref, wg_ref, wu_ref, o_ref):
    h = h_ref[...]
    g = _dot(h, wg_ref[...])
    u = _dot(h, wu_ref[...])
    o_ref[...] = (g * jax.nn.sigmoid(g) * u).astype(o_ref.dtype)


def _ffn_up(h, w_gate, w_up, tm=1024, tn=256):
    m, k = h.shape
    n = w_gate.shape[1]
    return pl.pallas_call(
        _ffn_up_kernel,
        out_shape=jax.ShapeDtypeStruct((m, n), BF16),
        grid=(m // tm, n // tn),
        in_specs=[pl.BlockSpec((tm, k), lambda i, j: (i, 0)),
                  pl.BlockSpec((k, tn), lambda i, j: (0, j)),
                  pl.BlockSpec((k, tn), lambda i, j: (0, j))],
        out_specs=pl.BlockSpec((tm, tn), lambda i, j: (i, j)),
        compiler_params=pltpu.CompilerParams(dimension_semantics=("arbitrary", "arbitrary"),
                                             vmem_limit_bytes=VMEM_LIMIT),
    )(h, w_gate, w_up)


def _ffn_down_kernel(a_ref, b_ref, x_ref, o_ref, acc_ref):
    kk = pl.program_id(2)

    @pl.when(kk == 0)
    def _():
        acc_ref[...] = x_ref[...]

    acc_ref[...] += _dot(a_ref[...], b_ref[...])

    @pl.when(kk == pl.num_programs(2) - 1)
    def _():
        o_ref[...] = acc_ref[...]


def _ffn_down(a, w_down, x, tm=1024, tn=512, tk=5504):
    m, k = a.shape
    n = w_down.shape[1]
    return pl.pallas_call(
        _ffn_down_kernel,
        out_shape=jax.ShapeDtypeStruct((m, n), F32),
        grid=(m // tm, n // tn, k // tk),
        in_specs=[pl.BlockSpec((tm, tk), lambda i, j, kk: (i, kk)),
                  pl.BlockSpec((tk, tn), lambda i, j, kk: (kk, j)),
                  pl.BlockSpec((tm, tn), lambda i, j, kk: (i, j))],
        out_specs=pl.BlockSpec((tm, tn), lambda i, j, kk: (i, j)),
        scratch_shapes=[pltpu.VMEM((tm, tn), F32)],
        compiler_params=pltpu.CompilerParams(
            dimension_semantics=("arbitrary", "arbitrary", "arbitrary"),
            vmem_limit_bytes=VMEM_LIMIT),
    )(a, w_down, x)


def _bias_value(rel, relb_ref, h):
    val = jnp.full(rel.shape, relb_ref[0, h], F32)
    for k in range(1, NUM_BUCKETS):
        val = jnp.where(rel >= _THR[k], relb_ref[k, h], val)
    return val


def _bias_tables_kernel(relb_ref, biasc_ref, dm_ref):
    h = pl.program_id(0)
    def cbody(c, _):
        rows = pl.ds(pl.multiple_of(c * TQ, TQ), TQ)
        t = c * TQ + lax.broadcasted_iota(jnp.int32, (TQ, LANES), 0)
        n = lax.broadcasted_iota(jnp.int32, (TQ, LANES), 1)
        biasc_ref[0, rows, :] = _bias_value(t - (CMP_STRIDE * n + CMP_BLOCK - 1), relb_ref, h)
        return 0
    lax.fori_loop(0, NQ, cbody, 0)
    a = lax.broadcasted_iota(jnp.int32, (TQ, TK), 0)
    b = lax.broadcasted_iota(jnp.int32, (TQ, TK), 1)
    far = jnp.full((TQ, TK), relb_ref[NUM_BUCKETS - 1, h], F32)
    dm_ref[0, 0] = _bias_value(a - b, relb_ref, h) + jnp.where(a >= b, 0.0, NEG_INF)
    dm_ref[0, 1] = _bias_value(a - b + TK, relb_ref, h)
    dm_ref[0, 2] = far
    dm_ref[0, 3] = far + jnp.where(a < b, 0.0, NEG_INF)


def _bias_tables(rel_bias):
    return pl.pallas_call(
        _bias_tables_kernel,
        out_shape=(jax.ShapeDtypeStruct((N_NSA_HEADS, SEQ, LANES), F32),
                   jax.ShapeDtypeStruct((N_KV_GROUPS, 4, GROUP_SIZE * TQ, TK), F32)),
        grid=(N_NSA_HEADS,),
        in_specs=[pl.BlockSpec(memory_space=pltpu.SMEM)],
        out_specs=(pl.BlockSpec((1, SEQ, LANES), lambda h: (h, 0, 0)),
                   pl.BlockSpec((1, 4, TQ, TK), lambda h: (h // GROUP_SIZE, 0, h % GROUP_SIZE, 0))),
        compiler_params=pltpu.CompilerParams(dimension_semantics=("arbitrary",),
                                             vmem_limit_bytes=VMEM_LIMIT),
    )(rel_bias)


def _row_rms(x, w):
    return x * lax.rsqrt(jnp.mean(x * x, axis=-1, keepdims=True) + EPS) * w


def _nsa_kernel(q_ref, k0_ref, v0_ref, k1_ref, v1_ref, k2_ref, v2_ref, gate_ref,
                qnw_ref, knw_ref, pek_ref, pev_ref, wck_ref, wcv_ref, biasc_ref, dm_ref, c2s_ref,
                o_ref,
                kc_s, vc_s, kaug_s, kw_s, vs_s, vw_s):
    qt = pl.program_id(2)
    rows_all = GROUP_SIZE * TQ

    @pl.when(qt == 0)
    def _prologue():
        kaug_s[:, :HEAD_DIM] = _row_rms(k1_ref[...], knw_ref[1:2, :]).astype(BF16)
        jblk = lax.broadcasted_iota(jnp.int32, (SEQ, LANES), 0) // SEL_BLOCK
        lane = lax.broadcasted_iota(jnp.int32, (SEQ, LANES), 1)
        kaug_s[:, HEAD_DIM:] = jnp.where(jblk == lane, 1.0, 0.0).astype(BF16)
        kw_s[...] = _row_rms(k2_ref[...], knw_ref[2:3, :]).astype(BF16)
        vs_s[...] = v1_ref[...].astype(BF16)
        vw_s[...] = v2_ref[...].astype(BF16)

        def compress(x_ref, pe_ref, w_ref):
            acc_lo = jnp.zeros((LANES, HEAD_DIM), F32)
            acc_hi = jnp.zeros((LANES, HEAD_DIM), F32)
            for j in range(CMP_STRIDE):
                y = x_ref[pl.ds(j, LANES, stride=CMP_STRIDE), :]
                acc_lo += _dot((y + pe_ref[j:j + 1, :]).astype(BF16),
                               w_ref[j * HEAD_DIM:(j + 1) * HEAD_DIM, :])
                jh = CMP_STRIDE + j
                acc_hi += _dot((y + pe_ref[jh:jh + 1, :]).astype(BF16),
                               w_ref[jh * HEAD_DIM:(jh + 1) * HEAD_DIM, :])
            return acc_lo + pltpu.roll(acc_hi, LANES - 1, 0)

        kc_s[...] = _row_rms(compress(k0_ref, pek_ref, wck_ref), knw_ref[0:1, :]).astype(BF16)
        vc_s[...] = compress(v0_ref, pev_ref, wcv_ref).astype(BF16)

    t0 = qt * TQ
    q = q_ref[...]
    qs = []
    for r in range(GROUP_SIZE):
        qs.append(_row_rms(q[:, r * HEAD_DIM:(r + 1) * HEAD_DIM], qnw_ref[...]).astype(BF16))
    q_stack = jnp.concatenate(qs, axis=0)

    sc = _dot_nt(q_stack, kc_s[...]) * SCALE + biasc_ref[...].reshape(rows_all, LANES)
    tpos = t0 + (lax.broadcasted_iota(jnp.int32, (rows_all, LANES), 0) & (TQ - 1))
    ncol = lax.broadcasted_iota(jnp.int32, (rows_all, LANES), 1)
    maskc = tpos >= CMP_STRIDE * ncol + (CMP_BLOCK - 1)
    scm = jnp.where(maskc, sc, NEG_INF)
    mc = jnp.max(scm, axis=-1, keepdims=True)
    ec = jnp.where(maskc, jnp.exp(scm - mc), 0.0)
    lc = jnp.sum(ec, axis=-1, keepdims=True)
    pc = ec * (1.0 / jnp.where(lc > 0.0, lc, 1.0))
    o_c = _dot(pc.astype(BF16), vc_s[...])

    psum = pc[0:TQ] + pc[TQ:2 * TQ] + pc[2 * TQ:3 * TQ] + pc[3 * TQ:4 * TQ]
    p_hi = psum.astype(BF16)
    p_lo = (psum - p_hi.astype(F32)).astype(BF16)
    imp = _dot(p_hi, c2s_ref[...]) + _dot(p_lo, c2s_ref[...])
    tq_pos = t0 + lax.broadcasted_iota(jnp.int32, (TQ, LANES), 0)
    sidx = lax.broadcasted_iota(jnp.int32, (TQ, LANES), 1)
    cur = tq_pos >> 6
    score = jnp.where(sidx <= cur, imp, NEG_INF)
    for forced_blk in (cur - 1, cur, jnp.zeros_like(cur)):
        score = jnp.where(sidx == forced_blk, FORCE_SCORE, score)
    valid = sidx < N_SEL
    score = jnp.where(valid, score, -3e38)
    rank = jnp.zeros((TQ, LANES), F32)
    for sp in range(N_SEL):
        col = score[:, sp:sp + 1]
        tie = jnp.where(sidx > sp, 1.0, 0.0)
        rank = rank + jnp.where(col > score, 1.0, jnp.where(col == score, tie, 0.0))
    selb = jnp.where(valid, jnp.where(rank < TOP_N, 0.0, NEG_INF), 0.0).astype(BF16)
    q_aug = jnp.concatenate([q_stack, jnp.concatenate([selb] * GROUP_SIZE, axis=0)], axis=1)

    m0 = jnp.full((rows_all, 1), NEG_INF, F32)
    l0 = jnp.zeros((rows_all, 1), F32)
    a0 = jnp.zeros((rows_all, HEAD_DIM), F32)

    def online(carry, s, v):
        m, l, acc = carry
        m_new = jnp.maximum(m, jnp.max(s, axis=-1, keepdims=True))
        alpha = jnp.exp(m - m_new)
        p = jnp.exp(s - m_new)
        l = alpha * l + jnp.sum(p, axis=-1, keepdims=True)
        acc = alpha * acc + _dot(p.astype(BF16), v)
        return m_new, l, acc

    def sel_body(kt, carry):
        krows = pl.ds(pl.multiple_of(kt * TK, TK), TK)
        s = _dot_nt(q_aug, kaug_s[krows, :]) * SCALE + dm_ref[0, jnp.minimum(qt - kt, 2)]
        return online(carry, s, vs_s[krows, :])

    _, l_s, acc_s = lax.fori_loop(0, qt + 1, sel_body, (m0, l0, a0))
    o_s = acc_s / l_s

    carry = (m0, l0, a0)
    for dist, tile in ((2, 3), (1, 1), (0, 0)):
        kt = jnp.maximum(qt - dist, 0)
        krows = pl.ds(pl.multiple_of(kt * TK, TK), TK)
        pen = jnp.where(qt >= dist, 0.0, NEG_INF)
        s = _dot_nt(q_stack, kw_s[krows, :]) * SCALE + dm_ref[0, tile] + pen
        carry = online(carry, s, vw_s[krows, :])
    _, l_w, acc_w = carry
    o_w = acc_w / l_w

    gates = jax.nn.sigmoid(gate_ref[...])
    for r in range(GROUP_SIZE):
        rs = slice(r * TQ, (r + 1) * TQ)
        c = N_BRANCH * r
        o = (gates[:, c:c + 1] * o_c[rs] + gates[:, c + 1:c + 2] * o_s[rs]
             + gates[:, c + 2:c + 3] * o_w[rs])
        o_ref[:, r * HEAD_DIM:(r + 1) * HEAD_DIM] = o.astype(o_ref.dtype)


def _cmp_to_sel():
    cs = np.arange(LANES)[:, None] * CMP_STRIDE
    ss = np.arange(LANES)[None, :] * SEL_BLOCK
    ov = np.clip(np.minimum(cs + CMP_BLOCK, ss + SEL_BLOCK) - np.maximum(cs, ss), 0, None)
    m = ov.astype(np.float32) / np.float32(CMP_BLOCK)
    m[N_CMP:, :] = 0.0
    m[:, N_SEL:] = 0.0
    return jnp.asarray(m, BF16)


def _nsa_attention(proj, batch, q_norm_w, k_norm_w, pe_k, pe_v, w_cmp_k, w_cmp_v, biasc, dm):
    kvb = COL_KV // LANES

    def kv_spec(branch, which):
        base = kvb + (branch * 2 + which) * N_KV_GROUPS
        return pl.BlockSpec((SEQ, LANES), lambda b, g, qt: (b, base + g))

    full = lambda shape: pl.BlockSpec(shape, lambda b, g, qt: (0,) * len(shape))
    return pl.pallas_call(
        _nsa_kernel,
        out_shape=jax.ShapeDtypeStruct((batch * SEQ, NSA_WIDTH), BF16),
        grid=(batch, N_KV_GROUPS, NQ),
        in_specs=[pl.BlockSpec((TQ, GROUP_SIZE * HEAD_DIM), lambda b, g, qt: (b * NQ + qt, g)),
                  kv_spec(0, 0), kv_spec(0, 1), kv_spec(1, 0), kv_spec(1, 1),
                  kv_spec(2, 0), kv_spec(2, 1),
                  pl.BlockSpec((TQ, LANES), lambda b, g, qt: (b * NQ + qt, COL_GATE // LANES + g)),
                  full((1, HEAD_DIM)), full((N_BRANCH, HEAD_DIM)),
                  full((CMP_BLOCK, HEAD_DIM)), full((CMP_BLOCK, HEAD_DIM)),
                  full((CMP_BLOCK * HEAD_DIM, HEAD_DIM)), full((CMP_BLOCK * HEAD_DIM, HEAD_DIM)),
                  pl.BlockSpec((GROUP_SIZE, TQ, LANES), lambda b, g, qt: (g, qt, 0)),
                  pl.BlockSpec((1, 4, GROUP_SIZE * TQ, TK), lambda b, g, qt: (g, 0, 0, 0)),
                  full((LANES, LANES))],
        out_specs=pl.BlockSpec((TQ, GROUP_SIZE * HEAD_DIM), lambda b, g, qt: (b * NQ + qt, g)),
        scratch_shapes=[pltpu.VMEM((LANES, HEAD_DIM), BF16),
                        pltpu.VMEM((LANES, HEAD_DIM), BF16),
                        pltpu.VMEM((SEQ, 2 * HEAD_DIM), BF16),
                        pltpu.VMEM((SEQ, HEAD_DIM), BF16),
                        pltpu.VMEM((SEQ, HEAD_DIM), BF16),
                        pltpu.VMEM((SEQ, HEAD_DIM), BF16)],
        compiler_params=pltpu.CompilerParams(
            dimension_semantics=("arbitrary", "arbitrary", "arbitrary"),
            vmem_limit_bytes=VMEM_LIMIT),
    )(proj, proj, proj, proj, proj, proj, proj, proj,
      q_norm_w.reshape(1, HEAD_DIM), k_norm_w, pe_k, pe_v,
      w_cmp_k.astype(BF16), w_cmp_v.astype(BF16), biasc, dm, _cmp_to_sel())


def _ret_kernel(lg_ref, q_ref, k_ref, v_ref, g_ref, cos_ref, sin_ref, gnw_ref, o_ref, state_ref):
    h = pl.program_id(1)
    lg = lg_ref[h]
    c_len = RET_CHUNK
    ii = lax.broadcasted_iota(jnp.int32, (c_len, c_len), 0)
    jj = lax.broadcasted_iota(jnp.int32, (c_len, c_len), 1)
    diff = (ii - jj).astype(F32)
    decay_in = jnp.where(diff >= 0.0, jnp.exp(jnp.maximum(diff, 0.0) * lg), 0.0)
    icol = lax.broadcasted_iota(jnp.int32, (c_len, 1), 0).astype(F32)
    xi = jnp.exp((icol + 1.0) * lg)
    zeta = jnp.exp((c_len - 1.0 - icol) * lg)
    chunk_decay = jnp.exp(jnp.full((1, RET_VAL_DIM), float(c_len), F32) * lg)
    half = RET_KEY_DIM // 2
    state_ref[...] = jnp.zeros_like(state_ref)

    def body(c, _):
        rows = pl.ds(pl.multiple_of(c * c_len, c_len), c_len)
        cos = cos_ref[rows, :]
        sin = sin_ref[rows, :]

        def rot(x):
            x1, x2 = x[:, :half], x[:, half:]
            return jnp.concatenate([x1 * cos - x2 * sin, x2 * cos + x1 * sin], axis=-1)

        qr = rot(q_ref[rows, :])
        kr = rot(k_ref[rows, :]) * (RET_KEY_DIM ** -0.5)
        vb = v_ref[rows, :].astype(BF16)
        scores = _dot_nt(qr.astype(BF16), kr.astype(BF16)) * decay_in
        inner = _dot(scores.astype(BF16), vb)
        st = state_ref[...]
        cross = _dot((qr * xi).astype(BF16), st.astype(BF16))
        kv = _dot((kr * zeta).T.astype(BF16), vb)
        state_ref[...] = chunk_decay * st + kv
        y = inner + cross
        y = y * lax.rsqrt(jnp.mean(y * y, axis=-1, keepdims=True) + EPS) * gnw_ref[...]
        g = g_ref[rows, :]
        o_ref[rows, :] = (g * jax.nn.sigmoid(g) * y).astype(o_ref.dtype)
        return 0

    lax.fori_loop(0, SEQ // c_len, body, 0)


def _retention(proj, batch, gn_w):
    pos = np.arange(SEQ, dtype=np.float64)
    half = RET_KEY_DIM // 2
    inv = np.exp(-np.linspace(0.0, 1.0, half) * math.log(ROPE_BASE)).astype(np.float32)
    ang = pos.astype(np.float32)[:, None] * inv[None, :]
    cos = jnp.asarray(np.cos(ang.astype(np.float64)), F32)
    sin = jnp.asarray(np.sin(ang.astype(np.float64)), F32)
    log_gamma = jnp.asarray(np.log(1.0 - 2.0 ** (-5.0 - np.arange(N_RET_HEADS))), F32)
    w = RET_KEY_DIM

    def col_spec(col0):
        return pl.BlockSpec((SEQ, w), lambda b, h: (b, col0 // w + h))

    return pl.pallas_call(
        _ret_kernel,
        out_shape=jax.ShapeDtypeStruct((batch * SEQ, RET_WIDTH), BF16),
        grid=(batch, N_RET_HEADS),
        in_specs=[pl.BlockSpec(memory_space=pltpu.SMEM),
                  col_spec(COL_QR), col_spec(COL_KR), col_spec(COL_VR), col_spec(COL_GR),
                  pl.BlockSpec((SEQ, half), lambda b, h: (0, 0)),
                  pl.BlockSpec((SEQ, half), lambda b, h: (0, 0)),
                  pl.BlockSpec((1, RET_VAL_DIM), lambda b, h: (0, h))],
        out_specs=pl.BlockSpec((SEQ, RET_VAL_DIM), lambda b, h: (b, h)),
        scratch_shapes=[pltpu.VMEM((RET_KEY_DIM, RET_VAL_DIM), F32)],
        compiler_params=pltpu.CompilerParams(dimension_semantics=("arbitrary", "arbitrary"),
                                             vmem_limit_bytes=VMEM_LIMIT),
    )(log_gamma, proj, proj, proj, proj, cos, sin, gn_w.reshape(1, RET_WIDTH))


def _permute_in_weight(w_in):
    gate0 = 5120
    gate1 = gate0 + N_NSA_HEADS * N_BRANCH
    per_group = GROUP_SIZE * N_BRANCH
    wg = w_in[:, gate0:gate1].reshape(D_MODEL, N_KV_GROUPS, per_group)
    wg = jnp.pad(wg, ((0, 0), (0, 0), (0, LANES - per_group))).reshape(D_MODEL, N_KV_GROUPS * LANES)
    return jnp.concatenate([w_in[:, :gate0], w_in[:, gate1:], wg], axis=1).astype(BF16)


def _layer(x, norm1_w, w_in, nsa_q_norm_w, nsa_k_norm_w, cmp_pe_k, cmp_pe_v, w_cmp_k, w_cmp_v,
           rel_bias, ret_gn_w, w_out, norm2_w, w_gate, w_up, w_down):
    batch = x.shape[0]
    xf = x.reshape(batch * SEQ, D_MODEL)
    h = _rmsnorm(xf, norm1_w)
    proj = _matmul(h, _permute_in_weight(w_in), F32, tm=1024, tn=512)
    biasc, dm = _bias_tables(rel_bias)
    o_nsa = _nsa_attention(proj, batch, nsa_q_norm_w, nsa_k_norm_w, cmp_pe_k, cmp_pe_v,
                           w_cmp_k, w_cmp_v, biasc, dm)
    o_ret = _retention(proj, batch, ret_gn_w)
    x1 = _out_projection(o_nsa, o_ret, w_out.astype(BF16), xf)
    hf = _rmsnorm(x1, norm2_w)
    mid = _ffn_up(hf, w_gate.astype(BF16), w_up.astype(BF16))
    out = _ffn_down(mid, w_down.astype(BF16), x1)
    return out.reshape(batch, SEQ, D_MODEL)


def kernel(x, norm1_w, w_in, nsa_q_norm_w, nsa_k_norm_w, cmp_pe_k, cmp_pe_v, w_cmp_k, w_cmp_v,
           rel_bias, ret_gn_w, w_out, norm2_w, w_gate, w_up, w_down):
    for l in range(norm1_w.shape[0]):
        x = _layer(x, norm1_w[l], w_in[l], nsa_q_norm_w[l], nsa_k_norm_w[l], cmp_pe_k[l], cmp_pe_v[l],
                   w_cmp_k[l], w_cmp_v[l], rel_bias, ret_gn_w[l], w_out[l], norm2_w[l],
                   w_gate[l], w_up[l], w_down[l])
    return x
```

```python
import functools
import math

import numpy as np
import jax
import jax.numpy as jnp
from jax import lax
from jax.experimental import pallas as pl
from jax.experimental.pallas import tpu as pltpu

F32 = jnp.float32
BF16 = jnp.bfloat16

D_MODEL = 4096
SEQ = 2048
HEAD_DIM = 128
N_NSA_HEADS = 16
N_KV_GROUPS = 4
GROUP_SIZE = 4
N_BRANCH = 3
CMP_BLOCK = 32
CMP_STRIDE = 16
SEL_BLOCK = 64
TOP_N = 16
WINDOW = 512
RET_KEY_DIM = 256
RET_VAL_DIM = 256
N_RET_HEADS = 8
RET_CHUNK = 128
ROPE_BASE = 10000.0
NSA_WIDTH = 2048
RET_WIDTH = 2048
D_FF = 11008
NUM_BUCKETS = 32
MAX_DISTANCE = 128
EPS = 1e-6
NEG_INF = -1e30
FORCE_SCORE = 1e9
N_SEL = SEQ // SEL_BLOCK
N_CMP = (SEQ - CMP_BLOCK) // CMP_STRIDE + 1
SCALE = HEAD_DIM ** -0.5

LANES = 128
VMEM_LIMIT = 56 * 1024 * 1024

TQ = 256
TK = 256
NQ = SEQ // TQ

COL_Q = 0
COL_KV = 2048
COL_QR = 5120
COL_KR = 7168
COL_VR = 9216
COL_GR = 11264
COL_GATE = 13312
IN_COLS_P = COL_GATE + N_KV_GROUPS * LANES


def _bucket_thresholds():
    rel = np.arange(0, 4 * MAX_DISTANCE)
    max_exact = NUM_BUCKETS // 2
    nf = np.maximum(rel, 1).astype(np.float64)
    large = max_exact + np.floor(np.log(nf / max_exact) / math.log(MAX_DISTANCE / max_exact)
                                 * (NUM_BUCKETS - max_exact)).astype(np.int64)
    large = np.minimum(large, NUM_BUCKETS - 1)
    b = np.where(rel < max_exact, rel, large)
    return [int(np.argmax(b >= k)) for k in range(NUM_BUCKETS)]


_THR = _bucket_thresholds()


def _dot(a, b):
    return jnp.dot(a, b, preferred_element_type=F32)


def _dot_nt(a, b):
    return lax.dot_general(a, b, (((1,), (1,)), ((), ())), preferred_element_type=F32)


def _rms_kernel(x_ref, w_ref, o_ref):
    x = x_ref[...]
    y = x * lax.rsqrt(jnp.mean(x * x, axis=-1, keepdims=True) + EPS)
    o_ref[...] = (y * w_ref[...]).astype(o_ref.dtype)


def _rmsnorm(x, w, tm=256):
    m, d = x.shape
    return pl.pallas_call(
        _rms_kernel,
        out_shape=jax.ShapeDtypeStruct((m, d), BF16),
        grid=(m // tm,),
        in_specs=[pl.BlockSpec((tm, d), lambda i: (i, 0)),
                  pl.BlockSpec((1, d), lambda i: (0, 0))],
        out_specs=pl.BlockSpec((tm, d), lambda i: (i, 0)),
        compiler_params=pltpu.CompilerParams(dimension_semantics=("arbitrary",),
                                             vmem_limit_bytes=VMEM_LIMIT),
    )(x, w.reshape(1, d))


def _mm_kernel(a_ref, b_ref, o_ref):
    o_ref[...] = _dot(a_ref[...], b_ref[...]).astype(o_ref.dtype)


def _matmul(a, b, out_dtype, tm, tn):
    m, k = a.shape
    _, n = b.shape
    return pl.pallas_call(
        _mm_kernel,
        out_shape=jax.ShapeDtypeStruct((m, n), out_dtype),
        grid=(m // tm, n // tn),
        in_specs=[pl.BlockSpec((tm, k), lambda i, j: (i, 0)),
                  pl.BlockSpec((k, tn), lambda i, j: (0, j))],
        out_specs=pl.BlockSpec((tm, tn), lambda i, j: (i, j)),
        compiler_params=pltpu.CompilerParams(dimension_semantics=("arbitrary", "arbitrary"),
                                             vmem_limit_bytes=VMEM_LIMIT),
    )(a, b)


def _outproj_kernel(a1_ref, a2_ref, w1_ref, w2_ref, x_ref, o_ref):
    acc = _dot(a1_ref[...], w1_ref[...]) + _dot(a2_ref[...], w2_ref[...])
    o_ref[...] = x_ref[...] + acc


def _out_projection(o_nsa, o_ret, w_out, x, tm=1024, tn=512):
    m = x.shape[0]
    kh = NSA_WIDTH
    return pl.pallas_call(
        _outproj_kernel,
        out_shape=jax.ShapeDtypeStruct((m, D_MODEL), F32),
        grid=(m // tm, D_MODEL // tn),
        in_specs=[pl.BlockSpec((tm, kh), lambda i, j: (i, 0)),
                  pl.BlockSpec((tm, kh), lambda i, j: (i, 0)),
                  pl.BlockSpec((kh, tn), lambda i, j: (0, j)),
                  pl.BlockSpec((kh, tn), lambda i, j: (1, j)),
                  pl.BlockSpec((tm, tn), lambda i, j: (i, j))],
        out_specs=pl.BlockSpec((tm, tn), lambda i, j: (i, j)),
        compiler_params=pltpu.CompilerParams(dimension_semantics=("arbitrary", "arbitrary"),
                                             vmem_limit_bytes=VMEM_LIMIT),
    )(o_nsa, o_ret, w_out, w_out, x)


def _ffn_up_kernel(h_ref, wg_ref, wu_ref, o_ref):
    h = h_ref[...]
    g = _dot(h, wg_ref[...])
    u = _dot(h, wu_ref[...])
    o_ref[...] = (g * jax.nn.sigmoid(g) * u).astype(o_ref.dtype)


def _ffn_up(h, w_gate, w_up, tm=1024, tn=256):
    m, k = h.shape
    n = w_gate.shape[1]
    return pl.pallas_call(
        _ffn_up_kernel,
        out_shape=jax.ShapeDtypeStruct((m, n), BF16),
        grid=(m // tm, n // tn),
        in_specs=[pl.BlockSpec((tm, k), lambda i, j: (i, 0)),
                  pl.BlockSpec((k, tn), lambda i, j: (0, j)),
                  pl.BlockSpec((k, tn), lambda i, j: (0, j))],
        out_specs=pl.BlockSpec((tm, tn), lambda i, j: (i, j)),
        compiler_params=pltpu.CompilerParams(dimension_semantics=("arbitrary", "arbitrary"),
                                             vmem_limit_bytes=VMEM_LIMIT),
    )(h, w_gate, w_up)


def _ffn_down_kernel(a_ref, b_ref, x_ref, o_ref, acc_ref):
    kk = pl.program_id(2)

    @pl.when(kk == 0)
    def _():
        acc_ref[...] = x_ref[...]

    acc_ref[...] += _dot(a_ref[...], b_ref[...])

    @pl.when(kk == pl.num_programs(2) - 1)
    def _():
        o_ref[...] = acc_ref[...]


def _ffn_down(a, w_down, x, tm=1024, tn=512, tk=5504):
    m, k = a.shape
    n = w_down.shape[1]
    return pl.pallas_call(
        _ffn_down_kernel,
        out_shape=jax.ShapeDtypeStruct((m, n), F32),
        grid=(m // tm, n // tn, k // tk),
        in_specs=[pl.BlockSpec((tm, tk), lambda i, j, kk: (i, kk)),
                  pl.BlockSpec((tk, tn), lambda i, j, kk: (kk, j)),
                  pl.BlockSpec((tm, tn), lambda i, j, kk: (i, j))],
        out_specs=pl.BlockSpec((tm, tn), lambda i, j, kk: (i, j)),
        scratch_shapes=[pltpu.VMEM((tm, tn), F32)],
        compiler_params=pltpu.CompilerParams(
            dimension_semantics=("arbitrary", "arbitrary", "arbitrary"),
            vmem_limit_bytes=VMEM_LIMIT),
    )(a, w_down, x)


def _bias_value(rel, relb_ref, h):
    val = jnp.full(rel.shape, relb_ref[0, h], F32)
    for k in range(1, NUM_BUCKETS):
        val = jnp.where(rel >= _THR[k], relb_ref[k, h], val)
    return val


def _bias_tables_kernel(relb_ref, biasc_ref, dm_ref):
    h = pl.program_id(0)
    n = lax.broadcasted_iota(jnp.int32, (LANES, TQ), 0)
    a_c = lax.broadcasted_iota(jnp.int32, (LANES, TQ), 1)
    for qt in range(NQ):
        rel_c = (qt * TQ + a_c) - (CMP_STRIDE * n + CMP_BLOCK - 1)
        biasc_ref[0, qt] = _bias_value(rel_c, relb_ref, h)
    b = lax.broadcasted_iota(jnp.int32, (TK, TQ), 0)
    a = lax.broadcasted_iota(jnp.int32, (TK, TQ), 1)
    far = relb_ref[NUM_BUCKETS - 1, h]
    inv_scale = 1.0 / SCALE
    dm_ref[0, 0] = ((_bias_value(a - b, relb_ref, h) - far) * inv_scale
                    + jnp.where(a >= b, 0.0, NEG_INF))
    dm_ref[0, 1] = (_bias_value(a - b + TK, relb_ref, h) - far) * inv_scale
    dm_ref[0, 2] = jnp.where(b > a, 0.0, NEG_INF)


def _bias_tables(rel_bias):
    return pl.pallas_call(
        _bias_tables_kernel,
        out_shape=(jax.ShapeDtypeStruct((N_KV_GROUPS, NQ, LANES, GROUP_SIZE * TQ), F32),
                   jax.ShapeDtypeStruct((N_KV_GROUPS, 3, TK, GROUP_SIZE * TQ), F32)),
        grid=(N_NSA_HEADS,),
        in_specs=[pl.BlockSpec(memory_space=pltpu.SMEM)],
        out_specs=(pl.BlockSpec((1, NQ, LANES, TQ), lambda h: (h // GROUP_SIZE, 0, 0, h % GROUP_SIZE)),
                   pl.BlockSpec((1, 3, TK, TQ), lambda h: (h // GROUP_SIZE, 0, 0, h % GROUP_SIZE))),
        compiler_params=pltpu.CompilerParams(dimension_semantics=("arbitrary",),
                                             vmem_limit_bytes=VMEM_LIMIT),
    )(rel_bias)


def _row_rms(x, w):
    return x * lax.rsqrt(jnp.mean(x * x, axis=-1, keepdims=True) + EPS) * w


def _nsa_kernel(q_ref, k0_ref, v0_ref, k1_ref, v1_ref, k2_ref, v2_ref, gate_ref,
                qnw_ref, knw_ref, pek_ref, pev_ref, wck_ref, wcv_ref, biasc_ref, dm_ref, c2st_ref,
                o_ref,
                kc_s, vct_s, kaug_s, kw_s, vst_s, vwt_s):
    qt = pl.program_id(2)
    cols_all = GROUP_SIZE * TQ

    @pl.when(qt == 0)
    def _prologue():
        kaug_s[:, :HEAD_DIM] = _row_rms(k1_ref[...], knw_ref[1:2, :]).astype(BF16)
        jblk = lax.broadcasted_iota(jnp.int32, (SEQ, LANES), 0) // SEL_BLOCK
        lane = lax.broadcasted_iota(jnp.int32, (SEQ, LANES), 1)
        kaug_s[:, HEAD_DIM:] = jnp.where(jblk == lane, 1.0, 0.0).astype(BF16)
        kw_s[...] = _row_rms(k2_ref[...], knw_ref[2:3, :]).astype(BF16)
        for t in range(SEQ // TK):
            vst_s[t] = v1_ref[t * TK:(t + 1) * TK, :].T.astype(BF16)
            vwt_s[t] = v2_ref[t * TK:(t + 1) * TK, :].T.astype(BF16)

        def compress(x_ref, pe_ref, w_ref):
            acc_lo = jnp.zeros((LANES, HEAD_DIM), F32)
            acc_hi = jnp.zeros((LANES, HEAD_DIM), F32)
            for j in range(CMP_STRIDE):
                y = x_ref[pl.ds(j, LANES, stride=CMP_STRIDE), :]
                acc_lo += _dot((y + pe_ref[j:j + 1, :]).astype(BF16),
                               w_ref[j * HEAD_DIM:(j + 1) * HEAD_DIM, :])
                jh = CMP_STRIDE + j
                acc_hi += _dot((y + pe_ref[jh:jh + 1, :]).astype(BF16),
                               w_ref[jh * HEAD_DIM:(jh + 1) * HEAD_DIM, :])
            return acc_lo + pltpu.roll(acc_hi, LANES - 1, 0)

        kc_s[...] = _row_rms(compress(k0_ref, pek_ref, wck_ref), knw_ref[0:1, :]).astype(BF16)
        vct_s[...] = compress(v0_ref, pev_ref, wcv_ref).T.astype(BF16)

    t0 = qt * TQ
    q = q_ref[...]
    qts = []
    for r in range(GROUP_SIZE):
        qts.append(_row_rms(q[:, r * HEAD_DIM:(r + 1) * HEAD_DIM], qnw_ref[...]).T)
    qt_stack = jnp.concatenate(qts, axis=1).astype(BF16)

    sc = _dot(kc_s[...], qt_stack) * SCALE + biasc_ref[0, 0]
    tpos = t0 + (lax.broadcasted_iota(jnp.int32, (LANES, cols_all), 1) & (TQ - 1))
    nrow = lax.broadcasted_iota(jnp.int32, (LANES, cols_all), 0)
    maskc = tpos >= CMP_STRIDE * nrow + (CMP_BLOCK - 1)
    scm = jnp.where(maskc, sc, NEG_INF)
    mc = jnp.max(scm, axis=0, keepdims=True)
    ec = jnp.where(maskc, jnp.exp(scm - mc), 0.0)
    lc = jnp.sum(ec, axis=0, keepdims=True)
    pc = ec * (1.0 / jnp.where(lc > 0.0, lc, 1.0))
    o_c = _dot(vct_s[...], pc.astype(BF16))

    psum = pc[:, 0:TQ] + pc[:, TQ:2 * TQ] + pc[:, 2 * TQ:3 * TQ] + pc[:, 3 * TQ:4 * TQ]
    p_hi = psum.astype(BF16)
    p_lo = (psum - p_hi.astype(F32)).astype(BF16)
    imp = (_dot(c2st_ref[...], p_hi) + _dot(c2st_ref[...], p_lo))[0:N_SEL]
    tq_pos = t0 + lax.broadcasted_iota(jnp.int32, (N_SEL, TQ), 1)
    sidx = lax.broadcasted_iota(jnp.int32, (N_SEL, TQ), 0)
    cur = tq_pos >> 6
    score = jnp.where(sidx <= cur, imp, NEG_INF)
    for forced_blk in (cur - 1, cur, jnp.zeros_like(cur)):
        score = jnp.where(sidx == forced_blk, FORCE_SCORE, score)
    rank = jnp.zeros((N_SEL, TQ), F32)
    for sp in range(N_SEL):
        row = score[sp:sp + 1, :]
        tie = jnp.where(sidx > sp, 1.0, 0.0)
        rank = rank + jnp.where(row > score, 1.0, jnp.where(row == score, tie, 0.0))
    selb = jnp.where(rank < TOP_N, 0.0, NEG_INF)
    selb = jnp.concatenate([selb, jnp.zeros((HEAD_DIM - N_SEL, TQ), F32)], axis=0).astype(BF16)
    qt_aug = jnp.concatenate([qt_stack, jnp.concatenate([selb] * GROUP_SIZE, axis=1)], axis=0)

    m0 = jnp.full((1, cols_all), NEG_INF, F32)
    l0 = jnp.zeros((1, cols_all), F32)
    a0 = jnp.zeros((HEAD_DIM, cols_all), F32)
    exp2_scale = SCALE * math.log2(math.e)

    def online(carry, u, vt):
        m, l, acc = carry
        m_new = jnp.maximum(m, jnp.max(u, axis=0, keepdims=True))
        alpha = jnp.exp2((m - m_new) * exp2_scale)
        p = jnp.exp2((u - m_new) * exp2_scale)
        l = alpha * l + jnp.sum(p, axis=0, keepdims=True)
        acc = alpha * acc + _dot(vt, p.astype(BF16))
        return m_new, l, acc

    def key_rows(kt):
        return pl.ds(pl.multiple_of(kt * TK, TK), TK)

    def sel_far(kt, carry):
        return online(carry, _dot(kaug_s[key_rows(kt), :], qt_aug), vst_s[kt])

    carry = lax.fori_loop(0, jnp.maximum(qt - 1, 0), sel_far, (m0, l0, a0))
    kt1 = jnp.maximum(qt - 1, 0)
    pen1 = jnp.where(qt >= 1, 0.0, NEG_INF)
    carry = online(carry, _dot(kaug_s[key_rows(kt1), :], qt_aug) + dm_ref[0, 1] + pen1, vst_s[kt1])
    carry = online(carry, _dot(kaug_s[key_rows(qt), :], qt_aug) + dm_ref[0, 0], vst_s[qt])
    _, l_s, acc_s = carry
    o_s = acc_s * (1.0 / l_s)

    carry = (m0, l0, a0)
    for dist, kind in ((2, 2), (1, 1), (0, 0)):
        kt = jnp.maximum(qt - dist, 0)
        u = _dot(kw_s[key_rows(kt), :], qt_stack) + dm_ref[0, kind]
        if dist > 0:
            u = u + jnp.where(qt >= dist, 0.0, NEG_INF)
        carry = online(carry, u, vwt_s[kt])
    _, l_w, acc_w = carry
    o_w = acc_w * (1.0 / l_w)

    gates_t = jax.nn.sigmoid(gate_ref[...]).T
    for r in range(GROUP_SIZE):
        cs = slice(r * TQ, (r + 1) * TQ)
        c = N_BRANCH * r
        o = (gates_t[c:c + 1, :] * o_c[:, cs] + gates_t[c + 1:c + 2, :] * o_s[:, cs]
             + gates_t[c + 2:c + 3, :] * o_w[:, cs])
        o_ref[:, r * HEAD_DIM:(r + 1) * HEAD_DIM] = o.T.astype(o_ref.dtype)


def _cmp_to_sel_t():
    ss = np.arange(LANES)[:, None] * SEL_BLOCK
    cs = np.arange(LANES)[None, :] * CMP_STRIDE
    ov = np.clip(np.minimum(cs + CMP_BLOCK, ss + SEL_BLOCK) - np.maximum(cs, ss), 0, None)
    m = ov.astype(np.float32) / np.float32(CMP_BLOCK)
    m[N_SEL:, :] = 0.0
    m[:, N_CMP:] = 0.0
    return jnp.asarray(m, BF16)


def _nsa_attention(proj, batch, q_norm_w, k_norm_w, pe_k, pe_v, w_cmp_k, w_cmp_v, biasc, dm):
    kvb = COL_KV // LANES

    def kv_spec(branch, which):
        base = kvb + (branch * 2 + which) * N_KV_GROUPS
        return pl.BlockSpec((SEQ, LANES), lambda b, g, qt: (b, base + g))

    full = lambda shape: pl.BlockSpec(shape, lambda b, g, qt: (0,) * len(shape))
    return pl.pallas_call(
        _nsa_kernel,
        out_shape=jax.ShapeDtypeStruct((batch * SEQ, NSA_WIDTH), BF16),
        grid=(batch, N_KV_GROUPS, NQ),
        in_specs=[pl.BlockSpec((TQ, GROUP_SIZE * HEAD_DIM), lambda b, g, qt: (b * NQ + qt, g)),
                  kv_spec(0, 0), kv_spec(0, 1), kv_spec(1, 0), kv_spec(1, 1),
                  kv_spec(2, 0), kv_spec(2, 1),
                  pl.BlockSpec((TQ, LANES), lambda b, g, qt: (b * NQ + qt, COL_GATE // LANES + g)),
                  full((1, HEAD_DIM)), full((N_BRANCH, HEAD_DIM)),
                  full((CMP_BLOCK, HEAD_DIM)), full((CMP_BLOCK, HEAD_DIM)),
                  full((CMP_BLOCK * HEAD_DIM, HEAD_DIM)), full((CMP_BLOCK * HEAD_DIM, HEAD_DIM)),
                  pl.BlockSpec((1, 1, LANES, GROUP_SIZE * TQ), lambda b, g, qt: (g, qt, 0, 0)),
                  pl.BlockSpec((1, 3, TK, GROUP_SIZE * TQ), lambda b, g, qt: (g, 0, 0, 0)),
                  full((LANES, LANES))],
        out_specs=pl.BlockSpec((TQ, GROUP_SIZE * HEAD_DIM), lambda b, g, qt: (b * NQ + qt, g)),
        scratch_shapes=[pltpu.VMEM((LANES, HEAD_DIM), BF16),
                        pltpu.VMEM((HEAD_DIM, LANES), BF16),
                        pltpu.VMEM((SEQ, 2 * HEAD_DIM), BF16),
                        pltpu.VMEM((SEQ, HEAD_DIM), BF16),
                        pltpu.VMEM((SEQ // TK, HEAD_DIM, TK), BF16),
                        pltpu.VMEM((SEQ // TK, HEAD_DIM, TK), BF16)],
        compiler_params=pltpu.CompilerParams(
            dimension_semantics=("arbitrary", "arbitrary", "arbitrary"),
            vmem_limit_bytes=VMEM_LIMIT),
    )(proj, proj, proj, proj, proj, proj, proj, proj,
      q_norm_w.reshape(1, HEAD_DIM), k_norm_w, pe_k, pe_v,
      w_cmp_k.astype(BF16), w_cmp_v.astype(BF16), biasc, dm, _cmp_to_sel_t())


def _ret_kernel(lg_ref, q_ref, k_ref, v_ref, g_ref, cos_ref, sin_ref, gnw_ref, o_ref, state_ref):
    h = pl.program_id(1)
    lg = lg_ref[h]
    c_len = RET_CHUNK
    ii = lax.broadcasted_iota(jnp.int32, (c_len, c_len), 0)
    jj = lax.broadcasted_iota(jnp.int32, (c_len, c_len), 1)
    diff = (ii - jj).astype(F32)
    decay_in = jnp.where(diff >= 0.0, jnp.exp(jnp.maximum(diff, 0.0) * lg), 0.0)
    icol = lax.broadcasted_iota(jnp.int32, (c_len, 1), 0).astype(F32)
    xi = jnp.exp((icol + 1.0) * lg)
    zeta = jnp.exp((c_len - 1.0 - icol) * lg)
    chunk_decay = jnp.exp(jnp.full((1, RET_VAL_DIM), float(c_len), F32) * lg)
    half = RET_KEY_DIM // 2
    state_ref[...] = jnp.zeros_like(state_ref)

    def body(c, _):
        rows = pl.ds(pl.multiple_of(c * c_len, c_len), c_len)
        cos = cos_ref[rows, :]
        sin = sin_ref[rows, :]

        def rot(x):
            x1, x2 = x[:, :half], x[:, half:]
            return jnp.concatenate([x1 * cos - x2 * sin, x2 * cos + x1 * sin], axis=-1)

        qr = rot(q_ref[rows, :])
        kr = rot(k_ref[rows, :]) * (RET_KEY_DIM ** -0.5)
        vb = v_ref[rows, :].astype(BF16)
        scores = _dot_nt(qr.astype(BF16), kr.astype(BF16)) * decay_in
        inner = _dot(scores.astype(BF16), vb)
        st = state_ref[...]
        cross = _dot((qr * xi).astype(BF16), st.astype(BF16))
        kv = _dot((kr * zeta).T.astype(BF16), vb)
        state_ref[...] = chunk_decay * st + kv
        y = inner + cross
        y = y * lax.rsqrt(jnp.mean(y * y, axis=-1, keepdims=True) + EPS) * gnw_ref[...]
        g = g_ref[rows, :]
        o_ref[rows, :] = (g * jax.nn.sigmoid(g) * y).astype(o_ref.dtype)
        return 0

    lax.fori_loop(0, SEQ // c_len, body, 0)


def _retention(proj, batch, gn_w):
    pos = np.arange(SEQ, dtype=np.float64)
    half = RET_KEY_DIM // 2
    inv = np.exp(-np.linspace(0.0, 1.0, half) * math.log(ROPE_BASE)).astype(np.float32)
    ang = pos.astype(np.float32)[:, None] * inv[None, :]
    cos = jnp.asarray(np.cos(ang.astype(np.float64)), F32)
    sin = jnp.asarray(np.sin(ang.astype(np.float64)), F32)
    log_gamma = jnp.asarray(np.log(1.0 - 2.0 ** (-5.0 - np.arange(N_RET_HEADS))), F32)
    w = RET_KEY_DIM

    def col_spec(col0):
        return pl.BlockSpec((SEQ, w), lambda b, h: (b, col0 // w + h))

    return pl.pallas_call(
        _ret_kernel,
        out_shape=jax.ShapeDtypeStruct((batch * SEQ, RET_WIDTH), BF16),
        grid=(batch, N_RET_HEADS),
        in_specs=[pl.BlockSpec(memory_space=pltpu.SMEM),
                  col_spec(COL_QR), col_spec(COL_KR), col_spec(COL_VR), col_spec(COL_GR),
                  pl.BlockSpec((SEQ, half), lambda b, h: (0, 0)),
                  pl.BlockSpec((SEQ, half), lambda b, h: (0, 0)),
                  pl.BlockSpec((1, RET_VAL_DIM), lambda b, h: (0, h))],
        out_specs=pl.BlockSpec((SEQ, RET_VAL_DIM), lambda b, h: (b, h)),
        scratch_shapes=[pltpu.VMEM((RET_KEY_DIM, RET_VAL_DIM), F32)],
        compiler_params=pltpu.CompilerParams(dimension_semantics=("arbitrary", "arbitrary"),
                                             vmem_limit_bytes=VMEM_LIMIT),
    )(log_gamma, proj, proj, proj, proj, cos, sin, gn_w.reshape(1, RET_WIDTH))


def _permute_in_weight(w_in):
    gate0 = 5120
    gate1 = gate0 + N_NSA_HEADS * N_BRANCH
    per_group = GROUP_SIZE * N_BRANCH
    wg = w_in[:, gate0:gate1].reshape(D_MODEL, N_KV_GROUPS, per_group)
    wg = jnp.pad(wg, ((0, 0), (0, 0), (0, LANES - per_group))).reshape(D_MODEL, N_KV_GROUPS * LANES)
    return jnp.concatenate([w_in[:, :gate0], w_in[:, gate1:], wg], axis=1).astype(BF16)


def _layer(x, norm1_w, w_in, nsa_q_norm_w, nsa_k_norm_w, cmp_pe_k, cmp_pe_v, w_cmp_k, w_cmp_v,
           rel_bias, ret_gn_w, w_out, norm2_w, w_gate, w_up, w_down):
    batch = x.shape[0]
    xf = x.reshape(batch * SEQ, D_MODEL)
    h = _rmsnorm(xf, norm1_w)
    proj = _matmul(h, _permute_in_weight(w_in), F32, tm=1024, tn=512)
    biasc, dm = _bias_tables(rel_bias)
    o_nsa = _nsa_attention(proj, batch, nsa_q_norm_w, nsa_k_norm_w, cmp_pe_k, cmp_pe_v,
                           w_cmp_k, w_cmp_v, biasc, dm)
    o_ret = _retention(proj, batch, ret_gn_w)
    x1 = _out_projection(o_nsa, o_ret, w_out.astype(BF16), xf)
    hf = _rmsnorm(x1, norm2_w)
    mid = _ffn_up(hf, w_gate.astype(BF16), w_up.astype(BF16))
    out = _ffn_down(mid, w_down.astype(BF16), x1)
    return out.reshape(batch, SEQ, D_MODEL)


def kernel(x, norm1_w, w_in, nsa_q_norm_w, nsa_k_norm_w, cmp_pe_k, cmp_pe_v, w_cmp_k, w_cmp_v,
           rel_bias, ret_gn_w, w_out, norm2_w, w_gate, w_up, w_down):
    for l in range(norm1_w.shape[0]):
        x = _layer(x, norm1_w[l], w_in[l], nsa_q_norm_w[l], nsa_k_norm_w[l], cmp_pe_k[l], cmp_pe_v[l],
                   w_cmp_k[l], w_cmp_v[l], rel_bias, ret_gn_w[l], w_out[l], norm2_w[l],
                   w_gate[l], w_up[l], w_down[l])
    return x
```

```python
import functools
import math

import numpy as np
import jax
import jax.numpy as jnp
from jax import lax
from jax.experimental import pallas as pl
from jax.experimental.pallas import tpu as pltpu

F32 = jnp.float32
BF16 = jnp.bfloat16

D_MODEL = 4096
SEQ = 2048
HEAD_DIM = 128
N_NSA_HEADS = 16
N_KV_GROUPS = 4
GROUP_SIZE = 4
N_BRANCH = 3
CMP_BLOCK = 32
CMP_STRIDE = 16
SEL_BLOCK = 64
TOP_N = 16
WINDOW = 512
RET_KEY_DIM = 256
RET_VAL_DIM = 256
N_RET_HEADS = 8
RET_CHUNK = 128
ROPE_BASE = 10000.0
NSA_WIDTH = 2048
RET_WIDTH = 2048
D_FF = 11008
NUM_BUCKETS = 32
MAX_DISTANCE = 128
EPS = 1e-6
NEG_INF = -1e30
FORCE_SCORE = 1e9
N_SEL = SEQ // SEL_BLOCK
N_CMP = (SEQ - CMP_BLOCK) // CMP_STRIDE + 1
SCALE = HEAD_DIM ** -0.5

LANES = 128
VMEM_LIMIT = 56 * 1024 * 1024

TQ = 256
TK = 256
NQ = SEQ // TQ

W_IN_NSA_COLS = NSA_WIDTH + N_BRANCH * 2 * N_KV_GROUPS * HEAD_DIM
W_IN_GATE_COLS = N_NSA_HEADS * N_BRANCH
COL_KV = NSA_WIDTH
COL_QR = 0
COL_KR = 2048
COL_VR = 4096
COL_GR = 6144


def _bucket_thresholds():
    rel = np.arange(0, 4 * MAX_DISTANCE)
    max_exact = NUM_BUCKETS // 2
    nf = np.maximum(rel, 1).astype(np.float64)
    large = max_exact + np.floor(np.log(nf / max_exact) / math.log(MAX_DISTANCE / max_exact)
                                 * (NUM_BUCKETS - max_exact)).astype(np.int64)
    large = np.minimum(large, NUM_BUCKETS - 1)
    b = np.where(rel < max_exact, rel, large)
    return [int(np.argmax(b >= k)) for k in range(NUM_BUCKETS)]


_THR = _bucket_thresholds()


def _dot(a, b):
    return jnp.dot(a, b, preferred_element_type=F32)


def _dot_nt(a, b):
    return lax.dot_general(a, b, (((1,), (1,)), ((), ())), preferred_element_type=F32)


def _rms_kernel(x_ref, w_ref, o_ref):
    x = x_ref[...]
    y = x * lax.rsqrt(jnp.mean(x * x, axis=-1, keepdims=True) + EPS)
    o_ref[...] = (y * w_ref[...]).astype(o_ref.dtype)


def _rmsnorm(x, w, tm=256):
    m, d = x.shape
    return pl.pallas_call(
        _rms_kernel,
        out_shape=jax.ShapeDtypeStruct((m, d), BF16),
        grid=(m // tm,),
        in_specs=[pl.BlockSpec((tm, d), lambda i: (i, 0)),
                  pl.BlockSpec((1, d), lambda i: (0, 0))],
        out_specs=pl.BlockSpec((tm, d), lambda i: (i, 0)),
        compiler_params=pltpu.CompilerParams(dimension_semantics=("arbitrary",),
                                             vmem_limit_bytes=VMEM_LIMIT),
    )(x, w.reshape(1, d))


def _cast_specs(casts, n_steps, nj):
    in_specs, out_specs, out_shapes = [], [], []
    for arr, ncols in casts:
        slab = arr.shape[0] // n_steps
        assert slab * n_steps == arr.shape[0] and slab % 16 == 0 and ncols % LANES == 0
        spec = pl.BlockSpec((slab, ncols), lambda i, j: (i * nj + j, 0))
        in_specs.append(spec)
        out_specs.append(spec)
        out_shapes.append(jax.ShapeDtypeStruct((arr.shape[0], ncols), BF16))
    return in_specs, out_specs, out_shapes


def _cast_slabs(refs, n_casts):
    for src, dst in zip(refs[:n_casts], refs[len(refs) - n_casts:]):
        dst[...] = src[...].astype(dst.dtype)


def _mm_kernel(a_ref, b_ref, *rest, n_casts):
    o_ref = rest[n_casts]
    o_ref[...] = _dot(a_ref[...], b_ref[...]).astype(o_ref.dtype)
    _cast_slabs(rest[:n_casts] + rest[n_casts + 1:], n_casts)


def _matmul(a, b, out_dtype, tm, tn, casts=()):
    m, k = a.shape
    _, n = b.shape
    grid = (m // tm, n // tn)
    c_in, c_out, c_shapes = _cast_specs(casts, grid[0] * grid[1], grid[1])
    outs = pl.pallas_call(
        functools.partial(_mm_kernel, n_casts=len(casts)),
        out_shape=[jax.ShapeDtypeStruct((m, n), out_dtype)] + c_shapes,
        grid=grid,
        in_specs=[pl.BlockSpec((tm, k), lambda i, j: (i, 0)),
                  pl.BlockSpec((k, tn), lambda i, j: (0, j))] + c_in,
        out_specs=[pl.BlockSpec((tm, tn), lambda i, j: (i, j))] + c_out,
        compiler_params=pltpu.CompilerParams(dimension_semantics=("arbitrary", "arbitrary"),
                                             vmem_limit_bytes=VMEM_LIMIT),
    )(a, b, *[arr for arr, _ in casts])
    return outs


def _outproj_kernel(a1_ref, a2_ref, w1_ref, w2_ref, x_ref, o_ref):
    acc = _dot(a1_ref[...], w1_ref[...]) + _dot(a2_ref[...], w2_ref[...])
    o_ref[...] = x_ref[...] + acc


def _out_projection(o_nsa, o_ret, w_out, x, tm=1024, tn=512):
    m = x.shape[0]
    kh = NSA_WIDTH
    return pl.pallas_call(
        _outproj_kernel,
        out_shape=jax.ShapeDtypeStruct((m, D_MODEL), F32),
        grid=(m // tm, D_MODEL // tn),
        in_specs=[pl.BlockSpec((tm, kh), lambda i, j: (i, 0)),
                  pl.BlockSpec((tm, kh), lambda i, j: (i, 0)),
                  pl.BlockSpec((kh, tn), lambda i, j: (0, j)),
                  pl.BlockSpec((kh, tn), lambda i, j: (1, j)),
                  pl.BlockSpec((tm, tn), lambda i, j: (i, j))],
        out_specs=pl.BlockSpec((tm, tn), lambda i, j: (i, j)),
        compiler_params=pltpu.CompilerParams(dimension_semantics=("arbitrary", "arbitrary"),
                                             vmem_limit_bytes=VMEM_LIMIT),
    )(o_nsa, o_ret, w_out, w_out, x)


def _ffn_up_kernel(h_ref, wg_ref, wu_ref, *rest, n_casts):
    o_ref = rest[n_casts]
    h = h_ref[...]
    g = _dot(h, wg_ref[...])
    u = _dot(h, wu_ref[...])
    o_ref[...] = (g * jax.nn.sigmoid(g) * u).astype(o_ref.dtype)
    _cast_slabs(rest[:n_casts] + rest[n_casts + 1:], n_casts)


def _ffn_up(h, w_gate, w_up, tm=1024, tn=256, casts=()):
    m, k = h.shape
    n = w_gate.shape[1]
    grid = (m // tm, n // tn)
    c_in, c_out, c_shapes = _cast_specs(casts, grid[0] * grid[1], grid[1])
    return pl.pallas_call(
        functools.partial(_ffn_up_kernel, n_casts=len(casts)),
        out_shape=[jax.ShapeDtypeStruct((m, n), BF16)] + c_shapes,
        grid=grid,
        in_specs=[pl.BlockSpec((tm, k), lambda i, j: (i, 0)),
                  pl.BlockSpec((k, tn), lambda i, j: (0, j)),
                  pl.BlockSpec((k, tn), lambda i, j: (0, j))] + c_in,
        out_specs=[pl.BlockSpec((tm, tn), lambda i, j: (i, j))] + c_out,
        compiler_params=pltpu.CompilerParams(dimension_semantics=("arbitrary", "arbitrary"),
                                             vmem_limit_bytes=VMEM_LIMIT),
    )(h, w_gate, w_up, *[arr for arr, _ in casts])


def _ffn_down_kernel(a_ref, b_ref, x_ref, o_ref, acc_ref):
    kk = pl.program_id(2)

    @pl.when(kk == 0)
    def _():
        acc_ref[...] = x_ref[...]

    acc_ref[...] += _dot(a_ref[...], b_ref[...])

    @pl.when(kk == pl.num_programs(2) - 1)
    def _():
        o_ref[...] = acc_ref[...]


def _ffn_down(a, w_down, x, tm=1024, tn=512, tk=5504):
    m, k = a.shape
    n = w_down.shape[1]
    return pl.pallas_call(
        _ffn_down_kernel,
        out_shape=jax.ShapeDtypeStruct((m, n), F32),
        grid=(m // tm, n // tn, k // tk),
        in_specs=[pl.BlockSpec((tm, tk), lambda i, j, kk: (i, kk)),
                  pl.BlockSpec((tk, tn), lambda i, j, kk: (kk, j)),
                  pl.BlockSpec((tm, tn), lambda i, j, kk: (i, j))],
        out_specs=pl.BlockSpec((tm, tn), lambda i, j, kk: (i, j)),
        scratch_shapes=[pltpu.VMEM((tm, tn), F32)],
        compiler_params=pltpu.CompilerParams(
            dimension_semantics=("arbitrary", "arbitrary", "arbitrary"),
            vmem_limit_bytes=VMEM_LIMIT),
    )(a, w_down, x)


def _bias_value(rel, relb_ref, h):
    val = jnp.full(rel.shape, relb_ref[0, h], F32)
    for k in range(1, NUM_BUCKETS):
        val = jnp.where(rel >= _THR[k], relb_ref[k, h], val)
    return val


def _bias_tables_kernel(relb_ref, biasc_ref, dm_ref):
    h = pl.program_id(0)
    n = lax.broadcasted_iota(jnp.int32, (LANES, TQ), 0)
    a_c = lax.broadcasted_iota(jnp.int32, (LANES, TQ), 1)
    for qt in range(NQ):
        rel_c = (qt * TQ + a_c) - (CMP_STRIDE * n + CMP_BLOCK - 1)
        biasc_ref[0, qt] = _bias_value(rel_c, relb_ref, h)
    b = lax.broadcasted_iota(jnp.int32, (TK, TQ), 0)
    a = lax.broadcasted_iota(jnp.int32, (TK, TQ), 1)
    far = relb_ref[NUM_BUCKETS - 1, h]
    inv_scale = 1.0 / SCALE
    dm_ref[0, 0] = ((_bias_value(a - b, relb_ref, h) - far) * inv_scale
                    + jnp.where(a >= b, 0.0, NEG_INF))
    dm_ref[0, 1] = (_bias_value(a - b + TK, relb_ref, h) - far) * inv_scale
    dm_ref[0, 2] = jnp.where(b > a, 0.0, NEG_INF)


def _bias_tables(rel_bias):
    return pl.pallas_call(
        _bias_tables_kernel,
        out_shape=(jax.ShapeDtypeStruct((N_KV_GROUPS, NQ, LANES, GROUP_SIZE * TQ), F32),
                   jax.ShapeDtypeStruct((N_KV_GROUPS, 3, TK, GROUP_SIZE * TQ), F32)),
        grid=(N_NSA_HEADS,),
        in_specs=[pl.BlockSpec(memory_space=pltpu.SMEM)],
        out_specs=(pl.BlockSpec((1, NQ, LANES, TQ), lambda h: (h // GROUP_SIZE, 0, 0, h % GROUP_SIZE)),
                   pl.BlockSpec((1, 3, TK, TQ), lambda h: (h // GROUP_SIZE, 0, 0, h % GROUP_SIZE))),
        compiler_params=pltpu.CompilerParams(dimension_semantics=("arbitrary",),
                                             vmem_limit_bytes=VMEM_LIMIT),
    )(rel_bias)


def _row_rms(x, w):
    return x * lax.rsqrt(jnp.mean(x * x, axis=-1, keepdims=True) + EPS) * w


def _nsa_kernel(q_ref, k0_ref, v0_ref, k1_ref, v1_ref, k2_ref, v2_ref, gate_ref,
                qnw_ref, knw_ref, pek_ref, pev_ref, wck_ref, wcv_ref, biasc_ref, dm_ref, c2st_ref,
                o_ref,
                kc_s, vct_s, kaug_s, kw_s, vst_s, vwt_s, gt_s):
    qt = pl.program_id(2)
    cols_all = GROUP_SIZE * TQ

    @pl.when(qt == 0)
    def _prologue():
        kaug_s[:, :HEAD_DIM] = _row_rms(k1_ref[...], knw_ref[1:2, :]).astype(BF16)
        jblk = lax.broadcasted_iota(jnp.int32, (SEQ, LANES), 0) // SEL_BLOCK
        lane = lax.broadcasted_iota(jnp.int32, (SEQ, LANES), 1)
        kaug_s[:, HEAD_DIM:] = jnp.where(jblk == lane, 1.0, 0.0).astype(BF16)
        kw_s[...] = _row_rms(k2_ref[...], knw_ref[2:3, :]).astype(BF16)
        for t in range(SEQ // TK):
            vst_s[t] = v1_ref[t * TK:(t + 1) * TK, :].T.astype(BF16)
            vwt_s[t] = v2_ref[t * TK:(t + 1) * TK, :].T.astype(BF16)

        def compress(x_ref, pe_ref, w_ref):
            acc_lo = jnp.zeros((LANES, HEAD_DIM), F32)
            acc_hi = jnp.zeros((LANES, HEAD_DIM), F32)
            for j in range(CMP_STRIDE):
                y = x_ref[pl.ds(j, LANES, stride=CMP_STRIDE), :]
                acc_lo += _dot((y + pe_ref[j:j + 1, :]).astype(BF16),
                               w_ref[j * HEAD_DIM:(j + 1) * HEAD_DIM, :])
                jh = CMP_STRIDE + j
                acc_hi += _dot((y + pe_ref[jh:jh + 1, :]).astype(BF16),
                               w_ref[jh * HEAD_DIM:(jh + 1) * HEAD_DIM, :])
            return acc_lo + pltpu.roll(acc_hi, LANES - 1, 0)

        kc_s[...] = _row_rms(compress(k0_ref, pek_ref, wck_ref), knw_ref[0:1, :]).astype(BF16)
        vct_s[...] = compress(v0_ref, pev_ref, wcv_ref).T.astype(BF16)

    t0 = qt * TQ
    q = q_ref[...]
    qts = []
    for r in range(GROUP_SIZE):
        qts.append(_row_rms(q[:, r * HEAD_DIM:(r + 1) * HEAD_DIM], qnw_ref[...]).T)
    qt_stack = jnp.concatenate(qts, axis=1).astype(BF16)

    sc = _dot(kc_s[...], qt_stack) * SCALE + biasc_ref[0, 0]
    tpos = t0 + (lax.broadcasted_iota(jnp.int32, (LANES, cols_all), 1) & (TQ - 1))
    nrow = lax.broadcasted_iota(jnp.int32, (LANES, cols_all), 0)
    maskc = tpos >= CMP_STRIDE * nrow + (CMP_BLOCK - 1)
    scm = jnp.where(maskc, sc, NEG_INF)
    mc = jnp.max(scm, axis=0, keepdims=True)
    ec = jnp.where(maskc, jnp.exp(scm - mc), 0.0)
    lc = jnp.sum(ec, axis=0, keepdims=True)
    pc = ec * (1.0 / jnp.where(lc > 0.0, lc, 1.0))
    o_c = _dot(vct_s[...], pc.astype(BF16))

    psum = pc[:, 0:TQ] + pc[:, TQ:2 * TQ] + pc[:, 2 * TQ:3 * TQ] + pc[:, 3 * TQ:4 * TQ]
    p_hi = psum.astype(BF16)
    p_lo = (psum - p_hi.astype(F32)).astype(BF16)
    imp = (_dot(c2st_ref[...], p_hi) + _dot(c2st_ref[...], p_lo))[0:N_SEL]
    tq_pos = t0 + lax.broadcasted_iota(jnp.int32, (N_SEL, TQ), 1)
    sidx = lax.broadcasted_iota(jnp.int32, (N_SEL, TQ), 0)
    cur = tq_pos >> 6
    score = jnp.where(sidx <= cur, imp, NEG_INF)
    for forced_blk in (cur - 1, cur, jnp.zeros_like(cur)):
        score = jnp.where(sidx == forced_blk, FORCE_SCORE, score)
    rank = jnp.zeros((N_SEL, TQ), F32)
    for sp in range(N_SEL):
        row = score[sp:sp + 1, :]
        tie = jnp.where(sidx > sp, 1.0, 0.0)
        rank = rank + jnp.where(row > score, 1.0, jnp.where(row == score, tie, 0.0))
    selb = jnp.where(rank < TOP_N, 0.0, NEG_INF)
    selb = jnp.concatenate([selb, jnp.zeros((HEAD_DIM - N_SEL, TQ), F32)], axis=0).astype(BF16)
    qt_aug = jnp.concatenate([qt_stack, jnp.concatenate([selb] * GROUP_SIZE, axis=1)], axis=0)

    m0 = jnp.full((1, cols_all), NEG_INF, F32)
    l0 = jnp.zeros((1, cols_all), F32)
    a0 = jnp.zeros((HEAD_DIM, cols_all), F32)
    exp2_scale = SCALE * math.log2(math.e)

    def online(carry, u, vt):
        m, l, acc = carry
        m_new = jnp.maximum(m, jnp.max(u, axis=0, keepdims=True))
        alpha = jnp.exp2((m - m_new) * exp2_scale)
        p = jnp.exp2((u - m_new) * exp2_scale)
        l = alpha * l + jnp.sum(p, axis=0, keepdims=True)
        acc = alpha * acc + _dot(vt, p.astype(BF16))
        return m_new, l, acc

    def key_rows(kt):
        return pl.ds(pl.multiple_of(kt * TK, TK), TK)

    def sel_far(kt, carry):
        return online(carry, _dot(kaug_s[key_rows(kt), :], qt_aug), vst_s[kt])

    carry = lax.fori_loop(0, jnp.maximum(qt - 1, 0), sel_far, (m0, l0, a0))
    kt1 = jnp.maximum(qt - 1, 0)
    pen1 = jnp.where(qt >= 1, 0.0, NEG_INF)
    carry = online(carry, _dot(kaug_s[key_rows(kt1), :], qt_aug) + dm_ref[0, 1] + pen1, vst_s[kt1])
    carry = online(carry, _dot(kaug_s[key_rows(qt), :], qt_aug) + dm_ref[0, 0], vst_s[qt])
    _, l_s, acc_s = carry
    o_s = acc_s * (1.0 / l_s)

    carry = (m0, l0, a0)
    for dist, kind in ((2, 2), (1, 1), (0, 0)):
        kt = jnp.maximum(qt - dist, 0)
        u = _dot(kw_s[key_rows(kt), :], qt_stack) + dm_ref[0, kind]
        if dist > 0:
            u = u + jnp.where(qt >= dist, 0.0, NEG_INF)
        carry = online(carry, u, vwt_s[kt])
    _, l_w, acc_w = carry
    o_w = acc_w * (1.0 / l_w)

    gt_s[...] = jax.nn.sigmoid(gate_ref[...]).T
    head0 = pl.program_id(1) * GROUP_SIZE
    for r in range(GROUP_SIZE):
        cs = slice(r * TQ, (r + 1) * TQ)
        c = (head0 + r) * N_BRANCH
        o = (gt_s[pl.ds(c, 1), :] * o_c[:, cs] + gt_s[pl.ds(c + 1, 1), :] * o_s[:, cs]
             + gt_s[pl.ds(c + 2, 1), :] * o_w[:, cs])
        o_ref[:, r * HEAD_DIM:(r + 1) * HEAD_DIM] = o.T.astype(o_ref.dtype)


def _cmp_to_sel_t():
    ss = np.arange(LANES)[:, None] * SEL_BLOCK
    cs = np.arange(LANES)[None, :] * CMP_STRIDE
    ov = np.clip(np.minimum(cs + CMP_BLOCK, ss + SEL_BLOCK) - np.maximum(cs, ss), 0, None)
    m = ov.astype(np.float32) / np.float32(CMP_BLOCK)
    m[N_SEL:, :] = 0.0
    m[:, N_CMP:] = 0.0
    return jnp.asarray(m, BF16)


def _nsa_attention(proj, gate_logits, batch, q_norm_w, k_norm_w, pe_k, pe_v, w_cmp_k, w_cmp_v, biasc, dm):
    kvb = COL_KV // LANES

    def kv_spec(branch, which):
        base = kvb + (branch * 2 + which) * N_KV_GROUPS
        return pl.BlockSpec((SEQ, LANES), lambda b, g, qt: (b, base + g))

    full = lambda shape: pl.BlockSpec(shape, lambda b, g, qt: (0,) * len(shape))
    return pl.pallas_call(
        _nsa_kernel,
        out_shape=jax.ShapeDtypeStruct((batch * SEQ, NSA_WIDTH), BF16),
        grid=(batch, N_KV_GROUPS, NQ),
        in_specs=[pl.BlockSpec((TQ, GROUP_SIZE * HEAD_DIM), lambda b, g, qt: (b * NQ + qt, g)),
                  kv_spec(0, 0), kv_spec(0, 1), kv_spec(1, 0), kv_spec(1, 1),
                  kv_spec(2, 0), kv_spec(2, 1),
                  pl.BlockSpec((TQ, LANES), lambda b, g, qt: (b * NQ + qt, 0)),
                  full((1, HEAD_DIM)), full((N_BRANCH, HEAD_DIM)),
                  full((CMP_BLOCK, HEAD_DIM)), full((CMP_BLOCK, HEAD_DIM)),
                  full((CMP_BLOCK * HEAD_DIM, HEAD_DIM)), full((CMP_BLOCK * HEAD_DIM, HEAD_DIM)),
                  pl.BlockSpec((1, 1, LANES, GROUP_SIZE * TQ), lambda b, g, qt: (g, qt, 0, 0)),
                  pl.BlockSpec((1, 3, TK, GROUP_SIZE * TQ), lambda b, g, qt: (g, 0, 0, 0)),
                  full((LANES, LANES))],
        out_specs=pl.BlockSpec((TQ, GROUP_SIZE * HEAD_DIM), lambda b, g, qt: (b * NQ + qt, g)),
        scratch_shapes=[pltpu.VMEM((LANES, HEAD_DIM), BF16),
                        pltpu.VMEM((HEAD_DIM, LANES), BF16),
                        pltpu.VMEM((SEQ, 2 * HEAD_DIM), BF16),
                        pltpu.VMEM((SEQ, HEAD_DIM), BF16),
                        pltpu.VMEM((SEQ // TK, HEAD_DIM, TK), BF16),
                        pltpu.VMEM((SEQ // TK, HEAD_DIM, TK), BF16),
                        pltpu.VMEM((LANES, TQ), F32)],
        compiler_params=pltpu.CompilerParams(
            dimension_semantics=("arbitrary", "arbitrary", "arbitrary"),
            vmem_limit_bytes=VMEM_LIMIT),
    )(proj, proj, proj, proj, proj, proj, proj, gate_logits,
      q_norm_w.reshape(1, HEAD_DIM), k_norm_w, pe_k, pe_v,
      w_cmp_k.astype(BF16), w_cmp_v.astype(BF16), biasc, dm, _cmp_to_sel_t())


def _ret_kernel(lg_ref, q_ref, k_ref, v_ref, g_ref, cos_ref, sin_ref, gnw_ref, o_ref, state_ref):
    h = pl.program_id(1)
    lg = lg_ref[h]
    c_len = RET_CHUNK
    ii = lax.broadcasted_iota(jnp.int32, (c_len, c_len), 0)
    jj = lax.broadcasted_iota(jnp.int32, (c_len, c_len), 1)
    diff = (ii - jj).astype(F32)
    decay_in = jnp.where(diff >= 0.0, jnp.exp(jnp.maximum(diff, 0.0) * lg), 0.0)
    icol = lax.broadcasted_iota(jnp.int32, (c_len, 1), 0).astype(F32)
    xi = jnp.exp((icol + 1.0) * lg)
    zeta = jnp.exp((c_len - 1.0 - icol) * lg)
    chunk_decay = jnp.exp(jnp.full((1, RET_VAL_DIM), float(c_len), F32) * lg)
    half = RET_KEY_DIM // 2
    state_ref[...] = jnp.zeros_like(state_ref)

    def body(c, _):
        rows = pl.ds(pl.multiple_of(c * c_len, c_len), c_len)
        cos = cos_ref[rows, :]
        sin = sin_ref[rows, :]

        def rot(x):
            x1, x2 = x[:, :half], x[:, half:]
            return jnp.concatenate([x1 * cos - x2 * sin, x2 * cos + x1 * sin], axis=-1)

        qr = rot(q_ref[rows, :])
        kr = rot(k_ref[rows, :]) * (RET_KEY_DIM ** -0.5)
        vb = v_ref[rows, :].astype(BF16)
        scores = _dot_nt(qr.astype(BF16), kr.astype(BF16)) * decay_in
        inner = _dot(scores.astype(BF16), vb)
        st = state_ref[...]
        cross = _dot((qr * xi).astype(BF16), st.astype(BF16))
        kv = _dot((kr * zeta).T.astype(BF16), vb)
        state_ref[...] = chunk_decay * st + kv
        y = inner + cross
        y = y * lax.rsqrt(jnp.mean(y * y, axis=-1, keepdims=True) + EPS) * gnw_ref[...]
        g = g_ref[rows, :]
        o_ref[rows, :] = (g * jax.nn.sigmoid(g) * y).astype(o_ref.dtype)
        return 0

    lax.fori_loop(0, SEQ // c_len, body, 0)


def _retention(proj, batch, gn_w):
    pos = np.arange(SEQ, dtype=np.float64)
    half = RET_KEY_DIM // 2
    inv = np.exp(-np.linspace(0.0, 1.0, half) * math.log(ROPE_BASE)).astype(np.float32)
    ang = pos.astype(np.float32)[:, None] * inv[None, :]
    cos = jnp.asarray(np.cos(ang.astype(np.float64)), F32)
    sin = jnp.asarray(np.sin(ang.astype(np.float64)), F32)
    log_gamma = jnp.asarray(np.log(1.0 - 2.0 ** (-5.0 - np.arange(N_RET_HEADS))), F32)
    w = RET_KEY_DIM

    def col_spec(col0):
        return pl.BlockSpec((SEQ, w), lambda b, h: (b, col0 // w + h))

    return pl.pallas_call(
        _ret_kernel,
        out_shape=jax.ShapeDtypeStruct((batch * SEQ, RET_WIDTH), BF16),
        grid=(batch, N_RET_HEADS),
        in_specs=[pl.BlockSpec(memory_space=pltpu.SMEM),
                  col_spec(COL_QR), col_spec(COL_KR), col_spec(COL_VR), col_spec(COL_GR),
                  pl.BlockSpec((SEQ, half), lambda b, h: (0, 0)),
                  pl.BlockSpec((SEQ, half), lambda b, h: (0, 0)),
                  pl.BlockSpec((1, RET_VAL_DIM), lambda b, h: (0, h))],
        out_specs=pl.BlockSpec((SEQ, RET_VAL_DIM), lambda b, h: (b, h)),
        scratch_shapes=[pltpu.VMEM((RET_KEY_DIM, RET_VAL_DIM), F32)],
        compiler_params=pltpu.CompilerParams(dimension_semantics=("arbitrary", "arbitrary"),
                                             vmem_limit_bytes=VMEM_LIMIT),
    )(log_gamma, proj, proj, proj, proj, cos, sin, gn_w.reshape(1, RET_WIDTH))


def _layer(x, norm1_w, w_in, nsa_q_norm_w, nsa_k_norm_w, cmp_pe_k, cmp_pe_v, w_cmp_k, w_cmp_v,
           rel_bias, ret_gn_w, w_out, norm2_w, w_gate, w_up, w_down):
    batch = x.shape[0]
    xf = x.reshape(batch * SEQ, D_MODEL)
    h = _rmsnorm(xf, norm1_w)
    gate0 = W_IN_NSA_COLS
    gate1 = gate0 + W_IN_GATE_COLS
    w_ret = w_in[:, gate1:].astype(BF16)
    w_gates = jnp.pad(w_in[:, gate0:gate1], ((0, 0), (0, LANES - W_IN_GATE_COLS))).astype(BF16)
    proj_ret, w_gate_b, w_up_b, w_out_b, w_nsa = _matmul(
        h, w_ret, F32, tm=1024, tn=512,
        casts=((w_gate, D_FF), (w_up, D_FF), (w_out, D_MODEL), (w_in, W_IN_NSA_COLS)))
    proj_nsa, = _matmul(h, w_nsa, F32, tm=1024, tn=512)
    gate_logits, = _matmul(h, w_gates, F32, tm=1024, tn=LANES)
    biasc, dm = _bias_tables(rel_bias)
    o_nsa = _nsa_attention(proj_nsa, gate_logits, batch, nsa_q_norm_w, nsa_k_norm_w, cmp_pe_k, cmp_pe_v,
                           w_cmp_k, w_cmp_v, biasc, dm)
    o_ret = _retention(proj_ret, batch, ret_gn_w)
    x1 = _out_projection(o_nsa, o_ret, w_out_b, xf)
    hf = _rmsnorm(x1, norm2_w)
    mid, w_down_b = _ffn_up(hf, w_gate_b, w_up_b, casts=((w_down, D_MODEL),))
    out = _ffn_down(mid, w_down_b, x1)
    return out.reshape(batch, SEQ, D_MODEL)


def kernel(x, norm1_w, w_in, nsa_q_norm_w, nsa_k_norm_w, cmp_pe_k, cmp_pe_v, w_cmp_k, w_cmp_v,
           rel_bias, ret_gn_w, w_out, norm2_w, w_gate, w_up, w_down):
    for l in range(norm1_w.shape[0]):
        x = _layer(x, norm1_w[l], w_in[l], nsa_q_norm_w[l], nsa_k_norm_w[l], cmp_pe_k[l], cmp_pe_v[l],
                   w_cmp_k[l], w_cmp_v[l], rel_bias, ret_gn_w[l], w_out[l], norm2_w[l],
                   w_gate[l], w_up[l], w_down[l])
    return x
```

```python
import functools
import math

import numpy as np
import jax
import jax.numpy as jnp
from jax import lax
from jax.experimental import pallas as pl
from jax.experimental.pallas import tpu as pltpu

F32 = jnp.float32
BF16 = jnp.bfloat16

D_MODEL = 4096
SEQ = 2048
HEAD_DIM = 128
N_NSA_HEADS = 16
N_KV_GROUPS = 4
GROUP_SIZE = 4
N_BRANCH = 3
CMP_BLOCK = 32
CMP_STRIDE = 16
SEL_BLOCK = 64
TOP_N = 16
WINDOW = 512
RET_KEY_DIM = 256
RET_VAL_DIM = 256
N_RET_HEADS = 8
RET_CHUNK = 128
ROPE_BASE = 10000.0
NSA_WIDTH = 2048
RET_WIDTH = 2048
D_FF = 11008
NUM_BUCKETS = 32
MAX_DISTANCE = 128
EPS = 1e-6
NEG_INF = -1e30
FORCE_SCORE = 1e9
N_SEL = SEQ // SEL_BLOCK
N_CMP = (SEQ - CMP_BLOCK) // CMP_STRIDE + 1
SCALE = HEAD_DIM ** -0.5

LANES = 128
VMEM_LIMIT = 56 * 1024 * 1024

TQ = 256
TK = 256
NQ = SEQ // TQ

W_IN_NSA_COLS = NSA_WIDTH + N_BRANCH * 2 * N_KV_GROUPS * HEAD_DIM
W_IN_GATE_COLS = N_NSA_HEADS * N_BRANCH
COL_KV = NSA_WIDTH
COL_QR = 0
COL_KR = 2048
COL_VR = 4096
COL_GR = 6144


def _bucket_thresholds():
    rel = np.arange(0, 4 * MAX_DISTANCE)
    max_exact = NUM_BUCKETS // 2
    nf = np.maximum(rel, 1).astype(np.float64)
    large = max_exact + np.floor(np.log(nf / max_exact) / math.log(MAX_DISTANCE / max_exact)
                                 * (NUM_BUCKETS - max_exact)).astype(np.int64)
    large = np.minimum(large, NUM_BUCKETS - 1)
    b = np.where(rel < max_exact, rel, large)
    return [int(np.argmax(b >= k)) for k in range(NUM_BUCKETS)]


_THR = _bucket_thresholds()


def _dot(a, b):
    return jnp.dot(a, b, preferred_element_type=F32)


def _dot_nt(a, b):
    return lax.dot_general(a, b, (((1,), (1,)), ((), ())), preferred_element_type=F32)


def _rms_kernel(x_ref, w_ref, o_ref):
    x = x_ref[...]
    y = x * lax.rsqrt(jnp.mean(x * x, axis=-1, keepdims=True) + EPS)
    o_ref[...] = (y * w_ref[...]).astype(o_ref.dtype)


def _rmsnorm(x, w, tm=256):
    m, d = x.shape
    return pl.pallas_call(
        _rms_kernel,
        out_shape=jax.ShapeDtypeStruct((m, d), BF16),
        grid=(m // tm,),
        in_specs=[pl.BlockSpec((tm, d), lambda i: (i, 0)),
                  pl.BlockSpec((1, d), lambda i: (0, 0))],
        out_specs=pl.BlockSpec((tm, d), lambda i: (i, 0)),
        compiler_params=pltpu.CompilerParams(dimension_semantics=("arbitrary",),
                                             vmem_limit_bytes=VMEM_LIMIT),
    )(x, w.reshape(1, d))


def _cast_specs(casts, n_steps, nj):
    in_specs, out_specs, out_shapes = [], [], []
    for arr, ncols in casts:
        slab = arr.shape[0] // n_steps
        assert slab * n_steps == arr.shape[0] and slab % 16 == 0 and ncols % LANES == 0
        spec = pl.BlockSpec((slab, ncols), lambda i, j: (i * nj + j, 0))
        in_specs.append(spec)
        out_specs.append(spec)
        out_shapes.append(jax.ShapeDtypeStruct((arr.shape[0], ncols), BF16))
    return in_specs, out_specs, out_shapes


def _cast_slabs(refs, n_casts):
    for src, dst in zip(refs[:n_casts], refs[len(refs) - n_casts:]):
        dst[...] = src[...].astype(dst.dtype)


def _mm_kernel(a_ref, b_ref, *rest, n_casts):
    o_ref = rest[n_casts]
    o_ref[...] = _dot(a_ref[...], b_ref[...]).astype(o_ref.dtype)
    _cast_slabs(rest[:n_casts] + rest[n_casts + 1:], n_casts)


def _matmul(a, b, out_dtype, tm, tn, casts=()):
    m, k = a.shape
    _, n = b.shape
    grid = (m // tm, n // tn)
    c_in, c_out, c_shapes = _cast_specs(casts, grid[0] * grid[1], grid[1])
    outs = pl.pallas_call(
        functools.partial(_mm_kernel, n_casts=len(casts)),
        out_shape=[jax.ShapeDtypeStruct((m, n), out_dtype)] + c_shapes,
        grid=grid,
        in_specs=[pl.BlockSpec((tm, k), lambda i, j: (i, 0)),
                  pl.BlockSpec((k, tn), lambda i, j: (0, j))] + c_in,
        out_specs=[pl.BlockSpec((tm, tn), lambda i, j: (i, j))] + c_out,
        compiler_params=pltpu.CompilerParams(dimension_semantics=("arbitrary", "arbitrary"),
                                             vmem_limit_bytes=VMEM_LIMIT),
    )(a, b, *[arr for arr, _ in casts])
    return outs


def _inproj_kernel(a_ref, wt_ref, *rest, n_casts):
    o_ref = rest[n_casts]
    wbf_ref = rest[-1]

    @pl.when(pl.program_id(1) == 0)
    def _():
        wbf_ref[...] = wt_ref[...].astype(BF16)

    o_ref[...] = _dot_nt(a_ref[...], wbf_ref[...])
    _cast_slabs(rest[:n_casts] + rest[n_casts + 1:-1], n_casts)


def _in_projection(a, w_t, row0, n_cols, tn, tm=1024, casts=()):
    m, k = a.shape
    grid = (n_cols // tn, m // tm)
    c_in, c_out, c_shapes = _cast_specs(casts, grid[0] * grid[1], grid[1])
    sub = 8
    assert row0 % sub == 0 and tn % sub == 0
    return pl.pallas_call(
        functools.partial(_inproj_kernel, n_casts=len(casts)),
        out_shape=[jax.ShapeDtypeStruct((m, n_cols), F32)] + c_shapes,
        grid=grid,
        in_specs=[pl.BlockSpec((tm, k), lambda j, i: (i, 0)),
                  pl.BlockSpec((pl.Element(tn), pl.Element(k)),
                               lambda j, i: ((row0 // sub + j * (tn // sub)) * sub, 0))] + c_in,
        out_specs=[pl.BlockSpec((tm, tn), lambda j, i: (i, j))] + c_out,
        scratch_shapes=[pltpu.VMEM((tn, k), BF16)],
        compiler_params=pltpu.CompilerParams(dimension_semantics=("arbitrary", "arbitrary"),
                                             vmem_limit_bytes=VMEM_LIMIT),
    )(a, w_t, *[arr for arr, _ in casts])


def _outproj_kernel(a1_ref, a2_ref, w1_ref, w2_ref, x_ref, o_ref):
    acc = _dot(a1_ref[...], w1_ref[...]) + _dot(a2_ref[...], w2_ref[...])
    o_ref[...] = x_ref[...] + acc


def _out_projection(o_nsa, o_ret, w_out, x, tm=1024, tn=512):
    m = x.shape[0]
    kh = NSA_WIDTH
    return pl.pallas_call(
        _outproj_kernel,
        out_shape=jax.ShapeDtypeStruct((m, D_MODEL), F32),
        grid=(m // tm, D_MODEL // tn),
        in_specs=[pl.BlockSpec((tm, kh), lambda i, j: (i, 0)),
                  pl.BlockSpec((tm, kh), lambda i, j: (i, 0)),
                  pl.BlockSpec((kh, tn), lambda i, j: (0, j)),
                  pl.BlockSpec((kh, tn), lambda i, j: (1, j)),
                  pl.BlockSpec((tm, tn), lambda i, j: (i, j))],
        out_specs=pl.BlockSpec((tm, tn), lambda i, j: (i, j)),
        compiler_params=pltpu.CompilerParams(dimension_semantics=("arbitrary", "arbitrary"),
                                             vmem_limit_bytes=VMEM_LIMIT),
    )(o_nsa, o_ret, w_out, w_out, x)


def _ffn_up_kernel(h_ref, wg_ref, wu_ref, *rest, n_casts):
    o_ref = rest[n_casts]
    h = h_ref[...]
    g = _dot(h, wg_ref[...])
    u = _dot(h, wu_ref[...])
    o_ref[...] = (g * jax.nn.sigmoid(g) * u).astype(o_ref.dtype)
    _cast_slabs(rest[:n_casts] + rest[n_casts + 1:], n_casts)


def _ffn_up(h, w_gate, w_up, tm=1024, tn=256, casts=()):
    m, k = h.shape
    n = w_gate.shape[1]
    grid = (m // tm, n // tn)
    c_in, c_out, c_shapes = _cast_specs(casts, grid[0] * grid[1], grid[1])
    return pl.pallas_call(
        functools.partial(_ffn_up_kernel, n_casts=len(casts)),
        out_shape=[jax.ShapeDtypeStruct((m, n), BF16)] + c_shapes,
        grid=grid,
        in_specs=[pl.BlockSpec((tm, k), lambda i, j: (i, 0)),
                  pl.BlockSpec((k, tn), lambda i, j: (0, j)),
                  pl.BlockSpec((k, tn), lambda i, j: (0, j))] + c_in,
        out_specs=[pl.BlockSpec((tm, tn), lambda i, j: (i, j))] + c_out,
        compiler_params=pltpu.CompilerParams(dimension_semantics=("arbitrary", "arbitrary"),
                                             vmem_limit_bytes=VMEM_LIMIT),
    )(h, w_gate, w_up, *[arr for arr, _ in casts])


def _ffn_down_kernel(a_ref, b_ref, x_ref, o_ref, acc_ref):
    kk = pl.program_id(2)

    @pl.when(kk == 0)
    def _():
        acc_ref[...] = x_ref[...]

    acc_ref[...] += _dot(a_ref[...], b_ref[...])

    @pl.when(kk == pl.num_programs(2) - 1)
    def _():
        o_ref[...] = acc_ref[...]


def _ffn_down(a, w_down, x, tm=1024, tn=512, tk=5504):
    m, k = a.shape
    n = w_down.shape[1]
    return pl.pallas_call(
        _ffn_down_kernel,
        out_shape=jax.ShapeDtypeStruct((m, n), F32),
        grid=(m // tm, n // tn, k // tk),
        in_specs=[pl.BlockSpec((tm, tk), lambda i, j, kk: (i, kk)),
                  pl.BlockSpec((tk, tn), lambda i, j, kk: (kk, j)),
                  pl.BlockSpec((tm, tn), lambda i, j, kk: (i, j))],
        out_specs=pl.BlockSpec((tm, tn), lambda i, j, kk: (i, j)),
        scratch_shapes=[pltpu.VMEM((tm, tn), F32)],
        compiler_params=pltpu.CompilerParams(
            dimension_semantics=("arbitrary", "arbitrary", "arbitrary"),
            vmem_limit_bytes=VMEM_LIMIT),
    )(a, w_down, x)


def _bias_value(rel, relb_ref, h):
    val = jnp.full(rel.shape, relb_ref[0, h], F32)
    for k in range(1, NUM_BUCKETS):
        val = jnp.where(rel >= _THR[k], relb_ref[k, h], val)
    return val


def _bias_tables_kernel(relb_ref, biasc_ref, dm_ref):
    h = pl.program_id(0)
    n = lax.broadcasted_iota(jnp.int32, (LANES, TQ), 0)
    a_c = lax.broadcasted_iota(jnp.int32, (LANES, TQ), 1)
    for qt in range(NQ):
        rel_c = (qt * TQ + a_c) - (CMP_STRIDE * n + CMP_BLOCK - 1)
        biasc_ref[0, qt] = _bias_value(rel_c, relb_ref, h)
    b = lax.broadcasted_iota(jnp.int32, (TK, TQ), 0)
    a = lax.broadcasted_iota(jnp.int32, (TK, TQ), 1)
    far = relb_ref[NUM_BUCKETS - 1, h]
    inv_scale = 1.0 / SCALE
    dm_ref[0, 0] = ((_bias_value(a - b, relb_ref, h) - far) * inv_scale
                    + jnp.where(a >= b, 0.0, NEG_INF))
    dm_ref[0, 1] = (_bias_value(a - b + TK, relb_ref, h) - far) * inv_scale
    dm_ref[0, 2] = jnp.where(b > a, 0.0, NEG_INF)


def _bias_tables(rel_bias):
    return pl.pallas_call(
        _bias_tables_kernel,
        out_shape=(jax.ShapeDtypeStruct((N_KV_GROUPS, NQ, LANES, GROUP_SIZE * TQ), F32),
                   jax.ShapeDtypeStruct((N_KV_GROUPS, 3, TK, GROUP_SIZE * TQ), F32)),
        grid=(N_NSA_HEADS,),
        in_specs=[pl.BlockSpec(memory_space=pltpu.SMEM)],
        out_specs=(pl.BlockSpec((1, NQ, LANES, TQ), lambda h: (h // GROUP_SIZE, 0, 0, h % GROUP_SIZE)),
                   pl.BlockSpec((1, 3, TK, TQ), lambda h: (h // GROUP_SIZE, 0, 0, h % GROUP_SIZE))),
        compiler_params=pltpu.CompilerParams(dimension_semantics=("arbitrary",),
                                             vmem_limit_bytes=VMEM_LIMIT),
    )(rel_bias)


def _row_rms(x, w):
    return x * lax.rsqrt(jnp.mean(x * x, axis=-1, keepdims=True) + EPS) * w


def _nsa_kernel(q_ref, k0_ref, v0_ref, k1_ref, v1_ref, k2_ref, v2_ref, gate_ref,
                qnw_ref, knw_ref, pek_ref, pev_ref, wck_ref, wcv_ref, biasc_ref, dm_ref, c2st_ref,
                o_ref,
                kc_s, vct_s, kaug_s, kw_s, vst_s, vwt_s, gt_s):
    qt = pl.program_id(2)
    cols_all = GROUP_SIZE * TQ

    @pl.when(qt == 0)
    def _prologue():
        kaug_s[:, :HEAD_DIM] = _row_rms(k1_ref[...], knw_ref[1:2, :]).astype(BF16)
        jblk = lax.broadcasted_iota(jnp.int32, (SEQ, LANES), 0) // SEL_BLOCK
        lane = lax.broadcasted_iota(jnp.int32, (SEQ, LANES), 1)
        kaug_s[:, HEAD_DIM:] = jnp.where(jblk == lane, 1.0, 0.0).astype(BF16)
        kw_s[...] = _row_rms(k2_ref[...], knw_ref[2:3, :]).astype(BF16)
        for t in range(SEQ // TK):
            vst_s[t] = v1_ref[t * TK:(t + 1) * TK, :].T.astype(BF16)
            vwt_s[t] = v2_ref[t * TK:(t + 1) * TK, :].T.astype(BF16)

        def compress(x_ref, pe_ref, w_ref):
            acc_lo = jnp.zeros((LANES, HEAD_DIM), F32)
            acc_hi = jnp.zeros((LANES, HEAD_DIM), F32)
            for j in range(CMP_STRIDE):
                y = x_ref[pl.ds(j, LANES, stride=CMP_STRIDE), :]
                acc_lo += _dot((y + pe_ref[j:j + 1, :]).astype(BF16),
                               w_ref[j * HEAD_DIM:(j + 1) * HEAD_DIM, :])
                jh = CMP_STRIDE + j
                acc_hi += _dot((y + pe_ref[jh:jh + 1, :]).astype(BF16),
                               w_ref[jh * HEAD_DIM:(jh + 1) * HEAD_DIM, :])
            return acc_lo + pltpu.roll(acc_hi, LANES - 1, 0)

        kc_s[...] = _row_rms(compress(k0_ref, pek_ref, wck_ref), knw_ref[0:1, :]).astype(BF16)
        vct_s[...] = compress(v0_ref, pev_ref, wcv_ref).T.astype(BF16)

    t0 = qt * TQ
    q = q_ref[...]
    qts = []
    for r in range(GROUP_SIZE):
        qts.append(_row_rms(q[:, r * HEAD_DIM:(r + 1) * HEAD_DIM], qnw_ref[...]).T)
    qt_stack = jnp.concatenate(qts, axis=1).astype(BF16)

    sc = _dot(kc_s[...], qt_stack) * SCALE + biasc_ref[0, 0]
    tpos = t0 + (lax.broadcasted_iota(jnp.int32, (LANES, cols_all), 1) & (TQ - 1))
    nrow = lax.broadcasted_iota(jnp.int32, (LANES, cols_all), 0)
    maskc = tpos >= CMP_STRIDE * nrow + (CMP_BLOCK - 1)
    scm = jnp.where(maskc, sc, NEG_INF)
    mc = jnp.max(scm, axis=0, keepdims=True)
    ec = jnp.where(maskc, jnp.exp(scm - mc), 0.0)
    lc = jnp.sum(ec, axis=0, keepdims=True)
    pc = ec * (1.0 / jnp.where(lc > 0.0, lc, 1.0))
    o_c = _dot(vct_s[...], pc.astype(BF16))

    psum = pc[:, 0:TQ] + pc[:, TQ:2 * TQ] + pc[:, 2 * TQ:3 * TQ] + pc[:, 3 * TQ:4 * TQ]
    p_hi = psum.astype(BF16)
    p_lo = (psum - p_hi.astype(F32)).astype(BF16)
    imp = (_dot(c2st_ref[...], p_hi) + _dot(c2st_ref[...], p_lo))[0:N_SEL]
    tq_pos = t0 + lax.broadcasted_iota(jnp.int32, (N_SEL, TQ), 1)
    sidx = lax.broadcasted_iota(jnp.int32, (N_SEL, TQ), 0)
    cur = tq_pos >> 6
    score = jnp.where(sidx <= cur, imp, NEG_INF)
    for forced_blk in (cur - 1, cur, jnp.zeros_like(cur)):
        score = jnp.where(sidx == forced_blk, FORCE_SCORE, score)
    rank = jnp.zeros((N_SEL, TQ), F32)
    for sp in range(N_SEL):
        row = score[sp:sp + 1, :]
        tie = jnp.where(sidx > sp, 1.0, 0.0)
        rank = rank + jnp.where(row > score, 1.0, jnp.where(row == score, tie, 0.0))
    selb = jnp.where(rank < TOP_N, 0.0, NEG_INF)
    selb = jnp.concatenate([selb, jnp.zeros((HEAD_DIM - N_SEL, TQ), F32)], axis=0).astype(BF16)
    qt_aug = jnp.concatenate([qt_stack, jnp.concatenate([selb] * GROUP_SIZE, axis=1)], axis=0)

    m0 = jnp.full((1, cols_all), NEG_INF, F32)
    l0 = jnp.zeros((1, cols_all), F32)
    a0 = jnp.zeros((HEAD_DIM, cols_all), F32)
    exp2_scale = SCALE * math.log2(math.e)

    def online(carry, u, vt):
        m, l, acc = carry
        m_new = jnp.maximum(m, jnp.max(u, axis=0, keepdims=True))
        alpha = jnp.exp2((m - m_new) * exp2_scale)
        p = jnp.exp2((u - m_new) * exp2_scale)
        l = alpha * l + jnp.sum(p, axis=0, keepdims=True)
        acc = alpha * acc + _dot(vt, p.astype(BF16))
        return m_new, l, acc

    def key_rows(kt):
        return pl.ds(pl.multiple_of(kt * TK, TK), TK)

    def sel_far(kt, carry):
        return online(carry, _dot(kaug_s[key_rows(kt), :], qt_aug), vst_s[kt])

    carry = lax.fori_loop(0, jnp.maximum(qt - 1, 0), sel_far, (m0, l0, a0))
    kt1 = jnp.maximum(qt - 1, 0)
    pen1 = jnp.where(qt >= 1, 0.0, NEG_INF)
    carry = online(carry, _dot(kaug_s[key_rows(kt1), :], qt_aug) + dm_ref[0, 1] + pen1, vst_s[kt1])
    carry = online(carry, _dot(kaug_s[key_rows(qt), :], qt_aug) + dm_ref[0, 0], vst_s[qt])
    _, l_s, acc_s = carry
    o_s = acc_s * (1.0 / l_s)

    carry = (m0, l0, a0)
    for dist, kind in ((2, 2), (1, 1), (0, 0)):
        kt = jnp.maximum(qt - dist, 0)
        u = _dot(kw_s[key_rows(kt), :], qt_stack) + dm_ref[0, kind]
        if dist > 0:
            u = u + jnp.where(qt >= dist, 0.0, NEG_INF)
        carry = online(carry, u, vwt_s[kt])
    _, l_w, acc_w = carry
    o_w = acc_w * (1.0 / l_w)

    gt_s[...] = jax.nn.sigmoid(gate_ref[...]).T
    head0 = pl.program_id(1) * GROUP_SIZE
    for r in range(GROUP_SIZE):
        cs = slice(r * TQ, (r + 1) * TQ)
        c = (head0 + r) * N_BRANCH
        o = (gt_s[pl.ds(c, 1), :] * o_c[:, cs] + gt_s[pl.ds(c + 1, 1), :] * o_s[:, cs]
             + gt_s[pl.ds(c + 2, 1), :] * o_w[:, cs])
        o_ref[:, r * HEAD_DIM:(r + 1) * HEAD_DIM] = o.T.astype(o_ref.dtype)


def _cmp_to_sel_t():
    ss = np.arange(LANES)[:, None] * SEL_BLOCK
    cs = np.arange(LANES)[None, :] * CMP_STRIDE
    ov = np.clip(np.minimum(cs + CMP_BLOCK, ss + SEL_BLOCK) - np.maximum(cs, ss), 0, None)
    m = ov.astype(np.float32) / np.float32(CMP_BLOCK)
    m[N_SEL:, :] = 0.0
    m[:, N_CMP:] = 0.0
    return jnp.asarray(m, BF16)


def _nsa_attention(proj, gate_logits, batch, q_norm_w, k_norm_w, pe_k, pe_v, w_cmp_k, w_cmp_v, biasc, dm):
    kvb = COL_KV // LANES

    def kv_spec(branch, which):
        base = kvb + (branch * 2 + which) * N_KV_GROUPS
        return pl.BlockSpec((SEQ, LANES), lambda b, g, qt: (b, base + g))

    full = lambda shape: pl.BlockSpec(shape, lambda b, g, qt: (0,) * len(shape))
    return pl.pallas_call(
        _nsa_kernel,
        out_shape=jax.ShapeDtypeStruct((batch * SEQ, NSA_WIDTH), BF16),
        grid=(batch, N_KV_GROUPS, NQ),
        in_specs=[pl.BlockSpec((TQ, GROUP_SIZE * HEAD_DIM), lambda b, g, qt: (b * NQ + qt, g)),
                  kv_spec(0, 0), kv_spec(0, 1), kv_spec(1, 0), kv_spec(1, 1),
                  kv_spec(2, 0), kv_spec(2, 1),
                  pl.BlockSpec((TQ, LANES), lambda b, g, qt: (b * NQ + qt, 0)),
                  full((1, HEAD_DIM)), full((N_BRANCH, HEAD_DIM)),
                  full((CMP_BLOCK, HEAD_DIM)), full((CMP_BLOCK, HEAD_DIM)),
                  full((CMP_BLOCK * HEAD_DIM, HEAD_DIM)), full((CMP_BLOCK * HEAD_DIM, HEAD_DIM)),
                  pl.BlockSpec((1, 1, LANES, GROUP_SIZE * TQ), lambda b, g, qt: (g, qt, 0, 0)),
                  pl.BlockSpec((1, 3, TK, GROUP_SIZE * TQ), lambda b, g, qt: (g, 0, 0, 0)),
                  full((LANES, LANES))],
        out_specs=pl.BlockSpec((TQ, GROUP_SIZE * HEAD_DIM), lambda b, g, qt: (b * NQ + qt, g)),
        scratch_shapes=[pltpu.VMEM((LANES, HEAD_DIM), BF16),
                        pltpu.VMEM((HEAD_DIM, LANES), BF16),
                        pltpu.VMEM((SEQ, 2 * HEAD_DIM), BF16),
                        pltpu.VMEM((SEQ, HEAD_DIM), BF16),
                        pltpu.VMEM((SEQ // TK, HEAD_DIM, TK), BF16),
                        pltpu.VMEM((SEQ // TK, HEAD_DIM, TK), BF16),
                        pltpu.VMEM((LANES, TQ), F32)],
        compiler_params=pltpu.CompilerParams(
            dimension_semantics=("arbitrary", "arbitrary", "arbitrary"),
            vmem_limit_bytes=VMEM_LIMIT),
    )(proj, proj, proj, proj, proj, proj, proj, gate_logits,
      q_norm_w.reshape(1, HEAD_DIM), k_norm_w, pe_k, pe_v,
      w_cmp_k.astype(BF16), w_cmp_v.astype(BF16), biasc, dm, _cmp_to_sel_t())


def _ret_kernel(lg_ref, q_ref, k_ref, v_ref, g_ref, cos_ref, sin_ref, gnw_ref, o_ref, state_ref):
    h = pl.program_id(1)
    lg = lg_ref[h]
    c_len = RET_CHUNK
    ii = lax.broadcasted_iota(jnp.int32, (c_len, c_len), 0)
    jj = lax.broadcasted_iota(jnp.int32, (c_len, c_len), 1)
    diff = (ii - jj).astype(F32)
    decay_in = jnp.where(diff >= 0.0, jnp.exp(jnp.maximum(diff, 0.0) * lg), 0.0)
    icol = lax.broadcasted_iota(jnp.int32, (c_len, 1), 0).astype(F32)
    xi = jnp.exp((icol + 1.0) * lg)
    zeta = jnp.exp((c_len - 1.0 - icol) * lg)
    chunk_decay = jnp.exp(jnp.full((1, RET_VAL_DIM), float(c_len), F32) * lg)
    half = RET_KEY_DIM // 2
    state_ref[...] = jnp.zeros_like(state_ref)

    def body(c, _):
        rows = pl.ds(pl.multiple_of(c * c_len, c_len), c_len)
        cos = cos_ref[rows, :]
        sin = sin_ref[rows, :]

        def rot(x):
            x1, x2 = x[:, :half], x[:, half:]
            return jnp.concatenate([x1 * cos - x2 * sin, x2 * cos + x1 * sin], axis=-1)

        qr = rot(q_ref[rows, :])
        kr = rot(k_ref[rows, :]) * (RET_KEY_DIM ** -0.5)
        vb = v_ref[rows, :].astype(BF16)
        scores = _dot_nt(qr.astype(BF16), kr.astype(BF16)) * decay_in
        inner = _dot(scores.astype(BF16), vb)
        st = state_ref[...]
        cross = _dot((qr * xi).astype(BF16), st.astype(BF16))
        kv = _dot((kr * zeta).T.astype(BF16), vb)
        state_ref[...] = chunk_decay * st + kv
        y = inner + cross
        y = y * lax.rsqrt(jnp.mean(y * y, axis=-1, keepdims=True) + EPS) * gnw_ref[...]
        g = g_ref[rows, :]
        o_ref[rows, :] = (g * jax.nn.sigmoid(g) * y).astype(o_ref.dtype)
        return 0

    lax.fori_loop(0, SEQ // c_len, body, 0)


def _retention(proj, batch, gn_w):
    pos = np.arange(SEQ, dtype=np.float64)
    half = RET_KEY_DIM // 2
    inv = np.exp(-np.linspace(0.0, 1.0, half) * math.log(ROPE_BASE)).astype(np.float32)
    ang = pos.astype(np.float32)[:, None] * inv[None, :]
    cos = jnp.asarray(np.cos(ang.astype(np.float64)), F32)
    sin = jnp.asarray(np.sin(ang.astype(np.float64)), F32)
    log_gamma = jnp.asarray(np.log(1.0 - 2.0 ** (-5.0 - np.arange(N_RET_HEADS))), F32)
    w = RET_KEY_DIM

    def col_spec(col0):
        return pl.BlockSpec((SEQ, w), lambda b, h: (b, col0 // w + h))

    return pl.pallas_call(
        _ret_kernel,
        out_shape=jax.ShapeDtypeStruct((batch * SEQ, RET_WIDTH), BF16),
        grid=(batch, N_RET_HEADS),
        in_specs=[pl.BlockSpec(memory_space=pltpu.SMEM),
                  col_spec(COL_QR), col_spec(COL_KR), col_spec(COL_VR), col_spec(COL_GR),
                  pl.BlockSpec((SEQ, half), lambda b, h: (0, 0)),
                  pl.BlockSpec((SEQ, half), lambda b, h: (0, 0)),
                  pl.BlockSpec((1, RET_VAL_DIM), lambda b, h: (0, h))],
        out_specs=pl.BlockSpec((SEQ, RET_VAL_DIM), lambda b, h: (b, h)),
        scratch_shapes=[pltpu.VMEM((RET_KEY_DIM, RET_VAL_DIM), F32)],
        compiler_params=pltpu.CompilerParams(dimension_semantics=("arbitrary", "arbitrary"),
                                             vmem_limit_bytes=VMEM_LIMIT),
    )(log_gamma, proj, proj, proj, proj, cos, sin, gn_w.reshape(1, RET_WIDTH))


def _layer(x, norm1_w, w_in, nsa_q_norm_w, nsa_k_norm_w, cmp_pe_k, cmp_pe_v, w_cmp_k, w_cmp_v,
           rel_bias, ret_gn_w, w_out, norm2_w, w_gate, w_up, w_down):
    batch = x.shape[0]
    xf = x.reshape(batch * SEQ, D_MODEL)
    h = _rmsnorm(xf, norm1_w)
    w_in_t = w_in.T
    gate0 = W_IN_NSA_COLS
    gate1 = gate0 + W_IN_GATE_COLS
    proj_ret, w_gate_b, w_up_b, w_out_b = _in_projection(
        h, w_in_t, gate1, 4 * RET_WIDTH, 512,
        casts=((w_gate, D_FF), (w_up, D_FF), (w_out, D_MODEL)))
    proj_nsa, = _in_projection(h, w_in_t, 0, W_IN_NSA_COLS, 512)
    gate_logits, = _in_projection(h, w_in_t, gate0, LANES, LANES)
    biasc, dm = _bias_tables(rel_bias)
    o_nsa = _nsa_attention(proj_nsa, gate_logits, batch, nsa_q_norm_w, nsa_k_norm_w, cmp_pe_k, cmp_pe_v,
                           w_cmp_k, w_cmp_v, biasc, dm)
    o_ret = _retention(proj_ret, batch, ret_gn_w)
    x1 = _out_projection(o_nsa, o_ret, w_out_b, xf)
    hf = _rmsnorm(x1, norm2_w)
    mid, w_down_b = _ffn_up(hf, w_gate_b, w_up_b, casts=((w_down, D_MODEL),))
    out = _ffn_down(mid, w_down_b, x1)
    return out.reshape(batch, SEQ, D_MODEL)


def kernel(x, norm1_w, w_in, nsa_q_norm_w, nsa_k_norm_w, cmp_pe_k, cmp_pe_v, w_cmp_k, w_cmp_v,
           rel_bias, ret_gn_w, w_out, norm2_w, w_gate, w_up, w_down):
    for l in range(norm1_w.shape[0]):
        x = _layer(x, norm1_w[l], w_in[l], nsa_q_norm_w[l], nsa_k_norm_w[l], cmp_pe_k[l], cmp_pe_v[l],
                   w_cmp_k[l], w_cmp_v[l], rel_bias, ret_gn_w[l], w_out[l], norm2_w[l],
                   w_gate[l], w_up[l], w_down[l])
    return x
```

```python
import functools
import math

import numpy as np
import jax
import jax.numpy as jnp
from jax import lax
from jax.experimental import pallas as pl
from jax.experimental.pallas import tpu as pltpu

F32 = jnp.float32
BF16 = jnp.bfloat16

D_MODEL = 4096
SEQ = 2048
HEAD_DIM = 128
N_NSA_HEADS = 16
N_KV_GROUPS = 4
GROUP_SIZE = 4
N_BRANCH = 3
CMP_BLOCK = 32
CMP_STRIDE = 16
SEL_BLOCK = 64
TOP_N = 16
WINDOW = 512
RET_KEY_DIM = 256
RET_VAL_DIM = 256
N_RET_HEADS = 8
RET_CHUNK = 128
ROPE_BASE = 10000.0
NSA_WIDTH = 2048
RET_WIDTH = 2048
D_FF = 11008
NUM_BUCKETS = 32
MAX_DISTANCE = 128
EPS = 1e-6
NEG_INF = -1e30
FORCE_SCORE = 1e9
N_SEL = SEQ // SEL_BLOCK
N_CMP = (SEQ - CMP_BLOCK) // CMP_STRIDE + 1
SCALE = HEAD_DIM ** -0.5

LANES = 128
VMEM_LIMIT = 56 * 1024 * 1024

TQ = 256
TK = 256
NQ = SEQ // TQ

W_IN_NSA_COLS = NSA_WIDTH + N_BRANCH * 2 * N_KV_GROUPS * HEAD_DIM
W_IN_GATE_COLS = N_NSA_HEADS * N_BRANCH
COL_KV = NSA_WIDTH
COL_QR = 0
COL_KR = 2048
COL_VR = 4096
COL_GR = 6144


def _bucket_thresholds():
    rel = np.arange(0, 4 * MAX_DISTANCE)
    max_exact = NUM_BUCKETS // 2
    nf = np.maximum(rel, 1).astype(np.float64)
    large = max_exact + np.floor(np.log(nf / max_exact) / math.log(MAX_DISTANCE / max_exact)
                                 * (NUM_BUCKETS - max_exact)).astype(np.int64)
    large = np.minimum(large, NUM_BUCKETS - 1)
    b = np.where(rel < max_exact, rel, large)
    return [int(np.argmax(b >= k)) for k in range(NUM_BUCKETS)]


_THR = _bucket_thresholds()


def _dot(a, b):
    return jnp.dot(a, b, preferred_element_type=F32)


def _dot_nt(a, b):
    return lax.dot_general(a, b, (((1,), (1,)), ((), ())), preferred_element_type=F32)


def _rms_kernel(x_ref, w_ref, o_ref):
    x = x_ref[...]
    y = x * lax.rsqrt(jnp.mean(x * x, axis=-1, keepdims=True) + EPS)
    o_ref[...] = (y * w_ref[...]).astype(o_ref.dtype)


def _rmsnorm(x, w, tm=256):
    m, d = x.shape
    return pl.pallas_call(
        _rms_kernel,
        out_shape=jax.ShapeDtypeStruct((m, d), BF16),
        grid=(m // tm,),
        in_specs=[pl.BlockSpec((tm, d), lambda i: (i, 0)),
                  pl.BlockSpec((1, d), lambda i: (0, 0))],
        out_specs=pl.BlockSpec((tm, d), lambda i: (i, 0)),
        compiler_params=pltpu.CompilerParams(dimension_semantics=("arbitrary",),
                                             vmem_limit_bytes=VMEM_LIMIT),
    )(x, w.reshape(1, d))


def _cast_specs(casts, n_steps, nj):
    in_specs, out_specs, out_shapes = [], [], []
    for arr, ncols in casts:
        slab = arr.shape[0] // n_steps
        assert slab * n_steps == arr.shape[0] and slab % 16 == 0 and ncols % LANES == 0
        spec = pl.BlockSpec((slab, ncols), lambda i, j: (i * nj + j, 0))
        in_specs.append(spec)
        out_specs.append(spec)
        out_shapes.append(jax.ShapeDtypeStruct((arr.shape[0], ncols), BF16))
    return in_specs, out_specs, out_shapes


def _cast_slabs(refs, n_casts):
    for src, dst in zip(refs[:n_casts], refs[len(refs) - n_casts:]):
        dst[...] = src[...].astype(dst.dtype)


def _mm_kernel(a_ref, b_ref, *rest, n_casts):
    o_ref = rest[n_casts]
    o_ref[...] = _dot(a_ref[...], b_ref[...]).astype(o_ref.dtype)
    _cast_slabs(rest[:n_casts] + rest[n_casts + 1:], n_casts)


def _matmul(a, b, out_dtype, tm, tn, casts=()):
    m, k = a.shape
    _, n = b.shape
    grid = (m // tm, n // tn)
    c_in, c_out, c_shapes = _cast_specs(casts, grid[0] * grid[1], grid[1])
    outs = pl.pallas_call(
        functools.partial(_mm_kernel, n_casts=len(casts)),
        out_shape=[jax.ShapeDtypeStruct((m, n), out_dtype)] + c_shapes,
        grid=grid,
        in_specs=[pl.BlockSpec((tm, k), lambda i, j: (i, 0)),
                  pl.BlockSpec((k, tn), lambda i, j: (0, j))] + c_in,
        out_specs=[pl.BlockSpec((tm, tn), lambda i, j: (i, j))] + c_out,
        compiler_params=pltpu.CompilerParams(dimension_semantics=("arbitrary", "arbitrary"),
                                             vmem_limit_bytes=VMEM_LIMIT),
    )(a, b, *[arr for arr, _ in casts])
    return outs


def _inproj_kernel(a_ref, wt_ref, *rest, n_casts):
    o_ref = rest[n_casts]
    wbf_ref = rest[-1]

    @pl.when(pl.program_id(1) == 0)
    def _():
        wbf_ref[...] = wt_ref[...].astype(BF16)

    o_ref[...] = _dot_nt(a_ref[...], wbf_ref[...])
    _cast_slabs(rest[:n_casts] + rest[n_casts + 1:-1], n_casts)


def _in_projection(a, w_t, row0, n_cols, tn, tm=1024, casts=()):
    m, k = a.shape
    grid = (n_cols // tn, m // tm)
    c_in, c_out, c_shapes = _cast_specs(casts, grid[0] * grid[1], grid[1])
    sub = 8
    assert row0 % sub == 0 and tn % sub == 0
    return pl.pallas_call(
        functools.partial(_inproj_kernel, n_casts=len(casts)),
        out_shape=[jax.ShapeDtypeStruct((m, n_cols), F32)] + c_shapes,
        grid=grid,
        in_specs=[pl.BlockSpec((tm, k), lambda j, i: (i, 0)),
                  pl.BlockSpec((pl.Element(tn), pl.Element(k)),
                               lambda j, i: ((row0 // sub + j * (tn // sub)) * sub, 0),
                               pipeline_mode=pl.Buffered(1))] + c_in,
        out_specs=[pl.BlockSpec((tm, tn), lambda j, i: (i, j))] + c_out,
        scratch_shapes=[pltpu.VMEM((tn, k), BF16)],
        compiler_params=pltpu.CompilerParams(dimension_semantics=("arbitrary", "arbitrary"),
                                             vmem_limit_bytes=VMEM_LIMIT),
    )(a, w_t, *[arr for arr, _ in casts])


def _outproj_kernel(a1_ref, a2_ref, w1_ref, w2_ref, x_ref, o_ref):
    acc = _dot(a1_ref[...], w1_ref[...]) + _dot(a2_ref[...], w2_ref[...])
    o_ref[...] = x_ref[...] + acc


def _out_projection(o_nsa, o_ret, w_out, x, tm=1024, tn=512):
    m = x.shape[0]
    kh = NSA_WIDTH
    return pl.pallas_call(
        _outproj_kernel,
        out_shape=jax.ShapeDtypeStruct((m, D_MODEL), F32),
        grid=(m // tm, D_MODEL // tn),
        in_specs=[pl.BlockSpec((tm, kh), lambda i, j: (i, 0)),
                  pl.BlockSpec((tm, kh), lambda i, j: (i, 0)),
                  pl.BlockSpec((kh, tn), lambda i, j: (0, j)),
                  pl.BlockSpec((kh, tn), lambda i, j: (1, j)),
                  pl.BlockSpec((tm, tn), lambda i, j: (i, j))],
        out_specs=pl.BlockSpec((tm, tn), lambda i, j: (i, j)),
        compiler_params=pltpu.CompilerParams(dimension_semantics=("arbitrary", "arbitrary"),
                                             vmem_limit_bytes=VMEM_LIMIT),
    )(o_nsa, o_ret, w_out, w_out, x)


def _ffn_up_kernel(h_ref, wg_ref, wu_ref, *rest, n_casts):
    o_ref = rest[n_casts]
    h = h_ref[...]
    g = _dot(h, wg_ref[...])
    u = _dot(h, wu_ref[...])
    o_ref[...] = (g * jax.nn.sigmoid(g) * u).astype(o_ref.dtype)
    _cast_slabs(rest[:n_casts] + rest[n_casts + 1:], n_casts)


def _ffn_up(h, w_gate, w_up, tm=1024, tn=256, casts=()):
    m, k = h.shape
    n = w_gate.shape[1]
    grid = (m // tm, n // tn)
    c_in, c_out, c_shapes = _cast_specs(casts, grid[0] * grid[1], grid[1])
    return pl.pallas_call(
        functools.partial(_ffn_up_kernel, n_casts=len(casts)),
        out_shape=[jax.ShapeDtypeStruct((m, n), BF16)] + c_shapes,
        grid=grid,
        in_specs=[pl.BlockSpec((tm, k), lambda i, j: (i, 0)),
                  pl.BlockSpec((k, tn), lambda i, j: (0, j)),
                  pl.BlockSpec((k, tn), lambda i, j: (0, j))] + c_in,
        out_specs=[pl.BlockSpec((tm, tn), lambda i, j: (i, j))] + c_out,
        compiler_params=pltpu.CompilerParams(dimension_semantics=("arbitrary", "arbitrary"),
                                             vmem_limit_bytes=VMEM_LIMIT),
    )(h, w_gate, w_up, *[arr for arr, _ in casts])


def _ffn_down_kernel(a_ref, b_ref, x_ref, o_ref, acc_ref):
    kk = pl.program_id(2)

    @pl.when(kk == 0)
    def _():
        acc_ref[...] = x_ref[...]

    acc_ref[...] += _dot(a_ref[...], b_ref[...])

    @pl.when(kk == pl.num_programs(2) - 1)
    def _():
        o_ref[...] = acc_ref[...]


def _ffn_down(a, w_down, x, tm=1024, tn=512, tk=5504):
    m, k = a.shape
    n = w_down.shape[1]
    return pl.pallas_call(
        _ffn_down_kernel,
        out_shape=jax.ShapeDtypeStruct((m, n), F32),
        grid=(m // tm, n // tn, k // tk),
        in_specs=[pl.BlockSpec((tm, tk), lambda i, j, kk: (i, kk)),
                  pl.BlockSpec((tk, tn), lambda i, j, kk: (kk, j)),
                  pl.BlockSpec((tm, tn), lambda i, j, kk: (i, j))],
        out_specs=pl.BlockSpec((tm, tn), lambda i, j, kk: (i, j)),
        scratch_shapes=[pltpu.VMEM((tm, tn), F32)],
        compiler_params=pltpu.CompilerParams(
            dimension_semantics=("arbitrary", "arbitrary", "arbitrary"),
            vmem_limit_bytes=VMEM_LIMIT),
    )(a, w_down, x)


def _bias_value(rel, relb_ref, h):
    val = jnp.full(rel.shape, relb_ref[0, h], F32)
    for k in range(1, NUM_BUCKETS):
        val = jnp.where(rel >= _THR[k], relb_ref[k, h], val)
    return val


def _bias_tables_kernel(relb_ref, biasc_ref, dm_ref):
    h = pl.program_id(0)
    n = lax.broadcasted_iota(jnp.int32, (LANES, TQ), 0)
    a_c = lax.broadcasted_iota(jnp.int32, (LANES, TQ), 1)
    for qt in range(NQ):
        rel_c = (qt * TQ + a_c) - (CMP_STRIDE * n + CMP_BLOCK - 1)
        biasc_ref[0, qt] = _bias_value(rel_c, relb_ref, h)
    b = lax.broadcasted_iota(jnp.int32, (TK, TQ), 0)
    a = lax.broadcasted_iota(jnp.int32, (TK, TQ), 1)
    far = relb_ref[NUM_BUCKETS - 1, h]
    inv_scale = 1.0 / SCALE
    dm_ref[0, 0] = ((_bias_value(a - b, relb_ref, h) - far) * inv_scale
                    + jnp.where(a >= b, 0.0, NEG_INF))
    dm_ref[0, 1] = (_bias_value(a - b + TK, relb_ref, h) - far) * inv_scale
    dm_ref[0, 2] = jnp.where(b > a, 0.0, NEG_INF)


def _bias_tables(rel_bias):
    return pl.pallas_call(
        _bias_tables_kernel,
        out_shape=(jax.ShapeDtypeStruct((N_KV_GROUPS, NQ, LANES, GROUP_SIZE * TQ), F32),
                   jax.ShapeDtypeStruct((N_KV_GROUPS, 3, TK, GROUP_SIZE * TQ), F32)),
        grid=(N_NSA_HEADS,),
        in_specs=[pl.BlockSpec(memory_space=pltpu.SMEM)],
        out_specs=(pl.BlockSpec((1, NQ, LANES, TQ), lambda h: (h // GROUP_SIZE, 0, 0, h % GROUP_SIZE)),
                   pl.BlockSpec((1, 3, TK, TQ), lambda h: (h // GROUP_SIZE, 0, 0, h % GROUP_SIZE))),
        compiler_params=pltpu.CompilerParams(dimension_semantics=("arbitrary",),
                                             vmem_limit_bytes=VMEM_LIMIT),
    )(rel_bias)


def _row_rms(x, w):
    return x * lax.rsqrt(jnp.mean(x * x, axis=-1, keepdims=True) + EPS) * w


def _nsa_kernel(q_ref, k0_ref, v0_ref, k1_ref, v1_ref, k2_ref, v2_ref, gate_ref,
                qnw_ref, knw_ref, pek_ref, pev_ref, wck_ref, wcv_ref, biasc_ref, dm_ref, c2st_ref,
                o_ref,
                kc_s, vct_s, kaug_s, kw_s, vst_s, vwt_s, gt_s):
    qt = pl.program_id(2)
    cols_all = GROUP_SIZE * TQ

    @pl.when(qt == 0)
    def _prologue():
        kaug_s[:, :HEAD_DIM] = _row_rms(k1_ref[...], knw_ref[1:2, :]).astype(BF16)
        jblk = lax.broadcasted_iota(jnp.int32, (SEQ, LANES), 0) // SEL_BLOCK
        lane = lax.broadcasted_iota(jnp.int32, (SEQ, LANES), 1)
        kaug_s[:, HEAD_DIM:] = jnp.where(jblk == lane, 1.0, 0.0).astype(BF16)
        kw_s[...] = _row_rms(k2_ref[...], knw_ref[2:3, :]).astype(BF16)
        for t in range(SEQ // TK):
            vst_s[t] = v1_ref[t * TK:(t + 1) * TK, :].T.astype(BF16)
            vwt_s[t] = v2_ref[t * TK:(t + 1) * TK, :].T.astype(BF16)

        def compress(x_ref, pe_ref, w_ref):
            acc_lo = jnp.zeros((LANES, HEAD_DIM), F32)
            acc_hi = jnp.zeros((LANES, HEAD_DIM), F32)
            for j in range(CMP_STRIDE):
                y = x_ref[pl.ds(j, LANES, stride=CMP_STRIDE), :]
                acc_lo += _dot((y + pe_ref[j:j + 1, :]).astype(BF16),
                               w_ref[j * HEAD_DIM:(j + 1) * HEAD_DIM, :])
                jh = CMP_STRIDE + j
                acc_hi += _dot((y + pe_ref[jh:jh + 1, :]).astype(BF16),
                               w_ref[jh * HEAD_DIM:(jh + 1) * HEAD_DIM, :])
            return acc_lo + pltpu.roll(acc_hi, LANES - 1, 0)

        kc_s[...] = _row_rms(compress(k0_ref, pek_ref, wck_ref), knw_ref[0:1, :]).astype(BF16)
        vct_s[...] = compress(v0_ref, pev_ref, wcv_ref).T.astype(BF16)

    t0 = qt * TQ
    q = q_ref[...]
    qts = []
    for r in range(GROUP_SIZE):
        qts.append(_row_rms(q[:, r * HEAD_DIM:(r + 1) * HEAD_DIM], qnw_ref[...]).T)
    qt_stack = jnp.concatenate(qts, axis=1).astype(BF16)

    sc = _dot(kc_s[...], qt_stack) * SCALE + biasc_ref[0, 0]
    tpos = t0 + (lax.broadcasted_iota(jnp.int32, (LANES, cols_all), 1) & (TQ - 1))
    nrow = lax.broadcasted_iota(jnp.int32, (LANES, cols_all), 0)
    maskc = tpos >= CMP_STRIDE * nrow + (CMP_BLOCK - 1)
    scm = jnp.where(maskc, sc, NEG_INF)
    mc = jnp.max(scm, axis=0, keepdims=True)
    ec = jnp.where(maskc, jnp.exp(scm - mc), 0.0)
    lc = jnp.sum(ec, axis=0, keepdims=True)
    pc = ec * (1.0 / jnp.where(lc > 0.0, lc, 1.0))
    o_c = _dot(vct_s[...], pc.astype(BF16))

    psum = pc[:, 0:TQ] + pc[:, TQ:2 * TQ] + pc[:, 2 * TQ:3 * TQ] + pc[:, 3 * TQ:4 * TQ]
    p_hi = psum.astype(BF16)
    p_lo = (psum - p_hi.astype(F32)).astype(BF16)
    imp = (_dot(c2st_ref[...], p_hi) + _dot(c2st_ref[...], p_lo))[0:N_SEL]
    tq_pos = t0 + lax.broadcasted_iota(jnp.int32, (N_SEL, TQ), 1)
    sidx = lax.broadcasted_iota(jnp.int32, (N_SEL, TQ), 0)
    cur = tq_pos >> 6
    score = jnp.where(sidx <= cur, imp, NEG_INF)
    for forced_blk in (cur - 1, cur, jnp.zeros_like(cur)):
        score = jnp.where(sidx == forced_blk, FORCE_SCORE, score)
    rank = jnp.zeros((N_SEL, TQ), F32)
    for sp in range(N_SEL):
        row = score[sp:sp + 1, :]
        tie = jnp.where(sidx > sp, 1.0, 0.0)
        rank = rank + jnp.where(row > score, 1.0, jnp.where(row == score, tie, 0.0))
    selb = jnp.where(rank < TOP_N, 0.0, NEG_INF)
    selb = jnp.concatenate([selb, jnp.zeros((HEAD_DIM - N_SEL, TQ), F32)], axis=0).astype(BF16)
    qt_aug = jnp.concatenate([qt_stack, jnp.concatenate([selb] * GROUP_SIZE, axis=1)], axis=0)

    m0 = jnp.full((1, cols_all), NEG_INF, F32)
    l0 = jnp.zeros((1, cols_all), F32)
    a0 = jnp.zeros((HEAD_DIM, cols_all), F32)
    exp2_scale = SCALE * math.log2(math.e)

    def online(carry, tiles):
        m, l, acc = carry
        m_new = m
        for u, _ in tiles:
            m_new = jnp.maximum(m_new, jnp.max(u, axis=0, keepdims=True))
        alpha = jnp.exp2((m - m_new) * exp2_scale)
        l = alpha * l
        acc = alpha * acc
        for u, vt in tiles:
            p = jnp.exp2((u - m_new) * exp2_scale)
            l = l + jnp.sum(p, axis=0, keepdims=True)
            acc = acc + _dot(vt, p.astype(BF16))
        return m_new, l, acc

    def key_rows(kt):
        return pl.ds(pl.multiple_of(kt * TK, TK), TK)

    def sel_tile(kt):
        return _dot(kaug_s[key_rows(kt), :], qt_aug), vst_s[kt]

    n_far = jnp.maximum(qt - 1, 0)
    carry = lax.fori_loop(0, n_far >> 1,
                          lambda i, c: online(c, [sel_tile(2 * i), sel_tile(2 * i + 1)]), (m0, l0, a0))
    carry = lax.fori_loop(0, n_far & 1, lambda i, c: online(c, [sel_tile(n_far - 1)]), carry)
    kt1 = jnp.maximum(qt - 1, 0)
    u1, vt1 = sel_tile(kt1)
    u0, vt0 = sel_tile(qt)
    carry = online(carry, [(u1 + dm_ref[0, 1] + jnp.where(qt >= 1, 0.0, NEG_INF), vt1),
                           (u0 + dm_ref[0, 0], vt0)])
    _, l_s, acc_s = carry
    o_s = acc_s * (1.0 / l_s)

    tiles = []
    for dist, kind in ((2, 2), (1, 1), (0, 0)):
        kt = jnp.maximum(qt - dist, 0)
        u = _dot(kw_s[key_rows(kt), :], qt_stack) + dm_ref[0, kind]
        if dist > 0:
            u = u + jnp.where(qt >= dist, 0.0, NEG_INF)
        tiles.append((u, vwt_s[kt]))
    _, l_w, acc_w = online((m0, l0, a0), tiles)
    o_w = acc_w * (1.0 / l_w)

    gt_s[...] = jax.nn.sigmoid(gate_ref[...]).T
    head0 = pl.program_id(1) * GROUP_SIZE
    for r in range(GROUP_SIZE):
        cs = slice(r * TQ, (r + 1) * TQ)
        c = (head0 + r) * N_BRANCH
        o = (gt_s[pl.ds(c, 1), :] * o_c[:, cs] + gt_s[pl.ds(c + 1, 1), :] * o_s[:, cs]
             + gt_s[pl.ds(c + 2, 1), :] * o_w[:, cs])
        o_ref[:, r * HEAD_DIM:(r + 1) * HEAD_DIM] = o.T.astype(o_ref.dtype)


def _cmp_to_sel_t():
    ss = np.arange(LANES)[:, None] * SEL_BLOCK
    cs = np.arange(LANES)[None, :] * CMP_STRIDE
    ov = np.clip(np.minimum(cs + CMP_BLOCK, ss + SEL_BLOCK) - np.maximum(cs, ss), 0, None)
    m = ov.astype(np.float32) / np.float32(CMP_BLOCK)
    m[N_SEL:, :] = 0.0
    m[:, N_CMP:] = 0.0
    return jnp.asarray(m, BF16)


def _nsa_attention(proj, gate_logits, batch, q_norm_w, k_norm_w, pe_k, pe_v, w_cmp_k, w_cmp_v, biasc, dm):
    kvb = COL_KV // LANES

    def kv_spec(branch, which):
        base = kvb + (branch * 2 + which) * N_KV_GROUPS
        return pl.BlockSpec((SEQ, LANES), lambda b, g, qt: (b, base + g))

    full = lambda shape: pl.BlockSpec(shape, lambda b, g, qt: (0,) * len(shape))
    return pl.pallas_call(
        _nsa_kernel,
        out_shape=jax.ShapeDtypeStruct((batch * SEQ, NSA_WIDTH), BF16),
        grid=(batch, N_KV_GROUPS, NQ),
        in_specs=[pl.BlockSpec((TQ, GROUP_SIZE * HEAD_DIM), lambda b, g, qt: (b * NQ + qt, g)),
                  kv_spec(0, 0), kv_spec(0, 1), kv_spec(1, 0), kv_spec(1, 1),
                  kv_spec(2, 0), kv_spec(2, 1),
                  pl.BlockSpec((TQ, LANES), lambda b, g, qt: (b * NQ + qt, 0)),
                  full((1, HEAD_DIM)), full((N_BRANCH, HEAD_DIM)),
                  full((CMP_BLOCK, HEAD_DIM)), full((CMP_BLOCK, HEAD_DIM)),
                  full((CMP_BLOCK * HEAD_DIM, HEAD_DIM)), full((CMP_BLOCK * HEAD_DIM, HEAD_DIM)),
                  pl.BlockSpec((1, 1, LANES, GROUP_SIZE * TQ), lambda b, g, qt: (g, qt, 0, 0)),
                  pl.BlockSpec((1, 3, TK, GROUP_SIZE * TQ), lambda b, g, qt: (g, 0, 0, 0)),
                  full((LANES, LANES))],
        out_specs=pl.BlockSpec((TQ, GROUP_SIZE * HEAD_DIM), lambda b, g, qt: (b * NQ + qt, g)),
        scratch_shapes=[pltpu.VMEM((LANES, HEAD_DIM), BF16),
                        pltpu.VMEM((HEAD_DIM, LANES), BF16),
                        pltpu.VMEM((SEQ, 2 * HEAD_DIM), BF16),
                        pltpu.VMEM((SEQ, HEAD_DIM), BF16),
                        pltpu.VMEM((SEQ // TK, HEAD_DIM, TK), BF16),
                        pltpu.VMEM((SEQ // TK, HEAD_DIM, TK), BF16),
                        pltpu.VMEM((LANES, TQ), F32)],
        compiler_params=pltpu.CompilerParams(
            dimension_semantics=("arbitrary", "arbitrary", "arbitrary"),
            vmem_limit_bytes=VMEM_LIMIT),
    )(proj, proj, proj, proj, proj, proj, proj, gate_logits,
      q_norm_w.reshape(1, HEAD_DIM), k_norm_w, pe_k, pe_v,
      w_cmp_k.astype(BF16), w_cmp_v.astype(BF16), biasc, dm, _cmp_to_sel_t())


def _ret_kernel(lg_ref, q_ref, k_ref, v_ref, g_ref, cos_ref, sin_ref, gnw_ref, o_ref, state_ref):
    h = pl.program_id(1)
    lg = lg_ref[h]
    c_len = RET_CHUNK
    ii = lax.broadcasted_iota(jnp.int32, (c_len, c_len), 0)
    jj = lax.broadcasted_iota(jnp.int32, (c_len, c_len), 1)
    diff = (ii - jj).astype(F32)
    decay_in = jnp.where(diff >= 0.0, jnp.exp(jnp.maximum(diff, 0.0) * lg), 0.0)
    icol = lax.broadcasted_iota(jnp.int32, (c_len, 1), 0).astype(F32)
    xi = jnp.exp((icol + 1.0) * lg)
    zeta = jnp.exp((c_len - 1.0 - icol) * lg)
    chunk_decay = jnp.exp(jnp.full((1, RET_VAL_DIM), float(c_len), F32) * lg)
    half = RET_KEY_DIM // 2
    state_ref[...] = jnp.zeros_like(state_ref)

    def body(c, _):
        rows = pl.ds(pl.multiple_of(c * c_len, c_len), c_len)
        cos = cos_ref[rows, :]
        sin = sin_ref[rows, :]

        def rot(x):
            x1, x2 = x[:, :half], x[:, half:]
            return jnp.concatenate([x1 * cos - x2 * sin, x2 * cos + x1 * sin], axis=-1)

        qr = rot(q_ref[rows, :])
        kr = rot(k_ref[rows, :]) * (RET_KEY_DIM ** -0.5)
        vb = v_ref[rows, :].astype(BF16)
        scores = _dot_nt(qr.astype(BF16), kr.astype(BF16)) * decay_in
        inner = _dot(scores.astype(BF16), vb)
        st = state_ref[...]
        cross = _dot((qr * xi).astype(BF16), st.astype(BF16))
        kv = _dot((kr * zeta).T.astype(BF16), vb)
        state_ref[...] = chunk_decay * st + kv
        y = inner + cross
        y = y * lax.rsqrt(jnp.mean(y * y, axis=-1, keepdims=True) + EPS) * gnw_ref[...]
        g = g_ref[rows, :]
        o_ref[rows, :] = (g * jax.nn.sigmoid(g) * y).astype(o_ref.dtype)
        return 0

    lax.fori_loop(0, SEQ // c_len, body, 0)


def _retention(proj, batch, gn_w):
    pos = np.arange(SEQ, dtype=np.float64)
    half = RET_KEY_DIM // 2
    inv = np.exp(-np.linspace(0.0, 1.0, half) * math.log(ROPE_BASE)).astype(np.float32)
    ang = pos.astype(np.float32)[:, None] * inv[None, :]
    cos = jnp.asarray(np.cos(ang.astype(np.float64)), F32)
    sin = jnp.asarray(np.sin(ang.astype(np.float64)), F32)
    log_gamma = jnp.asarray(np.log(1.0 - 2.0 ** (-5.0 - np.arange(N_RET_HEADS))), F32)
    w = RET_KEY_DIM

    def col_spec(col0):
        return pl.BlockSpec((SEQ, w), lambda b, h: (b, col0 // w + h))

    return pl.pallas_call(
        _ret_kernel,
        out_shape=jax.ShapeDtypeStruct((batch * SEQ, RET_WIDTH), BF16),
        grid=(batch, N_RET_HEADS),
        in_specs=[pl.BlockSpec(memory_space=pltpu.SMEM),
                  col_spec(COL_QR), col_spec(COL_KR), col_spec(COL_VR), col_spec(COL_GR),
                  pl.BlockSpec((SEQ, half), lambda b, h: (0, 0)),
                  pl.BlockSpec((SEQ, half), lambda b, h: (0, 0)),
                  pl.BlockSpec((1, RET_VAL_DIM), lambda b, h: (0, h))],
        out_specs=pl.BlockSpec((SEQ, RET_VAL_DIM), lambda b, h: (b, h)),
        scratch_shapes=[pltpu.VMEM((RET_KEY_DIM, RET_VAL_DIM), F32)],
        compiler_params=pltpu.CompilerParams(dimension_semantics=("arbitrary", "arbitrary"),
                                             vmem_limit_bytes=VMEM_LIMIT),
    )(log_gamma, proj, proj, proj, proj, cos, sin, gn_w.reshape(1, RET_WIDTH))


def _layer(x, norm1_w, w_in, nsa_q_norm_w, nsa_k_norm_w, cmp_pe_k, cmp_pe_v, w_cmp_k, w_cmp_v,
           rel_bias, ret_gn_w, w_out, norm2_w, w_gate, w_up, w_down):
    batch = x.shape[0]
    xf = x.reshape(batch * SEQ, D_MODEL)
    h = _rmsnorm(xf, norm1_w)
    w_in_t = w_in.T
    gate0 = W_IN_NSA_COLS
    gate1 = gate0 + W_IN_GATE_COLS
    proj_ret, w_gate_b, w_up_b, w_out_b = _in_projection(
        h, w_in_t, gate1, 4 * RET_WIDTH, 1024, tm=512,
        casts=((w_gate, D_FF), (w_up, D_FF), (w_out, D_MODEL)))
    proj_nsa, = _in_projection(h, w_in_t, 0, W_IN_NSA_COLS, 1024, tm=512)
    gate_logits, = _in_projection(h, w_in_t, gate0, LANES, LANES)
    biasc, dm = _bias_tables(rel_bias)
    o_nsa = _nsa_attention(proj_nsa, gate_logits, batch, nsa_q_norm_w, nsa_k_norm_w, cmp_pe_k, cmp_pe_v,
                           w_cmp_k, w_cmp_v, biasc, dm)
    o_ret = _retention(proj_ret, batch, ret_gn_w)
    x1 = _out_projection(o_nsa, o_ret, w_out_b, xf)
    hf = _rmsnorm(x1, norm2_w)
    mid, w_down_b = _ffn_up(hf, w_gate_b, w_up_b, casts=((w_down, D_MODEL),))
    out = _ffn_down(mid, w_down_b, x1)
    return out.reshape(batch, SEQ, D_MODEL)


def kernel(x, norm1_w, w_in, nsa_q_norm_w, nsa_k_norm_w, cmp_pe_k, cmp_pe_v, w_cmp_k, w_cmp_v,
           rel_bias, ret_gn_w, w_out, norm2_w, w_gate, w_up, w_down):
    for l in range(norm1_w.shape[0]):
        x = _layer(x, norm1_w[l], w_in[l], nsa_q_norm_w[l], nsa_k_norm_w[l], cmp_pe_k[l], cmp_pe_v[l],
                   w_cmp_k[l], w_cmp_v[l], rel_bias, ret_gn_w[l], w_out[l], norm2_w[l],
                   w_gate[l], w_up[l], w_down[l])
    return x
```

```python
import functools
import math

import numpy as np
import jax
import jax.numpy as jnp
from jax import lax
from jax.experimental import pallas as pl
from jax.experimental.pallas import tpu as pltpu

F32 = jnp.float32
BF16 = jnp.bfloat16

D_MODEL = 4096
SEQ = 2048
HEAD_DIM = 128
N_NSA_HEADS = 16
N_KV_GROUPS = 4
GROUP_SIZE = 4
N_BRANCH = 3
CMP_BLOCK = 32
CMP_STRIDE = 16
SEL_BLOCK = 64
TOP_N = 16
WINDOW = 512
RET_KEY_DIM = 256
RET_VAL_DIM = 256
N_RET_HEADS = 8
RET_CHUNK = 128
ROPE_BASE = 10000.0
NSA_WIDTH = 2048
RET_WIDTH = 2048
D_FF = 11008
NUM_BUCKETS = 32
MAX_DISTANCE = 128
EPS = 1e-6
NEG_INF = -1e30
FORCE_SCORE = 1e9
N_SEL = SEQ // SEL_BLOCK
N_CMP = (SEQ - CMP_BLOCK) // CMP_STRIDE + 1
SCALE = HEAD_DIM ** -0.5

LANES = 128
VMEM_LIMIT = 56 * 1024 * 1024

TQ = 256
TK = 256
NQ = SEQ // TQ

W_IN_NSA_COLS = NSA_WIDTH + N_BRANCH * 2 * N_KV_GROUPS * HEAD_DIM
W_IN_GATE_COLS = N_NSA_HEADS * N_BRANCH
COL_KV = NSA_WIDTH
COL_QR = 0
COL_KR = 2048
COL_VR = 4096
COL_GR = 6144


def _bucket_thresholds():
    rel = np.arange(0, 4 * MAX_DISTANCE)
    max_exact = NUM_BUCKETS // 2
    nf = np.maximum(rel, 1).astype(np.float64)
    large = max_exact + np.floor(np.log(nf / max_exact) / math.log(MAX_DISTANCE / max_exact)
                                 * (NUM_BUCKETS - max_exact)).astype(np.int64)
    large = np.minimum(large, NUM_BUCKETS - 1)
    b = np.where(rel < max_exact, rel, large)
    return [int(np.argmax(b >= k)) for k in range(NUM_BUCKETS)]


_THR = _bucket_thresholds()


def _dot(a, b):
    return jnp.dot(a, b, preferred_element_type=F32)


def _dot_nt(a, b):
    return lax.dot_general(a, b, (((1,), (1,)), ((), ())), preferred_element_type=F32)


def _rms_kernel(x_ref, w_ref, o_ref):
    x = x_ref[...]
    y = x * lax.rsqrt(jnp.mean(x * x, axis=-1, keepdims=True) + EPS)
    o_ref[...] = (y * w_ref[...]).astype(o_ref.dtype)


def _rmsnorm(x, w, tm=256):
    m, d = x.shape
    return pl.pallas_call(
        _rms_kernel,
        out_shape=jax.ShapeDtypeStruct((m, d), BF16),
        grid=(m // tm,),
        in_specs=[pl.BlockSpec((tm, d), lambda i: (i, 0)),
                  pl.BlockSpec((1, d), lambda i: (0, 0))],
        out_specs=pl.BlockSpec((tm, d), lambda i: (i, 0)),
        compiler_params=pltpu.CompilerParams(dimension_semantics=("arbitrary",),
                                             vmem_limit_bytes=VMEM_LIMIT),
    )(x, w.reshape(1, d))


def _cast_spec(grid, row0, n_rows, n_cols, n_slabs):
    slab = n_rows // n_slabs
    sub = 8
    n_steps = math.prod(grid)
    assert slab * n_slabs == n_rows and slab % 16 == 0 and n_cols % LANES == 0
    assert row0 % sub == 0 and n_slabs <= n_steps

    def slab_index(ids):
        step = ids[0]
        for dim, idx in zip(grid[1:], ids[1:]):
            step = step * dim + idx
        return step if n_slabs == n_steps else jnp.minimum(step, n_slabs - 1)

    in_spec = pl.BlockSpec((pl.Element(slab), pl.Element(n_cols)),
                           lambda *ids: ((row0 // sub + slab_index(ids) * (slab // sub)) * sub, 0))
    out_spec = pl.BlockSpec((slab, n_cols), lambda *ids: (slab_index(ids), 0))
    return in_spec, out_spec, jax.ShapeDtypeStruct((n_rows, n_cols), BF16)


def _cast_specs(casts, grid):
    specs = [_cast_spec(grid, row0, n_rows, arr.shape[1], n_slabs) for arr, row0, n_rows, n_slabs in casts]
    return [s[0] for s in specs], [s[1] for s in specs], [s[2] for s in specs]


def _cast_slabs(refs, n_casts):
    for src, dst in zip(refs[:n_casts], refs[len(refs) - n_casts:]):
        dst[...] = src[...].astype(dst.dtype)


def _mm_kernel(a_ref, b_ref, *rest, n_casts):
    o_ref = rest[n_casts]
    o_ref[...] = _dot(a_ref[...], b_ref[...]).astype(o_ref.dtype)
    _cast_slabs(rest[:n_casts] + rest[n_casts + 1:], n_casts)


def _matmul(a, b, out_dtype, tm, tn, casts=()):
    m, k = a.shape
    _, n = b.shape
    grid = (m // tm, n // tn)
    c_in, c_out, c_shapes = _cast_specs(casts, grid)
    outs = pl.pallas_call(
        functools.partial(_mm_kernel, n_casts=len(casts)),
        out_shape=[jax.ShapeDtypeStruct((m, n), out_dtype)] + c_shapes,
        grid=grid,
        in_specs=[pl.BlockSpec((tm, k), lambda i, j: (i, 0)),
                  pl.BlockSpec((k, tn), lambda i, j: (0, j))] + c_in,
        out_specs=[pl.BlockSpec((tm, tn), lambda i, j: (i, j))] + c_out,
        compiler_params=pltpu.CompilerParams(dimension_semantics=("arbitrary", "arbitrary"),
                                             vmem_limit_bytes=VMEM_LIMIT),
    )(a, b, *[c[0] for c in casts])
    return outs


def _inproj_kernel(a_ref, wt_ref, *rest, n_casts):
    o_ref = rest[n_casts]
    wbf_ref = rest[-1]

    @pl.when(pl.program_id(1) == 0)
    def _():
        wbf_ref[...] = wt_ref[...].astype(BF16)

    o_ref[...] = _dot_nt(a_ref[...], wbf_ref[...])
    _cast_slabs(rest[:n_casts] + rest[n_casts + 1:-1], n_casts)


def _in_projection(a, w_t, row0, n_cols, tn, tm=1024, casts=()):
    m, k = a.shape
    grid = (n_cols // tn, m // tm)
    c_in, c_out, c_shapes = _cast_specs(casts, grid)
    sub = 8
    assert row0 % sub == 0 and tn % sub == 0
    return pl.pallas_call(
        functools.partial(_inproj_kernel, n_casts=len(casts)),
        out_shape=[jax.ShapeDtypeStruct((m, n_cols), F32)] + c_shapes,
        grid=grid,
        in_specs=[pl.BlockSpec((tm, k), lambda j, i: (i, 0)),
                  pl.BlockSpec((pl.Element(tn), pl.Element(k)),
                               lambda j, i: ((row0 // sub + j * (tn // sub)) * sub, 0),
                               pipeline_mode=pl.Buffered(1))] + c_in,
        out_specs=[pl.BlockSpec((tm, tn), lambda j, i: (i, j))] + c_out,
        scratch_shapes=[pltpu.VMEM((tn, k), BF16)],
        compiler_params=pltpu.CompilerParams(dimension_semantics=("arbitrary", "arbitrary"),
                                             vmem_limit_bytes=VMEM_LIMIT),
    )(a, w_t, *[c[0] for c in casts])


def _outproj_kernel(a1_ref, a2_ref, w1_ref, w2_ref, x_ref, *rest, n_casts):
    o_ref = rest[n_casts]
    acc = _dot(a1_ref[...], w1_ref[...]) + _dot(a2_ref[...], w2_ref[...])
    o_ref[...] = x_ref[...] + acc
    _cast_slabs(rest[:n_casts] + rest[n_casts + 1:], n_casts)


def _out_projection(o_nsa, o_ret, w_out, x, tm=1024, tn=512, casts=()):
    m = x.shape[0]
    kh = NSA_WIDTH
    grid = (m // tm, D_MODEL // tn)
    c_in, c_out, c_shapes = _cast_specs(casts, grid)
    return pl.pallas_call(
        functools.partial(_outproj_kernel, n_casts=len(casts)),
        out_shape=[jax.ShapeDtypeStruct((m, D_MODEL), F32)] + c_shapes,
        grid=grid,
        in_specs=[pl.BlockSpec((tm, kh), lambda i, j: (i, 0)),
                  pl.BlockSpec((tm, kh), lambda i, j: (i, 0)),
                  pl.BlockSpec((kh, tn), lambda i, j: (0, j)),
                  pl.BlockSpec((kh, tn), lambda i, j: (1, j)),
                  pl.BlockSpec((tm, tn), lambda i, j: (i, j))] + c_in,
        out_specs=[pl.BlockSpec((tm, tn), lambda i, j: (i, j))] + c_out,
        compiler_params=pltpu.CompilerParams(dimension_semantics=("arbitrary", "arbitrary"),
                                             vmem_limit_bytes=VMEM_LIMIT),
    )(o_nsa, o_ret, w_out, w_out, x, *[c[0] for c in casts])


def _ffn_up_kernel(h_ref, wg_ref, wu_ref, *rest, n_casts):
    o_ref = rest[n_casts]
    h = h_ref[...]
    g = _dot(h, wg_ref[...])
    u = _dot(h, wu_ref[...])
    o_ref[...] = (g * jax.nn.sigmoid(g) * u).astype(o_ref.dtype)
    _cast_slabs(rest[:n_casts] + rest[n_casts + 1:], n_casts)


def _ffn_up(h, w_gate, w_up, tm=1024, tn=256, casts=()):
    m, k = h.shape
    n = w_gate.shape[1]
    grid = (m // tm, n // tn)
    c_in, c_out, c_shapes = _cast_specs(casts, grid)
    return pl.pallas_call(
        functools.partial(_ffn_up_kernel, n_casts=len(casts)),
        out_shape=[jax.ShapeDtypeStruct((m, n), BF16)] + c_shapes,
        grid=grid,
        in_specs=[pl.BlockSpec((tm, k), lambda i, j: (i, 0)),
                  pl.BlockSpec((k, tn), lambda i, j: (0, j)),
                  pl.BlockSpec((k, tn), lambda i, j: (0, j))] + c_in,
        out_specs=[pl.BlockSpec((tm, tn), lambda i, j: (i, j))] + c_out,
        compiler_params=pltpu.CompilerParams(dimension_semantics=("arbitrary", "arbitrary"),
                                             vmem_limit_bytes=VMEM_LIMIT),
    )(h, w_gate, w_up, *[c[0] for c in casts])


def _ffn_down_kernel(a_ref, b_ref, x_ref, o_ref, acc_ref):
    kk = pl.program_id(2)

    @pl.when(kk == 0)
    def _():
        acc_ref[...] = x_ref[...]

    acc_ref[...] += _dot(a_ref[...], b_ref[...])

    @pl.when(kk == pl.num_programs(2) - 1)
    def _():
        o_ref[...] = acc_ref[...]


def _ffn_down(a, w_down, x, tm=1024, tn=512, tk=5504):
    m, k = a.shape
    n = w_down.shape[1]
    return pl.pallas_call(
        _ffn_down_kernel,
        out_shape=jax.ShapeDtypeStruct((m, n), F32),
        grid=(m // tm, n // tn, k // tk),
        in_specs=[pl.BlockSpec((tm, tk), lambda i, j, kk: (i, kk)),
                  pl.BlockSpec((tk, tn), lambda i, j, kk: (kk, j)),
                  pl.BlockSpec((tm, tn), lambda i, j, kk: (i, j))],
        out_specs=pl.BlockSpec((tm, tn), lambda i, j, kk: (i, j)),
        scratch_shapes=[pltpu.VMEM((tm, tn), F32)],
        compiler_params=pltpu.CompilerParams(
            dimension_semantics=("arbitrary", "arbitrary", "arbitrary"),
            vmem_limit_bytes=VMEM_LIMIT),
    )(a, w_down, x)


def _bias_value(rel, relb_ref, h):
    val = jnp.full(rel.shape, relb_ref[0, h], F32)
    for k in range(1, NUM_BUCKETS):
        val = jnp.where(rel >= _THR[k], relb_ref[k, h], val)
    return val


def _bias_tables_kernel(relb_ref, biasc_ref, dm_ref):
    h = pl.program_id(0)
    n = lax.broadcasted_iota(jnp.int32, (LANES, TQ), 0)
    a_c = lax.broadcasted_iota(jnp.int32, (LANES, TQ), 1)
    for qt in range(NQ):
        rel_c = (qt * TQ + a_c) - (CMP_STRIDE * n + CMP_BLOCK - 1)
        biasc_ref[0, qt] = _bias_value(rel_c, relb_ref, h)
    b = lax.broadcasted_iota(jnp.int32, (TK, TQ), 0)
    a = lax.broadcasted_iota(jnp.int32, (TK, TQ), 1)
    far = relb_ref[NUM_BUCKETS - 1, h]
    inv_scale = 1.0 / SCALE
    dm_ref[0, 0] = ((_bias_value(a - b, relb_ref, h) - far) * inv_scale
                    + jnp.where(a >= b, 0.0, NEG_INF))
    dm_ref[0, 1] = (_bias_value(a - b + TK, relb_ref, h) - far) * inv_scale
    dm_ref[0, 2] = jnp.where(b > a, 0.0, NEG_INF)


def _bias_tables(rel_bias):
    return pl.pallas_call(
        _bias_tables_kernel,
        out_shape=(jax.ShapeDtypeStruct((N_KV_GROUPS, NQ, LANES, GROUP_SIZE * TQ), F32),
                   jax.ShapeDtypeStruct((N_KV_GROUPS, 3, TK, GROUP_SIZE * TQ), F32)),
        grid=(N_NSA_HEADS,),
        in_specs=[pl.BlockSpec(memory_space=pltpu.SMEM)],
        out_specs=(pl.BlockSpec((1, NQ, LANES, TQ), lambda h: (h // GROUP_SIZE, 0, 0, h % GROUP_SIZE)),
                   pl.BlockSpec((1, 3, TK, TQ), lambda h: (h // GROUP_SIZE, 0, 0, h % GROUP_SIZE))),
        compiler_params=pltpu.CompilerParams(dimension_semantics=("arbitrary",),
                                             vmem_limit_bytes=VMEM_LIMIT),
    )(rel_bias)


def _row_rms(x, w):
    return x * lax.rsqrt(jnp.mean(x * x, axis=-1, keepdims=True) + EPS) * w


N_NSA_INPUTS = 17
N_NSA_SCRATCH = 7


def _nsa_kernel(*refs, n_casts):
    (q_ref, k0_ref, v0_ref, k1_ref, v1_ref, k2_ref, v2_ref, gate_ref,
     qnw_ref, knw_ref, pek_ref, pev_ref, wck_ref, wcv_ref, biasc_ref, dm_ref, c2st_ref) = refs[:N_NSA_INPUTS]
    side_a_ref, side_bt_ref = refs[N_NSA_INPUTS:N_NSA_INPUTS + 2]
    n_in = N_NSA_INPUTS + 2 + n_casts
    cast_in = refs[N_NSA_INPUTS + 2:n_in]
    o_ref, side_o_ref = refs[n_in:n_in + 2]
    cast_out = refs[n_in + 2:n_in + 2 + n_casts]
    kc_s, vct_s, kaug_s, kw_s, vst_s, vwt_s, gt_s = refs[len(refs) - N_NSA_SCRATCH:]
    qt = pl.program_id(2)
    cols_all = GROUP_SIZE * TQ
    side_half = side_bt_ref.shape[0] // 2

    @pl.when(qt == 0)
    def _prologue():
        kaug_s[:, :HEAD_DIM] = _row_rms(k1_ref[...], knw_ref[1:2, :]).astype(BF16)
        jblk = lax.broadcasted_iota(jnp.int32, (SEQ, LANES), 0) // SEL_BLOCK
        lane = lax.broadcasted_iota(jnp.int32, (SEQ, LANES), 1)
        kaug_s[:, HEAD_DIM:] = jnp.where(jblk == lane, 1.0, 0.0).astype(BF16)
        kw_s[...] = _row_rms(k2_ref[...], knw_ref[2:3, :]).astype(BF16)
        for t in range(SEQ // TK):
            vst_s[t] = v1_ref[t * TK:(t + 1) * TK, :].T.astype(BF16)
            vwt_s[t] = v2_ref[t * TK:(t + 1) * TK, :].T.astype(BF16)

        def compress(x_ref, pe_ref, w_ref):
            acc_lo = jnp.zeros((LANES, HEAD_DIM), F32)
            acc_hi = jnp.zeros((LANES, HEAD_DIM), F32)
            for j in range(CMP_STRIDE):
                y = x_ref[pl.ds(j, LANES, stride=CMP_STRIDE), :]
                acc_lo += _dot((y + pe_ref[j:j + 1, :]).astype(BF16),
                               w_ref[j * HEAD_DIM:(j + 1) * HEAD_DIM, :])
                jh = CMP_STRIDE + j
                acc_hi += _dot((y + pe_ref[jh:jh + 1, :]).astype(BF16),
                               w_ref[jh * HEAD_DIM:(jh + 1) * HEAD_DIM, :])
            return acc_lo + pltpu.roll(acc_hi, LANES - 1, 0)

        kc_s[...] = _row_rms(compress(k0_ref, pek_ref, wck_ref), knw_ref[0:1, :]).astype(BF16)
        vct_s[...] = compress(v0_ref, pev_ref, wcv_ref).T.astype(BF16)

    side_o_ref[:, :side_half] = _dot_nt(side_a_ref[...], side_bt_ref[:side_half, :])
    _cast_slabs(cast_in + cast_out, n_casts)

    t0 = qt * TQ
    q = q_ref[...]
    qts = []
    for r in range(GROUP_SIZE):
        qts.append(_row_rms(q[:, r * HEAD_DIM:(r + 1) * HEAD_DIM], qnw_ref[...]).T)
    qt_stack = jnp.concatenate(qts, axis=1).astype(BF16)

    sc = _dot(kc_s[...], qt_stack) * SCALE + biasc_ref[0, 0]
    tpos = t0 + (lax.broadcasted_iota(jnp.int32, (LANES, cols_all), 1) & (TQ - 1))
    nrow = lax.broadcasted_iota(jnp.int32, (LANES, cols_all), 0)
    maskc = tpos >= CMP_STRIDE * nrow + (CMP_BLOCK - 1)
    scm = jnp.where(maskc, sc, NEG_INF)
    mc = jnp.max(scm, axis=0, keepdims=True)
    ec = jnp.where(maskc, jnp.exp(scm - mc), 0.0)
    lc = jnp.sum(ec, axis=0, keepdims=True)
    pc = ec * (1.0 / jnp.where(lc > 0.0, lc, 1.0))
    o_c = _dot(vct_s[...], pc.astype(BF16))

    psum = pc[:, 0:TQ] + pc[:, TQ:2 * TQ] + pc[:, 2 * TQ:3 * TQ] + pc[:, 3 * TQ:4 * TQ]
    p_hi = psum.astype(BF16)
    p_lo = (psum - p_hi.astype(F32)).astype(BF16)
    imp = (_dot(c2st_ref[...], p_hi) + _dot(c2st_ref[...], p_lo))[0:N_SEL]
    tq_pos = t0 + lax.broadcasted_iota(jnp.int32, (N_SEL, TQ), 1)
    sidx = lax.broadcasted_iota(jnp.int32, (N_SEL, TQ), 0)
    cur = tq_pos >> 6
    score = jnp.where(sidx <= cur, imp, NEG_INF)
    for forced_blk in (cur - 1, cur, jnp.zeros_like(cur)):
        score = jnp.where(sidx == forced_blk, FORCE_SCORE, score)
    rank = jnp.zeros((N_SEL, TQ), F32)
    for sp in range(N_SEL):
        row = score[sp:sp + 1, :]
        tie = jnp.where(sidx > sp, 1.0, 0.0)
        rank = rank + jnp.where(row > score, 1.0, jnp.where(row == score, tie, 0.0))
    selb = jnp.where(rank < TOP_N, 0.0, NEG_INF)
    selb = jnp.concatenate([selb, jnp.zeros((HEAD_DIM - N_SEL, TQ), F32)], axis=0).astype(BF16)
    qt_aug = jnp.concatenate([qt_stack, jnp.concatenate([selb] * GROUP_SIZE, axis=1)], axis=0)

    m0 = jnp.full((1, cols_all), NEG_INF, F32)
    l0 = jnp.zeros((1, cols_all), F32)
    a0 = jnp.zeros((HEAD_DIM, cols_all), F32)
    exp2_scale = SCALE * math.log2(math.e)

    def online(carry, tiles):
        m, l, acc = carry
        m_new = m
        for u, _ in tiles:
            m_new = jnp.maximum(m_new, jnp.max(u, axis=0, keepdims=True))
        alpha = jnp.exp2((m - m_new) * exp2_scale)
        l = alpha * l
        acc = alpha * acc
        for u, vt in tiles:
            p = jnp.exp2((u - m_new) * exp2_scale)
            l = l + jnp.sum(p, axis=0, keepdims=True)
            acc = acc + _dot(vt, p.astype(BF16))
        return m_new, l, acc

    def key_rows(kt):
        return pl.ds(pl.multiple_of(kt * TK, TK), TK)

    def sel_tile(kt):
        return _dot(kaug_s[key_rows(kt), :], qt_aug), vst_s[kt]

    n_far = jnp.maximum(qt - 1, 0)
    carry = lax.fori_loop(0, n_far >> 1,
                          lambda i, c: online(c, [sel_tile(2 * i), sel_tile(2 * i + 1)]), (m0, l0, a0))
    carry = lax.fori_loop(0, n_far & 1, lambda i, c: online(c, [sel_tile(n_far - 1)]), carry)
    kt1 = jnp.maximum(qt - 1, 0)
    u1, vt1 = sel_tile(kt1)
    u0, vt0 = sel_tile(qt)
    carry = online(carry, [(u1 + dm_ref[0, 1] + jnp.where(qt >= 1, 0.0, NEG_INF), vt1),
                           (u0 + dm_ref[0, 0], vt0)])
    _, l_s, acc_s = carry
    o_s = acc_s * (1.0 / l_s)

    side_o_ref[:, side_half:] = _dot_nt(side_a_ref[...], side_bt_ref[side_half:, :])

    tiles = []
    for dist, kind in ((2, 2), (1, 1), (0, 0)):
        kt = jnp.maximum(qt - dist, 0)
        u = _dot(kw_s[key_rows(kt), :], qt_stack) + dm_ref[0, kind]
        if dist > 0:
            u = u + jnp.where(qt >= dist, 0.0, NEG_INF)
        tiles.append((u, vwt_s[kt]))
    _, l_w, acc_w = online((m0, l0, a0), tiles)
    o_w = acc_w * (1.0 / l_w)

    gt_s[...] = jax.nn.sigmoid(gate_ref[...]).T
    head0 = pl.program_id(1) * GROUP_SIZE
    for r in range(GROUP_SIZE):
        cs = slice(r * TQ, (r + 1) * TQ)
        c = (head0 + r) * N_BRANCH
        o = (gt_s[pl.ds(c, 1), :] * o_c[:, cs] + gt_s[pl.ds(c + 1, 1), :] * o_s[:, cs]
             + gt_s[pl.ds(c + 2, 1), :] * o_w[:, cs])
        o_ref[:, r * HEAD_DIM:(r + 1) * HEAD_DIM] = o.T.astype(o_ref.dtype)


def _cmp_to_sel_t():
    ss = np.arange(LANES)[:, None] * SEL_BLOCK
    cs = np.arange(LANES)[None, :] * CMP_STRIDE
    ov = np.clip(np.minimum(cs + CMP_BLOCK, ss + SEL_BLOCK) - np.maximum(cs, ss), 0, None)
    m = ov.astype(np.float32) / np.float32(CMP_BLOCK)
    m[N_SEL:, :] = 0.0
    m[:, N_CMP:] = 0.0
    return jnp.asarray(m, BF16)


def _nsa_attention(proj, gate_logits, batch, q_norm_w, k_norm_w, pe_k, pe_v, w_cmp_k, w_cmp_v, biasc, dm,
                   side_a, side_bt, side_tm=1024, side_tn=512, casts=()):
    kvb = COL_KV // LANES

    def kv_spec(branch, which):
        base = kvb + (branch * 2 + which) * N_KV_GROUPS
        return pl.BlockSpec((SEQ, LANES), lambda b, g, qt: (b, base + g))

    full = lambda shape: pl.BlockSpec(shape, lambda b, g, qt: (0,) * len(shape))
    grid = (batch, N_KV_GROUPS, NQ)
    c_in, c_out, c_shapes = _cast_specs(casts, grid)
    side_m, side_k = side_a.shape
    side_n = side_bt.shape[0]
    n_j = side_n // side_tn
    assert (side_m // side_tm) * n_j == math.prod(grid)
    step = lambda b, g, qt: (b * N_KV_GROUPS + g) * NQ + qt
    side_in = [pl.BlockSpec((side_tm, side_k), lambda b, g, qt: (step(b, g, qt) // n_j, 0),
                            pipeline_mode=pl.Buffered(1)),
               pl.BlockSpec((side_tn, side_k), lambda b, g, qt: (step(b, g, qt) % n_j, 0))]
    side_out = pl.BlockSpec((side_tm, side_tn), lambda b, g, qt: (step(b, g, qt) // n_j, step(b, g, qt) % n_j))
    return pl.pallas_call(
        functools.partial(_nsa_kernel, n_casts=len(casts)),
        out_shape=[jax.ShapeDtypeStruct((batch * SEQ, NSA_WIDTH), BF16),
                   jax.ShapeDtypeStruct((side_m, side_n), F32)] + c_shapes,
        grid=grid,
        in_specs=[pl.BlockSpec((TQ, GROUP_SIZE * HEAD_DIM), lambda b, g, qt: (b * NQ + qt, g)),
                  kv_spec(0, 0), kv_spec(0, 1), kv_spec(1, 0), kv_spec(1, 1),
                  kv_spec(2, 0), kv_spec(2, 1),
                  pl.BlockSpec((TQ, LANES), lambda b, g, qt: (b * NQ + qt, 0)),
                  full((1, HEAD_DIM)), full((N_BRANCH, HEAD_DIM)),
                  full((CMP_BLOCK, HEAD_DIM)), full((CMP_BLOCK, HEAD_DIM)),
                  full((CMP_BLOCK * HEAD_DIM, HEAD_DIM)), full((CMP_BLOCK * HEAD_DIM, HEAD_DIM)),
                  pl.BlockSpec((1, 1, LANES, GROUP_SIZE * TQ), lambda b, g, qt: (g, qt, 0, 0)),
                  pl.BlockSpec((1, 3, TK, GROUP_SIZE * TQ), lambda b, g, qt: (g, 0, 0, 0),
                               pipeline_mode=pl.Buffered(1)),
                  full((LANES, LANES))] + side_in + c_in,
        out_specs=[pl.BlockSpec((TQ, GROUP_SIZE * HEAD_DIM), lambda b, g, qt: (b * NQ + qt, g)),
                   side_out] + c_out,
        scratch_shapes=[pltpu.VMEM((LANES, HEAD_DIM), BF16),
                        pltpu.VMEM((HEAD_DIM, LANES), BF16),
                        pltpu.VMEM((SEQ, 2 * HEAD_DIM), BF16),
                        pltpu.VMEM((SEQ, HEAD_DIM), BF16),
                        pltpu.VMEM((SEQ // TK, HEAD_DIM, TK), BF16),
                        pltpu.VMEM((SEQ // TK, HEAD_DIM, TK), BF16),
                        pltpu.VMEM((LANES, TQ), F32)],
        compiler_params=pltpu.CompilerParams(
            dimension_semantics=("arbitrary", "arbitrary", "arbitrary"),
            vmem_limit_bytes=VMEM_LIMIT),
    )(proj, proj, proj, proj, proj, proj, proj, gate_logits,
      q_norm_w.reshape(1, HEAD_DIM), k_norm_w, pe_k, pe_v,
      w_cmp_k.astype(BF16), w_cmp_v.astype(BF16), biasc, dm, _cmp_to_sel_t(),
      side_a, side_bt, *[c[0] for c in casts])


def _ret_kernel(lg_ref, q_ref, k_ref, v_ref, g_ref, cos_ref, sin_ref, gnw_ref, *rest, n_casts):
    o_ref = rest[n_casts]
    state_ref = rest[-1]
    _cast_slabs(rest[:n_casts] + rest[n_casts + 1:-1], n_casts)
    h = pl.program_id(1)
    lg = lg_ref[h]
    c_len = RET_CHUNK
    ii = lax.broadcasted_iota(jnp.int32, (c_len, c_len), 0)
    jj = lax.broadcasted_iota(jnp.int32, (c_len, c_len), 1)
    diff = (ii - jj).astype(F32)
    decay_in = jnp.where(diff >= 0.0, jnp.exp(jnp.maximum(diff, 0.0) * lg), 0.0)
    icol = lax.broadcasted_iota(jnp.int32, (c_len, 1), 0).astype(F32)
    xi = jnp.exp((icol + 1.0) * lg)
    zeta = jnp.exp((c_len - 1.0 - icol) * lg)
    chunk_decay = jnp.exp(jnp.full((1, RET_VAL_DIM), float(c_len), F32) * lg)
    half = RET_KEY_DIM // 2
    state_ref[...] = jnp.zeros_like(state_ref)

    def body(c, _):
        rows = pl.ds(pl.multiple_of(c * c_len, c_len), c_len)
        cos = cos_ref[rows, :]
        sin = sin_ref[rows, :]

        def rot(x):
            x1, x2 = x[:, :half], x[:, half:]
            return jnp.concatenate([x1 * cos - x2 * sin, x2 * cos + x1 * sin], axis=-1)

        qr = rot(q_ref[rows, :])
        kr = rot(k_ref[rows, :]) * (RET_KEY_DIM ** -0.5)
        vb = v_ref[rows, :].astype(BF16)
        scores = _dot_nt(qr.astype(BF16), kr.astype(BF16)) * decay_in
        inner = _dot(scores.astype(BF16), vb)
        st = state_ref[...]
        cross = _dot((qr * xi).astype(BF16), st.astype(BF16))
        kv = _dot((kr * zeta).T.astype(BF16), vb)
        state_ref[...] = chunk_decay * st + kv
        y = inner + cross
        y = y * lax.rsqrt(jnp.mean(y * y, axis=-1, keepdims=True) + EPS) * gnw_ref[...]
        g = g_ref[rows, :]
        o_ref[rows, :] = (g * jax.nn.sigmoid(g) * y).astype(o_ref.dtype)
        return 0

    lax.fori_loop(0, SEQ // c_len, body, 0)


def _retention(proj, batch, gn_w, casts=()):
    pos = np.arange(SEQ, dtype=np.float64)
    half = RET_KEY_DIM // 2
    inv = np.exp(-np.linspace(0.0, 1.0, half) * math.log(ROPE_BASE)).astype(np.float32)
    ang = pos.astype(np.float32)[:, None] * inv[None, :]
    cos = jnp.asarray(np.cos(ang.astype(np.float64)), F32)
    sin = jnp.asarray(np.sin(ang.astype(np.float64)), F32)
    log_gamma = jnp.asarray(np.log(1.0 - 2.0 ** (-5.0 - np.arange(N_RET_HEADS))), F32)
    w = RET_KEY_DIM

    def col_spec(col0):
        return pl.BlockSpec((SEQ, w), lambda b, h: (b, col0 // w + h))

    grid = (batch, N_RET_HEADS)
    c_in, c_out, c_shapes = _cast_specs(casts, grid)
    return pl.pallas_call(
        functools.partial(_ret_kernel, n_casts=len(casts)),
        out_shape=[jax.ShapeDtypeStruct((batch * SEQ, RET_WIDTH), BF16)] + c_shapes,
        grid=grid,
        in_specs=[pl.BlockSpec(memory_space=pltpu.SMEM),
                  col_spec(COL_QR), col_spec(COL_KR), col_spec(COL_VR), col_spec(COL_GR),
                  pl.BlockSpec((SEQ, half), lambda b, h: (0, 0)),
                  pl.BlockSpec((SEQ, half), lambda b, h: (0, 0)),
                  pl.BlockSpec((1, RET_VAL_DIM), lambda b, h: (0, h))] + c_in,
        out_specs=[pl.BlockSpec((SEQ, RET_VAL_DIM), lambda b, h: (b, h))] + c_out,
        scratch_shapes=[pltpu.VMEM((RET_KEY_DIM, RET_VAL_DIM), F32)],
        compiler_params=pltpu.CompilerParams(dimension_semantics=("arbitrary", "arbitrary"),
                                             vmem_limit_bytes=VMEM_LIMIT),
    )(log_gamma, proj, proj, proj, proj, cos, sin, gn_w.reshape(1, RET_WIDTH), *[c[0] for c in casts])


def _layer(x, norm1_w, w_in, nsa_q_norm_w, nsa_k_norm_w, cmp_pe_k, cmp_pe_v, w_cmp_k, w_cmp_v,
           rel_bias, ret_gn_w, w_out, norm2_w, w_gate, w_up, w_down):
    batch = x.shape[0]
    xf = x.reshape(batch * SEQ, D_MODEL)
    h = _rmsnorm(xf, norm1_w)
    w_in_t = w_in.T
    gate0 = W_IN_NSA_COLS
    gate1 = gate0 + W_IN_GATE_COLS
    ret_cols = 4 * RET_WIDTH
    proj_nsa, w_ret_t = _in_projection(h, w_in_t, 0, W_IN_NSA_COLS, 1024, tm=512,
                                       casts=((w_in_t, gate1, ret_cols, 64),))
    gate_logits, = _in_projection(h, w_in_t, gate0, LANES, LANES)
    biasc, dm = _bias_tables(rel_bias)
    n_steps = batch * N_KV_GROUPS * NQ
    o_nsa, proj_ret, w_gate_b = _nsa_attention(
        proj_nsa, gate_logits, batch, nsa_q_norm_w, nsa_k_norm_w, cmp_pe_k, cmp_pe_v,
        w_cmp_k, w_cmp_v, biasc, dm, h, w_ret_t, casts=((w_gate, 0, D_MODEL, n_steps),))
    o_ret, w_out_b = _retention(proj_ret, batch, ret_gn_w,
                                casts=((w_out, 0, D_MODEL, batch * N_RET_HEADS),))
    x1, w_up_b = _out_projection(o_nsa, o_ret, w_out_b, xf, casts=((w_up, 0, D_MODEL, 64),))
    hf = _rmsnorm(x1, norm2_w)
    mid, w_down_b = _ffn_up(hf, w_gate_b, w_up_b, casts=((w_down, 0, D_FF, 344),))
    out = _ffn_down(mid, w_down_b, x1)
    return out.reshape(batch, SEQ, D_MODEL)


def kernel(x, norm1_w, w_in, nsa_q_norm_w, nsa_k_norm_w, cmp_pe_k, cmp_pe_v, w_cmp_k, w_cmp_v,
           rel_bias, ret_gn_w, w_out, norm2_w, w_gate, w_up, w_down):
    for l in range(norm1_w.shape[0]):
        x = _layer(x, norm1_w[l], w_in[l], nsa_q_norm_w[l], nsa_k_norm_w[l], cmp_pe_k[l], cmp_pe_v[l],
                   w_cmp_k[l], w_cmp_v[l], rel_bias, ret_gn_w[l], w_out[l], norm2_w[l],
                   w_gate[l], w_up[l], w_down[l])
    return x
```

```python
import functools
import math

import numpy as np
import jax
import jax.numpy as jnp
from jax import lax
from jax.experimental import pallas as pl
from jax.experimental.pallas import tpu as pltpu

F32 = jnp.float32
BF16 = jnp.bfloat16

D_MODEL = 4096
SEQ = 2048
HEAD_DIM = 128
N_NSA_HEADS = 16
N_KV_GROUPS = 4
GROUP_SIZE = 4
N_BRANCH = 3
CMP_BLOCK = 32
CMP_STRIDE = 16
SEL_BLOCK = 64
TOP_N = 16
WINDOW = 512
RET_KEY_DIM = 256
RET_VAL_DIM = 256
N_RET_HEADS = 8
RET_CHUNK = 128
ROPE_BASE = 10000.0
NSA_WIDTH = 2048
RET_WIDTH = 2048
D_FF = 11008
NUM_BUCKETS = 32
MAX_DISTANCE = 128
EPS = 1e-6
NEG_INF = -1e30
FORCE_SCORE = 1e9
N_SEL = SEQ // SEL_BLOCK
N_CMP = (SEQ - CMP_BLOCK) // CMP_STRIDE + 1
SCALE = HEAD_DIM ** -0.5

LANES = 128
VMEM_LIMIT = 56 * 1024 * 1024

TQ = 256
TK = 256
NQ = SEQ // TQ
RET_HEADS_PER_STEP = 2

W_IN_NSA_COLS = NSA_WIDTH + N_BRANCH * 2 * N_KV_GROUPS * HEAD_DIM
W_IN_GATE_COLS = N_NSA_HEADS * N_BRANCH
COL_KV = NSA_WIDTH
COL_QR = 0
COL_KR = 2048
COL_VR = 4096
COL_GR = 6144


def _bucket_thresholds():
    rel = np.arange(0, 4 * MAX_DISTANCE)
    max_exact = NUM_BUCKETS // 2
    nf = np.maximum(rel, 1).astype(np.float64)
    large = max_exact + np.floor(np.log(nf / max_exact) / math.log(MAX_DISTANCE / max_exact)
                                 * (NUM_BUCKETS - max_exact)).astype(np.int64)
    large = np.minimum(large, NUM_BUCKETS - 1)
    b = np.where(rel < max_exact, rel, large)
    return [int(np.argmax(b >= k)) for k in range(NUM_BUCKETS)]


_THR = _bucket_thresholds()


def _dot(a, b):
    return jnp.dot(a, b, preferred_element_type=F32)


def _dot_nt(a, b):
    return lax.dot_general(a, b, (((1,), (1,)), ((), ())), preferred_element_type=F32)


def _rms_kernel(x_ref, w_ref, o_ref):
    x = x_ref[...]
    y = x * lax.rsqrt(jnp.mean(x * x, axis=-1, keepdims=True) + EPS)
    o_ref[...] = (y * w_ref[...]).astype(o_ref.dtype)


def _rmsnorm(x, w, tm=256):
    m, d = x.shape
    return pl.pallas_call(
        _rms_kernel,
        out_shape=jax.ShapeDtypeStruct((m, d), BF16),
        grid=(m // tm,),
        in_specs=[pl.BlockSpec((tm, d), lambda i: (i, 0)),
                  pl.BlockSpec((1, d), lambda i: (0, 0))],
        out_specs=pl.BlockSpec((tm, d), lambda i: (i, 0)),
        compiler_params=pltpu.CompilerParams(dimension_semantics=("arbitrary",),
                                             vmem_limit_bytes=VMEM_LIMIT),
    )(x, w.reshape(1, d))


def _cast_spec(grid, row0, n_rows, n_cols, n_slabs):
    slab = n_rows // n_slabs
    sub = 8
    n_steps = math.prod(grid)
    assert slab * n_slabs == n_rows and slab % 16 == 0 and n_cols % LANES == 0
    assert row0 % sub == 0 and n_slabs <= n_steps

    def slab_index(ids):
        step = ids[0]
        for dim, idx in zip(grid[1:], ids[1:]):
            step = step * dim + idx
        return step if n_slabs == n_steps else jnp.minimum(step, n_slabs - 1)

    in_spec = pl.BlockSpec((pl.Element(slab), pl.Element(n_cols)),
                           lambda *ids: ((row0 // sub + slab_index(ids) * (slab // sub)) * sub, 0))
    out_spec = pl.BlockSpec((slab, n_cols), lambda *ids: (slab_index(ids), 0))
    return in_spec, out_spec, jax.ShapeDtypeStruct((n_rows, n_cols), BF16)


def _cast_specs(casts, grid):
    specs = [_cast_spec(grid, row0, n_rows, arr.shape[1], n_slabs) for arr, row0, n_rows, n_slabs in casts]
    return [s[0] for s in specs], [s[1] for s in specs], [s[2] for s in specs]


def _cast_slabs(refs, n_casts):
    for src, dst in zip(refs[:n_casts], refs[len(refs) - n_casts:]):
        dst[...] = src[...].astype(dst.dtype)


def _mm_kernel(a_ref, b_ref, *rest, n_casts):
    o_ref = rest[n_casts]
    o_ref[...] = _dot(a_ref[...], b_ref[...]).astype(o_ref.dtype)
    _cast_slabs(rest[:n_casts] + rest[n_casts + 1:], n_casts)


def _matmul(a, b, out_dtype, tm, tn, casts=()):
    m, k = a.shape
    _, n = b.shape
    grid = (m // tm, n // tn)
    c_in, c_out, c_shapes = _cast_specs(casts, grid)
    outs = pl.pallas_call(
        functools.partial(_mm_kernel, n_casts=len(casts)),
        out_shape=[jax.ShapeDtypeStruct((m, n), out_dtype)] + c_shapes,
        grid=grid,
        in_specs=[pl.BlockSpec((tm, k), lambda i, j: (i, 0)),
                  pl.BlockSpec((k, tn), lambda i, j: (0, j))] + c_in,
        out_specs=[pl.BlockSpec((tm, tn), lambda i, j: (i, j))] + c_out,
        compiler_params=pltpu.CompilerParams(dimension_semantics=("arbitrary", "arbitrary"),
                                             vmem_limit_bytes=VMEM_LIMIT),
    )(a, b, *[c[0] for c in casts])
    return outs


def _inproj_kernel(a_ref, wt_ref, *rest, n_casts):
    o_ref = rest[n_casts]
    wbf_ref = rest[-1]

    @pl.when(pl.program_id(1) == 0)
    def _():
        wbf_ref[...] = wt_ref[...].astype(BF16)

    o_ref[...] = _dot_nt(a_ref[...], wbf_ref[...])
    _cast_slabs(rest[:n_casts] + rest[n_casts + 1:-1], n_casts)


def _in_projection(a, w_t, row0, n_cols, tn, tm=1024, casts=()):
    m, k = a.shape
    grid = (n_cols // tn, m // tm)
    c_in, c_out, c_shapes = _cast_specs(casts, grid)
    sub = 8
    assert row0 % sub == 0 and tn % sub == 0
    return pl.pallas_call(
        functools.partial(_inproj_kernel, n_casts=len(casts)),
        out_shape=[jax.ShapeDtypeStruct((m, n_cols), F32)] + c_shapes,
        grid=grid,
        in_specs=[pl.BlockSpec((tm, k), lambda j, i: (i, 0)),
                  pl.BlockSpec((pl.Element(tn), pl.Element(k)),
                               lambda j, i: ((row0 // sub + j * (tn // sub)) * sub, 0),
                               pipeline_mode=pl.Buffered(1))] + c_in,
        out_specs=[pl.BlockSpec((tm, tn), lambda j, i: (i, j))] + c_out,
        scratch_shapes=[pltpu.VMEM((tn, k), BF16)],
        compiler_params=pltpu.CompilerParams(dimension_semantics=("arbitrary", "arbitrary"),
                                             vmem_limit_bytes=VMEM_LIMIT),
    )(a, w_t, *[c[0] for c in casts])


def _outproj_kernel(a1_ref, a2_ref, w1_ref, w2_ref, x_ref, *rest, n_casts):
    o_ref = rest[n_casts]
    acc = _dot(a1_ref[...], w1_ref[...]) + _dot(a2_ref[...], w2_ref[...])
    o_ref[...] = x_ref[...] + acc
    _cast_slabs(rest[:n_casts] + rest[n_casts + 1:], n_casts)


def _out_projection(o_nsa, o_ret, w_out, x, tm=1024, tn=512, casts=()):
    m = x.shape[0]
    kh = NSA_WIDTH
    grid = (m // tm, D_MODEL // tn)
    c_in, c_out, c_shapes = _cast_specs(casts, grid)
    return pl.pallas_call(
        functools.partial(_outproj_kernel, n_casts=len(casts)),
        out_shape=[jax.ShapeDtypeStruct((m, D_MODEL), F32)] + c_shapes,
        grid=grid,
        in_specs=[pl.BlockSpec((tm, kh), lambda i, j: (i, 0)),
                  pl.BlockSpec((tm, kh), lambda i, j: (i, 0)),
                  pl.BlockSpec((kh, tn), lambda i, j: (0, j)),
                  pl.BlockSpec((kh, tn), lambda i, j: (1, j)),
                  pl.BlockSpec((tm, tn), lambda i, j: (i, j))] + c_in,
        out_specs=[pl.BlockSpec((tm, tn), lambda i, j: (i, j))] + c_out,
        compiler_params=pltpu.CompilerParams(dimension_semantics=("arbitrary", "arbitrary"),
                                             vmem_limit_bytes=VMEM_LIMIT),
    )(o_nsa, o_ret, w_out, w_out, x, *[c[0] for c in casts])


def _ffn_up_kernel(h_ref, wg_ref, wu_ref, *rest, n_casts):
    o_ref = rest[n_casts]
    h = h_ref[...]
    g = _dot(h, wg_ref[...])
    u = _dot(h, wu_ref[...])
    o_ref[...] = (g * jax.nn.sigmoid(g) * u).astype(o_ref.dtype)
    _cast_slabs(rest[:n_casts] + rest[n_casts + 1:], n_casts)


def _ffn_up(h, w_gate, w_up, tm=1024, tn=256, casts=()):
    m, k = h.shape
    n = w_gate.shape[1]
    grid = (m // tm, n // tn)
    c_in, c_out, c_shapes = _cast_specs(casts, grid)
    return pl.pallas_call(
        functools.partial(_ffn_up_kernel, n_casts=len(casts)),
        out_shape=[jax.ShapeDtypeStruct((m, n), BF16)] + c_shapes,
        grid=grid,
        in_specs=[pl.BlockSpec((tm, k), lambda i, j: (i, 0)),
                  pl.BlockSpec((k, tn), lambda i, j: (0, j)),
                  pl.BlockSpec((k, tn), lambda i, j: (0, j))] + c_in,
        out_specs=[pl.BlockSpec((tm, tn), lambda i, j: (i, j))] + c_out,
        compiler_params=pltpu.CompilerParams(dimension_semantics=("arbitrary", "arbitrary"),
                                             vmem_limit_bytes=VMEM_LIMIT),
    )(h, w_gate, w_up, *[c[0] for c in casts])


def _ffn_down_kernel(a_ref, b_ref, x_ref, o_ref, acc_ref):
    kk = pl.program_id(2)
    last = pl.num_programs(2) - 1

    @pl.when(kk == 0)
    def _():
        acc_ref[...] = x_ref[...] + _dot(a_ref[...], b_ref[...])

    @pl.when(jnp.logical_and(kk > 0, kk < last))
    def _():
        acc_ref[...] += _dot(a_ref[...], b_ref[...])

    @pl.when(kk == last)
    def _():
        o_ref[...] = acc_ref[...] + _dot(a_ref[...], b_ref[...])


def _ffn_down(a, w_down, x, tm=1024, tn=512, tk=5504):
    m, k = a.shape
    n = w_down.shape[1]
    assert k // tk >= 2
    return pl.pallas_call(
        _ffn_down_kernel,
        out_shape=jax.ShapeDtypeStruct((m, n), F32),
        grid=(m // tm, n // tn, k // tk),
        in_specs=[pl.BlockSpec((tm, tk), lambda i, j, kk: (i, kk)),
                  pl.BlockSpec((tk, tn), lambda i, j, kk: (kk, j)),
                  pl.BlockSpec((tm, tn), lambda i, j, kk: (i, j))],
        out_specs=pl.BlockSpec((tm, tn), lambda i, j, kk: (i, j)),
        scratch_shapes=[pltpu.VMEM((tm, tn), F32)],
        compiler_params=pltpu.CompilerParams(
            dimension_semantics=("arbitrary", "arbitrary", "arbitrary"),
            vmem_limit_bytes=VMEM_LIMIT),
    )(a, w_down, x)


def _bias_value(rel, relb_ref, h):
    val = jnp.full(rel.shape, relb_ref[0, h], F32)
    for k in range(1, NUM_BUCKETS):
        val = jnp.where(rel >= _THR[k], relb_ref[k, h], val)
    return val


def _bias_tables_kernel(relb_ref, biasc_ref, dm_ref):
    h = pl.program_id(0)
    n = lax.broadcasted_iota(jnp.int32, (LANES, TQ), 0)
    a_c = lax.broadcasted_iota(jnp.int32, (LANES, TQ), 1)
    for qt in range(NQ):
        rel_c = (qt * TQ + a_c) - (CMP_STRIDE * n + CMP_BLOCK - 1)
        biasc_ref[0, qt] = _bias_value(rel_c, relb_ref, h)
    b = lax.broadcasted_iota(jnp.int32, (TK, TQ), 0)
    a = lax.broadcasted_iota(jnp.int32, (TK, TQ), 1)
    far = relb_ref[NUM_BUCKETS - 1, h]
    inv_scale = 1.0 / SCALE
    dm_ref[0, 0] = ((_bias_value(a - b, relb_ref, h) - far) * inv_scale
                    + jnp.where(a >= b, 0.0, NEG_INF))
    dm_ref[0, 1] = (_bias_value(a - b + TK, relb_ref, h) - far) * inv_scale
    dm_ref[0, 2] = jnp.where(b > a, 0.0, NEG_INF)


def _bias_tables(rel_bias):
    return pl.pallas_call(
        _bias_tables_kernel,
        out_shape=(jax.ShapeDtypeStruct((N_KV_GROUPS, NQ, LANES, GROUP_SIZE * TQ), F32),
                   jax.ShapeDtypeStruct((N_KV_GROUPS, 3, TK, GROUP_SIZE * TQ), F32)),
        grid=(N_NSA_HEADS,),
        in_specs=[pl.BlockSpec(memory_space=pltpu.SMEM)],
        out_specs=(pl.BlockSpec((1, NQ, LANES, TQ), lambda h: (h // GROUP_SIZE, 0, 0, h % GROUP_SIZE)),
                   pl.BlockSpec((1, 3, TK, TQ), lambda h: (h // GROUP_SIZE, 0, 0, h % GROUP_SIZE))),
        compiler_params=pltpu.CompilerParams(dimension_semantics=("arbitrary",),
                                             vmem_limit_bytes=VMEM_LIMIT),
    )(rel_bias)


def _row_rms(x, w):
    return x * lax.rsqrt(jnp.mean(x * x, axis=-1, keepdims=True) + EPS) * w


N_NSA_INPUTS = 17
N_NSA_SCRATCH = 7


def _nsa_kernel(*refs, n_casts):
    (q_ref, k0_ref, v0_ref, k1_ref, v1_ref, k2_ref, v2_ref, gate_ref,
     qnw_ref, knw_ref, pek_ref, pev_ref, wck_ref, wcv_ref, biasc_ref, dm_ref, c2st_ref) = refs[:N_NSA_INPUTS]
    side_a_ref, side_bt_ref = refs[N_NSA_INPUTS:N_NSA_INPUTS + 2]
    n_in = N_NSA_INPUTS + 2 + n_casts
    cast_in = refs[N_NSA_INPUTS + 2:n_in]
    o_ref, side_o_ref = refs[n_in:n_in + 2]
    cast_out = refs[n_in + 2:n_in + 2 + n_casts]
    kc_s, vct_s, kaug_s, kw_s, vst_s, vwt_s, gt_s = refs[len(refs) - N_NSA_SCRATCH:]
    qt = pl.program_id(2)
    cols_all = GROUP_SIZE * TQ
    side_half = side_bt_ref.shape[0] // 2

    @pl.when(qt == 0)
    def _prologue():
        kaug_s[:, :HEAD_DIM] = _row_rms(k1_ref[...], knw_ref[1:2, :]).astype(BF16)
        jblk = lax.broadcasted_iota(jnp.int32, (SEQ, LANES), 0) // SEL_BLOCK
        lane = lax.broadcasted_iota(jnp.int32, (SEQ, LANES), 1)
        kaug_s[:, HEAD_DIM:] = jnp.where(jblk == lane, 1.0, 0.0).astype(BF16)
        kw_s[...] = _row_rms(k2_ref[...], knw_ref[2:3, :]).astype(BF16)
        for t in range(SEQ // TK):
            vst_s[t] = v1_ref[t * TK:(t + 1) * TK, :].T.astype(BF16)
            vwt_s[t] = v2_ref[t * TK:(t + 1) * TK, :].T.astype(BF16)

        def compress(x_ref, pe_ref, w_ref):
            acc_lo = jnp.zeros((LANES, HEAD_DIM), F32)
            acc_hi = jnp.zeros((LANES, HEAD_DIM), F32)
            for j in range(CMP_STRIDE):
                y = x_ref[pl.ds(j, LANES, stride=CMP_STRIDE), :]
                acc_lo += _dot((y + pe_ref[j:j + 1, :]).astype(BF16),
                               w_ref[j * HEAD_DIM:(j + 1) * HEAD_DIM, :])
                jh = CMP_STRIDE + j
                acc_hi += _dot((y + pe_ref[jh:jh + 1, :]).astype(BF16),
                               w_ref[jh * HEAD_DIM:(jh + 1) * HEAD_DIM, :])
            return acc_lo + pltpu.roll(acc_hi, LANES - 1, 0)

        kc_s[...] = _row_rms(compress(k0_ref, pek_ref, wck_ref), knw_ref[0:1, :]).astype(BF16)
        vct_s[...] = compress(v0_ref, pev_ref, wcv_ref).T.astype(BF16)

    side_o_ref[:, :side_half] = _dot_nt(side_a_ref[...], side_bt_ref[:side_half, :])
    _cast_slabs(cast_in + cast_out, n_casts)

    t0 = qt * TQ
    q = q_ref[...]
    qts = []
    for r in range(GROUP_SIZE):
        qts.append(_row_rms(q[:, r * HEAD_DIM:(r + 1) * HEAD_DIM], qnw_ref[...]).T)
    qt_stack = jnp.concatenate(qts, axis=1).astype(BF16)

    sc = _dot(kc_s[...], qt_stack) * SCALE + biasc_ref[0, 0]
    tpos = t0 + (lax.broadcasted_iota(jnp.int32, (LANES, cols_all), 1) & (TQ - 1))
    nrow = lax.broadcasted_iota(jnp.int32, (LANES, cols_all), 0)
    maskc = tpos >= CMP_STRIDE * nrow + (CMP_BLOCK - 1)
    scm = jnp.where(maskc, sc, NEG_INF)
    mc = jnp.max(scm, axis=0, keepdims=True)
    ec = jnp.where(maskc, jnp.exp(scm - mc), 0.0)
    lc = jnp.sum(ec, axis=0, keepdims=True)
    pc = ec * (1.0 / jnp.where(lc > 0.0, lc, 1.0))
    o_c = _dot(vct_s[...], pc.astype(BF16))

    psum = pc[:, 0:TQ] + pc[:, TQ:2 * TQ] + pc[:, 2 * TQ:3 * TQ] + pc[:, 3 * TQ:4 * TQ]
    p_hi = psum.astype(BF16)
    p_lo = (psum - p_hi.astype(F32)).astype(BF16)
    imp = (_dot(c2st_ref[...], p_hi) + _dot(c2st_ref[...], p_lo))[0:N_SEL]
    tq_pos = t0 + lax.broadcasted_iota(jnp.int32, (N_SEL, TQ), 1)
    sidx = lax.broadcasted_iota(jnp.int32, (N_SEL, TQ), 0)
    cur = tq_pos >> 6
    score = jnp.where(sidx <= cur, imp, NEG_INF)
    for forced_blk in (cur - 1, cur, jnp.zeros_like(cur)):
        score = jnp.where(sidx == forced_blk, FORCE_SCORE, score)
    rank = jnp.zeros((N_SEL, TQ), F32)
    for sp in range(N_SEL):
        row = score[sp:sp + 1, :]
        tie = jnp.where(sidx > sp, 1.0, 0.0)
        rank = rank + jnp.where(row > score, 1.0, jnp.where(row == score, tie, 0.0))
    selb = jnp.where(rank < TOP_N, 0.0, NEG_INF)
    selb = jnp.concatenate([selb, jnp.zeros((HEAD_DIM - N_SEL, TQ), F32)], axis=0).astype(BF16)
    qt_aug = jnp.concatenate([qt_stack, jnp.concatenate([selb] * GROUP_SIZE, axis=1)], axis=0)

    m0 = jnp.full((1, cols_all), NEG_INF, F32)
    l0 = jnp.zeros((1, cols_all), F32)
    a0 = jnp.zeros((HEAD_DIM, cols_all), F32)
    exp2_scale = SCALE * math.log2(math.e)

    def online(carry, tiles):
        m, l, acc = carry
        m_new = m
        for u, _ in tiles:
            m_new = jnp.maximum(m_new, jnp.max(u, axis=0, keepdims=True))
        alpha = jnp.exp2((m - m_new) * exp2_scale)
        l = alpha * l
        acc = alpha * acc
        for u, vt in tiles:
            p = jnp.exp2((u - m_new) * exp2_scale)
            l = l + jnp.sum(p, axis=0, keepdims=True)
            acc = acc + _dot(vt, p.astype(BF16))
        return m_new, l, acc

    def key_rows(kt):
        return pl.ds(pl.multiple_of(kt * TK, TK), TK)

    def sel_tile(kt):
        return _dot(kaug_s[key_rows(kt), :], qt_aug), vst_s[kt]

    n_far = jnp.maximum(qt - 1, 0)
    carry = lax.fori_loop(0, n_far >> 1,
                          lambda i, c: online(c, [sel_tile(2 * i), sel_tile(2 * i + 1)]), (m0, l0, a0))
    carry = lax.fori_loop(0, n_far & 1, lambda i, c: online(c, [sel_tile(n_far - 1)]), carry)
    kt1 = jnp.maximum(qt - 1, 0)
    u1, vt1 = sel_tile(kt1)
    u0, vt0 = sel_tile(qt)
    carry = online(carry, [(u1 + dm_ref[0, 1] + jnp.where(qt >= 1, 0.0, NEG_INF), vt1),
                           (u0 + dm_ref[0, 0], vt0)])
    _, l_s, acc_s = carry
    o_s = acc_s * (1.0 / l_s)

    side_o_ref[:, side_half:] = _dot_nt(side_a_ref[...], side_bt_ref[side_half:, :])

    tiles = []
    for dist, kind in ((2, 2), (1, 1), (0, 0)):
        kt = jnp.maximum(qt - dist, 0)
        u = _dot(kw_s[key_rows(kt), :], qt_stack) + dm_ref[0, kind]
        if dist > 0:
            u = u + jnp.where(qt >= dist, 0.0, NEG_INF)
        tiles.append((u, vwt_s[kt]))
    _, l_w, acc_w = online((m0, l0, a0), tiles)
    o_w = acc_w * (1.0 / l_w)

    gt_s[...] = jax.nn.sigmoid(gate_ref[...]).T
    head0 = pl.program_id(1) * GROUP_SIZE
    for r in range(GROUP_SIZE):
        cs = slice(r * TQ, (r + 1) * TQ)
        c = (head0 + r) * N_BRANCH
        o = (gt_s[pl.ds(c, 1), :] * o_c[:, cs] + gt_s[pl.ds(c + 1, 1), :] * o_s[:, cs]
             + gt_s[pl.ds(c + 2, 1), :] * o_w[:, cs])
        o_ref[:, r * HEAD_DIM:(r + 1) * HEAD_DIM] = o.T.astype(o_ref.dtype)


def _cmp_to_sel_t():
    ss = np.arange(LANES)[:, None] * SEL_BLOCK
    cs = np.arange(LANES)[None, :] * CMP_STRIDE
    ov = np.clip(np.minimum(cs + CMP_BLOCK, ss + SEL_BLOCK) - np.maximum(cs, ss), 0, None)
    m = ov.astype(np.float32) / np.float32(CMP_BLOCK)
    m[N_SEL:, :] = 0.0
    m[:, N_CMP:] = 0.0
    return jnp.asarray(m, BF16)


def _nsa_attention(proj, gate_logits, batch, q_norm_w, k_norm_w, pe_k, pe_v, w_cmp_k, w_cmp_v, biasc, dm,
                   side_a, side_bt, side_tm=1024, side_tn=512, casts=()):
    kvb = COL_KV // LANES

    def kv_spec(branch, which):
        base = kvb + (branch * 2 + which) * N_KV_GROUPS
        return pl.BlockSpec((SEQ, LANES), lambda b, g, qt: (b, base + g))

    full = lambda shape: pl.BlockSpec(shape, lambda b, g, qt: (0,) * len(shape))
    grid = (batch, N_KV_GROUPS, NQ)
    c_in, c_out, c_shapes = _cast_specs(casts, grid)
    side_m, side_k = side_a.shape
    side_n = side_bt.shape[0]
    n_j = side_n // side_tn
    assert (side_m // side_tm) * n_j == math.prod(grid)
    step = lambda b, g, qt: (b * N_KV_GROUPS + g) * NQ + qt
    side_in = [pl.BlockSpec((side_tm, side_k), lambda b, g, qt: (step(b, g, qt) // n_j, 0),
                            pipeline_mode=pl.Buffered(1)),
               pl.BlockSpec((side_tn, side_k), lambda b, g, qt: (step(b, g, qt) % n_j, 0))]
    side_out = pl.BlockSpec((side_tm, side_tn), lambda b, g, qt: (step(b, g, qt) // n_j, step(b, g, qt) % n_j))
    return pl.pallas_call(
        functools.partial(_nsa_kernel, n_casts=len(casts)),
        out_shape=[jax.ShapeDtypeStruct((batch * SEQ, NSA_WIDTH), BF16),
                   jax.ShapeDtypeStruct((side_m, side_n), F32)] + c_shapes,
        grid=grid,
        in_specs=[pl.BlockSpec((TQ, GROUP_SIZE * HEAD_DIM), lambda b, g, qt: (b * NQ + qt, g)),
                  kv_spec(0, 0), kv_spec(0, 1), kv_spec(1, 0), kv_spec(1, 1),
                  kv_spec(2, 0), kv_spec(2, 1),
                  pl.BlockSpec((TQ, LANES), lambda b, g, qt: (b * NQ + qt, 0)),
                  full((1, HEAD_DIM)), full((N_BRANCH, HEAD_DIM)),
                  full((CMP_BLOCK, HEAD_DIM)), full((CMP_BLOCK, HEAD_DIM)),
                  full((CMP_BLOCK * HEAD_DIM, HEAD_DIM)), full((CMP_BLOCK * HEAD_DIM, HEAD_DIM)),
                  pl.BlockSpec((1, 1, LANES, GROUP_SIZE * TQ), lambda b, g, qt: (g, qt, 0, 0)),
                  pl.BlockSpec((1, 3, TK, GROUP_SIZE * TQ), lambda b, g, qt: (g, 0, 0, 0),
                               pipeline_mode=pl.Buffered(1)),
                  full((LANES, LANES))] + side_in + c_in,
        out_specs=[pl.BlockSpec((TQ, GROUP_SIZE * HEAD_DIM), lambda b, g, qt: (b * NQ + qt, g)),
                   side_out] + c_out,
        scratch_shapes=[pltpu.VMEM((LANES, HEAD_DIM), BF16),
                        pltpu.VMEM((HEAD_DIM, LANES), BF16),
                        pltpu.VMEM((SEQ, 2 * HEAD_DIM), BF16),
                        pltpu.VMEM((SEQ, HEAD_DIM), BF16),
                        pltpu.VMEM((SEQ // TK, HEAD_DIM, TK), BF16),
                        pltpu.VMEM((SEQ // TK, HEAD_DIM, TK), BF16),
                        pltpu.VMEM((LANES, TQ), F32)],
        compiler_params=pltpu.CompilerParams(
            dimension_semantics=("arbitrary", "arbitrary", "arbitrary"),
            vmem_limit_bytes=VMEM_LIMIT),
    )(proj, proj, proj, proj, proj, proj, proj, gate_logits,
      q_norm_w.reshape(1, HEAD_DIM), k_norm_w, pe_k, pe_v,
      w_cmp_k.astype(BF16), w_cmp_v.astype(BF16), biasc, dm, _cmp_to_sel_t(),
      side_a, side_bt, *[c[0] for c in casts])


def _ret_kernel(lg_ref, q_ref, k_ref, v_ref, g_ref, cos_ref, sin_ref, gnw_ref, *rest, n_casts):
    o_ref = rest[n_casts]
    state_ref = rest[-1]
    _cast_slabs(rest[:n_casts] + rest[n_casts + 1:-1], n_casts)
    c_len = RET_CHUNK
    ii = lax.broadcasted_iota(jnp.int32, (c_len, c_len), 0)
    jj = lax.broadcasted_iota(jnp.int32, (c_len, c_len), 1)
    diff = (ii - jj).astype(F32)
    icol = lax.broadcasted_iota(jnp.int32, (c_len, 1), 0).astype(F32)
    half = RET_KEY_DIM // 2
    state_ref[...] = jnp.zeros_like(state_ref)
    tables = []
    for hh in range(RET_HEADS_PER_STEP):
        lg = lg_ref[pl.program_id(1) * RET_HEADS_PER_STEP + hh]
        tables.append((
            jnp.where(diff >= 0.0, jnp.exp(jnp.maximum(diff, 0.0) * lg), 0.0),
            jnp.exp((icol + 1.0) * lg),
            jnp.exp((c_len - 1.0 - icol) * lg),
            jnp.exp(jnp.full((1, RET_VAL_DIM), float(c_len), F32) * lg)))

    def body(c, _):
        rows = pl.ds(pl.multiple_of(c * c_len, c_len), c_len)
        cos = cos_ref[rows, :]
        sin = sin_ref[rows, :]

        def rot(x):
            x1, x2 = x[:, :half], x[:, half:]
            return jnp.concatenate([x1 * cos - x2 * sin, x2 * cos + x1 * sin], axis=-1)

        for hh, (decay_in, xi, zeta, chunk_decay) in enumerate(tables):
            kcols = slice(hh * RET_KEY_DIM, (hh + 1) * RET_KEY_DIM)
            vcols = slice(hh * RET_VAL_DIM, (hh + 1) * RET_VAL_DIM)
            qr = rot(q_ref[rows, kcols])
            kr = rot(k_ref[rows, kcols]) * (RET_KEY_DIM ** -0.5)
            vb = v_ref[rows, vcols].astype(BF16)
            scores = _dot_nt(qr.astype(BF16), kr.astype(BF16)) * decay_in
            inner = _dot(scores.astype(BF16), vb)
            st = state_ref[hh]
            cross = _dot((qr * xi).astype(BF16), st.astype(BF16))
            kv = _dot((kr * zeta).T.astype(BF16), vb)
            state_ref[hh] = chunk_decay * st + kv
            y = inner + cross
            y = y * lax.rsqrt(jnp.mean(y * y, axis=-1, keepdims=True) + EPS) * gnw_ref[:, vcols]
            g = g_ref[rows, vcols]
            o_ref[rows, vcols] = (g * jax.nn.sigmoid(g) * y).astype(o_ref.dtype)
        return 0

    lax.fori_loop(0, SEQ // c_len, body, 0)


def _retention(proj, batch, gn_w, casts=()):
    pos = np.arange(SEQ, dtype=np.float64)
    half = RET_KEY_DIM // 2
    inv = np.exp(-np.linspace(0.0, 1.0, half) * math.log(ROPE_BASE)).astype(np.float32)
    ang = pos.astype(np.float32)[:, None] * inv[None, :]
    cos = jnp.asarray(np.cos(ang.astype(np.float64)), F32)
    sin = jnp.asarray(np.sin(ang.astype(np.float64)), F32)
    log_gamma = jnp.asarray(np.log(1.0 - 2.0 ** (-5.0 - np.arange(N_RET_HEADS))), F32)
    w = RET_KEY_DIM * RET_HEADS_PER_STEP

    def col_spec(col0):
        return pl.BlockSpec((SEQ, w), lambda b, h: (b, col0 // w + h))

    grid = (batch, N_RET_HEADS // RET_HEADS_PER_STEP)
    c_in, c_out, c_shapes = _cast_specs(casts, grid)
    return pl.pallas_call(
        functools.partial(_ret_kernel, n_casts=len(casts)),
        out_shape=[jax.ShapeDtypeStruct((batch * SEQ, RET_WIDTH), BF16)] + c_shapes,
        grid=grid,
        in_specs=[pl.BlockSpec(memory_space=pltpu.SMEM),
                  col_spec(COL_QR), col_spec(COL_KR), col_spec(COL_VR), col_spec(COL_GR),
                  pl.BlockSpec((SEQ, half), lambda b, h: (0, 0)),
                  pl.BlockSpec((SEQ, half), lambda b, h: (0, 0)),
                  pl.BlockSpec((1, w), lambda b, h: (0, h))] + c_in,
        out_specs=[pl.BlockSpec((SEQ, w), lambda b, h: (b, h))] + c_out,
        scratch_shapes=[pltpu.VMEM((RET_HEADS_PER_STEP, RET_KEY_DIM, RET_VAL_DIM), F32)],
        compiler_params=pltpu.CompilerParams(dimension_semantics=("arbitrary", "arbitrary"),
                                             vmem_limit_bytes=VMEM_LIMIT),
    )(log_gamma, proj, proj, proj, proj, cos, sin, gn_w.reshape(1, RET_WIDTH), *[c[0] for c in casts])


def _layer(x, norm1_w, w_in, nsa_q_norm_w, nsa_k_norm_w, cmp_pe_k, cmp_pe_v, w_cmp_k, w_cmp_v,
           rel_bias, ret_gn_w, w_out, norm2_w, w_gate, w_up, w_down):
    batch = x.shape[0]
    xf = x.reshape(batch * SEQ, D_MODEL)
    h = _rmsnorm(xf, norm1_w)
    w_in_t = w_in.T
    gate0 = W_IN_NSA_COLS
    gate1 = gate0 + W_IN_GATE_COLS
    ret_cols = 4 * RET_WIDTH
    proj_nsa, w_ret_t, w_out_b = _in_projection(
        h, w_in_t, 0, W_IN_NSA_COLS, 1024, tm=512,
        casts=((w_in_t, gate1, ret_cols, 64), (w_out, 0, D_MODEL, 64)))
    gate_logits, = _in_projection(h, w_in_t, gate0, LANES, LANES)
    biasc, dm = _bias_tables(rel_bias)
    n_steps = batch * N_KV_GROUPS * NQ
    o_nsa, proj_ret, w_gate_b = _nsa_attention(
        proj_nsa, gate_logits, batch, nsa_q_norm_w, nsa_k_norm_w, cmp_pe_k, cmp_pe_v,
        w_cmp_k, w_cmp_v, biasc, dm, h, w_ret_t, casts=((w_gate, 0, D_MODEL, n_steps),))
    o_ret, = _retention(proj_ret, batch, ret_gn_w)
    x1, w_up_b = _out_projection(o_nsa, o_ret, w_out_b, xf, casts=((w_up, 0, D_MODEL, 64),))
    hf = _rmsnorm(x1, norm2_w)
    mid, w_down_b = _ffn_up(hf, w_gate_b, w_up_b, tm=2048, casts=((w_down, 0, D_FF, 172),))
    out = _ffn_down(mid, w_down_b, x1)
    return out.reshape(batch, SEQ, D_MODEL)


def kernel(x, norm1_w, w_in, nsa_q_norm_w, nsa_k_norm_w, cmp_pe_k, cmp_pe_v, w_cmp_k, w_cmp_v,
           rel_bias, ret_gn_w, w_out, norm2_w, w_gate, w_up, w_down):
    for l in range(norm1_w.shape[0]):
        x = _layer(x, norm1_w[l], w_in[l], nsa_q_norm_w[l], nsa_k_norm_w[l], cmp_pe_k[l], cmp_pe_v[l],
                   w_cmp_k[l], w_cmp_v[l], rel_bias, ret_gn_w[l], w_out[l], norm2_w[l],
                   w_gate[l], w_up[l], w_down[l])
    return x
```

```python
import functools
import math

import numpy as np
import jax
import jax.numpy as jnp
from jax import lax
from jax.experimental import pallas as pl
from jax.experimental.pallas import tpu as pltpu

F32 = jnp.float32
BF16 = jnp.bfloat16

D_MODEL = 4096
SEQ = 2048
HEAD_DIM = 128
N_NSA_HEADS = 16
N_KV_GROUPS = 4
GROUP_SIZE = 4
N_BRANCH = 3
CMP_BLOCK = 32
CMP_STRIDE = 16
SEL_BLOCK = 64
TOP_N = 16
WINDOW = 512
RET_KEY_DIM = 256
RET_VAL_DIM = 256
N_RET_HEADS = 8
RET_CHUNK = 128
ROPE_BASE = 10000.0
NSA_WIDTH = 2048
RET_WIDTH = 2048
D_FF = 11008
NUM_BUCKETS = 32
MAX_DISTANCE = 128
EPS = 1e-6
NEG_INF = -1e30
FORCE_SCORE = 1e9
N_SEL = SEQ // SEL_BLOCK
N_CMP = (SEQ - CMP_BLOCK) // CMP_STRIDE + 1
SCALE = HEAD_DIM ** -0.5
LOG2E = math.log2(math.e)
EXP2_SCALE = SCALE * LOG2E

LANES = 128
VMEM_LIMIT = 56 * 1024 * 1024

TQ = 256
TK = 256
NQ = SEQ // TQ
RET_HEADS_PER_STEP = 2
V_ROWS = HEAD_DIM + 16

W_IN_NSA_COLS = NSA_WIDTH + N_BRANCH * 2 * N_KV_GROUPS * HEAD_DIM
W_IN_GATE_COLS = N_NSA_HEADS * N_BRANCH
COL_KV = NSA_WIDTH
COL_QR = 0
COL_KR = 2048
COL_VR = 4096
COL_GR = 6144


def _bucket_thresholds():
    rel = np.arange(0, 4 * MAX_DISTANCE)
    max_exact = NUM_BUCKETS // 2
    nf = np.maximum(rel, 1).astype(np.float64)
    large = max_exact + np.floor(np.log(nf / max_exact) / math.log(MAX_DISTANCE / max_exact)
                                 * (NUM_BUCKETS - max_exact)).astype(np.int64)
    large = np.minimum(large, NUM_BUCKETS - 1)
    b = np.where(rel < max_exact, rel, large)
    return [int(np.argmax(b >= k)) for k in range(NUM_BUCKETS)]


_THR = _bucket_thresholds()


def _dot(a, b):
    return jnp.dot(a, b, preferred_element_type=F32)


def _dot_nt(a, b):
    return lax.dot_general(a, b, (((1,), (1,)), ((), ())), preferred_element_type=F32)


def _rms_kernel(x_ref, w_ref, o_ref):
    x = x_ref[...]
    y = x * lax.rsqrt(jnp.mean(x * x, axis=-1, keepdims=True) + EPS)
    o_ref[...] = (y * w_ref[...]).astype(o_ref.dtype)


def _rmsnorm(x, w, tm=256):
    m, d = x.shape
    return pl.pallas_call(
        _rms_kernel,
        out_shape=jax.ShapeDtypeStruct((m, d), BF16),
        grid=(m // tm,),
        in_specs=[pl.BlockSpec((tm, d), lambda i: (i, 0)),
                  pl.BlockSpec((1, d), lambda i: (0, 0))],
        out_specs=pl.BlockSpec((tm, d), lambda i: (i, 0)),
        compiler_params=pltpu.CompilerParams(dimension_semantics=("arbitrary",),
                                             vmem_limit_bytes=VMEM_LIMIT),
    )(x, w.reshape(1, d))


def _cast_spec(grid, row0, n_rows, n_cols, n_slabs):
    slab = n_rows // n_slabs
    sub = 8
    n_steps = math.prod(grid)
    assert slab * n_slabs == n_rows and slab % 16 == 0 and n_cols % LANES == 0
    assert row0 % sub == 0 and n_slabs <= n_steps

    def slab_index(ids):
        step = ids[0]
        for dim, idx in zip(grid[1:], ids[1:]):
            step = step * dim + idx
        return step if n_slabs == n_steps else jnp.minimum(step, n_slabs - 1)

    in_spec = pl.BlockSpec((pl.Element(slab), pl.Element(n_cols)),
                           lambda *ids: ((row0 // sub + slab_index(ids) * (slab // sub)) * sub, 0))
    out_spec = pl.BlockSpec((slab, n_cols), lambda *ids: (slab_index(ids), 0))
    return in_spec, out_spec, jax.ShapeDtypeStruct((n_rows, n_cols), BF16)


def _cast_specs(casts, grid):
    specs = [_cast_spec(grid, row0, n_rows, arr.shape[1], n_slabs) for arr, row0, n_rows, n_slabs in casts]
    return [s[0] for s in specs], [s[1] for s in specs], [s[2] for s in specs]


def _cast_slabs(refs, n_casts):
    for src, dst in zip(refs[:n_casts], refs[len(refs) - n_casts:]):
        dst[...] = src[...].astype(dst.dtype)


def _mm_kernel(a_ref, b_ref, *rest, n_casts):
    o_ref = rest[n_casts]
    o_ref[...] = _dot(a_ref[...], b_ref[...]).astype(o_ref.dtype)
    _cast_slabs(rest[:n_casts] + rest[n_casts + 1:], n_casts)


def _matmul(a, b, out_dtype, tm, tn, casts=()):
    m, k = a.shape
    _, n = b.shape
    grid = (m // tm, n // tn)
    c_in, c_out, c_shapes = _cast_specs(casts, grid)
    outs = pl.pallas_call(
        functools.partial(_mm_kernel, n_casts=len(casts)),
        out_shape=[jax.ShapeDtypeStruct((m, n), out_dtype)] + c_shapes,
        grid=grid,
        in_specs=[pl.BlockSpec((tm, k), lambda i, j: (i, 0)),
                  pl.BlockSpec((k, tn), lambda i, j: (0, j))] + c_in,
        out_specs=[pl.BlockSpec((tm, tn), lambda i, j: (i, j))] + c_out,
        compiler_params=pltpu.CompilerParams(dimension_semantics=("arbitrary", "arbitrary"),
                                             vmem_limit_bytes=VMEM_LIMIT),
    )(a, b, *[c[0] for c in casts])
    return outs


def _inproj_kernel(a_ref, wt_ref, *rest, n_casts):
    o_ref = rest[n_casts]
    wbf_ref = rest[-1]

    @pl.when(pl.program_id(1) == 0)
    def _():
        wbf_ref[...] = wt_ref[...].astype(BF16)

    o_ref[...] = _dot_nt(a_ref[...], wbf_ref[...])
    _cast_slabs(rest[:n_casts] + rest[n_casts + 1:-1], n_casts)


def _in_projection(a, w_t, row0, n_cols, tn, tm=1024, casts=()):
    m, k = a.shape
    grid = (n_cols // tn, m // tm)
    c_in, c_out, c_shapes = _cast_specs(casts, grid)
    sub = 8
    assert row0 % sub == 0 and tn % sub == 0
    return pl.pallas_call(
        functools.partial(_inproj_kernel, n_casts=len(casts)),
        out_shape=[jax.ShapeDtypeStruct((m, n_cols), F32)] + c_shapes,
        grid=grid,
        in_specs=[pl.BlockSpec((tm, k), lambda j, i: (i, 0)),
                  pl.BlockSpec((pl.Element(tn), pl.Element(k)),
                               lambda j, i: ((row0 // sub + j * (tn // sub)) * sub, 0),
                               pipeline_mode=pl.Buffered(1))] + c_in,
        out_specs=[pl.BlockSpec((tm, tn), lambda j, i: (i, j))] + c_out,
        scratch_shapes=[pltpu.VMEM((tn, k), BF16)],
        compiler_params=pltpu.CompilerParams(dimension_semantics=("arbitrary", "arbitrary"),
                                             vmem_limit_bytes=VMEM_LIMIT),
    )(a, w_t, *[c[0] for c in casts])


def _outproj_kernel(a1_ref, a2_ref, w1_ref, w2_ref, x_ref, *rest, n_casts):
    o_ref = rest[n_casts]
    acc = _dot(a1_ref[...], w1_ref[...]) + _dot(a2_ref[...], w2_ref[...])
    o_ref[...] = x_ref[...] + acc
    _cast_slabs(rest[:n_casts] + rest[n_casts + 1:], n_casts)


def _out_projection(o_nsa, o_ret, w_out, x, tm=1024, tn=512, casts=()):
    m = x.shape[0]
    kh = NSA_WIDTH
    grid = (m // tm, D_MODEL // tn)
    c_in, c_out, c_shapes = _cast_specs(casts, grid)
    return pl.pallas_call(
        functools.partial(_outproj_kernel, n_casts=len(casts)),
        out_shape=[jax.ShapeDtypeStruct((m, D_MODEL), F32)] + c_shapes,
        grid=grid,
        in_specs=[pl.BlockSpec((tm, kh), lambda i, j: (i, 0)),
                  pl.BlockSpec((tm, kh), lambda i, j: (i, 0)),
                  pl.BlockSpec((kh, tn), lambda i, j: (0, j)),
                  pl.BlockSpec((kh, tn), lambda i, j: (1, j)),
                  pl.BlockSpec((tm, tn), lambda i, j: (i, j))] + c_in,
        out_specs=[pl.BlockSpec((tm, tn), lambda i, j: (i, j))] + c_out,
        compiler_params=pltpu.CompilerParams(dimension_semantics=("arbitrary", "arbitrary"),
                                             vmem_limit_bytes=VMEM_LIMIT),
    )(o_nsa, o_ret, w_out, w_out, x, *[c[0] for c in casts])


def _ffn_up_kernel(h_ref, wg_ref, wu_ref, *rest, n_casts):
    o_ref = rest[n_casts]
    h = h_ref[...]
    g = _dot(h, wg_ref[...])
    u = _dot(h, wu_ref[...])
    o_ref[...] = (g * jax.nn.sigmoid(g) * u).astype(o_ref.dtype)
    _cast_slabs(rest[:n_casts] + rest[n_casts + 1:], n_casts)


def _ffn_up(h, w_gate, w_up, tm=1024, tn=256, casts=()):
    m, k = h.shape
    n = w_gate.shape[1]
    grid = (m // tm, n // tn)
    c_in, c_out, c_shapes = _cast_specs(casts, grid)
    return pl.pallas_call(
        functools.partial(_ffn_up_kernel, n_casts=len(casts)),
        out_shape=[jax.ShapeDtypeStruct((m, n), BF16)] + c_shapes,
        grid=grid,
        in_specs=[pl.BlockSpec((tm, k), lambda i, j: (i, 0)),
                  pl.BlockSpec((k, tn), lambda i, j: (0, j)),
                  pl.BlockSpec((k, tn), lambda i, j: (0, j))] + c_in,
        out_specs=[pl.BlockSpec((tm, tn), lambda i, j: (i, j))] + c_out,
        compiler_params=pltpu.CompilerParams(dimension_semantics=("arbitrary", "arbitrary"),
                                             vmem_limit_bytes=VMEM_LIMIT),
    )(h, w_gate, w_up, *[c[0] for c in casts])


def _ffn_down_kernel(a_ref, b_ref, x_ref, o_ref, acc_ref):
    kk = pl.program_id(2)
    last = pl.num_programs(2) - 1

    @pl.when(kk == 0)
    def _():
        acc_ref[...] = x_ref[...] + _dot(a_ref[...], b_ref[...])

    @pl.when(jnp.logical_and(kk > 0, kk < last))
    def _():
        acc_ref[...] += _dot(a_ref[...], b_ref[...])

    @pl.when(kk == last)
    def _():
        o_ref[...] = acc_ref[...] + _dot(a_ref[...], b_ref[...])


def _ffn_down(a, w_down, x, tm=1024, tn=512, tk=5504):
    m, k = a.shape
    n = w_down.shape[1]
    assert k // tk >= 2
    return pl.pallas_call(
        _ffn_down_kernel,
        out_shape=jax.ShapeDtypeStruct((m, n), F32),
        grid=(m // tm, n // tn, k // tk),
        in_specs=[pl.BlockSpec((tm, tk), lambda i, j, kk: (i, kk)),
                  pl.BlockSpec((tk, tn), lambda i, j, kk: (kk, j)),
                  pl.BlockSpec((tm, tn), lambda i, j, kk: (i, j))],
        out_specs=pl.BlockSpec((tm, tn), lambda i, j, kk: (i, j)),
        scratch_shapes=[pltpu.VMEM((tm, tn), F32)],
        compiler_params=pltpu.CompilerParams(
            dimension_semantics=("arbitrary", "arbitrary", "arbitrary"),
            vmem_limit_bytes=VMEM_LIMIT),
    )(a, w_down, x)


def _bias_value(rel, relb_ref, h):
    val = jnp.full(rel.shape, relb_ref[0, h], F32)
    for k in range(1, NUM_BUCKETS):
        val = jnp.where(rel >= _THR[k], relb_ref[k, h], val)
    return val


def _bias_tables_kernel(relb_ref, biasc_ref, dm_ref):
    h = pl.program_id(0)
    sub = 8
    j = lax.broadcasted_iota(jnp.int32, (sub, 2 * SEQ), 1)
    row_c = _bias_value(j - SEQ - (CMP_BLOCK - 1), relb_ref, h)[0:1]
    toe_c = pltpu.roll(jnp.broadcast_to(row_c, (LANES, 2 * SEQ)), 0, 1,
                       stride=CMP_STRIDE, stride_axis=0)
    for qt in range(NQ):
        biasc_ref[0, qt] = toe_c[:, SEQ + qt * TQ:SEQ + (qt + 1) * TQ]
    far = relb_ref[NUM_BUCKETS - 1, h]
    j2 = lax.broadcasted_iota(jnp.int32, (sub, 2 * TK), 1)
    b = lax.broadcasted_iota(jnp.int32, (TK, TQ), 0)
    a = lax.broadcasted_iota(jnp.int32, (TK, TQ), 1)
    for kind in range(2):
        row = ((_bias_value(j2 + (kind - 1) * TK, relb_ref, h) - far) * LOG2E)[0:1]
        toe = pltpu.roll(jnp.broadcast_to(row, (TK, 2 * TK)), 0, 1, stride=1, stride_axis=0)
        tile = toe[:, TK:]
        dm_ref[0, kind] = tile + jnp.where(a >= b, 0.0, NEG_INF) if kind == 0 else tile
    dm_ref[0, 2] = jnp.where(b > a, 0.0, NEG_INF)


def _bias_tables(rel_bias):
    return pl.pallas_call(
        _bias_tables_kernel,
        out_shape=(jax.ShapeDtypeStruct((N_KV_GROUPS, NQ, LANES, GROUP_SIZE * TQ), F32),
                   jax.ShapeDtypeStruct((N_KV_GROUPS, 3, TK, GROUP_SIZE * TQ), F32)),
        grid=(N_NSA_HEADS,),
        in_specs=[pl.BlockSpec(memory_space=pltpu.SMEM)],
        out_specs=(pl.BlockSpec((1, NQ, LANES, TQ), lambda h: (h // GROUP_SIZE, 0, 0, h % GROUP_SIZE)),
                   pl.BlockSpec((1, 3, TK, TQ), lambda h: (h // GROUP_SIZE, 0, 0, h % GROUP_SIZE))),
        compiler_params=pltpu.CompilerParams(dimension_semantics=("arbitrary",),
                                             vmem_limit_bytes=VMEM_LIMIT),
    )(rel_bias)


def _row_rms(x, w):
    return x * lax.rsqrt(jnp.mean(x * x, axis=-1, keepdims=True) + EPS) * w


N_NSA_INPUTS = 17
N_NSA_SCRATCH = 7


def _nsa_kernel(*refs, n_casts):
    (q_ref, k0_ref, v0_ref, k1_ref, v1_ref, k2_ref, v2_ref, gate_ref,
     qnw_ref, knw_ref, pek_ref, pev_ref, wck_ref, wcv_ref, biasc_ref, dm_ref, c2st_ref) = refs[:N_NSA_INPUTS]
    side_a_ref, side_bt_ref = refs[N_NSA_INPUTS:N_NSA_INPUTS + 2]
    n_in = N_NSA_INPUTS + 2 + n_casts
    cast_in = refs[N_NSA_INPUTS + 2:n_in]
    o_ref, side_o_ref = refs[n_in:n_in + 2]
    cast_out = refs[n_in + 2:n_in + 2 + n_casts]
    kc_s, vct_s, kaug_s, kw_s, vst_s, vwt_s, gt_s = refs[len(refs) - N_NSA_SCRATCH:]
    qt = pl.program_id(2)
    cols_all = GROUP_SIZE * TQ
    side_half = side_bt_ref.shape[0] // 2

    @pl.when(qt == 0)
    def _prologue():
        kaug_s[:, :HEAD_DIM] = (_row_rms(k1_ref[...], knw_ref[1:2, :]) * EXP2_SCALE).astype(BF16)
        jblk = lax.broadcasted_iota(jnp.int32, (SEQ, LANES), 0) // SEL_BLOCK
        lane = lax.broadcasted_iota(jnp.int32, (SEQ, LANES), 1)
        kaug_s[:, HEAD_DIM:] = jnp.where(jblk == lane, 1.0, 0.0).astype(BF16)
        kw_s[...] = (_row_rms(k2_ref[...], knw_ref[2:3, :]) * EXP2_SCALE).astype(BF16)
        ones_row = jnp.where(lax.broadcasted_iota(jnp.int32, (V_ROWS - HEAD_DIM, TK), 0) == 0,
                             1.0, 0.0).astype(BF16)
        for t in range(SEQ // TK):
            vst_s[t, :HEAD_DIM, :] = v1_ref[t * TK:(t + 1) * TK, :].T.astype(BF16)
            vst_s[t, HEAD_DIM:, :] = ones_row
            vwt_s[t, :HEAD_DIM, :] = v2_ref[t * TK:(t + 1) * TK, :].T.astype(BF16)
            vwt_s[t, HEAD_DIM:, :] = ones_row

        def compress(x_ref, pe_ref, w_ref):
            acc_lo = jnp.zeros((LANES, HEAD_DIM), F32)
            acc_hi = jnp.zeros((LANES, HEAD_DIM), F32)
            for j in range(CMP_STRIDE):
                y = x_ref[pl.ds(j, LANES, stride=CMP_STRIDE), :]
                acc_lo += _dot((y + pe_ref[j:j + 1, :]).astype(BF16),
                               w_ref[j * HEAD_DIM:(j + 1) * HEAD_DIM, :])
                jh = CMP_STRIDE + j
                acc_hi += _dot((y + pe_ref[jh:jh + 1, :]).astype(BF16),
                               w_ref[jh * HEAD_DIM:(jh + 1) * HEAD_DIM, :])
            return acc_lo + pltpu.roll(acc_hi, LANES - 1, 0)

        kc_s[...] = _row_rms(compress(k0_ref, pek_ref, wck_ref), knw_ref[0:1, :]).astype(BF16)
        vct_s[...] = compress(v0_ref, pev_ref, wcv_ref).T.astype(BF16)

    side_o_ref[:, :side_half] = _dot_nt(side_a_ref[...], side_bt_ref[:side_half, :])
    _cast_slabs(cast_in + cast_out, n_casts)

    t0 = qt * TQ
    q = q_ref[...]
    qts = []
    for r in range(GROUP_SIZE):
        qts.append(_row_rms(q[:, r * HEAD_DIM:(r + 1) * HEAD_DIM], qnw_ref[...]).T)
    qt_stack = jnp.concatenate(qts, axis=1).astype(BF16)

    sc = _dot(kc_s[...], qt_stack) * SCALE + biasc_ref[0, 0]
    tpos = t0 + (lax.broadcasted_iota(jnp.int32, (LANES, cols_all), 1) & (TQ - 1))
    nrow = lax.broadcasted_iota(jnp.int32, (LANES, cols_all), 0)
    maskc = tpos >= CMP_STRIDE * nrow + (CMP_BLOCK - 1)
    scm = jnp.where(maskc, sc, NEG_INF)
    mc = jnp.max(scm, axis=0, keepdims=True)
    ec = jnp.where(maskc, jnp.exp(scm - mc), 0.0)
    lc = jnp.sum(ec, axis=0, keepdims=True)
    pc = ec * (1.0 / jnp.where(lc > 0.0, lc, 1.0))
    o_c = _dot(vct_s[...], pc.astype(BF16))

    psum = pc[:, 0:TQ] + pc[:, TQ:2 * TQ] + pc[:, 2 * TQ:3 * TQ] + pc[:, 3 * TQ:4 * TQ]
    p_hi = psum.astype(BF16)
    p_lo = (psum - p_hi.astype(F32)).astype(BF16)
    imp = (_dot(c2st_ref[...], p_hi) + _dot(c2st_ref[...], p_lo))[0:N_SEL]
    tq_pos = t0 + lax.broadcasted_iota(jnp.int32, (N_SEL, TQ), 1)
    sidx = lax.broadcasted_iota(jnp.int32, (N_SEL, TQ), 0)
    cur = tq_pos >> 6
    score = jnp.where(sidx <= cur, imp, NEG_INF)
    for forced_blk in (cur - 1, cur, jnp.zeros_like(cur)):
        score = jnp.where(sidx == forced_blk, FORCE_SCORE, score)
    rank = jnp.zeros((N_SEL, TQ), F32)
    for sp in range(N_SEL):
        row = score[sp:sp + 1, :]
        tie = jnp.where(sidx > sp, 1.0, 0.0)
        rank = rank + jnp.where(row > score, 1.0, jnp.where(row == score, tie, 0.0))
    selb = jnp.where(rank < TOP_N, 0.0, NEG_INF)
    selb = jnp.concatenate([selb, jnp.zeros((HEAD_DIM - N_SEL, TQ), F32)], axis=0).astype(BF16)
    qt_aug = jnp.concatenate([qt_stack, jnp.concatenate([selb] * GROUP_SIZE, axis=1)], axis=0)

    m0 = jnp.full((1, cols_all), NEG_INF, F32)
    a0 = jnp.zeros((V_ROWS, cols_all), F32)

    def online(carry, tiles):
        m, acc = carry
        m_new = m
        for u, _ in tiles:
            m_new = jnp.maximum(m_new, jnp.max(u, axis=0, keepdims=True))
        acc = jnp.exp2(m - m_new) * acc
        for u, vt in tiles:
            acc = acc + _dot(vt, jnp.exp2(u - m_new).astype(BF16))
        return m_new, acc

    def normalised(acc):
        return acc[:HEAD_DIM] * (1.0 / acc[HEAD_DIM:HEAD_DIM + 1])

    def key_rows(kt):
        return pl.ds(pl.multiple_of(kt * TK, TK), TK)

    def sel_tile(kt):
        return _dot(kaug_s[key_rows(kt), :], qt_aug), vst_s[kt]

    n_far = jnp.maximum(qt - 1, 0)
    carry = lax.fori_loop(0, n_far >> 1,
                          lambda i, c: online(c, [sel_tile(2 * i), sel_tile(2 * i + 1)]), (m0, a0))
    carry = lax.fori_loop(0, n_far & 1, lambda i, c: online(c, [sel_tile(n_far - 1)]), carry)
    kt1 = jnp.maximum(qt - 1, 0)
    u1, vt1 = sel_tile(kt1)
    u0, vt0 = sel_tile(qt)
    _, acc_s = online(carry, [(u1 + dm_ref[0, 1] + jnp.where(qt >= 1, 0.0, NEG_INF), vt1),
                              (u0 + dm_ref[0, 0], vt0)])
    o_s = normalised(acc_s)

    side_o_ref[:, side_half:] = _dot_nt(side_a_ref[...], side_bt_ref[side_half:, :])

    tiles = []
    for dist, kind in ((2, 2), (1, 1), (0, 0)):
        kt = jnp.maximum(qt - dist, 0)
        u = _dot(kw_s[key_rows(kt), :], qt_stack) + dm_ref[0, kind]
        if dist > 0:
            u = u + jnp.where(qt >= dist, 0.0, NEG_INF)
        tiles.append((u, vwt_s[kt]))
    _, acc_w = online((m0, a0), tiles)
    o_w = normalised(acc_w)

    gt_s[...] = jax.nn.sigmoid(gate_ref[...]).T
    head0 = pl.program_id(1) * GROUP_SIZE
    for r in range(GROUP_SIZE):
        cs = slice(r * TQ, (r + 1) * TQ)
        c = (head0 + r) * N_BRANCH
        o = (gt_s[pl.ds(c, 1), :] * o_c[:, cs] + gt_s[pl.ds(c + 1, 1), :] * o_s[:, cs]
             + gt_s[pl.ds(c + 2, 1), :] * o_w[:, cs])
        o_ref[:, r * HEAD_DIM:(r + 1) * HEAD_DIM] = o.T.astype(o_ref.dtype)


def _cmp_to_sel_t():
    ss = np.arange(LANES)[:, None] * SEL_BLOCK
    cs = np.arange(LANES)[None, :] * CMP_STRIDE
    ov = np.clip(np.minimum(cs + CMP_BLOCK, ss + SEL_BLOCK) - np.maximum(cs, ss), 0, None)
    m = ov.astype(np.float32) / np.float32(CMP_BLOCK)
    m[N_SEL:, :] = 0.0
    m[:, N_CMP:] = 0.0
    return jnp.asarray(m, BF16)


def _nsa_attention(proj, gate_logits, batch, q_norm_w, k_norm_w, pe_k, pe_v, w_cmp_k, w_cmp_v, biasc, dm,
                   side_a, side_bt, side_tm=1024, side_tn=512, casts=()):
    kvb = COL_KV // LANES

    def kv_spec(branch, which):
        base = kvb + (branch * 2 + which) * N_KV_GROUPS
        return pl.BlockSpec((SEQ, LANES), lambda b, g, qt: (b, base + g))

    full = lambda shape: pl.BlockSpec(shape, lambda b, g, qt: (0,) * len(shape))
    grid = (batch, N_KV_GROUPS, NQ)
    c_in, c_out, c_shapes = _cast_specs(casts, grid)
    side_m, side_k = side_a.shape
    side_n = side_bt.shape[0]
    n_j = side_n // side_tn
    assert (side_m // side_tm) * n_j == math.prod(grid)
    step = lambda b, g, qt: (b * N_KV_GROUPS + g) * NQ + qt
    side_in = [pl.BlockSpec((side_tm, side_k), lambda b, g, qt: (step(b, g, qt) // n_j, 0),
                            pipeline_mode=pl.Buffered(1)),
               pl.BlockSpec((side_tn, side_k), lambda b, g, qt: (step(b, g, qt) % n_j, 0))]
    side_out = pl.BlockSpec((side_tm, side_tn), lambda b, g, qt: (step(b, g, qt) // n_j, step(b, g, qt) % n_j))
    return pl.pallas_call(
        functools.partial(_nsa_kernel, n_casts=len(casts)),
        out_shape=[jax.ShapeDtypeStruct((batch * SEQ, NSA_WIDTH), BF16),
                   jax.ShapeDtypeStruct((side_m, side_n), F32)] + c_shapes,
        grid=grid,
        in_specs=[pl.BlockSpec((TQ, GROUP_SIZE * HEAD_DIM), lambda b, g, qt: (b * NQ + qt, g)),
                  kv_spec(0, 0), kv_spec(0, 1), kv_spec(1, 0), kv_spec(1, 1),
                  kv_spec(2, 0), kv_spec(2, 1),
                  pl.BlockSpec((TQ, LANES), lambda b, g, qt: (b * NQ + qt, 0)),
                  full((1, HEAD_DIM)), full((N_BRANCH, HEAD_DIM)),
                  full((CMP_BLOCK, HEAD_DIM)), full((CMP_BLOCK, HEAD_DIM)),
                  full((CMP_BLOCK * HEAD_DIM, HEAD_DIM)), full((CMP_BLOCK * HEAD_DIM, HEAD_DIM)),
                  pl.BlockSpec((1, 1, LANES, GROUP_SIZE * TQ), lambda b, g, qt: (g, qt, 0, 0)),
                  pl.BlockSpec((1, 3, TK, GROUP_SIZE * TQ), lambda b, g, qt: (g, 0, 0, 0),
                               pipeline_mode=pl.Buffered(1)),
                  full((LANES, LANES))] + side_in + c_in,
        out_specs=[pl.BlockSpec((TQ, GROUP_SIZE * HEAD_DIM), lambda b, g, qt: (b * NQ + qt, g)),
                   side_out] + c_out,
        scratch_shapes=[pltpu.VMEM((LANES, HEAD_DIM), BF16),
                        pltpu.VMEM((HEAD_DIM, LANES), BF16),
                        pltpu.VMEM((SEQ, 2 * HEAD_DIM), BF16),
                        pltpu.VMEM((SEQ, HEAD_DIM), BF16),
                        pltpu.VMEM((SEQ // TK, V_ROWS, TK), BF16),
                        pltpu.VMEM((SEQ // TK, V_ROWS, TK), BF16),
                        pltpu.VMEM((LANES, TQ), F32)],
        compiler_params=pltpu.CompilerParams(
            dimension_semantics=("arbitrary", "arbitrary", "arbitrary"),
            vmem_limit_bytes=VMEM_LIMIT),
    )(proj, proj, proj, proj, proj, proj, proj, gate_logits,
      q_norm_w.reshape(1, HEAD_DIM), k_norm_w, pe_k, pe_v,
      w_cmp_k.astype(BF16), w_cmp_v.astype(BF16), biasc, dm, _cmp_to_sel_t(),
      side_a, side_bt, *[c[0] for c in casts])


def _ret_kernel(lg_ref, q_ref, k_ref, v_ref, g_ref, cos_ref, sin_ref, gnw_ref, *rest, n_casts):
    o_ref = rest[n_casts]
    state_ref = rest[-1]
    _cast_slabs(rest[:n_casts] + rest[n_casts + 1:-1], n_casts)
    c_len = RET_CHUNK
    ii = lax.broadcasted_iota(jnp.int32, (c_len, c_len), 0)
    jj = lax.broadcasted_iota(jnp.int32, (c_len, c_len), 1)
    diff = (ii - jj).astype(F32)
    icol = lax.broadcasted_iota(jnp.int32, (c_len, 1), 0).astype(F32)
    half = RET_KEY_DIM // 2
    state_ref[...] = jnp.zeros_like(state_ref)
    tables = []
    for hh in range(RET_HEADS_PER_STEP):
        lg = lg_ref[pl.program_id(1) * RET_HEADS_PER_STEP + hh]
        tables.append((
            jnp.where(diff >= 0.0, jnp.exp(jnp.maximum(diff, 0.0) * lg), 0.0),
            jnp.exp((icol + 1.0) * lg),
            jnp.exp((c_len - 1.0 - icol) * lg),
            jnp.exp(jnp.full((1, RET_VAL_DIM), float(c_len), F32) * lg)))

    def body(c, _):
        rows = pl.ds(pl.multiple_of(c * c_len, c_len), c_len)
        cos = cos_ref[rows, :]
        sin = sin_ref[rows, :]

        def rot(x):
            x1, x2 = x[:, :half], x[:, half:]
            return jnp.concatenate([x1 * cos - x2 * sin, x2 * cos + x1 * sin], axis=-1)

        for hh, (decay_in, xi, zeta, chunk_decay) in enumerate(tables):
            kcols = slice(hh * RET_KEY_DIM, (hh + 1) * RET_KEY_DIM)
            vcols = slice(hh * RET_VAL_DIM, (hh + 1) * RET_VAL_DIM)
            qr = rot(q_ref[rows, kcols])
            kr = rot(k_ref[rows, kcols]) * (RET_KEY_DIM ** -0.5)
            vb = v_ref[rows, vcols].astype(BF16)
            scores = _dot_nt(qr.astype(BF16), kr.astype(BF16)) * decay_in
            inner = _dot(scores.astype(BF16), vb)
            st = state_ref[hh]
            cross = _dot((qr * xi).astype(BF16), st.astype(BF16))
            kv = _dot((kr * zeta).T.astype(BF16), vb)
            state_ref[hh] = chunk_decay * st + kv
            y = inner + cross
            y = y * lax.rsqrt(jnp.mean(y * y, axis=-1, keepdims=True) + EPS) * gnw_ref[:, vcols]
            g = g_ref[rows, vcols]
            o_ref[rows, vcols] = (g * jax.nn.sigmoid(g) * y).astype(o_ref.dtype)
        return 0

    lax.fori_loop(0, SEQ // c_len, body, 0)


def _retention(proj, batch, gn_w, casts=()):
    pos = np.arange(SEQ, dtype=np.float64)
    half = RET_KEY_DIM // 2
    inv = np.exp(-np.linspace(0.0, 1.0, half) * math.log(ROPE_BASE)).astype(np.float32)
    ang = pos.astype(np.float32)[:, None] * inv[None, :]
    cos = jnp.asarray(np.cos(ang.astype(np.float64)), F32)
    sin = jnp.asarray(np.sin(ang.astype(np.float64)), F32)
    log_gamma = jnp.asarray(np.log(1.0 - 2.0 ** (-5.0 - np.arange(N_RET_HEADS))), F32)
    w = RET_KEY_DIM * RET_HEADS_PER_STEP

    def col_spec(col0):
        return pl.BlockSpec((SEQ, w), lambda b, h: (b, col0 // w + h))

    grid = (batch, N_RET_HEADS // RET_HEADS_PER_STEP)
    c_in, c_out, c_shapes = _cast_specs(casts, grid)
    return pl.pallas_call(
        functools.partial(_ret_kernel, n_casts=len(casts)),
        out_shape=[jax.ShapeDtypeStruct((batch * SEQ, RET_WIDTH), BF16)] + c_shapes,
        grid=grid,
        in_specs=[pl.BlockSpec(memory_space=pltpu.SMEM),
                  col_spec(COL_QR), col_spec(COL_KR), col_spec(COL_VR), col_spec(COL_GR),
                  pl.BlockSpec((SEQ, half), lambda b, h: (0, 0)),
                  pl.BlockSpec((SEQ, half), lambda b, h: (0, 0)),
                  pl.BlockSpec((1, w), lambda b, h: (0, h))] + c_in,
        out_specs=[pl.BlockSpec((SEQ, w), lambda b, h: (b, h))] + c_out,
        scratch_shapes=[pltpu.VMEM((RET_HEADS_PER_STEP, RET_KEY_DIM, RET_VAL_DIM), F32)],
        compiler_params=pltpu.CompilerParams(dimension_semantics=("arbitrary", "arbitrary"),
                                             vmem_limit_bytes=VMEM_LIMIT),
    )(log_gamma, proj, proj, proj, proj, cos, sin, gn_w.reshape(1, RET_WIDTH), *[c[0] for c in casts])


def _layer(x, norm1_w, w_in, nsa_q_norm_w, nsa_k_norm_w, cmp_pe_k, cmp_pe_v, w_cmp_k, w_cmp_v,
           rel_bias, ret_gn_w, w_out, norm2_w, w_gate, w_up, w_down):
    batch = x.shape[0]
    xf = x.reshape(batch * SEQ, D_MODEL)
    h = _rmsnorm(xf, norm1_w)
    w_in_t = w_in.T
    gate0 = W_IN_NSA_COLS
    gate1 = gate0 + W_IN_GATE_COLS
    ret_cols = 4 * RET_WIDTH
    proj_nsa, w_ret_t, w_out_b = _in_projection(
        h, w_in_t, 0, W_IN_NSA_COLS, 1024, tm=512,
        casts=((w_in_t, gate1, ret_cols, 64), (w_out, 0, D_MODEL, 64)))
    gate_logits, = _in_projection(h, w_in_t, gate0, LANES, LANES)
    biasc, dm = _bias_tables(rel_bias)
    n_steps = batch * N_KV_GROUPS * NQ
    o_nsa, proj_ret, w_gate_b = _nsa_attention(
        proj_nsa, gate_logits, batch, nsa_q_norm_w, nsa_k_norm_w, cmp_pe_k, cmp_pe_v,
        w_cmp_k, w_cmp_v, biasc, dm, h, w_ret_t, casts=((w_gate, 0, D_MODEL, n_steps),))
    o_ret, = _retention(proj_ret, batch, ret_gn_w)
    x1, w_up_b = _out_projection(o_nsa, o_ret, w_out_b, xf, casts=((w_up, 0, D_MODEL, 64),))
    hf = _rmsnorm(x1, norm2_w)
    mid, w_down_b = _ffn_up(hf, w_gate_b, w_up_b, tm=2048, casts=((w_down, 0, D_FF, 172),))
    out = _ffn_down(mid, w_down_b, x1)
    return out.reshape(batch, SEQ, D_MODEL)


def kernel(x, norm1_w, w_in, nsa_q_norm_w, nsa_k_norm_w, cmp_pe_k, cmp_pe_v, w_cmp_k, w_cmp_v,
           rel_bias, ret_gn_w, w_out, norm2_w, w_gate, w_up, w_down):
    for l in range(norm1_w.shape[0]):
        x = _layer(x, norm1_w[l], w_in[l], nsa_q_norm_w[l], nsa_k_norm_w[l], cmp_pe_k[l], cmp_pe_v[l],
                   w_cmp_k[l], w_cmp_v[l], rel_bias, ret_gn_w[l], w_out[l], norm2_w[l],
                   w_gate[l], w_up[l], w_down[l])
    return x
```

```python
import functools
import math

import numpy as np
import jax
import jax.numpy as jnp
from jax import lax
from jax.experimental import pallas as pl
from jax.experimental.pallas import tpu as pltpu

F32 = jnp.float32
BF16 = jnp.bfloat16

D_MODEL = 4096
SEQ = 2048
HEAD_DIM = 128
N_NSA_HEADS = 16
N_KV_GROUPS = 4
GROUP_SIZE = 4
N_BRANCH = 3
CMP_BLOCK = 32
CMP_STRIDE = 16
SEL_BLOCK = 64
TOP_N = 16
WINDOW = 512
RET_KEY_DIM = 256
RET_VAL_DIM = 256
N_RET_HEADS = 8
RET_CHUNK = 128
ROPE_BASE = 10000.0
NSA_WIDTH = 2048
RET_WIDTH = 2048
D_FF = 11008
NUM_BUCKETS = 32
MAX_DISTANCE = 128
EPS = 1e-6
NEG_INF = -1e30
FORCE_SCORE = 1e9
N_SEL = SEQ // SEL_BLOCK
N_CMP = (SEQ - CMP_BLOCK) // CMP_STRIDE + 1
SCALE = HEAD_DIM ** -0.5
LOG2E = math.log2(math.e)
EXP2_SCALE = SCALE * LOG2E

LANES = 128
VMEM_LIMIT = 56 * 1024 * 1024
NSA_VMEM_LIMIT = 60 * 1024 * 1024

TQ = 256
TK = 256
NQ = SEQ // TQ
RET_HEADS_PER_STEP = 2
V_ROWS = HEAD_DIM + 16

W_IN_NSA_COLS = NSA_WIDTH + N_BRANCH * 2 * N_KV_GROUPS * HEAD_DIM
W_IN_GATE_COLS = N_NSA_HEADS * N_BRANCH
COL_KV = NSA_WIDTH
COL_QR = 0
COL_KR = 2048
COL_VR = 4096
COL_GR = 6144


def _bucket_thresholds():
    rel = np.arange(0, 4 * MAX_DISTANCE)
    max_exact = NUM_BUCKETS // 2
    nf = np.maximum(rel, 1).astype(np.float64)
    large = max_exact + np.floor(np.log(nf / max_exact) / math.log(MAX_DISTANCE / max_exact)
                                 * (NUM_BUCKETS - max_exact)).astype(np.int64)
    large = np.minimum(large, NUM_BUCKETS - 1)
    b = np.where(rel < max_exact, rel, large)
    return [int(np.argmax(b >= k)) for k in range(NUM_BUCKETS)]


_THR = _bucket_thresholds()


def _dot(a, b):
    return jnp.dot(a, b, preferred_element_type=F32)


def _dot_nt(a, b):
    return lax.dot_general(a, b, (((1,), (1,)), ((), ())), preferred_element_type=F32)


def _rms_kernel(x_ref, w_ref, o_ref):
    x = x_ref[...]
    y = x * lax.rsqrt(jnp.mean(x * x, axis=-1, keepdims=True) + EPS)
    o_ref[...] = (y * w_ref[...]).astype(o_ref.dtype)


def _rmsnorm(x, w, tm=256):
    m, d = x.shape
    return pl.pallas_call(
        _rms_kernel,
        out_shape=jax.ShapeDtypeStruct((m, d), BF16),
        grid=(m // tm,),
        in_specs=[pl.BlockSpec((tm, d), lambda i: (i, 0)),
                  pl.BlockSpec((1, d), lambda i: (0, 0))],
        out_specs=pl.BlockSpec((tm, d), lambda i: (i, 0)),
        compiler_params=pltpu.CompilerParams(dimension_semantics=("arbitrary",),
                                             vmem_limit_bytes=VMEM_LIMIT),
    )(x, w.reshape(1, d))


def _cast_spec(grid, row0, n_rows, n_cols, n_slabs):
    slab = n_rows // n_slabs
    sub = 8
    n_steps = math.prod(grid)
    assert slab * n_slabs == n_rows and slab % 16 == 0 and n_cols % LANES == 0
    assert row0 % sub == 0 and n_slabs <= n_steps

    def slab_index(ids):
        step = ids[0]
        for dim, idx in zip(grid[1:], ids[1:]):
            step = step * dim + idx
        return step if n_slabs == n_steps else jnp.minimum(step, n_slabs - 1)

    in_spec = pl.BlockSpec((pl.Element(slab), pl.Element(n_cols)),
                           lambda *ids: ((row0 // sub + slab_index(ids) * (slab // sub)) * sub, 0))
    out_spec = pl.BlockSpec((slab, n_cols), lambda *ids: (slab_index(ids), 0))
    return in_spec, out_spec, jax.ShapeDtypeStruct((n_rows, n_cols), BF16)


def _cast_specs(casts, grid):
    specs = [_cast_spec(grid, row0, n_rows, arr.shape[1], n_slabs) for arr, row0, n_rows, n_slabs in casts]
    return [s[0] for s in specs], [s[1] for s in specs], [s[2] for s in specs]


def _cast_slabs(refs, n_casts):
    for src, dst in zip(refs[:n_casts], refs[len(refs) - n_casts:]):
        dst[...] = src[...].astype(dst.dtype)


def _mm_kernel(a_ref, b_ref, *rest, n_casts):
    o_ref = rest[n_casts]
    o_ref[...] = _dot(a_ref[...], b_ref[...]).astype(o_ref.dtype)
    _cast_slabs(rest[:n_casts] + rest[n_casts + 1:], n_casts)


def _matmul(a, b, out_dtype, tm, tn, casts=()):
    m, k = a.shape
    _, n = b.shape
    grid = (m // tm, n // tn)
    c_in, c_out, c_shapes = _cast_specs(casts, grid)
    outs = pl.pallas_call(
        functools.partial(_mm_kernel, n_casts=len(casts)),
        out_shape=[jax.ShapeDtypeStruct((m, n), out_dtype)] + c_shapes,
        grid=grid,
        in_specs=[pl.BlockSpec((tm, k), lambda i, j: (i, 0)),
                  pl.BlockSpec((k, tn), lambda i, j: (0, j))] + c_in,
        out_specs=[pl.BlockSpec((tm, tn), lambda i, j: (i, j))] + c_out,
        compiler_params=pltpu.CompilerParams(dimension_semantics=("arbitrary", "arbitrary"),
                                             vmem_limit_bytes=VMEM_LIMIT),
    )(a, b, *[c[0] for c in casts])
    return outs


def _inproj_kernel(a_ref, wt_ref, *rest, n_casts):
    o_ref = rest[n_casts]
    wbf_ref = rest[-1]

    @pl.when(pl.program_id(1) == 0)
    def _():
        wbf_ref[...] = wt_ref[...].astype(BF16)

    o_ref[...] = _dot_nt(a_ref[...], wbf_ref[...])
    _cast_slabs(rest[:n_casts] + rest[n_casts + 1:-1], n_casts)


def _in_projection(a, w_t, row0, n_cols, tn, tm=1024, casts=()):
    m, k = a.shape
    grid = (n_cols // tn, m // tm)
    c_in, c_out, c_shapes = _cast_specs(casts, grid)
    sub = 8
    assert row0 % sub == 0 and tn % sub == 0
    return pl.pallas_call(
        functools.partial(_inproj_kernel, n_casts=len(casts)),
        out_shape=[jax.ShapeDtypeStruct((m, n_cols), F32)] + c_shapes,
        grid=grid,
        in_specs=[pl.BlockSpec((tm, k), lambda j, i: (i, 0)),
                  pl.BlockSpec((pl.Element(tn), pl.Element(k)),
                               lambda j, i: ((row0 // sub + j * (tn // sub)) * sub, 0),
                               pipeline_mode=pl.Buffered(1))] + c_in,
        out_specs=[pl.BlockSpec((tm, tn), lambda j, i: (i, j))] + c_out,
        scratch_shapes=[pltpu.VMEM((tn, k), BF16)],
        compiler_params=pltpu.CompilerParams(dimension_semantics=("arbitrary", "arbitrary"),
                                             vmem_limit_bytes=VMEM_LIMIT),
    )(a, w_t, *[c[0] for c in casts])


def _outproj_kernel(a1_ref, a2_ref, w1_ref, w2_ref, x_ref, *rest, n_casts):
    o_ref = rest[n_casts]
    acc = _dot(a1_ref[...], w1_ref[...]) + _dot(a2_ref[...], w2_ref[...])
    o_ref[...] = x_ref[...] + acc
    _cast_slabs(rest[:n_casts] + rest[n_casts + 1:], n_casts)


def _out_projection(o_nsa, o_ret, w_out, x, tm=1024, tn=512, casts=()):
    m = x.shape[0]
    kh = NSA_WIDTH
    grid = (m // tm, D_MODEL // tn)
    c_in, c_out, c_shapes = _cast_specs(casts, grid)
    return pl.pallas_call(
        functools.partial(_outproj_kernel, n_casts=len(casts)),
        out_shape=[jax.ShapeDtypeStruct((m, D_MODEL), F32)] + c_shapes,
        grid=grid,
        in_specs=[pl.BlockSpec((tm, kh), lambda i, j: (i, 0)),
                  pl.BlockSpec((tm, kh), lambda i, j: (i, 0)),
                  pl.BlockSpec((kh, tn), lambda i, j: (0, j)),
                  pl.BlockSpec((kh, tn), lambda i, j: (1, j)),
                  pl.BlockSpec((tm, tn), lambda i, j: (i, j))] + c_in,
        out_specs=[pl.BlockSpec((tm, tn), lambda i, j: (i, j))] + c_out,
        compiler_params=pltpu.CompilerParams(dimension_semantics=("arbitrary", "arbitrary"),
                                             vmem_limit_bytes=VMEM_LIMIT),
    )(o_nsa, o_ret, w_out, w_out, x, *[c[0] for c in casts])


def _ffn_up_kernel(h_ref, wg_ref, wu_ref, *rest, n_casts):
    o_ref = rest[n_casts]
    h = h_ref[...]
    g = _dot(h, wg_ref[...])
    u = _dot(h, wu_ref[...])
    o_ref[...] = (g * jax.nn.sigmoid(g) * u).astype(o_ref.dtype)
    _cast_slabs(rest[:n_casts] + rest[n_casts + 1:], n_casts)


def _ffn_up(h, w_gate, w_up, tm=1024, tn=256, casts=()):
    m, k = h.shape
    n = w_gate.shape[1]
    grid = (m // tm, n // tn)
    c_in, c_out, c_shapes = _cast_specs(casts, grid)
    return pl.pallas_call(
        functools.partial(_ffn_up_kernel, n_casts=len(casts)),
        out_shape=[jax.ShapeDtypeStruct((m, n), BF16)] + c_shapes,
        grid=grid,
        in_specs=[pl.BlockSpec((tm, k), lambda i, j: (i, 0)),
                  pl.BlockSpec((k, tn), lambda i, j: (0, j)),
                  pl.BlockSpec((k, tn), lambda i, j: (0, j))] + c_in,
        out_specs=[pl.BlockSpec((tm, tn), lambda i, j: (i, j))] + c_out,
        compiler_params=pltpu.CompilerParams(dimension_semantics=("arbitrary", "arbitrary"),
                                             vmem_limit_bytes=VMEM_LIMIT),
    )(h, w_gate, w_up, *[c[0] for c in casts])


def _ffn_down_kernel(a_ref, b_ref, x_ref, o_ref, acc_ref):
    kk = pl.program_id(2)
    last = pl.num_programs(2) - 1

    @pl.when(kk == 0)
    def _():
        acc_ref[...] = x_ref[...] + _dot(a_ref[...], b_ref[...])

    @pl.when(jnp.logical_and(kk > 0, kk < last))
    def _():
        acc_ref[...] += _dot(a_ref[...], b_ref[...])

    @pl.when(kk == last)
    def _():
        o_ref[...] = acc_ref[...] + _dot(a_ref[...], b_ref[...])


def _ffn_down(a, w_down, x, tm=1024, tn=512, tk=5504):
    m, k = a.shape
    n = w_down.shape[1]
    assert k // tk >= 2
    return pl.pallas_call(
        _ffn_down_kernel,
        out_shape=jax.ShapeDtypeStruct((m, n), F32),
        grid=(m // tm, n // tn, k // tk),
        in_specs=[pl.BlockSpec((tm, tk), lambda i, j, kk: (i, kk)),
                  pl.BlockSpec((tk, tn), lambda i, j, kk: (kk, j)),
                  pl.BlockSpec((tm, tn), lambda i, j, kk: (i, j))],
        out_specs=pl.BlockSpec((tm, tn), lambda i, j, kk: (i, j)),
        scratch_shapes=[pltpu.VMEM((tm, tn), F32)],
        compiler_params=pltpu.CompilerParams(
            dimension_semantics=("arbitrary", "arbitrary", "arbitrary"),
            vmem_limit_bytes=VMEM_LIMIT),
    )(a, w_down, x)


def _bias_value(rel, relb_ref, h):
    val = jnp.full(rel.shape, relb_ref[0, h], F32)
    for k in range(1, NUM_BUCKETS):
        val = jnp.where(rel >= _THR[k], relb_ref[k, h], val)
    return val


def _bias_tables_kernel(relb_ref, biasc_ref, dm_ref):
    h = pl.program_id(0)
    sub = 8
    j = lax.broadcasted_iota(jnp.int32, (sub, 2 * SEQ), 1)
    row_c = _bias_value(j - SEQ - (CMP_BLOCK - 1), relb_ref, h)[0:1]
    toe_c = pltpu.roll(jnp.broadcast_to(row_c, (LANES, 2 * SEQ)), 0, 1,
                       stride=CMP_STRIDE, stride_axis=0)
    for qt in range(NQ):
        biasc_ref[0, qt] = toe_c[:, SEQ + qt * TQ:SEQ + (qt + 1) * TQ]
    far = relb_ref[NUM_BUCKETS - 1, h]
    j2 = lax.broadcasted_iota(jnp.int32, (sub, 2 * TK), 1)
    b = lax.broadcasted_iota(jnp.int32, (TK, TQ), 0)
    a = lax.broadcasted_iota(jnp.int32, (TK, TQ), 1)
    for kind in range(2):
        row = ((_bias_value(j2 + (kind - 1) * TK, relb_ref, h) - far) * LOG2E)[0:1]
        toe = pltpu.roll(jnp.broadcast_to(row, (TK, 2 * TK)), 0, 1, stride=1, stride_axis=0)
        tile = toe[:, TK:]
        dm_ref[0, kind] = tile + jnp.where(a >= b, 0.0, NEG_INF) if kind == 0 else tile
    dm_ref[0, 2] = jnp.where(b > a, 0.0, NEG_INF)


def _bias_tables(rel_bias):
    return pl.pallas_call(
        _bias_tables_kernel,
        out_shape=(jax.ShapeDtypeStruct((N_KV_GROUPS, NQ, LANES, GROUP_SIZE * TQ), F32),
                   jax.ShapeDtypeStruct((N_KV_GROUPS, 3, TK, GROUP_SIZE * TQ), F32)),
        grid=(N_NSA_HEADS,),
        in_specs=[pl.BlockSpec(memory_space=pltpu.SMEM)],
        out_specs=(pl.BlockSpec((1, NQ, LANES, TQ), lambda h: (h // GROUP_SIZE, 0, 0, h % GROUP_SIZE)),
                   pl.BlockSpec((1, 3, TK, TQ), lambda h: (h // GROUP_SIZE, 0, 0, h % GROUP_SIZE))),
        compiler_params=pltpu.CompilerParams(dimension_semantics=("arbitrary",),
                                             vmem_limit_bytes=VMEM_LIMIT),
    )(rel_bias)


def _row_rms(x, w):
    return x * lax.rsqrt(jnp.mean(x * x, axis=-1, keepdims=True) + EPS) * w


N_NSA_INPUTS = 17
N_NSA_SCRATCH = 7


def _nsa_kernel(*refs, n_casts):
    (q_ref, k0_ref, v0_ref, k1_ref, v1_ref, k2_ref, v2_ref, gate_ref,
     qnw_ref, knw_ref, pek_ref, pev_ref, wck_ref, wcv_ref, biasc_ref, dm_ref, c2st_ref) = refs[:N_NSA_INPUTS]
    side_a_ref, side_bt_ref = refs[N_NSA_INPUTS:N_NSA_INPUTS + 2]
    n_in = N_NSA_INPUTS + 2 + n_casts
    cast_in = refs[N_NSA_INPUTS + 2:n_in]
    o_ref, side_o_ref = refs[n_in:n_in + 2]
    cast_out = refs[n_in + 2:n_in + 2 + n_casts]
    kc_s, vct_s, kaug_s, kw_s, vst_s, vwt_s, gt_s = refs[len(refs) - N_NSA_SCRATCH:]
    qt = pl.program_id(2)
    cols_all = GROUP_SIZE * TQ
    side_half = side_bt_ref.shape[0] // 2

    @pl.when(qt == 0)
    def _prologue():
        kaug_s[:, :HEAD_DIM] = (_row_rms(k1_ref[...], knw_ref[1:2, :]) * EXP2_SCALE).astype(BF16)
        jblk = lax.broadcasted_iota(jnp.int32, (SEQ, LANES), 0) // SEL_BLOCK
        lane = lax.broadcasted_iota(jnp.int32, (SEQ, LANES), 1)
        kaug_s[:, HEAD_DIM:] = jnp.where(jblk == lane, 1.0, 0.0).astype(BF16)
        kw_s[...] = (_row_rms(k2_ref[...], knw_ref[2:3, :]) * EXP2_SCALE).astype(BF16)
        ones_row = jnp.where(lax.broadcasted_iota(jnp.int32, (V_ROWS - HEAD_DIM, TK), 0) == 0,
                             1.0, 0.0).astype(BF16)
        for t in range(SEQ // TK):
            vst_s[t, :HEAD_DIM, :] = v1_ref[t * TK:(t + 1) * TK, :].T.astype(BF16)
            vst_s[t, HEAD_DIM:, :] = ones_row
            vwt_s[t, :HEAD_DIM, :] = v2_ref[t * TK:(t + 1) * TK, :].T.astype(BF16)
            vwt_s[t, HEAD_DIM:, :] = ones_row

        def compress(x_ref, pe_ref, w_ref):
            acc_lo = jnp.zeros((LANES, HEAD_DIM), F32)
            acc_hi = jnp.zeros((LANES, HEAD_DIM), F32)
            for j in range(CMP_STRIDE):
                y = x_ref[pl.ds(j, LANES, stride=CMP_STRIDE), :]
                acc_lo += _dot((y + pe_ref[j:j + 1, :]).astype(BF16),
                               w_ref[j * HEAD_DIM:(j + 1) * HEAD_DIM, :])
                jh = CMP_STRIDE + j
                acc_hi += _dot((y + pe_ref[jh:jh + 1, :]).astype(BF16),
                               w_ref[jh * HEAD_DIM:(jh + 1) * HEAD_DIM, :])
            return acc_lo + pltpu.roll(acc_hi, LANES - 1, 0)

        kc_s[...] = _row_rms(compress(k0_ref, pek_ref, wck_ref), knw_ref[0:1, :]).astype(BF16)
        vct_s[...] = compress(v0_ref, pev_ref, wcv_ref).T.astype(BF16)

    side_o_ref[:, :side_half] = _dot_nt(side_a_ref[...], side_bt_ref[:side_half, :])
    _cast_slabs(cast_in + cast_out, n_casts)

    t0 = qt * TQ
    q = q_ref[...]
    qts = []
    for r in range(GROUP_SIZE):
        qts.append(_row_rms(q[:, r * HEAD_DIM:(r + 1) * HEAD_DIM], qnw_ref[...]).T)
    qt_stack = jnp.concatenate(qts, axis=1).astype(BF16)

    sc = _dot(kc_s[...], qt_stack) * SCALE + biasc_ref[0, 0]
    tpos = t0 + (lax.broadcasted_iota(jnp.int32, (LANES, cols_all), 1) & (TQ - 1))
    nrow = lax.broadcasted_iota(jnp.int32, (LANES, cols_all), 0)
    maskc = tpos >= CMP_STRIDE * nrow + (CMP_BLOCK - 1)
    scm = jnp.where(maskc, sc, NEG_INF)
    mc = jnp.max(scm, axis=0, keepdims=True)
    ec = jnp.where(maskc, jnp.exp(scm - mc), 0.0)
    lc = jnp.sum(ec, axis=0, keepdims=True)
    pc = ec * (1.0 / jnp.where(lc > 0.0, lc, 1.0))
    o_c = _dot(vct_s[...], pc.astype(BF16))

    psum = pc[:, 0:TQ] + pc[:, TQ:2 * TQ] + pc[:, 2 * TQ:3 * TQ] + pc[:, 3 * TQ:4 * TQ]
    p_hi = psum.astype(BF16)
    p_lo = (psum - p_hi.astype(F32)).astype(BF16)
    imp = (_dot(c2st_ref[...], p_hi) + _dot(c2st_ref[...], p_lo))[0:N_SEL]
    tq_pos = t0 + lax.broadcasted_iota(jnp.int32, (N_SEL, TQ), 1)
    sidx = lax.broadcasted_iota(jnp.int32, (N_SEL, TQ), 0)
    cur = tq_pos >> 6
    score = jnp.where(sidx <= cur, imp, NEG_INF)
    for forced_blk in (cur - 1, cur, jnp.zeros_like(cur)):
        score = jnp.where(sidx == forced_blk, FORCE_SCORE, score)
    rank = jnp.zeros((N_SEL, TQ), F32)
    for sp in range(N_SEL):
        row = score[sp:sp + 1, :]
        tie = jnp.where(sidx > sp, 1.0, 0.0)
        rank = rank + jnp.where(row > score, 1.0, jnp.where(row == score, tie, 0.0))
    selb = jnp.where(rank < TOP_N, 0.0, NEG_INF)
    selb = jnp.concatenate([selb, jnp.zeros((HEAD_DIM - N_SEL, TQ), F32)], axis=0).astype(BF16)
    qt_aug = jnp.concatenate([qt_stack, jnp.concatenate([selb] * GROUP_SIZE, axis=1)], axis=0)

    m0 = jnp.full((1, cols_all), NEG_INF, F32)
    a0 = jnp.zeros((V_ROWS, cols_all), F32)

    def online(carry, tiles):
        m, acc = carry
        m_new = m
        for u, _ in tiles:
            m_new = jnp.maximum(m_new, jnp.max(u, axis=0, keepdims=True))
        acc = jnp.exp2(m - m_new) * acc
        for u, vt in tiles:
            acc = acc + _dot(vt, jnp.exp2(u - m_new).astype(BF16))
        return m_new, acc

    def normalised(acc):
        return acc[:HEAD_DIM] * (1.0 / acc[HEAD_DIM:HEAD_DIM + 1])

    def key_rows(kt):
        return pl.ds(pl.multiple_of(kt * TK, TK), TK)

    def sel_tile(kt):
        return _dot(kaug_s[key_rows(kt), :], qt_aug), vst_s[kt]

    n_far = jnp.maximum(qt - 1, 0)
    carry = lax.fori_loop(0, n_far >> 1,
                          lambda i, c: online(c, [sel_tile(2 * i), sel_tile(2 * i + 1)]), (m0, a0))
    carry = lax.fori_loop(0, n_far & 1, lambda i, c: online(c, [sel_tile(n_far - 1)]), carry)
    kt1 = jnp.maximum(qt - 1, 0)
    u1, vt1 = sel_tile(kt1)
    u0, vt0 = sel_tile(qt)
    _, acc_s = online(carry, [(u1 + dm_ref[0, 1] + jnp.where(qt >= 1, 0.0, NEG_INF), vt1),
                              (u0 + dm_ref[0, 0], vt0)])
    o_s = normalised(acc_s)

    side_o_ref[:, side_half:] = _dot_nt(side_a_ref[...], side_bt_ref[side_half:, :])

    tiles = []
    for dist, kind in ((2, 2), (1, 1), (0, 0)):
        kt = jnp.maximum(qt - dist, 0)
        u = _dot(kw_s[key_rows(kt), :], qt_stack) + dm_ref[0, kind]
        if dist > 0:
            u = u + jnp.where(qt >= dist, 0.0, NEG_INF)
        tiles.append((u, vwt_s[kt]))
    _, acc_w = online((m0, a0), tiles)
    o_w = normalised(acc_w)

    gt_s[...] = jax.nn.sigmoid(gate_ref[...]).T
    head0 = pl.program_id(1) * GROUP_SIZE
    for r in range(GROUP_SIZE):
        cs = slice(r * TQ, (r + 1) * TQ)
        c = (head0 + r) * N_BRANCH
        o = (gt_s[pl.ds(c, 1), :] * o_c[:, cs] + gt_s[pl.ds(c + 1, 1), :] * o_s[:, cs]
             + gt_s[pl.ds(c + 2, 1), :] * o_w[:, cs])
        o_ref[:, r * HEAD_DIM:(r + 1) * HEAD_DIM] = o.T.astype(o_ref.dtype)


def _cmp_to_sel_t():
    ss = np.arange(LANES)[:, None] * SEL_BLOCK
    cs = np.arange(LANES)[None, :] * CMP_STRIDE
    ov = np.clip(np.minimum(cs + CMP_BLOCK, ss + SEL_BLOCK) - np.maximum(cs, ss), 0, None)
    m = ov.astype(np.float32) / np.float32(CMP_BLOCK)
    m[N_SEL:, :] = 0.0
    m[:, N_CMP:] = 0.0
    return jnp.asarray(m, BF16)


def _nsa_attention(proj, gate_logits, batch, q_norm_w, k_norm_w, pe_k, pe_v, w_cmp_k, w_cmp_v, biasc, dm,
                   side_a, side_bt, side_tm=1024, side_tn=512, casts=()):
    kvb = COL_KV // LANES

    def kv_spec(branch, which):
        base = kvb + (branch * 2 + which) * N_KV_GROUPS
        return pl.BlockSpec((SEQ, LANES), lambda b, g, qt: (b, base + g))

    full = lambda shape: pl.BlockSpec(shape, lambda b, g, qt: (0,) * len(shape))
    grid = (batch, N_KV_GROUPS, NQ)
    c_in, c_out, c_shapes = _cast_specs(casts, grid)
    side_m, side_k = side_a.shape
    side_n = side_bt.shape[0]
    n_j = side_n // side_tn
    assert (side_m // side_tm) * n_j == math.prod(grid)
    step = lambda b, g, qt: (b * N_KV_GROUPS + g) * NQ + qt
    side_in = [pl.BlockSpec((side_tm, side_k), lambda b, g, qt: (step(b, g, qt) // n_j, 0),
                            pipeline_mode=pl.Buffered(1)),
               pl.BlockSpec((side_tn, side_k), lambda b, g, qt: (step(b, g, qt) % n_j, 0))]
    side_out = pl.BlockSpec((side_tm, side_tn), lambda b, g, qt: (step(b, g, qt) // n_j, step(b, g, qt) % n_j))
    return pl.pallas_call(
        functools.partial(_nsa_kernel, n_casts=len(casts)),
        out_shape=[jax.ShapeDtypeStruct((batch * SEQ, NSA_WIDTH), BF16),
                   jax.ShapeDtypeStruct((side_m, side_n), F32)] + c_shapes,
        grid=grid,
        in_specs=[pl.BlockSpec((TQ, GROUP_SIZE * HEAD_DIM), lambda b, g, qt: (b * NQ + qt, g)),
                  kv_spec(0, 0), kv_spec(0, 1), kv_spec(1, 0), kv_spec(1, 1),
                  kv_spec(2, 0), kv_spec(2, 1),
                  pl.BlockSpec((TQ, LANES), lambda b, g, qt: (b * NQ + qt, 0)),
                  full((1, HEAD_DIM)), full((N_BRANCH, HEAD_DIM)),
                  full((CMP_BLOCK, HEAD_DIM)), full((CMP_BLOCK, HEAD_DIM)),
                  full((CMP_BLOCK * HEAD_DIM, HEAD_DIM)), full((CMP_BLOCK * HEAD_DIM, HEAD_DIM)),
                  pl.BlockSpec((1, 1, LANES, GROUP_SIZE * TQ), lambda b, g, qt: (g, qt, 0, 0)),
                  pl.BlockSpec((1, 3, TK, GROUP_SIZE * TQ), lambda b, g, qt: (g, 0, 0, 0),
                               pipeline_mode=pl.Buffered(1)),
                  full((LANES, LANES))] + side_in + c_in,
        out_specs=[pl.BlockSpec((TQ, GROUP_SIZE * HEAD_DIM), lambda b, g, qt: (b * NQ + qt, g)),
                   side_out] + c_out,
        scratch_shapes=[pltpu.VMEM((LANES, HEAD_DIM), BF16),
                        pltpu.VMEM((HEAD_DIM, LANES), BF16),
                        pltpu.VMEM((SEQ, 2 * HEAD_DIM), BF16),
                        pltpu.VMEM((SEQ, HEAD_DIM), BF16),
                        pltpu.VMEM((SEQ // TK, V_ROWS, TK), BF16),
                        pltpu.VMEM((SEQ // TK, V_ROWS, TK), BF16),
                        pltpu.VMEM((LANES, TQ), F32)],
        compiler_params=pltpu.CompilerParams(
            dimension_semantics=("arbitrary", "arbitrary", "arbitrary"),
            vmem_limit_bytes=NSA_VMEM_LIMIT),
    )(proj, proj, proj, proj, proj, proj, proj, gate_logits,
      q_norm_w.reshape(1, HEAD_DIM), k_norm_w, pe_k, pe_v,
      w_cmp_k.astype(BF16), w_cmp_v.astype(BF16), biasc, dm, _cmp_to_sel_t(),
      side_a, side_bt, *[c[0] for c in casts])


def _ret_kernel(lg_ref, q_ref, k_ref, v_ref, g_ref, cos_ref, sin_ref, gnw_ref, *rest, n_casts):
    o_ref = rest[n_casts]
    state_ref = rest[-1]
    _cast_slabs(rest[:n_casts] + rest[n_casts + 1:-1], n_casts)
    c_len = RET_CHUNK
    ii = lax.broadcasted_iota(jnp.int32, (c_len, c_len), 0)
    jj = lax.broadcasted_iota(jnp.int32, (c_len, c_len), 1)
    diff = (ii - jj).astype(F32)
    icol = lax.broadcasted_iota(jnp.int32, (c_len, 1), 0).astype(F32)
    half = RET_KEY_DIM // 2
    state_ref[...] = jnp.zeros_like(state_ref)
    tables = []
    for hh in range(RET_HEADS_PER_STEP):
        lg = lg_ref[pl.program_id(1) * RET_HEADS_PER_STEP + hh]
        tables.append((
            jnp.where(diff >= 0.0, jnp.exp(jnp.maximum(diff, 0.0) * lg), 0.0),
            jnp.exp((icol + 1.0) * lg),
            jnp.exp((c_len - 1.0 - icol) * lg),
            jnp.exp(jnp.full((1, RET_VAL_DIM), float(c_len), F32) * lg)))

    def body(c, _):
        rows = pl.ds(pl.multiple_of(c * c_len, c_len), c_len)
        cos = cos_ref[rows, :]
        sin = sin_ref[rows, :]

        def rot(x):
            x1, x2 = x[:, :half], x[:, half:]
            return jnp.concatenate([x1 * cos - x2 * sin, x2 * cos + x1 * sin], axis=-1)

        for hh, (decay_in, xi, zeta, chunk_decay) in enumerate(tables):
            kcols = slice(hh * RET_KEY_DIM, (hh + 1) * RET_KEY_DIM)
            vcols = slice(hh * RET_VAL_DIM, (hh + 1) * RET_VAL_DIM)
            qr = rot(q_ref[rows, kcols])
            kr = rot(k_ref[rows, kcols]) * (RET_KEY_DIM ** -0.5)
            vb = v_ref[rows, vcols].astype(BF16)
            scores = _dot_nt(qr.astype(BF16), kr.astype(BF16)) * decay_in
            inner = _dot(scores.astype(BF16), vb)
            st = state_ref[hh]
            cross = _dot((qr * xi).astype(BF16), st.astype(BF16))
            kv = _dot((kr * zeta).T.astype(BF16), vb)
            state_ref[hh] = chunk_decay * st + kv
            y = inner + cross
            y = y * lax.rsqrt(jnp.mean(y * y, axis=-1, keepdims=True) + EPS) * gnw_ref[:, vcols]
            g = g_ref[rows, vcols]
            o_ref[rows, vcols] = (g * jax.nn.sigmoid(g) * y).astype(o_ref.dtype)
        return 0

    lax.fori_loop(0, SEQ // c_len, body, 0)


def _retention(proj, batch, gn_w, casts=()):
    pos = np.arange(SEQ, dtype=np.float64)
    half = RET_KEY_DIM // 2
    inv = np.exp(-np.linspace(0.0, 1.0, half) * math.log(ROPE_BASE)).astype(np.float32)
    ang = pos.astype(np.float32)[:, None] * inv[None, :]
    cos = jnp.asarray(np.cos(ang.astype(np.float64)), F32)
    sin = jnp.asarray(np.sin(ang.astype(np.float64)), F32)
    log_gamma = jnp.asarray(np.log(1.0 - 2.0 ** (-5.0 - np.arange(N_RET_HEADS))), F32)
    w = RET_KEY_DIM * RET_HEADS_PER_STEP

    def col_spec(col0):
        return pl.BlockSpec((SEQ, w), lambda b, h: (b, col0 // w + h))

    grid = (batch, N_RET_HEADS // RET_HEADS_PER_STEP)
    c_in, c_out, c_shapes = _cast_specs(casts, grid)
    return pl.pallas_call(
        functools.partial(_ret_kernel, n_casts=len(casts)),
        out_shape=[jax.ShapeDtypeStruct((batch * SEQ, RET_WIDTH), BF16)] + c_shapes,
        grid=grid,
        in_specs=[pl.BlockSpec(memory_space=pltpu.SMEM),
                  col_spec(COL_QR), col_spec(COL_KR), col_spec(COL_VR), col_spec(COL_GR),
                  pl.BlockSpec((SEQ, half), lambda b, h: (0, 0)),
                  pl.BlockSpec((SEQ, half), lambda b, h: (0, 0)),
                  pl.BlockSpec((1, w), lambda b, h: (0, h))] + c_in,
        out_specs=[pl.BlockSpec((SEQ, w), lambda b, h: (b, h))] + c_out,
        scratch_shapes=[pltpu.VMEM((RET_HEADS_PER_STEP, RET_KEY_DIM, RET_VAL_DIM), F32)],
        compiler_params=pltpu.CompilerParams(dimension_semantics=("arbitrary", "arbitrary"),
                                             vmem_limit_bytes=VMEM_LIMIT),
    )(log_gamma, proj, proj, proj, proj, cos, sin, gn_w.reshape(1, RET_WIDTH), *[c[0] for c in casts])


def _layer(x, norm1_w, w_in, nsa_q_norm_w, nsa_k_norm_w, cmp_pe_k, cmp_pe_v, w_cmp_k, w_cmp_v,
           rel_bias, ret_gn_w, w_out, norm2_w, w_gate, w_up, w_down):
    batch = x.shape[0]
    xf = x.reshape(batch * SEQ, D_MODEL)
    h = _rmsnorm(xf, norm1_w)
    w_in_t = w_in.T
    gate0 = W_IN_NSA_COLS
    gate1 = gate0 + W_IN_GATE_COLS
    ret_cols = 4 * RET_WIDTH
    proj_nsa, w_ret_t = _in_projection(h, w_in_t, 0, W_IN_NSA_COLS, 1024, tm=512,
                                       casts=((w_in_t, gate1, ret_cols, 64),))
    gate_logits, = _in_projection(h, w_in_t, gate0, LANES, LANES)
    biasc, dm = _bias_tables(rel_bias)
    n_steps = batch * N_KV_GROUPS * NQ
    o_nsa, proj_ret, w_gate_b, w_up_b, w_out_b = _nsa_attention(
        proj_nsa, gate_logits, batch, nsa_q_norm_w, nsa_k_norm_w, cmp_pe_k, cmp_pe_v,
        w_cmp_k, w_cmp_v, biasc, dm, h, w_ret_t,
        casts=((w_gate, 0, D_MODEL, n_steps), (w_up, 0, D_MODEL, n_steps), (w_out, 0, D_MODEL, n_steps)))
    o_ret, = _retention(proj_ret, batch, ret_gn_w)
    x1, = _out_projection(o_nsa, o_ret, w_out_b, xf, tn=1024)
    hf = _rmsnorm(x1, norm2_w)
    mid, w_down_b = _ffn_up(hf, w_gate_b, w_up_b, tm=2048, casts=((w_down, 0, D_FF, 172),))
    out = _ffn_down(mid, w_down_b, x1)
    return out.reshape(batch, SEQ, D_MODEL)


def kernel(x, norm1_w, w_in, nsa_q_norm_w, nsa_k_norm_w, cmp_pe_k, cmp_pe_v, w_cmp_k, w_cmp_v,
           rel_bias, ret_gn_w, w_out, norm2_w, w_gate, w_up, w_down):
    for l in range(norm1_w.shape[0]):
        x = _layer(x, norm1_w[l], w_in[l], nsa_q_norm_w[l], nsa_k_norm_w[l], cmp_pe_k[l], cmp_pe_v[l],
                   w_cmp_k[l], w_cmp_v[l], rel_bias, ret_gn_w[l], w_out[l], norm2_w[l],
                   w_gate[l], w_up[l], w_down[l])
    return x
```

```python
import functools
import math

import numpy as np
import jax
import jax.numpy as jnp
from jax import lax
from jax.experimental import pallas as pl
from jax.experimental.pallas import tpu as pltpu

F32 = jnp.float32
BF16 = jnp.bfloat16

D_MODEL = 4096
SEQ = 2048
HEAD_DIM = 128
N_NSA_HEADS = 16
N_KV_GROUPS = 4
GROUP_SIZE = 4
N_BRANCH = 3
CMP_BLOCK = 32
CMP_STRIDE = 16
SEL_BLOCK = 64
TOP_N = 16
WINDOW = 512
RET_KEY_DIM = 256
RET_VAL_DIM = 256
N_RET_HEADS = 8
RET_CHUNK = 128
ROPE_BASE = 10000.0
NSA_WIDTH = 2048
RET_WIDTH = 2048
D_FF = 11008
NUM_BUCKETS = 32
MAX_DISTANCE = 128
EPS = 1e-6
NEG_INF = -1e30
FORCE_SCORE = 1e9
N_SEL = SEQ // SEL_BLOCK
N_CMP = (SEQ - CMP_BLOCK) // CMP_STRIDE + 1
SCALE = HEAD_DIM ** -0.5
LOG2E = math.log2(math.e)
EXP2_SCALE = SCALE * LOG2E

LANES = 128
VMEM_LIMIT = 56 * 1024 * 1024
NSA_VMEM_LIMIT = 60 * 1024 * 1024

TQ = 256
TK = 256
NQ = SEQ // TQ
RET_HEADS_PER_STEP = 2
V_ROWS = HEAD_DIM + 16

W_IN_NSA_COLS = NSA_WIDTH + N_BRANCH * 2 * N_KV_GROUPS * HEAD_DIM
W_IN_GATE_COLS = N_NSA_HEADS * N_BRANCH
COL_KV = NSA_WIDTH
COL_QR = 0
COL_KR = 2048
COL_VR = 4096
COL_GR = 6144


def _bucket_thresholds():
    rel = np.arange(0, 4 * MAX_DISTANCE)
    max_exact = NUM_BUCKETS // 2
    nf = np.maximum(rel, 1).astype(np.float64)
    large = max_exact + np.floor(np.log(nf / max_exact) / math.log(MAX_DISTANCE / max_exact)
                                 * (NUM_BUCKETS - max_exact)).astype(np.int64)
    large = np.minimum(large, NUM_BUCKETS - 1)
    b = np.where(rel < max_exact, rel, large)
    return [int(np.argmax(b >= k)) for k in range(NUM_BUCKETS)]


_THR = _bucket_thresholds()


def _dot(a, b):
    return jnp.dot(a, b, preferred_element_type=F32)


def _dot_nt(a, b):
    return lax.dot_general(a, b, (((1,), (1,)), ((), ())), preferred_element_type=F32)


def _rms_kernel(x_ref, w_ref, o_ref):
    x = x_ref[...]
    y = x * lax.rsqrt(jnp.mean(x * x, axis=-1, keepdims=True) + EPS)
    o_ref[...] = (y * w_ref[...]).astype(o_ref.dtype)


def _rmsnorm(x, w, tm=256):
    m, d = x.shape
    return pl.pallas_call(
        _rms_kernel,
        out_shape=jax.ShapeDtypeStruct((m, d), BF16),
        grid=(m // tm,),
        in_specs=[pl.BlockSpec((tm, d), lambda i: (i, 0)),
                  pl.BlockSpec((1, d), lambda i: (0, 0))],
        out_specs=pl.BlockSpec((tm, d), lambda i: (i, 0)),
        compiler_params=pltpu.CompilerParams(dimension_semantics=("arbitrary",),
                                             vmem_limit_bytes=VMEM_LIMIT),
    )(x, w.reshape(1, d))


def _cast_spec(grid, row0, n_rows, n_cols, n_slabs):
    slab = n_rows // n_slabs
    sub = 8
    n_steps = math.prod(grid)
    assert slab * n_slabs == n_rows and slab % 16 == 0 and n_cols % LANES == 0
    assert row0 % sub == 0 and n_slabs <= n_steps

    def slab_index(ids):
        step = ids[0]
        for dim, idx in zip(grid[1:], ids[1:]):
            step = step * dim + idx
        return step if n_slabs == n_steps else jnp.minimum(step, n_slabs - 1)

    in_spec = pl.BlockSpec((pl.Element(slab), pl.Element(n_cols)),
                           lambda *ids: ((row0 // sub + slab_index(ids) * (slab // sub)) * sub, 0))
    out_spec = pl.BlockSpec((slab, n_cols), lambda *ids: (slab_index(ids), 0))
    return in_spec, out_spec, jax.ShapeDtypeStruct((n_rows, n_cols), BF16)


def _cast_specs(casts, grid):
    specs = [_cast_spec(grid, row0, n_rows, arr.shape[1], n_slabs) for arr, row0, n_rows, n_slabs in casts]
    return [s[0] for s in specs], [s[1] for s in specs], [s[2] for s in specs]


def _cast_slabs(refs, n_casts):
    for src, dst in zip(refs[:n_casts], refs[len(refs) - n_casts:]):
        dst[...] = src[...].astype(dst.dtype)


def _mm_kernel(a_ref, b_ref, *rest, n_casts):
    o_ref = rest[n_casts]
    o_ref[...] = _dot(a_ref[...], b_ref[...]).astype(o_ref.dtype)
    _cast_slabs(rest[:n_casts] + rest[n_casts + 1:], n_casts)


def _matmul(a, b, out_dtype, tm, tn, casts=()):
    m, k = a.shape
    _, n = b.shape
    grid = (m // tm, n // tn)
    c_in, c_out, c_shapes = _cast_specs(casts, grid)
    outs = pl.pallas_call(
        functools.partial(_mm_kernel, n_casts=len(casts)),
        out_shape=[jax.ShapeDtypeStruct((m, n), out_dtype)] + c_shapes,
        grid=grid,
        in_specs=[pl.BlockSpec((tm, k), lambda i, j: (i, 0)),
                  pl.BlockSpec((k, tn), lambda i, j: (0, j))] + c_in,
        out_specs=[pl.BlockSpec((tm, tn), lambda i, j: (i, j))] + c_out,
        compiler_params=pltpu.CompilerParams(dimension_semantics=("arbitrary", "arbitrary"),
                                             vmem_limit_bytes=VMEM_LIMIT),
    )(a, b, *[c[0] for c in casts])
    return outs


def _inproj_kernel(a_ref, wt_ref, *rest, n_casts):
    o_ref = rest[n_casts]
    wbf_ref = rest[-1]

    @pl.when(pl.program_id(1) == 0)
    def _():
        wbf_ref[...] = wt_ref[...].astype(BF16)

    o_ref[...] = _dot_nt(a_ref[...], wbf_ref[...])
    _cast_slabs(rest[:n_casts] + rest[n_casts + 1:-1], n_casts)


def _in_projection(a, w_t, row0, n_cols, tn, tm=1024, casts=()):
    m, k = a.shape
    grid = (n_cols // tn, m // tm)
    c_in, c_out, c_shapes = _cast_specs(casts, grid)
    sub = 8
    assert row0 % sub == 0 and tn % sub == 0
    return pl.pallas_call(
        functools.partial(_inproj_kernel, n_casts=len(casts)),
        out_shape=[jax.ShapeDtypeStruct((m, n_cols), F32)] + c_shapes,
        grid=grid,
        in_specs=[pl.BlockSpec((tm, k), lambda j, i: (i, 0)),
                  pl.BlockSpec((pl.Element(tn), pl.Element(k)),
                               lambda j, i: ((row0 // sub + j * (tn // sub)) * sub, 0),
                               pipeline_mode=pl.Buffered(1))] + c_in,
        out_specs=[pl.BlockSpec((tm, tn), lambda j, i: (i, j))] + c_out,
        scratch_shapes=[pltpu.VMEM((tn, k), BF16)],
        compiler_params=pltpu.CompilerParams(dimension_semantics=("arbitrary", "arbitrary"),
                                             vmem_limit_bytes=VMEM_LIMIT),
    )(a, w_t, *[c[0] for c in casts])


def _outproj_kernel(a1_ref, a2_ref, w1_ref, w2_ref, x_ref, *rest, n_casts):
    o_ref = rest[n_casts]
    acc = _dot(a1_ref[...], w1_ref[...]) + _dot(a2_ref[...], w2_ref[...])
    o_ref[...] = x_ref[...] + acc
    _cast_slabs(rest[:n_casts] + rest[n_casts + 1:], n_casts)


def _out_projection(o_nsa, o_ret, w_out, x, tm=1024, tn=512, casts=()):
    m = x.shape[0]
    kh = NSA_WIDTH
    grid = (m // tm, D_MODEL // tn)
    c_in, c_out, c_shapes = _cast_specs(casts, grid)
    return pl.pallas_call(
        functools.partial(_outproj_kernel, n_casts=len(casts)),
        out_shape=[jax.ShapeDtypeStruct((m, D_MODEL), F32)] + c_shapes,
        grid=grid,
        in_specs=[pl.BlockSpec((tm, kh), lambda i, j: (i, 0)),
                  pl.BlockSpec((tm, kh), lambda i, j: (i, 0)),
                  pl.BlockSpec((kh, tn), lambda i, j: (0, j)),
                  pl.BlockSpec((kh, tn), lambda i, j: (1, j)),
                  pl.BlockSpec((tm, tn), lambda i, j: (i, j))] + c_in,
        out_specs=[pl.BlockSpec((tm, tn), lambda i, j: (i, j))] + c_out,
        compiler_params=pltpu.CompilerParams(dimension_semantics=("arbitrary", "arbitrary"),
                                             vmem_limit_bytes=VMEM_LIMIT),
    )(o_nsa, o_ret, w_out, w_out, x, *[c[0] for c in casts])


def _ffn_up_kernel(h_ref, wg_ref, wu_ref, *rest, n_casts):
    o_ref = rest[n_casts]
    h = h_ref[...]
    g = _dot(h, wg_ref[...])
    u = _dot(h, wu_ref[...])
    o_ref[...] = (g * jax.nn.sigmoid(g) * u).astype(o_ref.dtype)
    _cast_slabs(rest[:n_casts] + rest[n_casts + 1:], n_casts)


def _ffn_up(h, w_gate, w_up, tm=1024, tn=256, casts=()):
    m, k = h.shape
    n = w_gate.shape[1]
    grid = (m // tm, n // tn)
    c_in, c_out, c_shapes = _cast_specs(casts, grid)
    return pl.pallas_call(
        functools.partial(_ffn_up_kernel, n_casts=len(casts)),
        out_shape=[jax.ShapeDtypeStruct((m, n), BF16)] + c_shapes,
        grid=grid,
        in_specs=[pl.BlockSpec((tm, k), lambda i, j: (i, 0)),
                  pl.BlockSpec((k, tn), lambda i, j: (0, j)),
                  pl.BlockSpec((k, tn), lambda i, j: (0, j))] + c_in,
        out_specs=[pl.BlockSpec((tm, tn), lambda i, j: (i, j))] + c_out,
        compiler_params=pltpu.CompilerParams(dimension_semantics=("arbitrary", "arbitrary"),
                                             vmem_limit_bytes=VMEM_LIMIT),
    )(h, w_gate, w_up, *[c[0] for c in casts])


def _ffn_down_kernel(a_ref, b_ref, r_ref, o_ref):
    o_ref[...] = r_ref[...] + _dot(a_ref[...], b_ref[...])


def _ffn_down(a, w_down, x, tm=1024, tn=512, tk=5504):
    m, k = a.shape
    n = w_down.shape[1]
    res = x
    for kk in range(k // tk):
        res = pl.pallas_call(
            _ffn_down_kernel,
            out_shape=jax.ShapeDtypeStruct((m, n), F32),
            grid=(m // tm, n // tn),
            in_specs=[pl.BlockSpec((tm, tk), lambda i, j, kk=kk: (i, kk)),
                      pl.BlockSpec((tk, tn), lambda i, j, kk=kk: (kk, j)),
                      pl.BlockSpec((tm, tn), lambda i, j: (i, j))],
            out_specs=pl.BlockSpec((tm, tn), lambda i, j: (i, j)),
            compiler_params=pltpu.CompilerParams(dimension_semantics=("arbitrary", "arbitrary"),
                                                 vmem_limit_bytes=VMEM_LIMIT),
        )(a, w_down, res)
    return res


def _bias_value(rel, relb_ref, h):
    val = jnp.full(rel.shape, relb_ref[0, h], F32)
    for k in range(1, NUM_BUCKETS):
        val = jnp.where(rel >= _THR[k], relb_ref[k, h], val)
    return val


def _bias_tables_kernel(relb_ref, biasc_ref, dm_ref):
    h = pl.program_id(0)
    sub = 8
    j = lax.broadcasted_iota(jnp.int32, (sub, 2 * SEQ), 1)
    row_c = _bias_value(j - SEQ - (CMP_BLOCK - 1), relb_ref, h)[0:1]
    toe_c = pltpu.roll(jnp.broadcast_to(row_c, (LANES, 2 * SEQ)), 0, 1,
                       stride=CMP_STRIDE, stride_axis=0)
    for qt in range(NQ):
        biasc_ref[0, qt] = toe_c[:, SEQ + qt * TQ:SEQ + (qt + 1) * TQ]
    far = relb_ref[NUM_BUCKETS - 1, h]
    j2 = lax.broadcasted_iota(jnp.int32, (sub, 2 * TK), 1)
    b = lax.broadcasted_iota(jnp.int32, (TK, TQ), 0)
    a = lax.broadcasted_iota(jnp.int32, (TK, TQ), 1)
    for kind in range(2):
        row = ((_bias_value(j2 + (kind - 1) * TK, relb_ref, h) - far) * LOG2E)[0:1]
        toe = pltpu.roll(jnp.broadcast_to(row, (TK, 2 * TK)), 0, 1, stride=1, stride_axis=0)
        tile = toe[:, TK:]
        dm_ref[0, kind] = tile + jnp.where(a >= b, 0.0, NEG_INF) if kind == 0 else tile
    dm_ref[0, 2] = jnp.where(b > a, 0.0, NEG_INF)


def _bias_tables(rel_bias):
    return pl.pallas_call(
        _bias_tables_kernel,
        out_shape=(jax.ShapeDtypeStruct((N_KV_GROUPS, NQ, LANES, GROUP_SIZE * TQ), F32),
                   jax.ShapeDtypeStruct((N_KV_GROUPS, 3, TK, GROUP_SIZE * TQ), F32)),
        grid=(N_NSA_HEADS,),
        in_specs=[pl.BlockSpec(memory_space=pltpu.SMEM)],
        out_specs=(pl.BlockSpec((1, NQ, LANES, TQ), lambda h: (h // GROUP_SIZE, 0, 0, h % GROUP_SIZE)),
                   pl.BlockSpec((1, 3, TK, TQ), lambda h: (h // GROUP_SIZE, 0, 0, h % GROUP_SIZE))),
        compiler_params=pltpu.CompilerParams(dimension_semantics=("arbitrary",),
                                             vmem_limit_bytes=VMEM_LIMIT),
    )(rel_bias)


def _row_rms(x, w):
    return x * lax.rsqrt(jnp.mean(x * x, axis=-1, keepdims=True) + EPS) * w


N_NSA_INPUTS = 17
N_NSA_SCRATCH = 7


def _nsa_kernel(*refs, n_casts):
    (q_ref, k0_ref, v0_ref, k1_ref, v1_ref, k2_ref, v2_ref, gate_ref,
     qnw_ref, knw_ref, pek_ref, pev_ref, wck_ref, wcv_ref, biasc_ref, dm_ref, c2st_ref) = refs[:N_NSA_INPUTS]
    side_a_ref, side_bt_ref = refs[N_NSA_INPUTS:N_NSA_INPUTS + 2]
    n_in = N_NSA_INPUTS + 2 + n_casts
    cast_in = refs[N_NSA_INPUTS + 2:n_in]
    o_ref, side_o_ref = refs[n_in:n_in + 2]
    cast_out = refs[n_in + 2:n_in + 2 + n_casts]
    kc_s, vct_s, kaug_s, kw_s, vst_s, vwt_s, gt_s = refs[len(refs) - N_NSA_SCRATCH:]
    qt = pl.program_id(2)
    cols_all = GROUP_SIZE * TQ
    side_half = side_bt_ref.shape[0] // 2

    @pl.when(qt == 0)
    def _prologue():
        kaug_s[:, :HEAD_DIM] = (_row_rms(k1_ref[...], knw_ref[1:2, :]) * EXP2_SCALE).astype(BF16)
        jblk = lax.broadcasted_iota(jnp.int32, (SEQ, LANES), 0) // SEL_BLOCK
        lane = lax.broadcasted_iota(jnp.int32, (SEQ, LANES), 1)
        kaug_s[:, HEAD_DIM:] = jnp.where(jblk == lane, 1.0, 0.0).astype(BF16)
        kw_s[...] = (_row_rms(k2_ref[...], knw_ref[2:3, :]) * EXP2_SCALE).astype(BF16)
        ones_row = jnp.where(lax.broadcasted_iota(jnp.int32, (V_ROWS - HEAD_DIM, TK), 0) == 0,
                             1.0, 0.0).astype(BF16)
        for t in range(SEQ // TK):
            vst_s[t, :HEAD_DIM, :] = v1_ref[t * TK:(t + 1) * TK, :].T.astype(BF16)
            vst_s[t, HEAD_DIM:, :] = ones_row
            vwt_s[t, :HEAD_DIM, :] = v2_ref[t * TK:(t + 1) * TK, :].T.astype(BF16)
            vwt_s[t, HEAD_DIM:, :] = ones_row

        def compress(x_ref, pe_ref, w_ref):
            acc_lo = jnp.zeros((LANES, HEAD_DIM), F32)
            acc_hi = jnp.zeros((LANES, HEAD_DIM), F32)
            for j in range(CMP_STRIDE):
                y = x_ref[pl.ds(j, LANES, stride=CMP_STRIDE), :]
                acc_lo += _dot((y + pe_ref[j:j + 1, :]).astype(BF16),
                               w_ref[j * HEAD_DIM:(j + 1) * HEAD_DIM, :])
                jh = CMP_STRIDE + j
                acc_hi += _dot((y + pe_ref[jh:jh + 1, :]).astype(BF16),
                               w_ref[jh * HEAD_DIM:(jh + 1) * HEAD_DIM, :])
            return acc_lo + pltpu.roll(acc_hi, LANES - 1, 0)

        kc_s[...] = _row_rms(compress(k0_ref, pek_ref, wck_ref), knw_ref[0:1, :]).astype(BF16)
        vct_s[...] = compress(v0_ref, pev_ref, wcv_ref).T.astype(BF16)

    side_o_ref[:, :side_half] = _dot_nt(side_a_ref[...], side_bt_ref[:side_half, :])
    _cast_slabs(cast_in + cast_out, n_casts)

    t0 = qt * TQ
    q = q_ref[...]
    qts = []
    for r in range(GROUP_SIZE):
        qts.append(_row_rms(q[:, r * HEAD_DIM:(r + 1) * HEAD_DIM], qnw_ref[...]).T)
    qt_stack = jnp.concatenate(qts, axis=1).astype(BF16)

    sc = _dot(kc_s[...], qt_stack) * SCALE + biasc_ref[0, 0]
    tpos = t0 + (lax.broadcasted_iota(jnp.int32, (LANES, cols_all), 1) & (TQ - 1))
    nrow = lax.broadcasted_iota(jnp.int32, (LANES, cols_all), 0)
    maskc = tpos >= CMP_STRIDE * nrow + (CMP_BLOCK - 1)
    scm = jnp.where(maskc, sc, NEG_INF)
    mc = jnp.max(scm, axis=0, keepdims=True)
    ec = jnp.where(maskc, jnp.exp(scm - mc), 0.0)
    lc = jnp.sum(ec, axis=0, keepdims=True)
    pc = ec * (1.0 / jnp.where(lc > 0.0, lc, 1.0))
    o_c = _dot(vct_s[...], pc.astype(BF16))

    psum = pc[:, 0:TQ] + pc[:, TQ:2 * TQ] + pc[:, 2 * TQ:3 * TQ] + pc[:, 3 * TQ:4 * TQ]
    p_hi = psum.astype(BF16)
    p_lo = (psum - p_hi.astype(F32)).astype(BF16)
    imp = (_dot(c2st_ref[...], p_hi) + _dot(c2st_ref[...], p_lo))[0:N_SEL]
    tq_pos = t0 + lax.broadcasted_iota(jnp.int32, (N_SEL, TQ), 1)
    sidx = lax.broadcasted_iota(jnp.int32, (N_SEL, TQ), 0)
    cur = tq_pos >> 6
    score = jnp.where(sidx <= cur, imp, NEG_INF)
    for forced_blk in (cur - 1, cur, jnp.zeros_like(cur)):
        score = jnp.where(sidx == forced_blk, FORCE_SCORE, score)
    rank = jnp.zeros((N_SEL, TQ), F32)
    for sp in range(N_SEL):
        row = score[sp:sp + 1, :]
        tie = jnp.where(sidx > sp, 1.0, 0.0)
        rank = rank + jnp.where(row > score, 1.0, jnp.where(row == score, tie, 0.0))
    selb = jnp.where(rank < TOP_N, 0.0, NEG_INF)
    selb = jnp.concatenate([selb, jnp.zeros((HEAD_DIM - N_SEL, TQ), F32)], axis=0).astype(BF16)
    qt_aug = jnp.concatenate([qt_stack, jnp.concatenate([selb] * GROUP_SIZE, axis=1)], axis=0)

    m0 = jnp.full((1, cols_all), NEG_INF, F32)
    a0 = jnp.zeros((V_ROWS, cols_all), F32)

    def online(carry, tiles):
        m, acc = carry
        m_new = m
        for u, _ in tiles:
            m_new = jnp.maximum(m_new, jnp.max(u, axis=0, keepdims=True))
        acc = jnp.exp2(m - m_new) * acc
        for u, vt in tiles:
            acc = acc + _dot(vt, jnp.exp2(u - m_new).astype(BF16))
        return m_new, acc

    def normalised(acc):
        return acc[:HEAD_DIM] * (1.0 / acc[HEAD_DIM:HEAD_DIM + 1])

    def key_rows(kt):
        return pl.ds(pl.multiple_of(kt * TK, TK), TK)

    def sel_tile(kt):
        return _dot(kaug_s[key_rows(kt), :], qt_aug), vst_s[kt]

    n_far = jnp.maximum(qt - 1, 0)
    carry = lax.fori_loop(0, n_far >> 1,
                          lambda i, c: online(c, [sel_tile(2 * i), sel_tile(2 * i + 1)]), (m0, a0))
    carry = lax.fori_loop(0, n_far & 1, lambda i, c: online(c, [sel_tile(n_far - 1)]), carry)
    kt1 = jnp.maximum(qt - 1, 0)
    u1, vt1 = sel_tile(kt1)
    u0, vt0 = sel_tile(qt)
    _, acc_s = online(carry, [(u1 + dm_ref[0, 1] + jnp.where(qt >= 1, 0.0, NEG_INF), vt1),
                              (u0 + dm_ref[0, 0], vt0)])
    o_s = normalised(acc_s)

    side_o_ref[:, side_half:] = _dot_nt(side_a_ref[...], side_bt_ref[side_half:, :])

    tiles = []
    for dist, kind in ((2, 2), (1, 1), (0, 0)):
        kt = jnp.maximum(qt - dist, 0)
        u = _dot(kw_s[key_rows(kt), :], qt_stack) + dm_ref[0, kind]
        if dist > 0:
            u = u + jnp.where(qt >= dist, 0.0, NEG_INF)
        tiles.append((u, vwt_s[kt]))
    _, acc_w = online((m0, a0), tiles)
    o_w = normalised(acc_w)

    gt_s[...] = jax.nn.sigmoid(gate_ref[...]).T
    head0 = pl.program_id(1) * GROUP_SIZE
    for r in range(GROUP_SIZE):
        cs = slice(r * TQ, (r + 1) * TQ)
        c = (head0 + r) * N_BRANCH
        o = (gt_s[pl.ds(c, 1), :] * o_c[:, cs] + gt_s[pl.ds(c + 1, 1), :] * o_s[:, cs]
             + gt_s[pl.ds(c + 2, 1), :] * o_w[:, cs])
        o_ref[:, r * HEAD_DIM:(r + 1) * HEAD_DIM] = o.T.astype(o_ref.dtype)


def _cmp_to_sel_t():
    ss = np.arange(LANES)[:, None] * SEL_BLOCK
    cs = np.arange(LANES)[None, :] * CMP_STRIDE
    ov = np.clip(np.minimum(cs + CMP_BLOCK, ss + SEL_BLOCK) - np.maximum(cs, ss), 0, None)
    m = ov.astype(np.float32) / np.float32(CMP_BLOCK)
    m[N_SEL:, :] = 0.0
    m[:, N_CMP:] = 0.0
    return jnp.asarray(m, BF16)


def _nsa_attention(proj, gate_logits, batch, q_norm_w, k_norm_w, pe_k, pe_v, w_cmp_k, w_cmp_v, biasc, dm,
                   side_a, side_bt, side_tm=1024, side_tn=512, casts=()):
    kvb = COL_KV // LANES

    def kv_spec(branch, which):
        base = kvb + (branch * 2 + which) * N_KV_GROUPS
        return pl.BlockSpec((SEQ, LANES), lambda b, g, qt: (b, base + g))

    full = lambda shape: pl.BlockSpec(shape, lambda b, g, qt: (0,) * len(shape))
    grid = (batch, N_KV_GROUPS, NQ)
    c_in, c_out, c_shapes = _cast_specs(casts, grid)
    side_m, side_k = side_a.shape
    side_n = side_bt.shape[0]
    n_j = side_n // side_tn
    assert (side_m // side_tm) * n_j == math.prod(grid)
    step = lambda b, g, qt: (b * N_KV_GROUPS + g) * NQ + qt
    side_in = [pl.BlockSpec((side_tm, side_k), lambda b, g, qt: (step(b, g, qt) // n_j, 0),
                            pipeline_mode=pl.Buffered(1)),
               pl.BlockSpec((side_tn, side_k), lambda b, g, qt: (step(b, g, qt) % n_j, 0))]
    side_out = pl.BlockSpec((side_tm, side_tn), lambda b, g, qt: (step(b, g, qt) // n_j, step(b, g, qt) % n_j))
    return pl.pallas_call(
        functools.partial(_nsa_kernel, n_casts=len(casts)),
        out_shape=[jax.ShapeDtypeStruct((batch * SEQ, NSA_WIDTH), BF16),
                   jax.ShapeDtypeStruct((side_m, side_n), F32)] + c_shapes,
        grid=grid,
        in_specs=[pl.BlockSpec((TQ, GROUP_SIZE * HEAD_DIM), lambda b, g, qt: (b * NQ + qt, g)),
                  kv_spec(0, 0), kv_spec(0, 1), kv_spec(1, 0), kv_spec(1, 1),
                  kv_spec(2, 0), kv_spec(2, 1),
                  pl.BlockSpec((TQ, LANES), lambda b, g, qt: (b * NQ + qt, 0)),
                  full((1, HEAD_DIM)), full((N_BRANCH, HEAD_DIM)),
                  full((CMP_BLOCK, HEAD_DIM)), full((CMP_BLOCK, HEAD_DIM)),
                  full((CMP_BLOCK * HEAD_DIM, HEAD_DIM)), full((CMP_BLOCK * HEAD_DIM, HEAD_DIM)),
                  pl.BlockSpec((1, 1, LANES, GROUP_SIZE * TQ), lambda b, g, qt: (g, qt, 0, 0)),
                  pl.BlockSpec((1, 3, TK, GROUP_SIZE * TQ), lambda b, g, qt: (g, 0, 0, 0),
                               pipeline_mode=pl.Buffered(1)),
                  full((LANES, LANES))] + side_in + c_in,
        out_specs=[pl.BlockSpec((TQ, GROUP_SIZE * HEAD_DIM), lambda b, g, qt: (b * NQ + qt, g)),
                   side_out] + c_out,
        scratch_shapes=[pltpu.VMEM((LANES, HEAD_DIM), BF16),
                        pltpu.VMEM((HEAD_DIM, LANES), BF16),
                        pltpu.VMEM((SEQ, 2 * HEAD_DIM), BF16),
                        pltpu.VMEM((SEQ, HEAD_DIM), BF16),
                        pltpu.VMEM((SEQ // TK, V_ROWS, TK), BF16),
                        pltpu.VMEM((SEQ // TK, V_ROWS, TK), BF16),
                        pltpu.VMEM((LANES, TQ), F32)],
        compiler_params=pltpu.CompilerParams(
            dimension_semantics=("arbitrary", "arbitrary", "arbitrary"),
            vmem_limit_bytes=NSA_VMEM_LIMIT),
    )(proj, proj, proj, proj, proj, proj, proj, gate_logits,
      q_norm_w.reshape(1, HEAD_DIM), k_norm_w, pe_k, pe_v,
      w_cmp_k.astype(BF16), w_cmp_v.astype(BF16), biasc, dm, _cmp_to_sel_t(),
      side_a, side_bt, *[c[0] for c in casts])


def _ret_kernel(lg_ref, q_ref, k_ref, v_ref, g_ref, cos_ref, sin_ref, gnw_ref, *rest, n_casts):
    o_ref = rest[n_casts]
    state_ref = rest[-1]
    _cast_slabs(rest[:n_casts] + rest[n_casts + 1:-1], n_casts)
    c_len = RET_CHUNK
    ii = lax.broadcasted_iota(jnp.int32, (c_len, c_len), 0)
    jj = lax.broadcasted_iota(jnp.int32, (c_len, c_len), 1)
    diff = (ii - jj).astype(F32)
    icol = lax.broadcasted_iota(jnp.int32, (c_len, 1), 0).astype(F32)
    half = RET_KEY_DIM // 2
    state_ref[...] = jnp.zeros_like(state_ref)
    tables = []
    for hh in range(RET_HEADS_PER_STEP):
        lg = lg_ref[pl.program_id(1) * RET_HEADS_PER_STEP + hh]
        tables.append((
            jnp.where(diff >= 0.0, jnp.exp(jnp.maximum(diff, 0.0) * lg), 0.0),
            jnp.exp((icol + 1.0) * lg),
            jnp.exp((c_len - 1.0 - icol) * lg),
            jnp.exp(jnp.full((1, RET_VAL_DIM), float(c_len), F32) * lg)))

    def body(c, _):
        rows = pl.ds(pl.multiple_of(c * c_len, c_len), c_len)
        cos = cos_ref[rows, :]
        sin = sin_ref[rows, :]

        def rot(x):
            x1, x2 = x[:, :half], x[:, half:]
            return jnp.concatenate([x1 * cos - x2 * sin, x2 * cos + x1 * sin], axis=-1)

        for hh, (decay_in, xi, zeta, chunk_decay) in enumerate(tables):
            kcols = slice(hh * RET_KEY_DIM, (hh + 1) * RET_KEY_DIM)
            vcols = slice(hh * RET_VAL_DIM, (hh + 1) * RET_VAL_DIM)
            qr = rot(q_ref[rows, kcols])
            kr = rot(k_ref[rows, kcols]) * (RET_KEY_DIM ** -0.5)
            vb = v_ref[rows, vcols].astype(BF16)
            scores = _dot_nt(qr.astype(BF16), kr.astype(BF16)) * decay_in
            inner = _dot(scores.astype(BF16), vb)
            st = state_ref[hh]
            cross = _dot((qr * xi).astype(BF16), st.astype(BF16))
            kv = _dot((kr * zeta).T.astype(BF16), vb)
            state_ref[hh] = chunk_decay * st + kv
            y = inner + cross
            y = y * lax.rsqrt(jnp.mean(y * y, axis=-1, keepdims=True) + EPS) * gnw_ref[:, vcols]
            g = g_ref[rows, vcols]
            o_ref[rows, vcols] = (g * jax.nn.sigmoid(g) * y).astype(o_ref.dtype)
        return 0

    lax.fori_loop(0, SEQ // c_len, body, 0)


def _retention(proj, batch, gn_w, casts=()):
    pos = np.arange(SEQ, dtype=np.float64)
    half = RET_KEY_DIM // 2
    inv = np.exp(-np.linspace(0.0, 1.0, half) * math.log(ROPE_BASE)).astype(np.float32)
    ang = pos.astype(np.float32)[:, None] * inv[None, :]
    cos = jnp.asarray(np.cos(ang.astype(np.float64)), F32)
    sin = jnp.asarray(np.sin(ang.astype(np.float64)), F32)
    log_gamma = jnp.asarray(np.log(1.0 - 2.0 ** (-5.0 - np.arange(N_RET_HEADS))), F32)
    w = RET_KEY_DIM * RET_HEADS_PER_STEP

    def col_spec(col0):
        return pl.BlockSpec((SEQ, w), lambda b, h: (b, col0 // w + h))

    grid = (batch, N_RET_HEADS // RET_HEADS_PER_STEP)
    c_in, c_out, c_shapes = _cast_specs(casts, grid)
    return pl.pallas_call(
        functools.partial(_ret_kernel, n_casts=len(casts)),
        out_shape=[jax.ShapeDtypeStruct((batch * SEQ, RET_WIDTH), BF16)] + c_shapes,
        grid=grid,
        in_specs=[pl.BlockSpec(memory_space=pltpu.SMEM),
                  col_spec(COL_QR), col_spec(COL_KR), col_spec(COL_VR), col_spec(COL_GR),
                  pl.BlockSpec((SEQ, half), lambda b, h: (0, 0)),
                  pl.BlockSpec((SEQ, half), lambda b, h: (0, 0)),
                  pl.BlockSpec((1, w), lambda b, h: (0, h))] + c_in,
        out_specs=[pl.BlockSpec((SEQ, w), lambda b, h: (b, h))] + c_out,
        scratch_shapes=[pltpu.VMEM((RET_HEADS_PER_STEP, RET_KEY_DIM, RET_VAL_DIM), F32)],
        compiler_params=pltpu.CompilerParams(dimension_semantics=("arbitrary", "arbitrary"),
                                             vmem_limit_bytes=VMEM_LIMIT),
    )(log_gamma, proj, proj, proj, proj, cos, sin, gn_w.reshape(1, RET_WIDTH), *[c[0] for c in casts])


def _layer(x, norm1_w, w_in, nsa_q_norm_w, nsa_k_norm_w, cmp_pe_k, cmp_pe_v, w_cmp_k, w_cmp_v,
           rel_bias, ret_gn_w, w_out, norm2_w, w_gate, w_up, w_down):
    batch = x.shape[0]
    xf = x.reshape(batch * SEQ, D_MODEL)
    h = _rmsnorm(xf, norm1_w)
    w_in_t = w_in.T
    gate0 = W_IN_NSA_COLS
    gate1 = gate0 + W_IN_GATE_COLS
    ret_cols = 4 * RET_WIDTH
    proj_nsa, w_ret_t = _in_projection(h, w_in_t, 0, W_IN_NSA_COLS, 1024, tm=512,
                                       casts=((w_in_t, gate1, ret_cols, 64),))
    gate_logits, = _in_projection(h, w_in_t, gate0, LANES, LANES)
    biasc, dm = _bias_tables(rel_bias)
    n_steps = batch * N_KV_GROUPS * NQ
    o_nsa, proj_ret, w_gate_b, w_up_b, w_out_b = _nsa_attention(
        proj_nsa, gate_logits, batch, nsa_q_norm_w, nsa_k_norm_w, cmp_pe_k, cmp_pe_v,
        w_cmp_k, w_cmp_v, biasc, dm, h, w_ret_t,
        casts=((w_gate, 0, D_MODEL, n_steps), (w_up, 0, D_MODEL, n_steps), (w_out, 0, D_MODEL, n_steps)))
    o_ret, = _retention(proj_ret, batch, ret_gn_w)
    x1, = _out_projection(o_nsa, o_ret, w_out_b, xf, tn=1024)
    hf = _rmsnorm(x1, norm2_w)
    mid, w_down_b = _ffn_up(hf, w_gate_b, w_up_b, tm=2048, casts=((w_down, 0, D_FF, 172),))
    out = _ffn_down(mid, w_down_b, x1)
    return out.reshape(batch, SEQ, D_MODEL)


def kernel(x, norm1_w, w_in, nsa_q_norm_w, nsa_k_norm_w, cmp_pe_k, cmp_pe_v, w_cmp_k, w_cmp_v,
           rel_bias, ret_gn_w, w_out, norm2_w, w_gate, w_up, w_down):
    for l in range(norm1_w.shape[0]):
        x = _layer(x, norm1_w[l], w_in[l], nsa_q_norm_w[l], nsa_k_norm_w[l], cmp_pe_k[l], cmp_pe_v[l],
                   w_cmp_k[l], w_cmp_v[l], rel_bias, ret_gn_w[l], w_out[l], norm2_w[l],
                   w_gate[l], w_up[l], w_down[l])
    return x
```

```python
import functools
import math

import numpy as np
import jax
import jax.numpy as jnp
from jax import lax
from jax.experimental import pallas as pl
from jax.experimental.pallas import tpu as pltpu

F32 = jnp.float32
BF16 = jnp.bfloat16

D_MODEL = 4096
SEQ = 2048
HEAD_DIM = 128
N_NSA_HEADS = 16
N_KV_GROUPS = 4
GROUP_SIZE = 4
N_BRANCH = 3
CMP_BLOCK = 32
CMP_STRIDE = 16
SEL_BLOCK = 64
TOP_N = 16
WINDOW = 512
RET_KEY_DIM = 256
RET_VAL_DIM = 256
N_RET_HEADS = 8
RET_CHUNK = 128
ROPE_BASE = 10000.0
NSA_WIDTH = 2048
RET_WIDTH = 2048
D_FF = 11008
NUM_BUCKETS = 32
MAX_DISTANCE = 128
EPS = 1e-6
NEG_INF = -1e30
FORCE_SCORE = 1e9
N_SEL = SEQ // SEL_BLOCK
N_CMP = (SEQ - CMP_BLOCK) // CMP_STRIDE + 1
SCALE = HEAD_DIM ** -0.5
LOG2E = math.log2(math.e)
EXP2_SCALE = SCALE * LOG2E

LANES = 128
VMEM_LIMIT = 56 * 1024 * 1024
NSA_VMEM_LIMIT = 60 * 1024 * 1024

TQ = 256
TK = 256
NQ = SEQ // TQ
RET_HEADS_PER_STEP = 2
V_ROWS = HEAD_DIM + 16
assert TQ == TK and WINDOW == 2 * TK and TK % SEL_BLOCK == 0 and N_SEL <= HEAD_DIM and TK >= MAX_DISTANCE

W_IN_NSA_COLS = NSA_WIDTH + N_BRANCH * 2 * N_KV_GROUPS * HEAD_DIM
W_IN_GATE_COLS = N_NSA_HEADS * N_BRANCH
COL_KV = NSA_WIDTH
COL_QR = 0
COL_KR = 2048
COL_VR = 4096
COL_GR = 6144


def _bucket_thresholds():
    rel = np.arange(0, 4 * MAX_DISTANCE)
    max_exact = NUM_BUCKETS // 2
    nf = np.maximum(rel, 1).astype(np.float64)
    large = max_exact + np.floor(np.log(nf / max_exact) / math.log(MAX_DISTANCE / max_exact)
                                 * (NUM_BUCKETS - max_exact)).astype(np.int64)
    large = np.minimum(large, NUM_BUCKETS - 1)
    b = np.where(rel < max_exact, rel, large)
    return [int(np.argmax(b >= k)) for k in range(NUM_BUCKETS)]


_THR = _bucket_thresholds()


def _dot(a, b):
    return jnp.dot(a, b, preferred_element_type=F32)


def _dot_nt(a, b):
    return lax.dot_general(a, b, (((1,), (1,)), ((), ())), preferred_element_type=F32)


def _rms_kernel(x_ref, w_ref, o_ref):
    x = x_ref[...]
    y = x * lax.rsqrt(jnp.mean(x * x, axis=-1, keepdims=True) + EPS)
    o_ref[...] = (y * w_ref[...]).astype(o_ref.dtype)


def _rmsnorm(x, w, tm=512):
    m, d = x.shape
    return pl.pallas_call(
        _rms_kernel,
        out_shape=jax.ShapeDtypeStruct((m, d), BF16),
        grid=(m // tm,),
        in_specs=[pl.BlockSpec((tm, d), lambda i: (i, 0)),
                  pl.BlockSpec((1, d), lambda i: (0, 0))],
        out_specs=pl.BlockSpec((tm, d), lambda i: (i, 0)),
        compiler_params=pltpu.CompilerParams(dimension_semantics=("arbitrary",),
                                             vmem_limit_bytes=VMEM_LIMIT),
    )(x, w.reshape(1, d))


def _cast_spec(grid, row0, n_rows, n_cols, n_slabs):
    slab = n_rows // n_slabs
    sub = 8
    n_steps = math.prod(grid)
    assert slab * n_slabs == n_rows and slab % 16 == 0 and n_cols % LANES == 0
    assert row0 % sub == 0 and n_slabs <= n_steps

    def slab_index(ids):
        step = ids[0]
        for dim, idx in zip(grid[1:], ids[1:]):
            step = step * dim + idx
        return step if n_slabs == n_steps else jnp.minimum(step, n_slabs - 1)

    in_spec = pl.BlockSpec((pl.Element(slab), pl.Element(n_cols)),
                           lambda *ids: ((row0 // sub + slab_index(ids) * (slab // sub)) * sub, 0))
    out_spec = pl.BlockSpec((slab, n_cols), lambda *ids: (slab_index(ids), 0))
    return in_spec, out_spec, jax.ShapeDtypeStruct((n_rows, n_cols), BF16)


def _cast_specs(casts, grid):
    specs = [_cast_spec(grid, row0, n_rows, arr.shape[1], n_slabs) for arr, row0, n_rows, n_slabs in casts]
    return [s[0] for s in specs], [s[1] for s in specs], [s[2] for s in specs]


def _cast_slabs(refs, n_casts):
    for src, dst in zip(refs[:n_casts], refs[len(refs) - n_casts:]):
        dst[...] = src[...].astype(dst.dtype)


def _inproj_kernel(a_ref, wt_ref, *rest, n_casts):
    o_ref = rest[n_casts]
    wbf_ref = rest[-1]

    @pl.when(pl.program_id(1) == 0)
    def _():
        wbf_ref[...] = wt_ref[...].astype(BF16)

    o_ref[...] = _dot_nt(a_ref[...], wbf_ref[...])
    _cast_slabs(rest[:n_casts] + rest[n_casts + 1:-1], n_casts)


def _in_projection(a, w_t, row0, n_cols, tn, tm=1024, casts=()):
    m, k = a.shape
    grid = (n_cols // tn, m // tm)
    c_in, c_out, c_shapes = _cast_specs(casts, grid)
    sub = 8
    assert row0 % sub == 0 and tn % sub == 0
    return pl.pallas_call(
        functools.partial(_inproj_kernel, n_casts=len(casts)),
        out_shape=[jax.ShapeDtypeStruct((m, n_cols), F32)] + c_shapes,
        grid=grid,
        in_specs=[pl.BlockSpec((tm, k), lambda j, i: (i, 0)),
                  pl.BlockSpec((pl.Element(tn), pl.Element(k)),
                               lambda j, i: ((row0 // sub + j * (tn // sub)) * sub, 0),
                               pipeline_mode=pl.Buffered(1))] + c_in,
        out_specs=[pl.BlockSpec((tm, tn), lambda j, i: (i, j))] + c_out,
        scratch_shapes=[pltpu.VMEM((tn, k), BF16)],
        compiler_params=pltpu.CompilerParams(dimension_semantics=("arbitrary", "arbitrary"),
                                             vmem_limit_bytes=VMEM_LIMIT),
    )(a, w_t, *[c[0] for c in casts])


def _outproj_kernel(a1_ref, a2_ref, w1_ref, w2_ref, x_ref, *rest, n_casts):
    o_ref = rest[n_casts]
    acc = _dot(a1_ref[...], w1_ref[...]) + _dot(a2_ref[...], w2_ref[...])
    o_ref[...] = x_ref[...] + acc
    _cast_slabs(rest[:n_casts] + rest[n_casts + 1:], n_casts)


def _out_projection(o_nsa, o_ret, w_out, x, tm=1024, tn=512, casts=()):
    m = x.shape[0]
    kh = NSA_WIDTH
    grid = (m // tm, D_MODEL // tn)
    c_in, c_out, c_shapes = _cast_specs(casts, grid)
    return pl.pallas_call(
        functools.partial(_outproj_kernel, n_casts=len(casts)),
        out_shape=[jax.ShapeDtypeStruct((m, D_MODEL), F32)] + c_shapes,
        grid=grid,
        in_specs=[pl.BlockSpec((tm, kh), lambda i, j: (i, 0)),
                  pl.BlockSpec((tm, kh), lambda i, j: (i, 0)),
                  pl.BlockSpec((kh, tn), lambda i, j: (0, j)),
                  pl.BlockSpec((kh, tn), lambda i, j: (1, j)),
                  pl.BlockSpec((tm, tn), lambda i, j: (i, j))] + c_in,
        out_specs=[pl.BlockSpec((tm, tn), lambda i, j: (i, j))] + c_out,
        compiler_params=pltpu.CompilerParams(dimension_semantics=("arbitrary", "arbitrary"),
                                             vmem_limit_bytes=VMEM_LIMIT),
    )(o_nsa, o_ret, w_out, w_out, x, *[c[0] for c in casts])


def _ffn_up_kernel(h_ref, wg_ref, wu_ref, *rest, n_casts):
    o_ref = rest[n_casts]
    h = h_ref[...]
    g = _dot(h, wg_ref[...])
    u = _dot(h, wu_ref[...])
    o_ref[...] = (g * jax.nn.sigmoid(g) * u).astype(o_ref.dtype)
    _cast_slabs(rest[:n_casts] + rest[n_casts + 1:], n_casts)


def _ffn_up(h, w_gate, w_up, tm=1024, tn=256, casts=()):
    m, k = h.shape
    n = w_gate.shape[1]
    grid = (m // tm, n // tn)
    c_in, c_out, c_shapes = _cast_specs(casts, grid)
    return pl.pallas_call(
        functools.partial(_ffn_up_kernel, n_casts=len(casts)),
        out_shape=[jax.ShapeDtypeStruct((m, n), BF16)] + c_shapes,
        grid=grid,
        in_specs=[pl.BlockSpec((tm, k), lambda i, j: (i, 0)),
                  pl.BlockSpec((k, tn), lambda i, j: (0, j)),
                  pl.BlockSpec((k, tn), lambda i, j: (0, j))] + c_in,
        out_specs=[pl.BlockSpec((tm, tn), lambda i, j: (i, j))] + c_out,
        compiler_params=pltpu.CompilerParams(dimension_semantics=("arbitrary", "arbitrary"),
                                             vmem_limit_bytes=VMEM_LIMIT),
    )(h, w_gate, w_up, *[c[0] for c in casts])


def _ffn_down_kernel(a_ref, b_ref, r_ref, o_ref):
    o_ref[...] = r_ref[...] + _dot(a_ref[...], b_ref[...])


def _ffn_down(a, w_down, x, tm=1024, tn=512, tk=5504):
    m, k = a.shape
    n = w_down.shape[1]
    res = x
    for kk in range(k // tk):
        res = pl.pallas_call(
            _ffn_down_kernel,
            out_shape=jax.ShapeDtypeStruct((m, n), F32),
            grid=(m // tm, n // tn),
            in_specs=[pl.BlockSpec((tm, tk), lambda i, j, kk=kk: (i, kk)),
                      pl.BlockSpec((tk, tn), lambda i, j, kk=kk: (kk, j)),
                      pl.BlockSpec((tm, tn), lambda i, j: (i, j))],
            out_specs=pl.BlockSpec((tm, tn), lambda i, j: (i, j)),
            compiler_params=pltpu.CompilerParams(dimension_semantics=("arbitrary", "arbitrary"),
                                                 vmem_limit_bytes=VMEM_LIMIT),
        )(a, w_down, res)
    return res


def _bias_value(rel, relb_ref, h):
    val = jnp.full(rel.shape, relb_ref[0, h], F32)
    for k in range(1, NUM_BUCKETS):
        val = jnp.where(rel >= _THR[k], relb_ref[k, h], val)
    return val


def _bias_tables_kernel(relb_ref, biasc_ref, dm_ref):
    h = pl.program_id(0)
    sub = 8
    j = lax.broadcasted_iota(jnp.int32, (sub, 2 * SEQ), 1)
    row_c = _bias_value(j - SEQ - (CMP_BLOCK - 1), relb_ref, h)[0:1]
    toe_c = pltpu.roll(jnp.broadcast_to(row_c, (LANES, 2 * SEQ)), 0, 1,
                       stride=CMP_STRIDE, stride_axis=0)
    for qt in range(NQ):
        biasc_ref[0, qt] = toe_c[:, SEQ + qt * TQ:SEQ + (qt + 1) * TQ]
    far = relb_ref[NUM_BUCKETS - 1, h]
    j2 = lax.broadcasted_iota(jnp.int32, (sub, 2 * TK), 1)
    b = lax.broadcasted_iota(jnp.int32, (TK, TQ), 0)
    a = lax.broadcasted_iota(jnp.int32, (TK, TQ), 1)
    for kind in range(2):
        row = ((_bias_value(j2 + (kind - 1) * TK, relb_ref, h) - far) * LOG2E)[0:1]
        toe = pltpu.roll(jnp.broadcast_to(row, (TK, 2 * TK)), 0, 1, stride=1, stride_axis=0)
        tile = toe[:, TK:]
        dm_ref[0, kind] = tile + jnp.where(a >= b, 0.0, NEG_INF) if kind == 0 else tile
    dm_ref[0, 2] = jnp.where(b > a, 0.0, NEG_INF)


def _bias_tables(rel_bias):
    return pl.pallas_call(
        _bias_tables_kernel,
        out_shape=(jax.ShapeDtypeStruct((N_KV_GROUPS, NQ, LANES, GROUP_SIZE * TQ), F32),
                   jax.ShapeDtypeStruct((N_KV_GROUPS, 3, TK, GROUP_SIZE * TQ), F32)),
        grid=(N_NSA_HEADS,),
        in_specs=[pl.BlockSpec(memory_space=pltpu.SMEM)],
        out_specs=(pl.BlockSpec((1, NQ, LANES, TQ), lambda h: (h // GROUP_SIZE, 0, 0, h % GROUP_SIZE)),
                   pl.BlockSpec((1, 3, TK, TQ), lambda h: (h // GROUP_SIZE, 0, 0, h % GROUP_SIZE))),
        compiler_params=pltpu.CompilerParams(dimension_semantics=("arbitrary",),
                                             vmem_limit_bytes=VMEM_LIMIT),
    )(rel_bias)


def _row_rms(x, w):
    return x * lax.rsqrt(jnp.mean(x * x, axis=-1, keepdims=True) + EPS) * w


N_NSA_INPUTS = 17
N_NSA_SCRATCH = 7


def _nsa_kernel(*refs, n_casts):
    (q_ref, k0_ref, v0_ref, k1_ref, v1_ref, k2_ref, v2_ref, gate_ref,
     qnw_ref, knw_ref, pek_ref, pev_ref, wck_ref, wcv_ref, biasc_ref, dm_ref, c2st_ref) = refs[:N_NSA_INPUTS]
    side_a_ref, side_bt_ref = refs[N_NSA_INPUTS:N_NSA_INPUTS + 2]
    n_in = N_NSA_INPUTS + 2 + n_casts
    cast_in = refs[N_NSA_INPUTS + 2:n_in]
    o_ref, side_o_ref = refs[n_in:n_in + 2]
    cast_out = refs[n_in + 2:n_in + 2 + n_casts]
    kc_s, vct_s, kaug_s, kw_s, vst_s, vwt_s, gt_s = refs[len(refs) - N_NSA_SCRATCH:]
    qt = pl.program_id(2)
    cols_all = GROUP_SIZE * TQ
    side_half = side_bt_ref.shape[0] // 2

    @pl.when(qt == 0)
    def _prologue():
        kaug_s[:, :HEAD_DIM] = (_row_rms(k1_ref[...], knw_ref[1:2, :]) * EXP2_SCALE).astype(BF16)
        jblk = lax.broadcasted_iota(jnp.int32, (SEQ, LANES), 0) // SEL_BLOCK
        lane = lax.broadcasted_iota(jnp.int32, (SEQ, LANES), 1)
        kaug_s[:, HEAD_DIM:] = jnp.where(jblk == lane, 1.0, 0.0).astype(BF16)
        kw_s[...] = (_row_rms(k2_ref[...], knw_ref[2:3, :]) * EXP2_SCALE).astype(BF16)
        ones_row = jnp.where(lax.broadcasted_iota(jnp.int32, (V_ROWS - HEAD_DIM, TK), 0) == 0,
                             1.0, 0.0).astype(BF16)
        for t in range(SEQ // TK):
            vst_s[t, :HEAD_DIM, :] = v1_ref[t * TK:(t + 1) * TK, :].T.astype(BF16)
            vst_s[t, HEAD_DIM:, :] = ones_row
            vwt_s[t, :HEAD_DIM, :] = v2_ref[t * TK:(t + 1) * TK, :].T.astype(BF16)
            vwt_s[t, HEAD_DIM:, :] = ones_row

        def compress(x_ref, pe_ref, w_ref):
            acc_lo = jnp.zeros((LANES, HEAD_DIM), F32)
            acc_hi = jnp.zeros((LANES, HEAD_DIM), F32)
            for j in range(CMP_STRIDE):
                y = x_ref[pl.ds(j, LANES, stride=CMP_STRIDE), :]
                acc_lo += _dot((y + pe_ref[j:j + 1, :]).astype(BF16),
                               w_ref[j * HEAD_DIM:(j + 1) * HEAD_DIM, :])
                jh = CMP_STRIDE + j
                acc_hi += _dot((y + pe_ref[jh:jh + 1, :]).astype(BF16),
                               w_ref[jh * HEAD_DIM:(jh + 1) * HEAD_DIM, :])
            return acc_lo + pltpu.roll(acc_hi, LANES - 1, 0)

        kc_s[...] = _row_rms(compress(k0_ref, pek_ref, wck_ref), knw_ref[0:1, :]).astype(BF16)
        vct_s[...] = compress(v0_ref, pev_ref, wcv_ref).T.astype(BF16)

    side_o_ref[:, :side_half] = _dot_nt(side_a_ref[...], side_bt_ref[:side_half, :])
    _cast_slabs(cast_in + cast_out, n_casts)

    t0 = qt * TQ
    q = q_ref[...]
    qts = []
    for r in range(GROUP_SIZE):
        qts.append(_row_rms(q[:, r * HEAD_DIM:(r + 1) * HEAD_DIM], qnw_ref[...]).T)
    qt_stack = jnp.concatenate(qts, axis=1).astype(BF16)

    sc = _dot(kc_s[...], qt_stack) * SCALE + biasc_ref[0, 0]
    tpos = t0 + (lax.broadcasted_iota(jnp.int32, (LANES, cols_all), 1) & (TQ - 1))
    nrow = lax.broadcasted_iota(jnp.int32, (LANES, cols_all), 0)
    maskc = tpos >= CMP_STRIDE * nrow + (CMP_BLOCK - 1)
    scm = jnp.where(maskc, sc, NEG_INF)
    mc = jnp.max(scm, axis=0, keepdims=True)
    ec = jnp.where(maskc, jnp.exp(scm - mc), 0.0)
    lc = jnp.sum(ec, axis=0, keepdims=True)
    pc = ec * (1.0 / jnp.where(lc > 0.0, lc, 1.0))
    o_c = _dot(vct_s[...], pc.astype(BF16))

    psum = pc[:, 0:TQ] + pc[:, TQ:2 * TQ] + pc[:, 2 * TQ:3 * TQ] + pc[:, 3 * TQ:4 * TQ]
    p_hi = psum.astype(BF16)
    p_lo = (psum - p_hi.astype(F32)).astype(BF16)
    imp = (_dot(c2st_ref[...], p_hi) + _dot(c2st_ref[...], p_lo))[0:N_SEL]
    tq_pos = t0 + lax.broadcasted_iota(jnp.int32, (N_SEL, TQ), 1)
    sidx = lax.broadcasted_iota(jnp.int32, (N_SEL, TQ), 0)
    cur = tq_pos >> (SEL_BLOCK.bit_length() - 1)
    score = jnp.where(sidx <= cur, imp, NEG_INF)
    for forced_blk in (cur - 1, cur, jnp.zeros_like(cur)):
        score = jnp.where(sidx == forced_blk, FORCE_SCORE, score)
    rank = jnp.zeros((N_SEL, TQ), F32)
    for sp in range(N_SEL):
        row = score[sp:sp + 1, :]
        tie = jnp.where(sidx > sp, 1.0, 0.0)
        rank = rank + jnp.where(row > score, 1.0, jnp.where(row == score, tie, 0.0))
    selb = jnp.where(rank < TOP_N, 0.0, NEG_INF)
    selb = jnp.concatenate([selb, jnp.zeros((HEAD_DIM - N_SEL, TQ), F32)], axis=0).astype(BF16)
    qt_aug = jnp.concatenate([qt_stack, jnp.concatenate([selb] * GROUP_SIZE, axis=1)], axis=0)

    m0 = jnp.full((1, cols_all), NEG_INF, F32)
    a0 = jnp.zeros((V_ROWS, cols_all), F32)

    def online(carry, tiles):
        m, acc = carry
        m_new = m
        for u, _ in tiles:
            m_new = jnp.maximum(m_new, jnp.max(u, axis=0, keepdims=True))
        acc = jnp.exp2(m - m_new) * acc
        for u, vt in tiles:
            acc = acc + _dot(vt, jnp.exp2(u - m_new).astype(BF16))
        return m_new, acc

    def normalised(acc):
        return acc[:HEAD_DIM] * (1.0 / acc[HEAD_DIM:HEAD_DIM + 1])

    def key_rows(kt):
        return pl.ds(pl.multiple_of(kt * TK, TK), TK)

    def sel_tile(kt):
        return _dot(kaug_s[key_rows(kt), :], qt_aug), vst_s[kt]

    n_far = jnp.maximum(qt - 1, 0)
    carry = lax.fori_loop(0, n_far >> 1,
                          lambda i, c: online(c, [sel_tile(2 * i), sel_tile(2 * i + 1)]), (m0, a0))
    carry = lax.fori_loop(0, n_far & 1, lambda i, c: online(c, [sel_tile(n_far - 1)]), carry)
    kt1 = jnp.maximum(qt - 1, 0)
    u1, vt1 = sel_tile(kt1)
    u0, vt0 = sel_tile(qt)
    _, acc_s = online(carry, [(u1 + dm_ref[0, 1] + jnp.where(qt >= 1, 0.0, NEG_INF), vt1),
                              (u0 + dm_ref[0, 0], vt0)])
    o_s = normalised(acc_s)

    side_o_ref[:, side_half:] = _dot_nt(side_a_ref[...], side_bt_ref[side_half:, :])

    tiles = []
    for dist, kind in ((2, 2), (1, 1), (0, 0)):
        kt = jnp.maximum(qt - dist, 0)
        u = _dot(kw_s[key_rows(kt), :], qt_stack) + dm_ref[0, kind]
        if dist > 0:
            u = u + jnp.where(qt >= dist, 0.0, NEG_INF)
        tiles.append((u, vwt_s[kt]))
    _, acc_w = online((m0, a0), tiles)
    o_w = normalised(acc_w)

    gt_s[...] = jax.nn.sigmoid(gate_ref[...]).T
    head0 = pl.program_id(1) * GROUP_SIZE
    for r in range(GROUP_SIZE):
        cs = slice(r * TQ, (r + 1) * TQ)
        c = (head0 + r) * N_BRANCH
        o = (gt_s[pl.ds(c, 1), :] * o_c[:, cs] + gt_s[pl.ds(c + 1, 1), :] * o_s[:, cs]
             + gt_s[pl.ds(c + 2, 1), :] * o_w[:, cs])
        o_ref[:, r * HEAD_DIM:(r + 1) * HEAD_DIM] = o.T.astype(o_ref.dtype)


def _cmp_to_sel_t():
    ss = np.arange(LANES)[:, None] * SEL_BLOCK
    cs = np.arange(LANES)[None, :] * CMP_STRIDE
    ov = np.clip(np.minimum(cs + CMP_BLOCK, ss + SEL_BLOCK) - np.maximum(cs, ss), 0, None)
    m = ov.astype(np.float32) / np.float32(CMP_BLOCK)
    m[N_SEL:, :] = 0.0
    m[:, N_CMP:] = 0.0
    return jnp.asarray(m, BF16)


def _nsa_attention(proj, gate_logits, batch, q_norm_w, k_norm_w, pe_k, pe_v, w_cmp_k, w_cmp_v, biasc, dm,
                   side_a, side_bt, side_tm=1024, side_tn=512, casts=()):
    kvb = COL_KV // LANES

    def kv_spec(branch, which):
        base = kvb + (branch * 2 + which) * N_KV_GROUPS
        return pl.BlockSpec((SEQ, LANES), lambda b, g, qt: (b, base + g))

    full = lambda shape: pl.BlockSpec(shape, lambda b, g, qt: (0,) * len(shape))
    grid = (batch, N_KV_GROUPS, NQ)
    c_in, c_out, c_shapes = _cast_specs(casts, grid)
    side_m, side_k = side_a.shape
    side_n = side_bt.shape[0]
    n_j = side_n // side_tn
    assert (side_m // side_tm) * n_j == math.prod(grid)
    step = lambda b, g, qt: (b * N_KV_GROUPS + g) * NQ + qt
    side_in = [pl.BlockSpec((side_tm, side_k), lambda b, g, qt: (step(b, g, qt) // n_j, 0),
                            pipeline_mode=pl.Buffered(1)),
               pl.BlockSpec((side_tn, side_k), lambda b, g, qt: (step(b, g, qt) % n_j, 0))]
    side_out = pl.BlockSpec((side_tm, side_tn), lambda b, g, qt: (step(b, g, qt) // n_j, step(b, g, qt) % n_j))
    return pl.pallas_call(
        functools.partial(_nsa_kernel, n_casts=len(casts)),
        out_shape=[jax.ShapeDtypeStruct((batch * SEQ, NSA_WIDTH), BF16),
                   jax.ShapeDtypeStruct((side_m, side_n), F32)] + c_shapes,
        grid=grid,
        in_specs=[pl.BlockSpec((TQ, GROUP_SIZE * HEAD_DIM), lambda b, g, qt: (b * NQ + qt, g)),
                  kv_spec(0, 0), kv_spec(0, 1), kv_spec(1, 0), kv_spec(1, 1),
                  kv_spec(2, 0), kv_spec(2, 1),
                  pl.BlockSpec((TQ, LANES), lambda b, g, qt: (b * NQ + qt, 0)),
                  full((1, HEAD_DIM)), full((N_BRANCH, HEAD_DIM)),
                  full((CMP_BLOCK, HEAD_DIM)), full((CMP_BLOCK, HEAD_DIM)),
                  full((CMP_BLOCK * HEAD_DIM, HEAD_DIM)), full((CMP_BLOCK * HEAD_DIM, HEAD_DIM)),
                  pl.BlockSpec((1, 1, LANES, GROUP_SIZE * TQ), lambda b, g, qt: (g, qt, 0, 0)),
                  pl.BlockSpec((1, 3, TK, GROUP_SIZE * TQ), lambda b, g, qt: (g, 0, 0, 0),
                               pipeline_mode=pl.Buffered(1)),
                  full((LANES, LANES))] + side_in + c_in,
        out_specs=[pl.BlockSpec((TQ, GROUP_SIZE * HEAD_DIM), lambda b, g, qt: (b * NQ + qt, g)),
                   side_out] + c_out,
        scratch_shapes=[pltpu.VMEM((LANES, HEAD_DIM), BF16),
                        pltpu.VMEM((HEAD_DIM, LANES), BF16),
                        pltpu.VMEM((SEQ, 2 * HEAD_DIM), BF16),
                        pltpu.VMEM((SEQ, HEAD_DIM), BF16),
                        pltpu.VMEM((SEQ // TK, V_ROWS, TK), BF16),
                        pltpu.VMEM((SEQ // TK, V_ROWS, TK), BF16),
                        pltpu.VMEM((LANES, TQ), F32)],
        compiler_params=pltpu.CompilerParams(
            dimension_semantics=("arbitrary", "arbitrary", "arbitrary"),
            vmem_limit_bytes=NSA_VMEM_LIMIT),
    )(proj, proj, proj, proj, proj, proj, proj, gate_logits,
      q_norm_w.reshape(1, HEAD_DIM), k_norm_w, pe_k, pe_v,
      w_cmp_k.astype(BF16), w_cmp_v.astype(BF16), biasc, dm, _cmp_to_sel_t(),
      side_a, side_bt, *[c[0] for c in casts])


def _ret_kernel(lg_ref, q_ref, k_ref, v_ref, g_ref, cos_ref, sin_ref, gnw_ref, *rest, n_casts):
    o_ref = rest[n_casts]
    state_ref = rest[-1]
    _cast_slabs(rest[:n_casts] + rest[n_casts + 1:-1], n_casts)
    c_len = RET_CHUNK
    ii = lax.broadcasted_iota(jnp.int32, (c_len, c_len), 0)
    jj = lax.broadcasted_iota(jnp.int32, (c_len, c_len), 1)
    diff = (ii - jj).astype(F32)
    icol = lax.broadcasted_iota(jnp.int32, (c_len, 1), 0).astype(F32)
    half = RET_KEY_DIM // 2
    state_ref[...] = jnp.zeros_like(state_ref)
    tables = []
    for hh in range(RET_HEADS_PER_STEP):
        lg = lg_ref[pl.program_id(1) * RET_HEADS_PER_STEP + hh]
        tables.append((
            jnp.where(diff >= 0.0, jnp.exp(jnp.maximum(diff, 0.0) * lg), 0.0),
            jnp.exp((icol + 1.0) * lg),
            jnp.exp((c_len - 1.0 - icol) * lg),
            jnp.exp(jnp.full((1, RET_VAL_DIM), float(c_len), F32) * lg)))

    def body(c, _):
        rows = pl.ds(pl.multiple_of(c * c_len, c_len), c_len)
        cos = cos_ref[rows, :]
        sin = sin_ref[rows, :]

        def rot(x):
            x1, x2 = x[:, :half], x[:, half:]
            return jnp.concatenate([x1 * cos - x2 * sin, x2 * cos + x1 * sin], axis=-1)

        for hh, (decay_in, xi, zeta, chunk_decay) in enumerate(tables):
            kcols = slice(hh * RET_KEY_DIM, (hh + 1) * RET_KEY_DIM)
            vcols = slice(hh * RET_VAL_DIM, (hh + 1) * RET_VAL_DIM)
            qr = rot(q_ref[rows, kcols])
            kr = rot(k_ref[rows, kcols]) * (RET_KEY_DIM ** -0.5)
            vb = v_ref[rows, vcols].astype(BF16)
            scores = _dot_nt(qr.astype(BF16), kr.astype(BF16)) * decay_in
            inner = _dot(scores.astype(BF16), vb)
            st = state_ref[hh]
            cross = _dot((qr * xi).astype(BF16), st.astype(BF16))
            kv = _dot((kr * zeta).T.astype(BF16), vb)
            state_ref[hh] = chunk_decay * st + kv
            y = inner + cross
            y = y * lax.rsqrt(jnp.mean(y * y, axis=-1, keepdims=True) + EPS) * gnw_ref[:, vcols]
            g = g_ref[rows, vcols]
            o_ref[rows, vcols] = (g * jax.nn.sigmoid(g) * y).astype(o_ref.dtype)
        return 0

    lax.fori_loop(0, SEQ // c_len, body, 0, unroll=2)


def _retention(proj, batch, gn_w, casts=()):
    pos = np.arange(SEQ, dtype=np.float64)
    half = RET_KEY_DIM // 2
    inv = np.exp(-np.linspace(0.0, 1.0, half) * math.log(ROPE_BASE)).astype(np.float32)
    ang = pos.astype(np.float32)[:, None] * inv[None, :]
    cos = jnp.asarray(np.cos(ang.astype(np.float64)), F32)
    sin = jnp.asarray(np.sin(ang.astype(np.float64)), F32)
    log_gamma = jnp.asarray(np.log(1.0 - 2.0 ** (-5.0 - np.arange(N_RET_HEADS))), F32)
    w = RET_KEY_DIM * RET_HEADS_PER_STEP

    def col_spec(col0):
        return pl.BlockSpec((SEQ, w), lambda b, h: (b, col0 // w + h))

    grid = (batch, N_RET_HEADS // RET_HEADS_PER_STEP)
    c_in, c_out, c_shapes = _cast_specs(casts, grid)
    return pl.pallas_call(
        functools.partial(_ret_kernel, n_casts=len(casts)),
        out_shape=[jax.ShapeDtypeStruct((batch * SEQ, RET_WIDTH), BF16)] + c_shapes,
        grid=grid,
        in_specs=[pl.BlockSpec(memory_space=pltpu.SMEM),
                  col_spec(COL_QR), col_spec(COL_KR), col_spec(COL_VR), col_spec(COL_GR),
                  pl.BlockSpec((SEQ, half), lambda b, h: (0, 0)),
                  pl.BlockSpec((SEQ, half), lambda b, h: (0, 0)),
                  pl.BlockSpec((1, w), lambda b, h: (0, h))] + c_in,
        out_specs=[pl.BlockSpec((SEQ, w), lambda b, h: (b, h))] + c_out,
        scratch_shapes=[pltpu.VMEM((RET_HEADS_PER_STEP, RET_KEY_DIM, RET_VAL_DIM), F32)],
        compiler_params=pltpu.CompilerParams(dimension_semantics=("arbitrary", "arbitrary"),
                                             vmem_limit_bytes=VMEM_LIMIT),
    )(log_gamma, proj, proj, proj, proj, cos, sin, gn_w.reshape(1, RET_WIDTH), *[c[0] for c in casts])


def _layer(x, norm1_w, w_in, nsa_q_norm_w, nsa_k_norm_w, cmp_pe_k, cmp_pe_v, w_cmp_k, w_cmp_v,
           rel_bias, ret_gn_w, w_out, norm2_w, w_gate, w_up, w_down):
    batch = x.shape[0]
    rows = batch * SEQ
    xf = x.reshape(rows, D_MODEL)
    h = _rmsnorm(xf, norm1_w)
    w_in_t = w_in.T
    gate0 = W_IN_NSA_COLS
    gate1 = gate0 + W_IN_GATE_COLS
    ret_cols = 4 * RET_WIDTH
    nsa_tn, nsa_tm = 1024, 512
    nsa_steps = (W_IN_NSA_COLS // nsa_tn) * (rows // nsa_tm)
    ret_slabs = 1 << (nsa_steps.bit_length() - 1)
    proj_nsa, w_ret_t = _in_projection(h, w_in_t, 0, W_IN_NSA_COLS, nsa_tn, tm=nsa_tm,
                                       casts=((w_in_t, gate1, ret_cols, ret_slabs),))
    gate_logits, = _in_projection(h, w_in_t, gate0, LANES, LANES)
    biasc, dm = _bias_tables(rel_bias)
    n_steps = batch * N_KV_GROUPS * NQ
    o_nsa, proj_ret, w_gate_b, w_up_b, w_out_b = _nsa_attention(
        proj_nsa, gate_logits, batch, nsa_q_norm_w, nsa_k_norm_w, cmp_pe_k, cmp_pe_v,
        w_cmp_k, w_cmp_v, biasc, dm, h, w_ret_t,
        casts=((w_gate, 0, D_MODEL, n_steps), (w_up, 0, D_MODEL, n_steps), (w_out, 0, D_MODEL, n_steps)))
    o_ret, = _retention(proj_ret, batch, ret_gn_w)
    x1, = _out_projection(o_nsa, o_ret, w_out_b, xf, tn=1024)
    hf = _rmsnorm(x1, norm2_w)
    up_tm, up_tn = 2048, 256
    up_steps = (rows // up_tm) * (D_FF // up_tn)
    mid, w_down_b = _ffn_up(hf, w_gate_b, w_up_b, tm=up_tm, tn=up_tn, casts=((w_down, 0, D_FF, up_steps),))
    out = _ffn_down(mid, w_down_b, x1)
    return out.reshape(batch, SEQ, D_MODEL)


def kernel(x, norm1_w, w_in, nsa_q_norm_w, nsa_k_norm_w, cmp_pe_k, cmp_pe_v, w_cmp_k, w_cmp_v,
           rel_bias, ret_gn_w, w_out, norm2_w, w_gate, w_up, w_down):
    for l in range(norm1_w.shape[0]):
        x = _layer(x, norm1_w[l], w_in[l], nsa_q_norm_w[l], nsa_k_norm_w[l], cmp_pe_k[l], cmp_pe_v[l],
                   w_cmp_k[l], w_cmp_v[l], rel_bias, ret_gn_w[l], w_out[l], norm2_w[l],
                   w_gate[l], w_up[l], w_down[l])
    return x
```

```python
import functools
import math

import numpy as np
import jax
import jax.numpy as jnp
from jax import lax
from jax.experimental import pallas as pl
from jax.experimental.pallas import tpu as pltpu

F32 = jnp.float32
BF16 = jnp.bfloat16

D_MODEL = 4096
SEQ = 2048
HEAD_DIM = 128
N_NSA_HEADS = 16
N_KV_GROUPS = 4
GROUP_SIZE = 4
N_BRANCH = 3
CMP_BLOCK = 32
CMP_STRIDE = 16
SEL_BLOCK = 64
TOP_N = 16
WINDOW = 512
RET_KEY_DIM = 256
RET_VAL_DIM = 256
N_RET_HEADS = 8
RET_CHUNK = 128
ROPE_BASE = 10000.0
NSA_WIDTH = 2048
RET_WIDTH = 2048
D_FF = 11008
NUM_BUCKETS = 32
MAX_DISTANCE = 128
EPS = 1e-6
NEG_INF = -1e30
FORCE_SCORE = 1e9
N_SEL = SEQ // SEL_BLOCK
N_CMP = (SEQ - CMP_BLOCK) // CMP_STRIDE + 1
SCALE = HEAD_DIM ** -0.5
LOG2E = math.log2(math.e)
EXP2_SCALE = SCALE * LOG2E

LANES = 128
VMEM_LIMIT = 56 * 1024 * 1024
NSA_VMEM_LIMIT = 60 * 1024 * 1024

TQ = 256
TK = 256
NQ = SEQ // TQ
RET_HEADS_PER_STEP = 2
V_ROWS = HEAD_DIM + 16
assert TQ == TK and WINDOW == 2 * TK and TK % SEL_BLOCK == 0 and N_SEL <= HEAD_DIM and TK >= MAX_DISTANCE

W_IN_NSA_COLS = NSA_WIDTH + N_BRANCH * 2 * N_KV_GROUPS * HEAD_DIM
W_IN_GATE_COLS = N_NSA_HEADS * N_BRANCH
COL_KV = NSA_WIDTH
COL_QR = 0
COL_KR = 2048
COL_VR = 4096
COL_GR = 6144


def _bucket_thresholds():
    rel = np.arange(0, 4 * MAX_DISTANCE)
    max_exact = NUM_BUCKETS // 2
    nf = np.maximum(rel, 1).astype(np.float64)
    large = max_exact + np.floor(np.log(nf / max_exact) / math.log(MAX_DISTANCE / max_exact)
                                 * (NUM_BUCKETS - max_exact)).astype(np.int64)
    large = np.minimum(large, NUM_BUCKETS - 1)
    b = np.where(rel < max_exact, rel, large)
    return [int(np.argmax(b >= k)) for k in range(NUM_BUCKETS)]


_THR = _bucket_thresholds()


def _dot(a, b):
    return jnp.dot(a, b, preferred_element_type=F32)


def _dot_nt(a, b):
    return lax.dot_general(a, b, (((1,), (1,)), ((), ())), preferred_element_type=F32)


def _rms_kernel(x_ref, w_ref, o_ref):
    x = x_ref[...]
    y = x * lax.rsqrt(jnp.mean(x * x, axis=-1, keepdims=True) + EPS)
    o_ref[...] = (y * w_ref[...]).astype(o_ref.dtype)


def _rmsnorm(x, w, tm=512):
    m, d = x.shape
    return pl.pallas_call(
        _rms_kernel,
        out_shape=jax.ShapeDtypeStruct((m, d), BF16),
        grid=(m // tm,),
        in_specs=[pl.BlockSpec((tm, d), lambda i: (i, 0)),
                  pl.BlockSpec((1, d), lambda i: (0, 0))],
        out_specs=pl.BlockSpec((tm, d), lambda i: (i, 0)),
        compiler_params=pltpu.CompilerParams(dimension_semantics=("arbitrary",),
                                             vmem_limit_bytes=VMEM_LIMIT),
    )(x, w.reshape(1, d))


def _cast_spec(grid, row0, n_rows, n_cols, n_slabs):
    slab = n_rows // n_slabs
    sub = 8
    n_steps = math.prod(grid)
    assert slab * n_slabs == n_rows and slab % 16 == 0 and n_cols % LANES == 0
    assert row0 % sub == 0 and n_slabs <= n_steps

    def slab_index(ids):
        step = ids[0]
        for dim, idx in zip(grid[1:], ids[1:]):
            step = step * dim + idx
        return step if n_slabs == n_steps else jnp.minimum(step, n_slabs - 1)

    in_spec = pl.BlockSpec((pl.Element(slab), pl.Element(n_cols)),
                           lambda *ids: ((row0 // sub + slab_index(ids) * (slab // sub)) * sub, 0))
    out_spec = pl.BlockSpec((slab, n_cols), lambda *ids: (slab_index(ids), 0))
    return in_spec, out_spec, jax.ShapeDtypeStruct((n_rows, n_cols), BF16)


def _cast_specs(casts, grid):
    specs = [_cast_spec(grid, row0, n_rows, arr.shape[1], n_slabs) for arr, row0, n_rows, n_slabs in casts]
    return [s[0] for s in specs], [s[1] for s in specs], [s[2] for s in specs]


def _cast_slabs(refs, n_casts):
    for src, dst in zip(refs[:n_casts], refs[len(refs) - n_casts:]):
        dst[...] = src[...].astype(dst.dtype)


def _inproj_kernel(a_ref, wt_ref, *rest, n_casts):
    o_ref = rest[n_casts]
    wbf_ref = rest[-1]

    @pl.when(pl.program_id(1) == 0)
    def _():
        wbf_ref[...] = wt_ref[...].astype(BF16)

    o_ref[...] = _dot_nt(a_ref[...], wbf_ref[...])
    _cast_slabs(rest[:n_casts] + rest[n_casts + 1:-1], n_casts)


def _in_projection(a, w_t, row0, n_cols, tn, tm=1024, casts=()):
    m, k = a.shape
    grid = (n_cols // tn, m // tm)
    c_in, c_out, c_shapes = _cast_specs(casts, grid)
    sub = 8
    assert row0 % sub == 0 and tn % sub == 0
    return pl.pallas_call(
        functools.partial(_inproj_kernel, n_casts=len(casts)),
        out_shape=[jax.ShapeDtypeStruct((m, n_cols), F32)] + c_shapes,
        grid=grid,
        in_specs=[pl.BlockSpec((tm, k), lambda j, i: (i, 0)),
                  pl.BlockSpec((pl.Element(tn), pl.Element(k)),
                               lambda j, i: ((row0 // sub + j * (tn // sub)) * sub, 0),
                               pipeline_mode=pl.Buffered(1))] + c_in,
        out_specs=[pl.BlockSpec((tm, tn), lambda j, i: (i, j))] + c_out,
        scratch_shapes=[pltpu.VMEM((tn, k), BF16)],
        compiler_params=pltpu.CompilerParams(dimension_semantics=("arbitrary", "arbitrary"),
                                             vmem_limit_bytes=VMEM_LIMIT),
    )(a, w_t, *[c[0] for c in casts])


def _outproj_kernel(a1_ref, a2_ref, w1_ref, w2_ref, x_ref, *rest, n_casts):
    o_ref = rest[n_casts]
    acc = _dot(a1_ref[...], w1_ref[...]) + _dot(a2_ref[...], w2_ref[...])
    o_ref[...] = x_ref[...] + acc
    _cast_slabs(rest[:n_casts] + rest[n_casts + 1:], n_casts)


def _out_projection(o_nsa, o_ret, w_out, x, tm=1024, tn=512, casts=()):
    m = x.shape[0]
    kh = NSA_WIDTH
    grid = (m // tm, D_MODEL // tn)
    c_in, c_out, c_shapes = _cast_specs(casts, grid)
    return pl.pallas_call(
        functools.partial(_outproj_kernel, n_casts=len(casts)),
        out_shape=[jax.ShapeDtypeStruct((m, D_MODEL), F32)] + c_shapes,
        grid=grid,
        in_specs=[pl.BlockSpec((tm, kh), lambda i, j: (i, 0)),
                  pl.BlockSpec((tm, kh), lambda i, j: (i, 0)),
                  pl.BlockSpec((kh, tn), lambda i, j: (0, j)),
                  pl.BlockSpec((kh, tn), lambda i, j: (1, j)),
                  pl.BlockSpec((tm, tn), lambda i, j: (i, j))] + c_in,
        out_specs=[pl.BlockSpec((tm, tn), lambda i, j: (i, j))] + c_out,
        compiler_params=pltpu.CompilerParams(dimension_semantics=("arbitrary", "arbitrary"),
                                             vmem_limit_bytes=VMEM_LIMIT),
    )(o_nsa, o_ret, w_out, w_out, x, *[c[0] for c in casts])


def _ffn_up_kernel(h_ref, wg_ref, wu_ref, *rest, n_casts):
    o_ref = rest[n_casts]
    h = h_ref[...]
    g = _dot(h, wg_ref[...])
    u = _dot(h, wu_ref[...])
    o_ref[...] = (g * jax.nn.sigmoid(g) * u).astype(o_ref.dtype)
    _cast_slabs(rest[:n_casts] + rest[n_casts + 1:], n_casts)


def _ffn_up(h, w_gate, w_up, tm=1024, tn=256, casts=()):
    m, k = h.shape
    n = w_gate.shape[1]
    grid = (m // tm, n // tn)
    c_in, c_out, c_shapes = _cast_specs(casts, grid)
    return pl.pallas_call(
        functools.partial(_ffn_up_kernel, n_casts=len(casts)),
        out_shape=[jax.ShapeDtypeStruct((m, n), BF16)] + c_shapes,
        grid=grid,
        in_specs=[pl.BlockSpec((tm, k), lambda i, j: (i, 0)),
                  pl.BlockSpec((k, tn), lambda i, j: (0, j)),
                  pl.BlockSpec((k, tn), lambda i, j: (0, j))] + c_in,
        out_specs=[pl.BlockSpec((tm, tn), lambda i, j: (i, j))] + c_out,
        compiler_params=pltpu.CompilerParams(dimension_semantics=("arbitrary", "arbitrary"),
                                             vmem_limit_bytes=VMEM_LIMIT),
    )(h, w_gate, w_up, *[c[0] for c in casts])


def _ffn_down_kernel(a_ref, b_ref, r_ref, o_ref):
    o_ref[...] = r_ref[...] + _dot(a_ref[...], b_ref[...])


def _ffn_down(a, w_down, x, tm=1024, tn=512, tk=5504):
    m, k = a.shape
    n = w_down.shape[1]
    res = x
    for kk in range(k // tk):
        res = pl.pallas_call(
            _ffn_down_kernel,
            out_shape=jax.ShapeDtypeStruct((m, n), F32),
            grid=(m // tm, n // tn),
            in_specs=[pl.BlockSpec((tm, tk), lambda i, j, kk=kk: (i, kk)),
                      pl.BlockSpec((tk, tn), lambda i, j, kk=kk: (kk, j)),
                      pl.BlockSpec((tm, tn), lambda i, j: (i, j))],
            out_specs=pl.BlockSpec((tm, tn), lambda i, j: (i, j)),
            compiler_params=pltpu.CompilerParams(dimension_semantics=("arbitrary", "arbitrary"),
                                                 vmem_limit_bytes=VMEM_LIMIT),
        )(a, w_down, res)
    return res


def _bias_value(rel, relb_ref, h):
    val = jnp.full(rel.shape, relb_ref[0, h], F32)
    for k in range(1, NUM_BUCKETS):
        val = jnp.where(rel >= _THR[k], relb_ref[k, h], val)
    return val


def _bias_tables_kernel(relb_ref, biasc_ref, dm_ref):
    h = pl.program_id(0)
    sub = 8
    j = lax.broadcasted_iota(jnp.int32, (sub, 2 * SEQ), 1)
    row_c = _bias_value(j - SEQ - (CMP_BLOCK - 1), relb_ref, h)[0:1]
    toe_c = pltpu.roll(jnp.broadcast_to(row_c, (LANES, 2 * SEQ)), 0, 1,
                       stride=CMP_STRIDE, stride_axis=0)
    for qt in range(NQ):
        biasc_ref[0, qt] = toe_c[:, SEQ + qt * TQ:SEQ + (qt + 1) * TQ]
    far = relb_ref[NUM_BUCKETS - 1, h]
    j2 = lax.broadcasted_iota(jnp.int32, (sub, 2 * TK), 1)
    b = lax.broadcasted_iota(jnp.int32, (TK, TQ), 0)
    a = lax.broadcasted_iota(jnp.int32, (TK, TQ), 1)
    for kind in range(2):
        row = ((_bias_value(j2 + (kind - 1) * TK, relb_ref, h) - far) * LOG2E)[0:1]
        toe = pltpu.roll(jnp.broadcast_to(row, (TK, 2 * TK)), 0, 1, stride=1, stride_axis=0)
        tile = toe[:, TK:]
        dm_ref[0, kind] = tile + jnp.where(a >= b, 0.0, NEG_INF) if kind == 0 else tile
    dm_ref[0, 2] = jnp.where(b > a, 0.0, NEG_INF)


def _bias_tables(rel_bias):
    return pl.pallas_call(
        _bias_tables_kernel,
        out_shape=(jax.ShapeDtypeStruct((N_KV_GROUPS, NQ, LANES, GROUP_SIZE * TQ), F32),
                   jax.ShapeDtypeStruct((N_KV_GROUPS, 3, TK, GROUP_SIZE * TQ), F32)),
        grid=(N_NSA_HEADS,),
        in_specs=[pl.BlockSpec(memory_space=pltpu.SMEM)],
        out_specs=(pl.BlockSpec((1, NQ, LANES, TQ), lambda h: (h // GROUP_SIZE, 0, 0, h % GROUP_SIZE)),
                   pl.BlockSpec((1, 3, TK, TQ), lambda h: (h // GROUP_SIZE, 0, 0, h % GROUP_SIZE))),
        compiler_params=pltpu.CompilerParams(dimension_semantics=("arbitrary",),
                                             vmem_limit_bytes=VMEM_LIMIT),
    )(rel_bias)


def _row_rms(x, w):
    return x * lax.rsqrt(jnp.mean(x * x, axis=-1, keepdims=True) + EPS) * w


N_NSA_INPUTS = 17
N_NSA_SCRATCH = 7


def _nsa_kernel(*refs, n_casts):
    (q_ref, k0_ref, v0_ref, k1_ref, v1_ref, k2_ref, v2_ref, gate_ref,
     qnw_ref, knw_ref, pek_ref, pev_ref, wck_ref, wcv_ref, biasc_ref, dm_ref, c2st_ref) = refs[:N_NSA_INPUTS]
    side_a_ref, side_bt_ref = refs[N_NSA_INPUTS:N_NSA_INPUTS + 2]
    n_in = N_NSA_INPUTS + 2 + n_casts
    cast_in = refs[N_NSA_INPUTS + 2:n_in]
    o_ref, side_o_ref = refs[n_in:n_in + 2]
    cast_out = refs[n_in + 2:n_in + 2 + n_casts]
    kc_s, vct_s, kaug_s, kw_s, vst_s, vwt_s, gt_s = refs[len(refs) - N_NSA_SCRATCH:]
    qt = pl.program_id(2)
    cols_all = GROUP_SIZE * TQ
    side_half = side_bt_ref.shape[0] // 2

    @pl.when(qt == 0)
    def _prologue():
        kaug_s[:, :HEAD_DIM] = (_row_rms(k1_ref[...], knw_ref[1:2, :]) * EXP2_SCALE).astype(BF16)
        jblk = lax.broadcasted_iota(jnp.int32, (SEQ, LANES), 0) // SEL_BLOCK
        lane = lax.broadcasted_iota(jnp.int32, (SEQ, LANES), 1)
        kaug_s[:, HEAD_DIM:] = jnp.where(jblk == lane, 1.0, 0.0).astype(BF16)
        kw_s[...] = (_row_rms(k2_ref[...], knw_ref[2:3, :]) * EXP2_SCALE).astype(BF16)
        ones_row = jnp.where(lax.broadcasted_iota(jnp.int32, (V_ROWS - HEAD_DIM, TK), 0) == 0,
                             1.0, 0.0).astype(BF16)
        for t in range(SEQ // TK):
            vst_s[t, :HEAD_DIM, :] = v1_ref[t * TK:(t + 1) * TK, :].T.astype(BF16)
            vst_s[t, HEAD_DIM:, :] = ones_row
            vwt_s[t, :HEAD_DIM, :] = v2_ref[t * TK:(t + 1) * TK, :].T.astype(BF16)
            vwt_s[t, HEAD_DIM:, :] = ones_row

        def compress(x_ref, pe_ref, w_ref):
            acc_lo = jnp.zeros((LANES, HEAD_DIM), F32)
            acc_hi = jnp.zeros((LANES, HEAD_DIM), F32)
            for j in range(CMP_STRIDE):
                y = x_ref[pl.ds(j, LANES, stride=CMP_STRIDE), :]
                acc_lo += _dot((y + pe_ref[j:j + 1, :]).astype(BF16),
                               w_ref[j * HEAD_DIM:(j + 1) * HEAD_DIM, :])
                jh = CMP_STRIDE + j
                acc_hi += _dot((y + pe_ref[jh:jh + 1, :]).astype(BF16),
                               w_ref[jh * HEAD_DIM:(jh + 1) * HEAD_DIM, :])
            return acc_lo + pltpu.roll(acc_hi, LANES - 1, 0)

        kc_s[...] = _row_rms(compress(k0_ref, pek_ref, wck_ref), knw_ref[0:1, :]).astype(BF16)
        vct_s[...] = compress(v0_ref, pev_ref, wcv_ref).T.astype(BF16)

    side_o_ref[:, :side_half] = _dot_nt(side_a_ref[...], side_bt_ref[:side_half, :])
    _cast_slabs(cast_in + cast_out, n_casts)

    t0 = qt * TQ
    q = q_ref[...]
    qts = []
    for r in range(GROUP_SIZE):
        qts.append(_row_rms(q[:, r * HEAD_DIM:(r + 1) * HEAD_DIM], qnw_ref[...]).T)
    qt_stack = jnp.concatenate(qts, axis=1).astype(BF16)

    sc = _dot(kc_s[...], qt_stack) * SCALE + biasc_ref[0, 0]
    tpos = t0 + (lax.broadcasted_iota(jnp.int32, (LANES, cols_all), 1) & (TQ - 1))
    nrow = lax.broadcasted_iota(jnp.int32, (LANES, cols_all), 0)
    maskc = tpos >= CMP_STRIDE * nrow + (CMP_BLOCK - 1)
    scm = jnp.where(maskc, sc, NEG_INF)
    mc = jnp.max(scm, axis=0, keepdims=True)
    ec = jnp.where(maskc, jnp.exp(scm - mc), 0.0)
    lc = jnp.sum(ec, axis=0, keepdims=True)
    pc = ec * (1.0 / jnp.where(lc > 0.0, lc, 1.0))
    o_c = _dot(vct_s[...], pc.astype(BF16))

    psum = pc[:, 0:TQ] + pc[:, TQ:2 * TQ] + pc[:, 2 * TQ:3 * TQ] + pc[:, 3 * TQ:4 * TQ]
    p_hi = psum.astype(BF16)
    p_lo = (psum - p_hi.astype(F32)).astype(BF16)
    imp = (_dot(c2st_ref[...], p_hi) + _dot(c2st_ref[...], p_lo))[0:N_SEL]
    tq_pos = t0 + lax.broadcasted_iota(jnp.int32, (N_SEL, TQ), 1)
    sidx = lax.broadcasted_iota(jnp.int32, (N_SEL, TQ), 0)
    cur = tq_pos >> (SEL_BLOCK.bit_length() - 1)
    score = jnp.where(sidx <= cur, imp, NEG_INF)
    for forced_blk in (cur - 1, cur, jnp.zeros_like(cur)):
        score = jnp.where(sidx == forced_blk, FORCE_SCORE, score)
    rank = jnp.zeros((N_SEL, TQ), F32)
    for sp in range(N_SEL):
        row = score[sp:sp + 1, :]
        tie = jnp.where(sidx > sp, 1.0, 0.0)
        rank = rank + jnp.where(row > score, 1.0, jnp.where(row == score, tie, 0.0))
    selb = jnp.where(rank < TOP_N, 0.0, NEG_INF)
    selb = jnp.concatenate([selb, jnp.zeros((HEAD_DIM - N_SEL, TQ), F32)], axis=0).astype(BF16)
    qt_aug = jnp.concatenate([qt_stack, jnp.concatenate([selb] * GROUP_SIZE, axis=1)], axis=0)

    m0 = jnp.full((1, cols_all), NEG_INF, F32)
    a0 = jnp.zeros((V_ROWS, cols_all), F32)

    def online(carry, tiles):
        m, acc = carry
        m_new = m
        for u, _ in tiles:
            m_new = jnp.maximum(m_new, jnp.max(u, axis=0, keepdims=True))
        acc = jnp.exp2(m - m_new) * acc
        for u, vt in tiles:
            acc = acc + _dot(vt, jnp.exp2(u - m_new).astype(BF16))
        return m_new, acc

    def online_independent(carry, tiles):
        m, acc = carry
        parts = []
        for u, vt in tiles:
            mt = jnp.max(u, axis=0, keepdims=True)
            parts.append((mt, _dot(vt, jnp.exp2(u - mt).astype(BF16))))
        m_new = m
        for mt, _ in parts:
            m_new = jnp.maximum(m_new, mt)
        acc = jnp.exp2(m - m_new) * acc
        for mt, pv in parts:
            acc = acc + jnp.exp2(mt - m_new) * pv
        return m_new, acc

    def normalised(acc):
        return acc[:HEAD_DIM] * (1.0 / acc[HEAD_DIM:HEAD_DIM + 1])

    def key_rows(kt):
        return pl.ds(pl.multiple_of(kt * TK, TK), TK)

    def sel_tile(kt):
        return _dot(kaug_s[key_rows(kt), :], qt_aug), vst_s[kt]

    n_far = jnp.maximum(qt - 1, 0)
    carry = lax.fori_loop(0, n_far >> 1,
                          lambda i, c: online_independent(c, [sel_tile(2 * i), sel_tile(2 * i + 1)]),
                          (m0, a0))
    carry = lax.fori_loop(0, n_far & 1, lambda i, c: online(c, [sel_tile(n_far - 1)]), carry)
    kt1 = jnp.maximum(qt - 1, 0)
    u1, vt1 = sel_tile(kt1)
    u0, vt0 = sel_tile(qt)
    _, acc_s = online_independent(
        carry, [(u1 + dm_ref[0, 1] + jnp.where(qt >= 1, 0.0, NEG_INF), vt1), (u0 + dm_ref[0, 0], vt0)])
    o_s = normalised(acc_s)

    side_o_ref[:, side_half:] = _dot_nt(side_a_ref[...], side_bt_ref[side_half:, :])

    tiles = []
    for dist, kind in ((2, 2), (1, 1), (0, 0)):
        kt = jnp.maximum(qt - dist, 0)
        u = _dot(kw_s[key_rows(kt), :], qt_stack) + dm_ref[0, kind]
        if dist > 0:
            u = u + jnp.where(qt >= dist, 0.0, NEG_INF)
        tiles.append((u, vwt_s[kt]))
    _, acc_w = online_independent((m0, a0), tiles)
    o_w = normalised(acc_w)

    gt_s[...] = jax.nn.sigmoid(gate_ref[...]).T
    head0 = pl.program_id(1) * GROUP_SIZE
    for r in range(GROUP_SIZE):
        cs = slice(r * TQ, (r + 1) * TQ)
        c = (head0 + r) * N_BRANCH
        o = (gt_s[pl.ds(c, 1), :] * o_c[:, cs] + gt_s[pl.ds(c + 1, 1), :] * o_s[:, cs]
             + gt_s[pl.ds(c + 2, 1), :] * o_w[:, cs])
        o_ref[:, r * HEAD_DIM:(r + 1) * HEAD_DIM] = o.T.astype(o_ref.dtype)


def _cmp_to_sel_t():
    ss = np.arange(LANES)[:, None] * SEL_BLOCK
    cs = np.arange(LANES)[None, :] * CMP_STRIDE
    ov = np.clip(np.minimum(cs + CMP_BLOCK, ss + SEL_BLOCK) - np.maximum(cs, ss), 0, None)
    m = ov.astype(np.float32) / np.float32(CMP_BLOCK)
    m[N_SEL:, :] = 0.0
    m[:, N_CMP:] = 0.0
    return jnp.asarray(m, BF16)


def _nsa_attention(proj, gate_logits, batch, q_norm_w, k_norm_w, pe_k, pe_v, w_cmp_k, w_cmp_v, biasc, dm,
                   side_a, side_bt, side_tm=1024, side_tn=512, casts=()):
    kvb = COL_KV // LANES

    def kv_spec(branch, which):
        base = kvb + (branch * 2 + which) * N_KV_GROUPS
        return pl.BlockSpec((SEQ, LANES), lambda b, g, qt: (b, base + g))

    full = lambda shape: pl.BlockSpec(shape, lambda b, g, qt: (0,) * len(shape))
    grid = (batch, N_KV_GROUPS, NQ)
    c_in, c_out, c_shapes = _cast_specs(casts, grid)
    side_m, side_k = side_a.shape
    side_n = side_bt.shape[0]
    n_j = side_n // side_tn
    assert (side_m // side_tm) * n_j == math.prod(grid)
    step = lambda b, g, qt: (b * N_KV_GROUPS + g) * NQ + qt
    side_in = [pl.BlockSpec((side_tm, side_k), lambda b, g, qt: (step(b, g, qt) // n_j, 0),
                            pipeline_mode=pl.Buffered(1)),
               pl.BlockSpec((side_tn, side_k), lambda b, g, qt: (step(b, g, qt) % n_j, 0))]
    side_out = pl.BlockSpec((side_tm, side_tn), lambda b, g, qt: (step(b, g, qt) // n_j, step(b, g, qt) % n_j))
    return pl.pallas_call(
        functools.partial(_nsa_kernel, n_casts=len(casts)),
        out_shape=[jax.ShapeDtypeStruct((batch * SEQ, NSA_WIDTH), BF16),
                   jax.ShapeDtypeStruct((side_m, side_n), F32)] + c_shapes,
        grid=grid,
        in_specs=[pl.BlockSpec((TQ, GROUP_SIZE * HEAD_DIM), lambda b, g, qt: (b * NQ + qt, g)),
                  kv_spec(0, 0), kv_spec(0, 1), kv_spec(1, 0), kv_spec(1, 1),
                  kv_spec(2, 0), kv_spec(2, 1),
                  pl.BlockSpec((TQ, LANES), lambda b, g, qt: (b * NQ + qt, 0)),
                  full((1, HEAD_DIM)), full((N_BRANCH, HEAD_DIM)),
                  full((CMP_BLOCK, HEAD_DIM)), full((CMP_BLOCK, HEAD_DIM)),
                  full((CMP_BLOCK * HEAD_DIM, HEAD_DIM)), full((CMP_BLOCK * HEAD_DIM, HEAD_DIM)),
                  pl.BlockSpec((1, 1, LANES, GROUP_SIZE * TQ), lambda b, g, qt: (g, qt, 0, 0)),
                  pl.BlockSpec((1, 3, TK, GROUP_SIZE * TQ), lambda b, g, qt: (g, 0, 0, 0),
                               pipeline_mode=pl.Buffered(1)),
                  full((LANES, LANES))] + side_in + c_in,
        out_specs=[pl.BlockSpec((TQ, GROUP_SIZE * HEAD_DIM), lambda b, g, qt: (b * NQ + qt, g)),
                   side_out] + c_out,
        scratch_shapes=[pltpu.VMEM((LANES, HEAD_DIM), BF16),
                        pltpu.VMEM((HEAD_DIM, LANES), BF16),
                        pltpu.VMEM((SEQ, 2 * HEAD_DIM), BF16),
                        pltpu.VMEM((SEQ, HEAD_DIM), BF16),
                        pltpu.VMEM((SEQ // TK, V_ROWS, TK), BF16),
                        pltpu.VMEM((SEQ // TK, V_ROWS, TK), BF16),
                        pltpu.VMEM((LANES, TQ), F32)],
        compiler_params=pltpu.CompilerParams(
            dimension_semantics=("arbitrary", "arbitrary", "arbitrary"),
            vmem_limit_bytes=NSA_VMEM_LIMIT),
    )(proj, proj, proj, proj, proj, proj, proj, gate_logits,
      q_norm_w.reshape(1, HEAD_DIM), k_norm_w, pe_k, pe_v,
      w_cmp_k.astype(BF16), w_cmp_v.astype(BF16), biasc, dm, _cmp_to_sel_t(),
      side_a, side_bt, *[c[0] for c in casts])


def _ret_kernel(lg_ref, q_ref, k_ref, v_ref, g_ref, cos_ref, sin_ref, gnw_ref, *rest, n_casts):
    o_ref = rest[n_casts]
    state_ref = rest[-1]
    _cast_slabs(rest[:n_casts] + rest[n_casts + 1:-1], n_casts)
    c_len = RET_CHUNK
    ii = lax.broadcasted_iota(jnp.int32, (c_len, c_len), 0)
    jj = lax.broadcasted_iota(jnp.int32, (c_len, c_len), 1)
    diff = (ii - jj).astype(F32)
    icol = lax.broadcasted_iota(jnp.int32, (c_len, 1), 0).astype(F32)
    half = RET_KEY_DIM // 2
    state_ref[...] = jnp.zeros_like(state_ref)
    tables = []
    for hh in range(RET_HEADS_PER_STEP):
        lg = lg_ref[pl.program_id(1) * RET_HEADS_PER_STEP + hh]
        tables.append((
            jnp.where(diff >= 0.0, jnp.exp(jnp.maximum(diff, 0.0) * lg), 0.0),
            jnp.exp((icol + 1.0) * lg),
            jnp.exp((c_len - 1.0 - icol) * lg),
            jnp.exp(jnp.full((1, RET_VAL_DIM), float(c_len), F32) * lg)))

    def body(c, _):
        rows = pl.ds(pl.multiple_of(c * c_len, c_len), c_len)
        cos = cos_ref[rows, :]
        sin = sin_ref[rows, :]

        def rot(x):
            x1, x2 = x[:, :half], x[:, half:]
            return jnp.concatenate([x1 * cos - x2 * sin, x2 * cos + x1 * sin], axis=-1)

        for hh, (decay_in, xi, zeta, chunk_decay) in enumerate(tables):
            kcols = slice(hh * RET_KEY_DIM, (hh + 1) * RET_KEY_DIM)
            vcols = slice(hh * RET_VAL_DIM, (hh + 1) * RET_VAL_DIM)
            qr = rot(q_ref[rows, kcols])
            kr = rot(k_ref[rows, kcols]) * (RET_KEY_DIM ** -0.5)
            vb = v_ref[rows, vcols].astype(BF16)
            scores = _dot_nt(qr.astype(BF16), kr.astype(BF16)) * decay_in
            inner = _dot(scores.astype(BF16), vb)
            st = state_ref[hh]
            cross = _dot((qr * xi).astype(BF16), st.astype(BF16))
            kv = _dot((kr * zeta).T.astype(BF16), vb)
            state_ref[hh] = chunk_decay * st + kv
            y = inner + cross
            y = y * lax.rsqrt(jnp.mean(y * y, axis=-1, keepdims=True) + EPS) * gnw_ref[:, vcols]
            g = g_ref[rows, vcols]
            o_ref[rows, vcols] = (g * jax.nn.sigmoid(g) * y).astype(o_ref.dtype)
        return 0

    lax.fori_loop(0, SEQ // c_len, body, 0, unroll=4)


def _retention(proj, batch, gn_w, casts=()):
    pos = np.arange(SEQ, dtype=np.float64)
    half = RET_KEY_DIM // 2
    inv = np.exp(-np.linspace(0.0, 1.0, half) * math.log(ROPE_BASE)).astype(np.float32)
    ang = pos.astype(np.float32)[:, None] * inv[None, :]
    cos = jnp.asarray(np.cos(ang.astype(np.float64)), F32)
    sin = jnp.asarray(np.sin(ang.astype(np.float64)), F32)
    log_gamma = jnp.asarray(np.log(1.0 - 2.0 ** (-5.0 - np.arange(N_RET_HEADS))), F32)
    w = RET_KEY_DIM * RET_HEADS_PER_STEP

    def col_spec(col0):
        return pl.BlockSpec((SEQ, w), lambda b, h: (b, col0 // w + h))

    grid = (batch, N_RET_HEADS // RET_HEADS_PER_STEP)
    c_in, c_out, c_shapes = _cast_specs(casts, grid)
    return pl.pallas_call(
        functools.partial(_ret_kernel, n_casts=len(casts)),
        out_shape=[jax.ShapeDtypeStruct((batch * SEQ, RET_WIDTH), BF16)] + c_shapes,
        grid=grid,
        in_specs=[pl.BlockSpec(memory_space=pltpu.SMEM),
                  col_spec(COL_QR), col_spec(COL_KR), col_spec(COL_VR), col_spec(COL_GR),
                  pl.BlockSpec((SEQ, half), lambda b, h: (0, 0)),
                  pl.BlockSpec((SEQ, half), lambda b, h: (0, 0)),
                  pl.BlockSpec((1, w), lambda b, h: (0, h))] + c_in,
        out_specs=[pl.BlockSpec((SEQ, w), lambda b, h: (b, h))] + c_out,
        scratch_shapes=[pltpu.VMEM((RET_HEADS_PER_STEP, RET_KEY_DIM, RET_VAL_DIM), F32)],
        compiler_params=pltpu.CompilerParams(dimension_semantics=("arbitrary", "arbitrary"),
                                             vmem_limit_bytes=VMEM_LIMIT),
    )(log_gamma, proj, proj, proj, proj, cos, sin, gn_w.reshape(1, RET_WIDTH), *[c[0] for c in casts])


def _layer(x, norm1_w, w_in, nsa_q_norm_w, nsa_k_norm_w, cmp_pe_k, cmp_pe_v, w_cmp_k, w_cmp_v,
           rel_bias, ret_gn_w, w_out, norm2_w, w_gate, w_up, w_down):
    batch = x.shape[0]
    rows = batch * SEQ
    xf = x.reshape(rows, D_MODEL)
    h = _rmsnorm(xf, norm1_w)
    w_in_t = w_in.T
    gate0 = W_IN_NSA_COLS
    gate1 = gate0 + W_IN_GATE_COLS
    ret_cols = 4 * RET_WIDTH
    nsa_tn, nsa_tm = 1024, 512
    nsa_steps = (W_IN_NSA_COLS // nsa_tn) * (rows // nsa_tm)
    ret_slabs = 1 << (nsa_steps.bit_length() - 1)
    proj_nsa, w_ret_t = _in_projection(h, w_in_t, 0, W_IN_NSA_COLS, nsa_tn, tm=nsa_tm,
                                       casts=((w_in_t, gate1, ret_cols, ret_slabs),))
    gate_logits, = _in_projection(h, w_in_t, gate0, LANES, LANES)
    biasc, dm = _bias_tables(rel_bias)
    n_steps = batch * N_KV_GROUPS * NQ
    o_nsa, proj_ret, w_gate_b, w_up_b, w_out_b = _nsa_attention(
        proj_nsa, gate_logits, batch, nsa_q_norm_w, nsa_k_norm_w, cmp_pe_k, cmp_pe_v,
        w_cmp_k, w_cmp_v, biasc, dm, h, w_ret_t,
        casts=((w_gate, 0, D_MODEL, n_steps), (w_up, 0, D_MODEL, n_steps), (w_out, 0, D_MODEL, n_steps)))
    o_ret, = _retention(proj_ret, batch, ret_gn_w)
    x1, = _out_projection(o_nsa, o_ret, w_out_b, xf, tn=1024)
    hf = _rmsnorm(x1, norm2_w)
    up_tm, up_tn = 2048, 256
    up_steps = (rows // up_tm) * (D_FF // up_tn)
    mid, w_down_b = _ffn_up(hf, w_gate_b, w_up_b, tm=up_tm, tn=up_tn, casts=((w_down, 0, D_FF, up_steps),))
    out = _ffn_down(mid, w_down_b, x1)
    return out.reshape(batch, SEQ, D_MODEL)


def kernel(x, norm1_w, w_in, nsa_q_norm_w, nsa_k_norm_w, cmp_pe_k, cmp_pe_v, w_cmp_k, w_cmp_v,
           rel_bias, ret_gn_w, w_out, norm2_w, w_gate, w_up, w_down):
    for l in range(norm1_w.shape[0]):
        x = _layer(x, norm1_w[l], w_in[l], nsa_q_norm_w[l], nsa_k_norm_w[l], cmp_pe_k[l], cmp_pe_v[l],
                   w_cmp_k[l], w_cmp_v[l], rel_bias, ret_gn_w[l], w_out[l], norm2_w[l],
                   w_gate[l], w_up[l], w_down[l])
    return x
```

```python
import functools
import math

import numpy as np
import jax
import jax.numpy as jnp
from jax import lax
from jax.experimental import pallas as pl
from jax.experimental.pallas import tpu as pltpu

F32 = jnp.float32
BF16 = jnp.bfloat16

D_MODEL = 4096
SEQ = 2048
HEAD_DIM = 128
N_NSA_HEADS = 16
N_KV_GROUPS = 4
GROUP_SIZE = 4
N_BRANCH = 3
CMP_BLOCK = 32
CMP_STRIDE = 16
SEL_BLOCK = 64
TOP_N = 16
WINDOW = 512
RET_KEY_DIM = 256
RET_VAL_DIM = 256
N_RET_HEADS = 8
RET_CHUNK = 128
ROPE_BASE = 10000.0
NSA_WIDTH = 2048
RET_WIDTH = 2048
D_FF = 11008
NUM_BUCKETS = 32
MAX_DISTANCE = 128
EPS = 1e-6
NEG_INF = -1e30
FORCE_SCORE = 1e9
N_SEL = SEQ // SEL_BLOCK
N_CMP = (SEQ - CMP_BLOCK) // CMP_STRIDE + 1
SCALE = HEAD_DIM ** -0.5
LOG2E = math.log2(math.e)
EXP2_SCALE = SCALE * LOG2E

LANES = 128
VMEM_LIMIT = 56 * 1024 * 1024
NSA_VMEM_LIMIT = 60 * 1024 * 1024

TQ = 256
TK = 256
NQ = SEQ // TQ
RET_HEADS_PER_STEP = 2
V_ROWS = HEAD_DIM + 16
assert TQ == TK and WINDOW == 2 * TK and TK % SEL_BLOCK == 0 and N_SEL <= HEAD_DIM and TK >= MAX_DISTANCE

W_IN_NSA_COLS = NSA_WIDTH + N_BRANCH * 2 * N_KV_GROUPS * HEAD_DIM
W_IN_GATE_COLS = N_NSA_HEADS * N_BRANCH
COL_KV = NSA_WIDTH
COL_QR = 0
COL_KR = 2048
COL_VR = 4096
COL_GR = 6144


def _bucket_thresholds():
    rel = np.arange(0, 4 * MAX_DISTANCE)
    max_exact = NUM_BUCKETS // 2
    nf = np.maximum(rel, 1).astype(np.float64)
    large = max_exact + np.floor(np.log(nf / max_exact) / math.log(MAX_DISTANCE / max_exact)
                                 * (NUM_BUCKETS - max_exact)).astype(np.int64)
    large = np.minimum(large, NUM_BUCKETS - 1)
    b = np.where(rel < max_exact, rel, large)
    return [int(np.argmax(b >= k)) for k in range(NUM_BUCKETS)]


_THR = _bucket_thresholds()


def _dot(a, b):
    return jnp.dot(a, b, preferred_element_type=F32)


def _dot_nt(a, b):
    return lax.dot_general(a, b, (((1,), (1,)), ((), ())), preferred_element_type=F32)


def _rms_kernel(x_ref, w_ref, o_ref):
    x = x_ref[...]
    y = x * lax.rsqrt(jnp.mean(x * x, axis=-1, keepdims=True) + EPS)
    o_ref[...] = (y * w_ref[...]).astype(o_ref.dtype)


def _rmsnorm(x, w, tm=512):
    m, d = x.shape
    return pl.pallas_call(
        _rms_kernel,
        out_shape=jax.ShapeDtypeStruct((m, d), BF16),
        grid=(m // tm,),
        in_specs=[pl.BlockSpec((tm, d), lambda i: (i, 0)),
                  pl.BlockSpec((1, d), lambda i: (0, 0))],
        out_specs=pl.BlockSpec((tm, d), lambda i: (i, 0)),
        compiler_params=pltpu.CompilerParams(dimension_semantics=("arbitrary",),
                                             vmem_limit_bytes=VMEM_LIMIT),
    )(x, w.reshape(1, d))


def _cast_spec(grid, row0, n_rows, n_cols, n_slabs):
    slab = n_rows // n_slabs
    sub = 8
    n_steps = math.prod(grid)
    assert slab * n_slabs == n_rows and slab % 16 == 0 and n_cols % LANES == 0
    assert row0 % sub == 0 and n_slabs <= n_steps

    def slab_index(ids):
        step = ids[0]
        for dim, idx in zip(grid[1:], ids[1:]):
            step = step * dim + idx
        return step if n_slabs == n_steps else jnp.minimum(step, n_slabs - 1)

    in_spec = pl.BlockSpec((pl.Element(slab), pl.Element(n_cols)),
                           lambda *ids: ((row0 // sub + slab_index(ids) * (slab // sub)) * sub, 0))
    out_spec = pl.BlockSpec((slab, n_cols), lambda *ids: (slab_index(ids), 0))
    return in_spec, out_spec, jax.ShapeDtypeStruct((n_rows, n_cols), BF16)


def _cast_specs(casts, grid):
    specs = [_cast_spec(grid, row0, n_rows, arr.shape[1], n_slabs) for arr, row0, n_rows, n_slabs in casts]
    return [s[0] for s in specs], [s[1] for s in specs], [s[2] for s in specs]


def _cast_slabs(refs, n_casts):
    for src, dst in zip(refs[:n_casts], refs[len(refs) - n_casts:]):
        dst[...] = src[...].astype(dst.dtype)


def _inproj_kernel(a_ref, wt_ref, *rest, n_casts):
    o_ref = rest[n_casts]
    wbf_ref = rest[-1]

    @pl.when(pl.program_id(1) == 0)
    def _():
        wbf_ref[...] = wt_ref[...].astype(BF16)

    o_ref[...] = _dot_nt(a_ref[...], wbf_ref[...])
    _cast_slabs(rest[:n_casts] + rest[n_casts + 1:-1], n_casts)


def _in_projection(a, w_t, row0, n_cols, tn, tm=1024, casts=()):
    m, k = a.shape
    grid = (n_cols // tn, m // tm)
    c_in, c_out, c_shapes = _cast_specs(casts, grid)
    sub = 8
    assert row0 % sub == 0 and tn % sub == 0
    return pl.pallas_call(
        functools.partial(_inproj_kernel, n_casts=len(casts)),
        out_shape=[jax.ShapeDtypeStruct((m, n_cols), F32)] + c_shapes,
        grid=grid,
        in_specs=[pl.BlockSpec((tm, k), lambda j, i: (i, 0)),
                  pl.BlockSpec((pl.Element(tn), pl.Element(k)),
                               lambda j, i: ((row0 // sub + j * (tn // sub)) * sub, 0),
                               pipeline_mode=pl.Buffered(1))] + c_in,
        out_specs=[pl.BlockSpec((tm, tn), lambda j, i: (i, j))] + c_out,
        scratch_shapes=[pltpu.VMEM((tn, k), BF16)],
        compiler_params=pltpu.CompilerParams(dimension_semantics=("arbitrary", "arbitrary"),
                                             vmem_limit_bytes=VMEM_LIMIT),
    )(a, w_t, *[c[0] for c in casts])


def _outproj_kernel(a1_ref, a2_ref, w1_ref, w2_ref, x_ref, *rest, n_casts):
    o_ref = rest[n_casts]
    acc = _dot(a1_ref[...], w1_ref[...]) + _dot(a2_ref[...], w2_ref[...])
    o_ref[...] = x_ref[...] + acc
    _cast_slabs(rest[:n_casts] + rest[n_casts + 1:], n_casts)


def _out_projection(o_nsa, o_ret, w_out, x, tm=1024, tn=512, casts=()):
    m = x.shape[0]
    kh = NSA_WIDTH
    grid = (m // tm, D_MODEL // tn)
    c_in, c_out, c_shapes = _cast_specs(casts, grid)
    return pl.pallas_call(
        functools.partial(_outproj_kernel, n_casts=len(casts)),
        out_shape=[jax.ShapeDtypeStruct((m, D_MODEL), F32)] + c_shapes,
        grid=grid,
        in_specs=[pl.BlockSpec((tm, kh), lambda i, j: (i, 0)),
                  pl.BlockSpec((tm, kh), lambda i, j: (i, 0)),
                  pl.BlockSpec((kh, tn), lambda i, j: (0, j)),
                  pl.BlockSpec((kh, tn), lambda i, j: (1, j)),
                  pl.BlockSpec((tm, tn), lambda i, j: (i, j))] + c_in,
        out_specs=[pl.BlockSpec((tm, tn), lambda i, j: (i, j))] + c_out,
        compiler_params=pltpu.CompilerParams(dimension_semantics=("arbitrary", "arbitrary"),
                                             vmem_limit_bytes=VMEM_LIMIT),
    )(o_nsa, o_ret, w_out, w_out, x, *[c[0] for c in casts])


def _ffn_up_kernel(h_ref, wg_ref, wu_ref, *rest, n_casts):
    o_ref = rest[n_casts]
    h = h_ref[...]
    g = _dot(h, wg_ref[...])
    u = _dot(h, wu_ref[...])
    o_ref[...] = (g * jax.nn.sigmoid(g) * u).astype(o_ref.dtype)
    _cast_slabs(rest[:n_casts] + rest[n_casts + 1:], n_casts)


def _ffn_up(h, w_gate, w_up, tm=1024, tn=256, casts=()):
    m, k = h.shape
    n = w_gate.shape[1]
    grid = (m // tm, n // tn)
    c_in, c_out, c_shapes = _cast_specs(casts, grid)
    return pl.pallas_call(
        functools.partial(_ffn_up_kernel, n_casts=len(casts)),
        out_shape=[jax.ShapeDtypeStruct((m, n), BF16)] + c_shapes,
        grid=grid,
        in_specs=[pl.BlockSpec((tm, k), lambda i, j: (i, 0)),
                  pl.BlockSpec((k, tn), lambda i, j: (0, j)),
                  pl.BlockSpec((k, tn), lambda i, j: (0, j))] + c_in,
        out_specs=[pl.BlockSpec((tm, tn), lambda i, j: (i, j))] + c_out,
        compiler_params=pltpu.CompilerParams(dimension_semantics=("arbitrary", "arbitrary"),
                                             vmem_limit_bytes=VMEM_LIMIT),
    )(h, w_gate, w_up, *[c[0] for c in casts])


def _ffn_down_kernel(a_ref, b_ref, r_ref, o_ref):
    o_ref[...] = r_ref[...] + _dot(a_ref[...], b_ref[...])


def _ffn_down(a, w_down, x, tm=1024, tn=512, tk=5504):
    m, k = a.shape
    n = w_down.shape[1]
    res = x
    for kk in range(k // tk):
        res = pl.pallas_call(
            _ffn_down_kernel,
            out_shape=jax.ShapeDtypeStruct((m, n), F32),
            grid=(m // tm, n // tn),
            in_specs=[pl.BlockSpec((tm, tk), lambda i, j, kk=kk: (i, kk)),
                      pl.BlockSpec((tk, tn), lambda i, j, kk=kk: (kk, j)),
                      pl.BlockSpec((tm, tn), lambda i, j: (i, j))],
            out_specs=pl.BlockSpec((tm, tn), lambda i, j: (i, j)),
            compiler_params=pltpu.CompilerParams(dimension_semantics=("arbitrary", "arbitrary"),
                                                 vmem_limit_bytes=VMEM_LIMIT),
        )(a, w_down, res)
    return res


def _bias_value(rel, relb_ref, h):
    val = jnp.full(rel.shape, relb_ref[0, h], F32)
    for k in range(1, NUM_BUCKETS):
        val = jnp.where(rel >= _THR[k], relb_ref[k, h], val)
    return val


def _bias_tables_kernel(relb_ref, biasc_ref, dm_ref):
    h = pl.program_id(0)
    sub = 8
    j = lax.broadcasted_iota(jnp.int32, (sub, 2 * SEQ), 1)
    row_c = _bias_value(j - SEQ - (CMP_BLOCK - 1), relb_ref, h)[0:1]
    toe_c = pltpu.roll(jnp.broadcast_to(row_c, (LANES, 2 * SEQ)), 0, 1,
                       stride=CMP_STRIDE, stride_axis=0)
    for qt in range(NQ):
        biasc_ref[0, qt] = toe_c[:, SEQ + qt * TQ:SEQ + (qt + 1) * TQ]
    far = relb_ref[NUM_BUCKETS - 1, h]
    j2 = lax.broadcasted_iota(jnp.int32, (sub, 2 * TK), 1)
    b = lax.broadcasted_iota(jnp.int32, (TK, TQ), 0)
    a = lax.broadcasted_iota(jnp.int32, (TK, TQ), 1)
    for kind in range(2):
        row = ((_bias_value(j2 + (kind - 1) * TK, relb_ref, h) - far) * LOG2E)[0:1]
        toe = pltpu.roll(jnp.broadcast_to(row, (TK, 2 * TK)), 0, 1, stride=1, stride_axis=0)
        tile = toe[:, TK:]
        dm_ref[0, kind] = tile + jnp.where(a >= b, 0.0, NEG_INF) if kind == 0 else tile
    dm_ref[0, 2] = jnp.where(b > a, 0.0, NEG_INF)


def _bias_tables(rel_bias):
    return pl.pallas_call(
        _bias_tables_kernel,
        out_shape=(jax.ShapeDtypeStruct((N_KV_GROUPS, NQ, LANES, GROUP_SIZE * TQ), F32),
                   jax.ShapeDtypeStruct((N_KV_GROUPS, 3, TK, GROUP_SIZE * TQ), F32)),
        grid=(N_NSA_HEADS,),
        in_specs=[pl.BlockSpec(memory_space=pltpu.SMEM)],
        out_specs=(pl.BlockSpec((1, NQ, LANES, TQ), lambda h: (h // GROUP_SIZE, 0, 0, h % GROUP_SIZE)),
                   pl.BlockSpec((1, 3, TK, TQ), lambda h: (h // GROUP_SIZE, 0, 0, h % GROUP_SIZE))),
        compiler_params=pltpu.CompilerParams(dimension_semantics=("arbitrary",),
                                             vmem_limit_bytes=VMEM_LIMIT),
    )(rel_bias)


def _row_rms(x, w):
    return x * lax.rsqrt(jnp.mean(x * x, axis=-1, keepdims=True) + EPS) * w


N_NSA_INPUTS = 17
N_NSA_SCRATCH = 7


def _nsa_kernel(*refs, n_casts):
    (q_ref, k0_ref, v0_ref, k1_ref, v1_ref, k2_ref, v2_ref, gate_ref,
     qnw_ref, knw_ref, pek_ref, pev_ref, wck_ref, wcv_ref, biasc_ref, dm_ref, c2st_ref) = refs[:N_NSA_INPUTS]
    side_a_ref, side_bt_ref = refs[N_NSA_INPUTS:N_NSA_INPUTS + 2]
    n_in = N_NSA_INPUTS + 2 + n_casts
    cast_in = refs[N_NSA_INPUTS + 2:n_in]
    o_ref, side_o_ref = refs[n_in:n_in + 2]
    cast_out = refs[n_in + 2:n_in + 2 + n_casts]
    kc_s, vct_s, kaug_s, kw_s, vst_s, vwt_s, gt_s = refs[len(refs) - N_NSA_SCRATCH:]
    qt = pl.program_id(2)
    cols_all = GROUP_SIZE * TQ
    side_half = side_bt_ref.shape[0] // 2

    @pl.when(qt == 0)
    def _prologue():
        kaug_s[:, :HEAD_DIM] = (_row_rms(k1_ref[...], knw_ref[1:2, :]) * EXP2_SCALE).astype(BF16)
        jblk = lax.broadcasted_iota(jnp.int32, (SEQ, LANES), 0) // SEL_BLOCK
        lane = lax.broadcasted_iota(jnp.int32, (SEQ, LANES), 1)
        kaug_s[:, HEAD_DIM:] = jnp.where(jblk == lane, 1.0, 0.0).astype(BF16)
        kw_s[...] = (_row_rms(k2_ref[...], knw_ref[2:3, :]) * EXP2_SCALE).astype(BF16)
        ones_row = jnp.where(lax.broadcasted_iota(jnp.int32, (V_ROWS - HEAD_DIM, TK), 0) == 0,
                             1.0, 0.0).astype(BF16)
        for t in range(SEQ // TK):
            vst_s[t, :HEAD_DIM, :] = v1_ref[t * TK:(t + 1) * TK, :].T.astype(BF16)
            vst_s[t, HEAD_DIM:, :] = ones_row
            vwt_s[t, :HEAD_DIM, :] = v2_ref[t * TK:(t + 1) * TK, :].T.astype(BF16)
            vwt_s[t, HEAD_DIM:, :] = ones_row

        def compress(x_ref, pe_ref, w_ref):
            acc_lo = jnp.zeros((LANES, HEAD_DIM), F32)
            acc_hi = jnp.zeros((LANES, HEAD_DIM), F32)
            for j in range(CMP_STRIDE):
                y = x_ref[pl.ds(j, LANES, stride=CMP_STRIDE), :]
                acc_lo += _dot((y + pe_ref[j:j + 1, :]).astype(BF16),
                               w_ref[j * HEAD_DIM:(j + 1) * HEAD_DIM, :])
                jh = CMP_STRIDE + j
                acc_hi += _dot((y + pe_ref[jh:jh + 1, :]).astype(BF16),
                               w_ref[jh * HEAD_DIM:(jh + 1) * HEAD_DIM, :])
            return acc_lo + pltpu.roll(acc_hi, LANES - 1, 0)

        kc_s[...] = _row_rms(compress(k0_ref, pek_ref, wck_ref), knw_ref[0:1, :]).astype(BF16)
        vct_s[...] = compress(v0_ref, pev_ref, wcv_ref).T.astype(BF16)

    side_o_ref[:, :side_half] = _dot_nt(side_a_ref[...], side_bt_ref[:side_half, :])
    _cast_slabs(cast_in + cast_out, n_casts)

    t0 = qt * TQ
    q = q_ref[...]
    qts = []
    for r in range(GROUP_SIZE):
        qts.append(_row_rms(q[:, r * HEAD_DIM:(r + 1) * HEAD_DIM], qnw_ref[...]).T)
    qt_stack = jnp.concatenate(qts, axis=1).astype(BF16)

    m0 = jnp.full((1, cols_all), NEG_INF, F32)
    a0 = jnp.zeros((V_ROWS, cols_all), F32)

    def online(carry, tiles):
        m, acc = carry
        m_new = m
        for u, _ in tiles:
            m_new = jnp.maximum(m_new, jnp.max(u, axis=0, keepdims=True))
        acc = jnp.exp2(m - m_new) * acc
        for u, vt in tiles:
            acc = acc + _dot(vt, jnp.exp2(u - m_new).astype(BF16))
        return m_new, acc

    def online_independent(carry, tiles):
        m, acc = carry
        parts = []
        for u, vt in tiles:
            mt = jnp.max(u, axis=0, keepdims=True)
            parts.append((mt, _dot(vt, jnp.exp2(u - mt).astype(BF16))))
        m_new = m
        for mt, _ in parts:
            m_new = jnp.maximum(m_new, mt)
        acc = jnp.exp2(m - m_new) * acc
        for mt, pv in parts:
            acc = acc + jnp.exp2(mt - m_new) * pv
        return m_new, acc

    def normalised(acc):
        return acc[:HEAD_DIM] * (1.0 / acc[HEAD_DIM:HEAD_DIM + 1])

    def key_rows(kt):
        return pl.ds(pl.multiple_of(kt * TK, TK), TK)

    sc = _dot(kc_s[...], qt_stack) * SCALE + biasc_ref[0, 0]
    tpos = t0 + (lax.broadcasted_iota(jnp.int32, (LANES, cols_all), 1) & (TQ - 1))
    nrow = lax.broadcasted_iota(jnp.int32, (LANES, cols_all), 0)
    maskc = tpos >= CMP_STRIDE * nrow + (CMP_BLOCK - 1)
    scm = jnp.where(maskc, sc, NEG_INF)
    mc = jnp.max(scm, axis=0, keepdims=True)
    ec = jnp.where(maskc, jnp.exp(scm - mc), 0.0)
    lc = jnp.sum(ec, axis=0, keepdims=True)
    pc = ec * (1.0 / jnp.where(lc > 0.0, lc, 1.0))
    o_c = _dot(vct_s[...], pc.astype(BF16))

    psum = pc[:, 0:TQ] + pc[:, TQ:2 * TQ] + pc[:, 2 * TQ:3 * TQ] + pc[:, 3 * TQ:4 * TQ]
    p_hi = psum.astype(BF16)
    p_lo = (psum - p_hi.astype(F32)).astype(BF16)
    imp = (_dot(c2st_ref[...], p_hi) + _dot(c2st_ref[...], p_lo))[0:N_SEL]

    tiles = []
    for dist, kind in ((2, 2), (1, 1), (0, 0)):
        kt = jnp.maximum(qt - dist, 0)
        u = _dot(kw_s[key_rows(kt), :], qt_stack) + dm_ref[0, kind]
        if dist > 0:
            u = u + jnp.where(qt >= dist, 0.0, NEG_INF)
        tiles.append((u, vwt_s[kt]))
    _, acc_w = online_independent((m0, a0), tiles)
    o_w = normalised(acc_w)

    tq_pos = t0 + lax.broadcasted_iota(jnp.int32, (N_SEL, TQ), 1)
    sidx = lax.broadcasted_iota(jnp.int32, (N_SEL, TQ), 0)
    cur = tq_pos >> (SEL_BLOCK.bit_length() - 1)
    score = jnp.where(sidx <= cur, imp, NEG_INF)
    for forced_blk in (cur - 1, cur, jnp.zeros_like(cur)):
        score = jnp.where(sidx == forced_blk, FORCE_SCORE, score)
    rank = jnp.zeros((N_SEL, TQ), F32)
    for sp in range(N_SEL):
        row = score[sp:sp + 1, :]
        tie = jnp.where(sidx > sp, 1.0, 0.0)
        rank = rank + jnp.where(row > score, 1.0, jnp.where(row == score, tie, 0.0))
    selb = jnp.where(rank < TOP_N, 0.0, NEG_INF)
    selb = jnp.concatenate([selb, jnp.zeros((HEAD_DIM - N_SEL, TQ), F32)], axis=0).astype(BF16)
    qt_aug = jnp.concatenate([qt_stack, jnp.concatenate([selb] * GROUP_SIZE, axis=1)], axis=0)

    def sel_tile(kt):
        return _dot(kaug_s[key_rows(kt), :], qt_aug), vst_s[kt]

    n_far = jnp.maximum(qt - 1, 0)
    carry = lax.fori_loop(0, n_far >> 1,
                          lambda i, c: online_independent(c, [sel_tile(2 * i), sel_tile(2 * i + 1)]),
                          (m0, a0))
    carry = lax.fori_loop(0, n_far & 1, lambda i, c: online(c, [sel_tile(n_far - 1)]), carry)
    kt1 = jnp.maximum(qt - 1, 0)
    u1, vt1 = sel_tile(kt1)
    u0, vt0 = sel_tile(qt)
    _, acc_s = online_independent(
        carry, [(u1 + dm_ref[0, 1] + jnp.where(qt >= 1, 0.0, NEG_INF), vt1), (u0 + dm_ref[0, 0], vt0)])
    o_s = normalised(acc_s)

    side_o_ref[:, side_half:] = _dot_nt(side_a_ref[...], side_bt_ref[side_half:, :])

    gt_s[...] = jax.nn.sigmoid(gate_ref[...]).T
    head0 = pl.program_id(1) * GROUP_SIZE
    for r in range(GROUP_SIZE):
        cs = slice(r * TQ, (r + 1) * TQ)
        c = (head0 + r) * N_BRANCH
        o = (gt_s[pl.ds(c, 1), :] * o_c[:, cs] + gt_s[pl.ds(c + 1, 1), :] * o_s[:, cs]
             + gt_s[pl.ds(c + 2, 1), :] * o_w[:, cs])
        o_ref[:, r * HEAD_DIM:(r + 1) * HEAD_DIM] = o.T.astype(o_ref.dtype)


def _cmp_to_sel_t():
    ss = np.arange(LANES)[:, None] * SEL_BLOCK
    cs = np.arange(LANES)[None, :] * CMP_STRIDE
    ov = np.clip(np.minimum(cs + CMP_BLOCK, ss + SEL_BLOCK) - np.maximum(cs, ss), 0, None)
    m = ov.astype(np.float32) / np.float32(CMP_BLOCK)
    m[N_SEL:, :] = 0.0
    m[:, N_CMP:] = 0.0
    return jnp.asarray(m, BF16)


def _nsa_attention(proj, gate_logits, batch, q_norm_w, k_norm_w, pe_k, pe_v, w_cmp_k, w_cmp_v, biasc, dm,
                   side_a, side_bt, side_tm=1024, side_tn=512, casts=()):
    kvb = COL_KV // LANES

    def kv_spec(branch, which):
        base = kvb + (branch * 2 + which) * N_KV_GROUPS
        return pl.BlockSpec((SEQ, LANES), lambda b, g, qt: (b, base + g))

    full = lambda shape: pl.BlockSpec(shape, lambda b, g, qt: (0,) * len(shape))
    grid = (batch, N_KV_GROUPS, NQ)
    c_in, c_out, c_shapes = _cast_specs(casts, grid)
    side_m, side_k = side_a.shape
    side_n = side_bt.shape[0]
    n_j = side_n // side_tn
    assert (side_m // side_tm) * n_j == math.prod(grid)
    step = lambda b, g, qt: (b * N_KV_GROUPS + g) * NQ + qt
    side_in = [pl.BlockSpec((side_tm, side_k), lambda b, g, qt: (step(b, g, qt) // n_j, 0),
                            pipeline_mode=pl.Buffered(1)),
               pl.BlockSpec((side_tn, side_k), lambda b, g, qt: (step(b, g, qt) % n_j, 0))]
    side_out = pl.BlockSpec((side_tm, side_tn), lambda b, g, qt: (step(b, g, qt) // n_j, step(b, g, qt) % n_j))
    return pl.pallas_call(
        functools.partial(_nsa_kernel, n_casts=len(casts)),
        out_shape=[jax.ShapeDtypeStruct((batch * SEQ, NSA_WIDTH), BF16),
                   jax.ShapeDtypeStruct((side_m, side_n), F32)] + c_shapes,
        grid=grid,
        in_specs=[pl.BlockSpec((TQ, GROUP_SIZE * HEAD_DIM), lambda b, g, qt: (b * NQ + qt, g)),
                  kv_spec(0, 0), kv_spec(0, 1), kv_spec(1, 0), kv_spec(1, 1),
                  kv_spec(2, 0), kv_spec(2, 1),
                  pl.BlockSpec((TQ, LANES), lambda b, g, qt: (b * NQ + qt, 0)),
                  full((1, HEAD_DIM)), full((N_BRANCH, HEAD_DIM)),
                  full((CMP_BLOCK, HEAD_DIM)), full((CMP_BLOCK, HEAD_DIM)),
                  full((CMP_BLOCK * HEAD_DIM, HEAD_DIM)), full((CMP_BLOCK * HEAD_DIM, HEAD_DIM)),
                  pl.BlockSpec((1, 1, LANES, GROUP_SIZE * TQ), lambda b, g, qt: (g, qt, 0, 0)),
                  pl.BlockSpec((1, 3, TK, GROUP_SIZE * TQ), lambda b, g, qt: (g, 0, 0, 0),
                               pipeline_mode=pl.Buffered(1)),
                  full((LANES, LANES))] + side_in + c_in,
        out_specs=[pl.BlockSpec((TQ, GROUP_SIZE * HEAD_DIM), lambda b, g, qt: (b * NQ + qt, g)),
                   side_out] + c_out,
        scratch_shapes=[pltpu.VMEM((LANES, HEAD_DIM), BF16),
                        pltpu.VMEM((HEAD_DIM, LANES), BF16),
                        pltpu.VMEM((SEQ, 2 * HEAD_DIM), BF16),
                        pltpu.VMEM((SEQ, HEAD_DIM), BF16),
                        pltpu.VMEM((SEQ // TK, V_ROWS, TK), BF16),
                        pltpu.VMEM((SEQ // TK, V_ROWS, TK), BF16),
                        pltpu.VMEM((LANES, TQ), F32)],
        compiler_params=pltpu.CompilerParams(
            dimension_semantics=("arbitrary", "arbitrary", "arbitrary"),
            vmem_limit_bytes=NSA_VMEM_LIMIT),
    )(proj, proj, proj, proj, proj, proj, proj, gate_logits,
      q_norm_w.reshape(1, HEAD_DIM), k_norm_w, pe_k, pe_v,
      w_cmp_k.astype(BF16), w_cmp_v.astype(BF16), biasc, dm, _cmp_to_sel_t(),
      side_a, side_bt, *[c[0] for c in casts])


def _ret_kernel(lg_ref, q_ref, k_ref, v_ref, g_ref, cos_ref, sin_ref, gnw_ref, *rest, n_casts):
    o_ref = rest[n_casts]
    state_ref = rest[-1]
    _cast_slabs(rest[:n_casts] + rest[n_casts + 1:-1], n_casts)
    c_len = RET_CHUNK
    ii = lax.broadcasted_iota(jnp.int32, (c_len, c_len), 0)
    jj = lax.broadcasted_iota(jnp.int32, (c_len, c_len), 1)
    diff = (ii - jj).astype(F32)
    icol = lax.broadcasted_iota(jnp.int32, (c_len, 1), 0).astype(F32)
    half = RET_KEY_DIM // 2
    state_ref[...] = jnp.zeros_like(state_ref)
    tables = []
    for hh in range(RET_HEADS_PER_STEP):
        lg = lg_ref[pl.program_id(1) * RET_HEADS_PER_STEP + hh]
        tables.append((
            jnp.where(diff >= 0.0, jnp.exp(jnp.maximum(diff, 0.0) * lg), 0.0),
            jnp.exp((icol + 1.0) * lg),
            jnp.exp((c_len - 1.0 - icol) * lg),
            jnp.exp(jnp.full((1, RET_VAL_DIM), float(c_len), F32) * lg)))

    def body(c, _):
        rows = pl.ds(pl.multiple_of(c * c_len, c_len), c_len)
        cos = cos_ref[rows, :]
        sin = sin_ref[rows, :]

        def rot(x):
            x1, x2 = x[:, :half], x[:, half:]
            return jnp.concatenate([x1 * cos - x2 * sin, x2 * cos + x1 * sin], axis=-1)

        for hh, (decay_in, xi, zeta, chunk_decay) in enumerate(tables):
            kcols = slice(hh * RET_KEY_DIM, (hh + 1) * RET_KEY_DIM)
            vcols = slice(hh * RET_VAL_DIM, (hh + 1) * RET_VAL_DIM)
            qr = rot(q_ref[rows, kcols])
            kr = rot(k_ref[rows, kcols]) * (RET_KEY_DIM ** -0.5)
            vb = v_ref[rows, vcols].astype(BF16)
            scores = _dot_nt(qr.astype(BF16), kr.astype(BF16)) * decay_in
            inner = _dot(scores.astype(BF16), vb)
            st = state_ref[hh]
            cross = _dot((qr * xi).astype(BF16), st.astype(BF16))
            kv = _dot((kr * zeta).T.astype(BF16), vb)
            state_ref[hh] = chunk_decay * st + kv
            y = inner + cross
            y = y * lax.rsqrt(jnp.mean(y * y, axis=-1, keepdims=True) + EPS) * gnw_ref[:, vcols]
            g = g_ref[rows, vcols]
            o_ref[rows, vcols] = (g * jax.nn.sigmoid(g) * y).astype(o_ref.dtype)
        return 0

    lax.fori_loop(0, SEQ // c_len, body, 0, unroll=4)


def _retention(proj, batch, gn_w, casts=()):
    pos = np.arange(SEQ, dtype=np.float64)
    half = RET_KEY_DIM // 2
    inv = np.exp(-np.linspace(0.0, 1.0, half) * math.log(ROPE_BASE)).astype(np.float32)
    ang = pos.astype(np.float32)[:, None] * inv[None, :]
    cos = jnp.asarray(np.cos(ang.astype(np.float64)), F32)
    sin = jnp.asarray(np.sin(ang.astype(np.float64)), F32)
    log_gamma = jnp.asarray(np.log(1.0 - 2.0 ** (-5.0 - np.arange(N_RET_HEADS))), F32)
    w = RET_KEY_DIM * RET_HEADS_PER_STEP

    def col_spec(col0):
        return pl.BlockSpec((SEQ, w), lambda b, h: (b, col0 // w + h))

    grid = (batch, N_RET_HEADS // RET_HEADS_PER_STEP)
    c_in, c_out, c_shapes = _cast_specs(casts, grid)
    return pl.pallas_call(
        functools.partial(_ret_kernel, n_casts=len(casts)),
        out_shape=[jax.ShapeDtypeStruct((batch * SEQ, RET_WIDTH), BF16)] + c_shapes,
        grid=grid,
        in_specs=[pl.BlockSpec(memory_space=pltpu.SMEM),
                  col_spec(COL_QR), col_spec(COL_KR), col_spec(COL_VR), col_spec(COL_GR),
                  pl.BlockSpec((SEQ, half), lambda b, h: (0, 0)),
                  pl.BlockSpec((SEQ, half), lambda b, h: (0, 0)),
                  pl.BlockSpec((1, w), lambda b, h: (0, h))] + c_in,
        out_specs=[pl.BlockSpec((SEQ, w), lambda b, h: (b, h))] + c_out,
        scratch_shapes=[pltpu.VMEM((RET_HEADS_PER_STEP, RET_KEY_DIM, RET_VAL_DIM), F32)],
        compiler_params=pltpu.CompilerParams(dimension_semantics=("arbitrary", "arbitrary"),
                                             vmem_limit_bytes=VMEM_LIMIT),
    )(log_gamma, proj, proj, proj, proj, cos, sin, gn_w.reshape(1, RET_WIDTH), *[c[0] for c in casts])


def _layer(x, norm1_w, w_in, nsa_q_norm_w, nsa_k_norm_w, cmp_pe_k, cmp_pe_v, w_cmp_k, w_cmp_v,
           rel_bias, ret_gn_w, w_out, norm2_w, w_gate, w_up, w_down):
    batch = x.shape[0]
    rows = batch * SEQ
    xf = x.reshape(rows, D_MODEL)
    h = _rmsnorm(xf, norm1_w)
    w_in_t = w_in.T
    gate0 = W_IN_NSA_COLS
    gate1 = gate0 + W_IN_GATE_COLS
    ret_cols = 4 * RET_WIDTH
    nsa_tn, nsa_tm = 1024, 512
    nsa_steps = (W_IN_NSA_COLS // nsa_tn) * (rows // nsa_tm)
    ret_slabs = 1 << (nsa_steps.bit_length() - 1)
    proj_nsa, w_ret_t = _in_projection(h, w_in_t, 0, W_IN_NSA_COLS, nsa_tn, tm=nsa_tm,
                                       casts=((w_in_t, gate1, ret_cols, ret_slabs),))
    gate_logits, = _in_projection(h, w_in_t, gate0, LANES, LANES)
    biasc, dm = _bias_tables(rel_bias)
    n_steps = batch * N_KV_GROUPS * NQ
    o_nsa, proj_ret, w_gate_b, w_up_b, w_out_b = _nsa_attention(
        proj_nsa, gate_logits, batch, nsa_q_norm_w, nsa_k_norm_w, cmp_pe_k, cmp_pe_v,
        w_cmp_k, w_cmp_v, biasc, dm, h, w_ret_t,
        casts=((w_gate, 0, D_MODEL, n_steps), (w_up, 0, D_MODEL, n_steps), (w_out, 0, D_MODEL, n_steps)))
    o_ret, = _retention(proj_ret, batch, ret_gn_w)
    x1, = _out_projection(o_nsa, o_ret, w_out_b, xf, tn=1024)
    hf = _rmsnorm(x1, norm2_w)
    up_tm, up_tn = 2048, 256
    up_steps = (rows // up_tm) * (D_FF // up_tn)
    mid, w_down_b = _ffn_up(hf, w_gate_b, w_up_b, tm=up_tm, tn=up_tn, casts=((w_down, 0, D_FF, up_steps),))
    out = _ffn_down(mid, w_down_b, x1)
    return out.reshape(batch, SEQ, D_MODEL)


def kernel(x, norm1_w, w_in, nsa_q_norm_w, nsa_k_norm_w, cmp_pe_k, cmp_pe_v, w_cmp_k, w_cmp_v,
           rel_bias, ret_gn_w, w_out, norm2_w, w_gate, w_up, w_down):
    for l in range(norm1_w.shape[0]):
        x = _layer(x, norm1_w[l], w_in[l], nsa_q_norm_w[l], nsa_k_norm_w[l], cmp_pe_k[l], cmp_pe_v[l],
                   w_cmp_k[l], w_cmp_v[l], rel_bias, ret_gn_w[l], w_out[l], norm2_w[l],
                   w_gate[l], w_up[l], w_down[l])
    return x
```

```python
import functools
import math

import numpy as np
import jax
import jax.numpy as jnp
from jax import lax
from jax.experimental import pallas as pl
from jax.experimental.pallas import tpu as pltpu

F32 = jnp.float32
BF16 = jnp.bfloat16

D_MODEL = 4096
SEQ = 2048
HEAD_DIM = 128
N_NSA_HEADS = 16
N_KV_GROUPS = 4
GROUP_SIZE = 4
N_BRANCH = 3
CMP_BLOCK = 32
CMP_STRIDE = 16
SEL_BLOCK = 64
TOP_N = 16
WINDOW = 512
RET_KEY_DIM = 256
RET_VAL_DIM = 256
N_RET_HEADS = 8
RET_CHUNK = 128
ROPE_BASE = 10000.0
NSA_WIDTH = 2048
RET_WIDTH = 2048
D_FF = 11008
NUM_BUCKETS = 32
MAX_DISTANCE = 128
EPS = 1e-6
NEG_INF = -1e30
FORCE_SCORE = 1e9
N_SEL = SEQ // SEL_BLOCK
N_CMP = (SEQ - CMP_BLOCK) // CMP_STRIDE + 1
SCALE = HEAD_DIM ** -0.5
LOG2E = math.log2(math.e)
EXP2_SCALE = SCALE * LOG2E

LANES = 128
VMEM_LIMIT = 56 * 1024 * 1024
NSA_VMEM_LIMIT = 60 * 1024 * 1024

TQ = 256
TK = 256
NQ = SEQ // TQ
RET_HEADS_PER_STEP = 2
FFN_UP_SUBBLOCKS = 4
V_ROWS = HEAD_DIM + 16
assert TQ == TK and WINDOW == 2 * TK and TK % SEL_BLOCK == 0 and N_SEL <= HEAD_DIM and TK >= MAX_DISTANCE

W_IN_NSA_COLS = NSA_WIDTH + N_BRANCH * 2 * N_KV_GROUPS * HEAD_DIM
W_IN_GATE_COLS = N_NSA_HEADS * N_BRANCH
COL_KV = NSA_WIDTH
COL_QR = 0
COL_KR = 2048
COL_VR = 4096
COL_GR = 6144


def _bucket_thresholds():
    rel = np.arange(0, 4 * MAX_DISTANCE)
    max_exact = NUM_BUCKETS // 2
    nf = np.maximum(rel, 1).astype(np.float64)
    large = max_exact + np.floor(np.log(nf / max_exact) / math.log(MAX_DISTANCE / max_exact)
                                 * (NUM_BUCKETS - max_exact)).astype(np.int64)
    large = np.minimum(large, NUM_BUCKETS - 1)
    b = np.where(rel < max_exact, rel, large)
    return [int(np.argmax(b >= k)) for k in range(NUM_BUCKETS)]


_THR = _bucket_thresholds()


def _dot(a, b):
    return jnp.dot(a, b, preferred_element_type=F32)


def _dot_nt(a, b):
    return lax.dot_general(a, b, (((1,), (1,)), ((), ())), preferred_element_type=F32)


def _rms_kernel(x_ref, w_ref, o_ref):
    x = x_ref[...]
    y = x * lax.rsqrt(jnp.mean(x * x, axis=-1, keepdims=True) + EPS)
    o_ref[...] = (y * w_ref[...]).astype(o_ref.dtype)


def _rmsnorm(x, w, tm=512):
    m, d = x.shape
    return pl.pallas_call(
        _rms_kernel,
        out_shape=jax.ShapeDtypeStruct((m, d), BF16),
        grid=(m // tm,),
        in_specs=[pl.BlockSpec((tm, d), lambda i: (i, 0)),
                  pl.BlockSpec((1, d), lambda i: (0, 0))],
        out_specs=pl.BlockSpec((tm, d), lambda i: (i, 0)),
        compiler_params=pltpu.CompilerParams(dimension_semantics=("arbitrary",),
                                             vmem_limit_bytes=VMEM_LIMIT),
    )(x, w.reshape(1, d))


def _cast_spec(grid, row0, n_rows, n_cols, n_slabs):
    slab = n_rows // n_slabs
    sub = 8
    n_steps = math.prod(grid)
    assert slab * n_slabs == n_rows and slab % 16 == 0 and n_cols % LANES == 0
    assert row0 % sub == 0 and n_slabs <= n_steps

    def slab_index(ids):
        step = ids[0]
        for dim, idx in zip(grid[1:], ids[1:]):
            step = step * dim + idx
        return step if n_slabs == n_steps else jnp.minimum(step, n_slabs - 1)

    in_spec = pl.BlockSpec((pl.Element(slab), pl.Element(n_cols)),
                           lambda *ids: ((row0 // sub + slab_index(ids) * (slab // sub)) * sub, 0))
    out_spec = pl.BlockSpec((slab, n_cols), lambda *ids: (slab_index(ids), 0))
    return in_spec, out_spec, jax.ShapeDtypeStruct((n_rows, n_cols), BF16)


def _cast_specs(casts, grid):
    specs = [_cast_spec(grid, row0, n_rows, arr.shape[1], n_slabs) for arr, row0, n_rows, n_slabs in casts]
    return [s[0] for s in specs], [s[1] for s in specs], [s[2] for s in specs]


def _cast_slabs(refs, n_casts):
    for src, dst in zip(refs[:n_casts], refs[len(refs) - n_casts:]):
        dst[...] = src[...].astype(dst.dtype)


def _inproj_kernel(a_ref, wt_ref, *rest, n_casts):
    o_ref = rest[n_casts]
    wbf_ref = rest[-1]

    @pl.when(pl.program_id(1) == 0)
    def _():
        wbf_ref[...] = wt_ref[...].astype(BF16)

    o_ref[...] = _dot_nt(a_ref[...], wbf_ref[...])
    _cast_slabs(rest[:n_casts] + rest[n_casts + 1:-1], n_casts)


def _in_projection(a, w_t, row0, n_cols, tn, tm=1024, casts=()):
    m, k = a.shape
    grid = (n_cols // tn, m // tm)
    c_in, c_out, c_shapes = _cast_specs(casts, grid)
    sub = 8
    assert row0 % sub == 0 and tn % sub == 0
    return pl.pallas_call(
        functools.partial(_inproj_kernel, n_casts=len(casts)),
        out_shape=[jax.ShapeDtypeStruct((m, n_cols), F32)] + c_shapes,
        grid=grid,
        in_specs=[pl.BlockSpec((tm, k), lambda j, i: (i, 0)),
                  pl.BlockSpec((pl.Element(tn), pl.Element(k)),
                               lambda j, i: ((row0 // sub + j * (tn // sub)) * sub, 0),
                               pipeline_mode=pl.Buffered(1))] + c_in,
        out_specs=[pl.BlockSpec((tm, tn), lambda j, i: (i, j))] + c_out,
        scratch_shapes=[pltpu.VMEM((tn, k), BF16)],
        compiler_params=pltpu.CompilerParams(dimension_semantics=("arbitrary", "arbitrary"),
                                             vmem_limit_bytes=VMEM_LIMIT),
    )(a, w_t, *[c[0] for c in casts])


def _outproj_kernel(a1_ref, a2_ref, w1_ref, w2_ref, x_ref, *rest, n_casts):
    o_ref = rest[n_casts]
    acc = _dot(a1_ref[...], w1_ref[...]) + _dot(a2_ref[...], w2_ref[...])
    o_ref[...] = x_ref[...] + acc
    _cast_slabs(rest[:n_casts] + rest[n_casts + 1:], n_casts)


def _out_projection(o_nsa, o_ret, w_out, x, tm=1024, tn=512, casts=()):
    m = x.shape[0]
    kh = NSA_WIDTH
    grid = (m // tm, D_MODEL // tn)
    c_in, c_out, c_shapes = _cast_specs(casts, grid)
    return pl.pallas_call(
        functools.partial(_outproj_kernel, n_casts=len(casts)),
        out_shape=[jax.ShapeDtypeStruct((m, D_MODEL), F32)] + c_shapes,
        grid=grid,
        in_specs=[pl.BlockSpec((tm, kh), lambda i, j: (i, 0)),
                  pl.BlockSpec((tm, kh), lambda i, j: (i, 0)),
                  pl.BlockSpec((kh, tn), lambda i, j: (0, j)),
                  pl.BlockSpec((kh, tn), lambda i, j: (1, j)),
                  pl.BlockSpec((tm, tn), lambda i, j: (i, j))] + c_in,
        out_specs=[pl.BlockSpec((tm, tn), lambda i, j: (i, j))] + c_out,
        compiler_params=pltpu.CompilerParams(dimension_semantics=("arbitrary", "arbitrary"),
                                             vmem_limit_bytes=VMEM_LIMIT),
    )(o_nsa, o_ret, w_out, w_out, x, *[c[0] for c in casts])


def _ffn_up_kernel(h_ref, wg_ref, wu_ref, *rest, n_casts):
    o_ref = rest[n_casts]
    sub_rows = h_ref.shape[0] // FFN_UP_SUBBLOCKS
    for s in range(FFN_UP_SUBBLOCKS):
        rows = slice(s * sub_rows, (s + 1) * sub_rows)
        h = h_ref[rows, :]
        g = _dot(h, wg_ref[...])
        u = _dot(h, wu_ref[...])
        o_ref[rows, :] = (g * jax.nn.sigmoid(g) * u).astype(o_ref.dtype)
    _cast_slabs(rest[:n_casts] + rest[n_casts + 1:], n_casts)


def _ffn_up(h, w_gate, w_up, tm=1024, tn=256, casts=()):
    m, k = h.shape
    n = w_gate.shape[1]
    grid = (m // tm, n // tn)
    c_in, c_out, c_shapes = _cast_specs(casts, grid)
    return pl.pallas_call(
        functools.partial(_ffn_up_kernel, n_casts=len(casts)),
        out_shape=[jax.ShapeDtypeStruct((m, n), BF16)] + c_shapes,
        grid=grid,
        in_specs=[pl.BlockSpec((tm, k), lambda i, j: (i, 0)),
                  pl.BlockSpec((k, tn), lambda i, j: (0, j)),
                  pl.BlockSpec((k, tn), lambda i, j: (0, j))] + c_in,
        out_specs=[pl.BlockSpec((tm, tn), lambda i, j: (i, j))] + c_out,
        compiler_params=pltpu.CompilerParams(dimension_semantics=("arbitrary", "arbitrary"),
                                             vmem_limit_bytes=VMEM_LIMIT),
    )(h, w_gate, w_up, *[c[0] for c in casts])


def _ffn_down_kernel(a_ref, b_ref, r_ref, o_ref):
    o_ref[...] = r_ref[...] + _dot(a_ref[...], b_ref[...])


def _ffn_down(a, w_down, x, tm=1024, tn=512, tk=5504):
    m, k = a.shape
    n = w_down.shape[1]
    res = x
    for kk in range(k // tk):
        res = pl.pallas_call(
            _ffn_down_kernel,
            out_shape=jax.ShapeDtypeStruct((m, n), F32),
            grid=(m // tm, n // tn),
            in_specs=[pl.BlockSpec((tm, tk), lambda i, j, kk=kk: (i, kk)),
                      pl.BlockSpec((tk, tn), lambda i, j, kk=kk: (kk, j)),
                      pl.BlockSpec((tm, tn), lambda i, j: (i, j))],
            out_specs=pl.BlockSpec((tm, tn), lambda i, j: (i, j)),
            compiler_params=pltpu.CompilerParams(dimension_semantics=("arbitrary", "arbitrary"),
                                                 vmem_limit_bytes=VMEM_LIMIT),
        )(a, w_down, res)
    return res


def _bias_value(rel, relb_ref, h):
    val = jnp.full(rel.shape, relb_ref[0, h], F32)
    for k in range(1, NUM_BUCKETS):
        val = jnp.where(rel >= _THR[k], relb_ref[k, h], val)
    return val


def _bias_tables_kernel(relb_ref, biasc_ref, dm_ref):
    h = pl.program_id(0)
    sub = 8
    j = lax.broadcasted_iota(jnp.int32, (sub, 2 * SEQ), 1)
    row_c = _bias_value(j - SEQ - (CMP_BLOCK - 1), relb_ref, h)[0:1]
    toe_c = pltpu.roll(jnp.broadcast_to(row_c, (LANES, 2 * SEQ)), 0, 1,
                       stride=CMP_STRIDE, stride_axis=0)
    for qt in range(NQ):
        biasc_ref[0, qt] = toe_c[:, SEQ + qt * TQ:SEQ + (qt + 1) * TQ]
    far = relb_ref[NUM_BUCKETS - 1, h]
    j2 = lax.broadcasted_iota(jnp.int32, (sub, 2 * TK), 1)
    b = lax.broadcasted_iota(jnp.int32, (TK, TQ), 0)
    a = lax.broadcasted_iota(jnp.int32, (TK, TQ), 1)
    for kind in range(2):
        row = ((_bias_value(j2 + (kind - 1) * TK, relb_ref, h) - far) * LOG2E)[0:1]
        toe = pltpu.roll(jnp.broadcast_to(row, (TK, 2 * TK)), 0, 1, stride=1, stride_axis=0)
        tile = toe[:, TK:]
        dm_ref[0, kind] = tile + jnp.where(a >= b, 0.0, NEG_INF) if kind == 0 else tile
    dm_ref[0, 2] = jnp.where(b > a, 0.0, NEG_INF)


def _bias_tables(rel_bias):
    return pl.pallas_call(
        _bias_tables_kernel,
        out_shape=(jax.ShapeDtypeStruct((N_KV_GROUPS, NQ, LANES, GROUP_SIZE * TQ), F32),
                   jax.ShapeDtypeStruct((N_KV_GROUPS, 3, TK, GROUP_SIZE * TQ), F32)),
        grid=(N_NSA_HEADS,),
        in_specs=[pl.BlockSpec(memory_space=pltpu.SMEM)],
        out_specs=(pl.BlockSpec((1, NQ, LANES, TQ), lambda h: (h // GROUP_SIZE, 0, 0, h % GROUP_SIZE)),
                   pl.BlockSpec((1, 3, TK, TQ), lambda h: (h // GROUP_SIZE, 0, 0, h % GROUP_SIZE))),
        compiler_params=pltpu.CompilerParams(dimension_semantics=("arbitrary",),
                                             vmem_limit_bytes=VMEM_LIMIT),
    )(rel_bias)


def _row_rms(x, w):
    return x * lax.rsqrt(jnp.mean(x * x, axis=-1, keepdims=True) + EPS) * w


N_NSA_INPUTS = 17
N_NSA_SCRATCH = 7


def _nsa_kernel(*refs, n_casts):
    (q_ref, k0_ref, v0_ref, k1_ref, v1_ref, k2_ref, v2_ref, gate_ref,
     qnw_ref, knw_ref, pek_ref, pev_ref, wck_ref, wcv_ref, biasc_ref, dm_ref, c2st_ref) = refs[:N_NSA_INPUTS]
    side_a_ref, side_bt_ref = refs[N_NSA_INPUTS:N_NSA_INPUTS + 2]
    n_in = N_NSA_INPUTS + 2 + n_casts
    cast_in = refs[N_NSA_INPUTS + 2:n_in]
    o_ref, side_o_ref = refs[n_in:n_in + 2]
    cast_out = refs[n_in + 2:n_in + 2 + n_casts]
    kc_s, vct_s, kaug_s, kw_s, vst_s, vwt_s, gt_s = refs[len(refs) - N_NSA_SCRATCH:]
    qt = pl.program_id(2)
    cols_all = GROUP_SIZE * TQ
    side_half = side_bt_ref.shape[0] // 2

    @pl.when(qt == 0)
    def _prologue():
        kaug_s[:, :HEAD_DIM] = (_row_rms(k1_ref[...], knw_ref[1:2, :]) * EXP2_SCALE).astype(BF16)
        jblk = lax.broadcasted_iota(jnp.int32, (SEQ, LANES), 0) // SEL_BLOCK
        lane = lax.broadcasted_iota(jnp.int32, (SEQ, LANES), 1)
        kaug_s[:, HEAD_DIM:] = jnp.where(jblk == lane, 1.0, 0.0).astype(BF16)
        kw_s[...] = (_row_rms(k2_ref[...], knw_ref[2:3, :]) * EXP2_SCALE).astype(BF16)
        ones_row = jnp.where(lax.broadcasted_iota(jnp.int32, (V_ROWS - HEAD_DIM, TK), 0) == 0,
                             1.0, 0.0).astype(BF16)
        for t in range(SEQ // TK):
            vst_s[t, :HEAD_DIM, :] = v1_ref[t * TK:(t + 1) * TK, :].T.astype(BF16)
            vst_s[t, HEAD_DIM:, :] = ones_row
            vwt_s[t, :HEAD_DIM, :] = v2_ref[t * TK:(t + 1) * TK, :].T.astype(BF16)
            vwt_s[t, HEAD_DIM:, :] = ones_row

        def compress(x_ref, pe_ref, w_ref):
            acc_lo = jnp.zeros((LANES, HEAD_DIM), F32)
            acc_hi = jnp.zeros((LANES, HEAD_DIM), F32)
            for j in range(CMP_STRIDE):
                y = x_ref[pl.ds(j, LANES, stride=CMP_STRIDE), :]
                acc_lo += _dot((y + pe_ref[j:j + 1, :]).astype(BF16),
                               w_ref[j * HEAD_DIM:(j + 1) * HEAD_DIM, :])
                jh = CMP_STRIDE + j
                acc_hi += _dot((y + pe_ref[jh:jh + 1, :]).astype(BF16),
                               w_ref[jh * HEAD_DIM:(jh + 1) * HEAD_DIM, :])
            return acc_lo + pltpu.roll(acc_hi, LANES - 1, 0)

        kc_s[...] = _row_rms(compress(k0_ref, pek_ref, wck_ref), knw_ref[0:1, :]).astype(BF16)
        vct_s[...] = compress(v0_ref, pev_ref, wcv_ref).T.astype(BF16)

    side_o_ref[:, :side_half] = _dot_nt(side_a_ref[...], side_bt_ref[:side_half, :])
    _cast_slabs(cast_in + cast_out, n_casts)

    t0 = qt * TQ
    q = q_ref[...]
    qts = []
    for r in range(GROUP_SIZE):
        qts.append(_row_rms(q[:, r * HEAD_DIM:(r + 1) * HEAD_DIM], qnw_ref[...]).T)
    qt_stack = jnp.concatenate(qts, axis=1).astype(BF16)

    m0 = jnp.full((1, cols_all), NEG_INF, F32)
    a0 = jnp.zeros((V_ROWS, cols_all), F32)

    def online(carry, tiles):
        m, acc = carry
        m_new = m
        for u, _ in tiles:
            m_new = jnp.maximum(m_new, jnp.max(u, axis=0, keepdims=True))
        acc = jnp.exp2(m - m_new) * acc
        for u, vt in tiles:
            acc = acc + _dot(vt, jnp.exp2(u - m_new).astype(BF16))
        return m_new, acc

    def online_independent(carry, tiles):
        m, acc = carry
        parts = []
        for u, vt in tiles:
            mt = jnp.max(u, axis=0, keepdims=True)
            parts.append((mt, _dot(vt, jnp.exp2(u - mt).astype(BF16))))
        m_new = m
        for mt, _ in parts:
            m_new = jnp.maximum(m_new, mt)
        acc = jnp.exp2(m - m_new) * acc
        for mt, pv in parts:
            acc = acc + jnp.exp2(mt - m_new) * pv
        return m_new, acc

    def normalised(acc):
        return acc[:HEAD_DIM] * (1.0 / acc[HEAD_DIM:HEAD_DIM + 1])

    def key_rows(kt):
        return pl.ds(pl.multiple_of(kt * TK, TK), TK)

    sc = _dot(kc_s[...], qt_stack) * SCALE + biasc_ref[0, 0]
    tpos = t0 + (lax.broadcasted_iota(jnp.int32, (LANES, cols_all), 1) & (TQ - 1))
    nrow = lax.broadcasted_iota(jnp.int32, (LANES, cols_all), 0)
    maskc = tpos >= CMP_STRIDE * nrow + (CMP_BLOCK - 1)
    scm = jnp.where(maskc, sc, NEG_INF)
    mc = jnp.max(scm, axis=0, keepdims=True)
    ec = jnp.where(maskc, jnp.exp(scm - mc), 0.0)
    lc = jnp.sum(ec, axis=0, keepdims=True)
    pc = ec * (1.0 / jnp.where(lc > 0.0, lc, 1.0))
    o_c = _dot(vct_s[...], pc.astype(BF16))

    psum = pc[:, 0:TQ] + pc[:, TQ:2 * TQ] + pc[:, 2 * TQ:3 * TQ] + pc[:, 3 * TQ:4 * TQ]
    p_hi = psum.astype(BF16)
    p_lo = (psum - p_hi.astype(F32)).astype(BF16)
    imp = (_dot(c2st_ref[...], p_hi) + _dot(c2st_ref[...], p_lo))[0:N_SEL]

    tiles = []
    for dist, kind in ((2, 2), (1, 1), (0, 0)):
        kt = jnp.maximum(qt - dist, 0)
        u = _dot(kw_s[key_rows(kt), :], qt_stack) + dm_ref[0, kind]
        if dist > 0:
            u = u + jnp.where(qt >= dist, 0.0, NEG_INF)
        tiles.append((u, vwt_s[kt]))
    _, acc_w = online_independent((m0, a0), tiles)
    o_w = normalised(acc_w)

    tq_pos = t0 + lax.broadcasted_iota(jnp.int32, (N_SEL, TQ), 1)
    sidx = lax.broadcasted_iota(jnp.int32, (N_SEL, TQ), 0)
    cur = tq_pos >> (SEL_BLOCK.bit_length() - 1)
    score = jnp.where(sidx <= cur, imp, NEG_INF)
    for forced_blk in (cur - 1, cur, jnp.zeros_like(cur)):
        score = jnp.where(sidx == forced_blk, FORCE_SCORE, score)
    rank = jnp.zeros((N_SEL, TQ), F32)
    for sp in range(N_SEL):
        row = score[sp:sp + 1, :]
        tie = jnp.where(sidx > sp, 1.0, 0.0)
        rank = rank + jnp.where(row > score, 1.0, jnp.where(row == score, tie, 0.0))
    selb = jnp.where(rank < TOP_N, 0.0, NEG_INF)
    selb = jnp.concatenate([selb, jnp.zeros((HEAD_DIM - N_SEL, TQ), F32)], axis=0).astype(BF16)
    qt_aug = jnp.concatenate([qt_stack, jnp.concatenate([selb] * GROUP_SIZE, axis=1)], axis=0)

    def sel_tile(kt):
        return _dot(kaug_s[key_rows(kt), :], qt_aug), vst_s[kt]

    n_far = jnp.maximum(qt - 1, 0)
    carry = lax.fori_loop(0, n_far >> 1,
                          lambda i, c: online_independent(c, [sel_tile(2 * i), sel_tile(2 * i + 1)]),
                          (m0, a0))
    carry = lax.fori_loop(0, n_far & 1, lambda i, c: online(c, [sel_tile(n_far - 1)]), carry)
    kt1 = jnp.maximum(qt - 1, 0)
    u1, vt1 = sel_tile(kt1)
    u0, vt0 = sel_tile(qt)
    _, acc_s = online_independent(
        carry, [(u1 + dm_ref[0, 1] + jnp.where(qt >= 1, 0.0, NEG_INF), vt1), (u0 + dm_ref[0, 0], vt0)])
    o_s = normalised(acc_s)

    side_o_ref[:, side_half:] = _dot_nt(side_a_ref[...], side_bt_ref[side_half:, :])

    gt_s[...] = jax.nn.sigmoid(gate_ref[...]).T
    head0 = pl.program_id(1) * GROUP_SIZE
    for r in range(GROUP_SIZE):
        cs = slice(r * TQ, (r + 1) * TQ)
        c = (head0 + r) * N_BRANCH
        o = (gt_s[pl.ds(c, 1), :] * o_c[:, cs] + gt_s[pl.ds(c + 1, 1), :] * o_s[:, cs]
             + gt_s[pl.ds(c + 2, 1), :] * o_w[:, cs])
        o_ref[:, r * HEAD_DIM:(r + 1) * HEAD_DIM] = o.T.astype(o_ref.dtype)


def _cmp_to_sel_t():
    ss = np.arange(LANES)[:, None] * SEL_BLOCK
    cs = np.arange(LANES)[None, :] * CMP_STRIDE
    ov = np.clip(np.minimum(cs + CMP_BLOCK, ss + SEL_BLOCK) - np.maximum(cs, ss), 0, None)
    m = ov.astype(np.float32) / np.float32(CMP_BLOCK)
    m[N_SEL:, :] = 0.0
    m[:, N_CMP:] = 0.0
    return jnp.asarray(m, BF16)


def _nsa_attention(proj, gate_logits, batch, q_norm_w, k_norm_w, pe_k, pe_v, w_cmp_k, w_cmp_v, biasc, dm,
                   side_a, side_bt, side_tm=1024, side_tn=512, casts=()):
    kvb = COL_KV // LANES

    def kv_spec(branch, which):
        base = kvb + (branch * 2 + which) * N_KV_GROUPS
        return pl.BlockSpec((SEQ, LANES), lambda b, g, qt: (b, base + g))

    full = lambda shape: pl.BlockSpec(shape, lambda b, g, qt: (0,) * len(shape))
    grid = (batch, N_KV_GROUPS, NQ)
    c_in, c_out, c_shapes = _cast_specs(casts, grid)
    side_m, side_k = side_a.shape
    side_n = side_bt.shape[0]
    n_j = side_n // side_tn
    assert (side_m // side_tm) * n_j == math.prod(grid)
    step = lambda b, g, qt: (b * N_KV_GROUPS + g) * NQ + qt
    side_in = [pl.BlockSpec((side_tm, side_k), lambda b, g, qt: (step(b, g, qt) // n_j, 0),
                            pipeline_mode=pl.Buffered(1)),
               pl.BlockSpec((side_tn, side_k), lambda b, g, qt: (step(b, g, qt) % n_j, 0))]
    side_out = pl.BlockSpec((side_tm, side_tn), lambda b, g, qt: (step(b, g, qt) // n_j, step(b, g, qt) % n_j))
    return pl.pallas_call(
        functools.partial(_nsa_kernel, n_casts=len(casts)),
        out_shape=[jax.ShapeDtypeStruct((batch * SEQ, NSA_WIDTH), BF16),
                   jax.ShapeDtypeStruct((side_m, side_n), F32)] + c_shapes,
        grid=grid,
        in_specs=[pl.BlockSpec((TQ, GROUP_SIZE * HEAD_DIM), lambda b, g, qt: (b * NQ + qt, g)),
                  kv_spec(0, 0), kv_spec(0, 1), kv_spec(1, 0), kv_spec(1, 1),
                  kv_spec(2, 0), kv_spec(2, 1),
                  pl.BlockSpec((TQ, LANES), lambda b, g, qt: (b * NQ + qt, 0)),
                  full((1, HEAD_DIM)), full((N_BRANCH, HEAD_DIM)),
                  full((CMP_BLOCK, HEAD_DIM)), full((CMP_BLOCK, HEAD_DIM)),
                  full((CMP_BLOCK * HEAD_DIM, HEAD_DIM)), full((CMP_BLOCK * HEAD_DIM, HEAD_DIM)),
                  pl.BlockSpec((1, 1, LANES, GROUP_SIZE * TQ), lambda b, g, qt: (g, qt, 0, 0)),
                  pl.BlockSpec((1, 3, TK, GROUP_SIZE * TQ), lambda b, g, qt: (g, 0, 0, 0),
                               pipeline_mode=pl.Buffered(1)),
                  full((LANES, LANES))] + side_in + c_in,
        out_specs=[pl.BlockSpec((TQ, GROUP_SIZE * HEAD_DIM), lambda b, g, qt: (b * NQ + qt, g)),
                   side_out] + c_out,
        scratch_shapes=[pltpu.VMEM((LANES, HEAD_DIM), BF16),
                        pltpu.VMEM((HEAD_DIM, LANES), BF16),
                        pltpu.VMEM((SEQ, 2 * HEAD_DIM), BF16),
                        pltpu.VMEM((SEQ, HEAD_DIM), BF16),
                        pltpu.VMEM((SEQ // TK, V_ROWS, TK), BF16),
                        pltpu.VMEM((SEQ // TK, V_ROWS, TK), BF16),
                        pltpu.VMEM((LANES, TQ), F32)],
        compiler_params=pltpu.CompilerParams(
            dimension_semantics=("arbitrary", "arbitrary", "arbitrary"),
            vmem_limit_bytes=NSA_VMEM_LIMIT),
    )(proj, proj, proj, proj, proj, proj, proj, gate_logits,
      q_norm_w.reshape(1, HEAD_DIM), k_norm_w, pe_k, pe_v,
      w_cmp_k.astype(BF16), w_cmp_v.astype(BF16), biasc, dm, _cmp_to_sel_t(),
      side_a, side_bt, *[c[0] for c in casts])


def _ret_kernel(lg_ref, q_ref, k_ref, v_ref, g_ref, cos_ref, sin_ref, gnw_ref, *rest, n_casts):
    o_ref = rest[n_casts]
    state_ref = rest[-1]
    _cast_slabs(rest[:n_casts] + rest[n_casts + 1:-1], n_casts)
    c_len = RET_CHUNK
    ii = lax.broadcasted_iota(jnp.int32, (c_len, c_len), 0)
    jj = lax.broadcasted_iota(jnp.int32, (c_len, c_len), 1)
    diff = (ii - jj).astype(F32)
    icol = lax.broadcasted_iota(jnp.int32, (c_len, 1), 0).astype(F32)
    half = RET_KEY_DIM // 2
    state_ref[...] = jnp.zeros_like(state_ref)
    tables = []
    for hh in range(RET_HEADS_PER_STEP):
        lg = lg_ref[pl.program_id(1) * RET_HEADS_PER_STEP + hh]
        tables.append((
            jnp.where(diff >= 0.0, jnp.exp(jnp.maximum(diff, 0.0) * lg), 0.0),
            jnp.exp((icol + 1.0) * lg),
            jnp.exp((c_len - 1.0 - icol) * lg),
            jnp.exp(jnp.full((1, RET_VAL_DIM), float(c_len), F32) * lg)))

    def body(c, _):
        rows = pl.ds(pl.multiple_of(c * c_len, c_len), c_len)
        cos = cos_ref[rows, :]
        sin = sin_ref[rows, :]

        def rot(x):
            x1, x2 = x[:, :half], x[:, half:]
            return jnp.concatenate([x1 * cos - x2 * sin, x2 * cos + x1 * sin], axis=-1)

        for hh, (decay_in, xi, zeta, chunk_decay) in enumerate(tables):
            kcols = slice(hh * RET_KEY_DIM, (hh + 1) * RET_KEY_DIM)
            vcols = slice(hh * RET_VAL_DIM, (hh + 1) * RET_VAL_DIM)
            qr = rot(q_ref[rows, kcols])
            kr = rot(k_ref[rows, kcols]) * (RET_KEY_DIM ** -0.5)
            vb = v_ref[rows, vcols].astype(BF16)
            scores = _dot_nt(qr.astype(BF16), kr.astype(BF16)) * decay_in
            inner = _dot(scores.astype(BF16), vb)
            st = state_ref[hh]
            cross = _dot((qr * xi).astype(BF16), st.astype(BF16))
            kv = _dot((kr * zeta).T.astype(BF16), vb)
            state_ref[hh] = chunk_decay * st + kv
            y = inner + cross
            y = y * lax.rsqrt(jnp.mean(y * y, axis=-1, keepdims=True) + EPS) * gnw_ref[:, vcols]
            g = g_ref[rows, vcols]
            o_ref[rows, vcols] = (g * jax.nn.sigmoid(g) * y).astype(o_ref.dtype)
        return 0

    lax.fori_loop(0, SEQ // c_len, body, 0, unroll=4)


def _retention(proj, batch, gn_w, casts=()):
    pos = np.arange(SEQ, dtype=np.float64)
    half = RET_KEY_DIM // 2
    inv = np.exp(-np.linspace(0.0, 1.0, half) * math.log(ROPE_BASE)).astype(np.float32)
    ang = pos.astype(np.float32)[:, None] * inv[None, :]
    cos = jnp.asarray(np.cos(ang.astype(np.float64)), F32)
    sin = jnp.asarray(np.sin(ang.astype(np.float64)), F32)
    log_gamma = jnp.asarray(np.log(1.0 - 2.0 ** (-5.0 - np.arange(N_RET_HEADS))), F32)
    w = RET_KEY_DIM * RET_HEADS_PER_STEP

    def col_spec(col0):
        return pl.BlockSpec((SEQ, w), lambda b, h: (b, col0 // w + h))

    grid = (batch, N_RET_HEADS // RET_HEADS_PER_STEP)
    c_in, c_out, c_shapes = _cast_specs(casts, grid)
    return pl.pallas_call(
        functools.partial(_ret_kernel, n_casts=len(casts)),
        out_shape=[jax.ShapeDtypeStruct((batch * SEQ, RET_WIDTH), BF16)] + c_shapes,
        grid=grid,
        in_specs=[pl.BlockSpec(memory_space=pltpu.SMEM),
                  col_spec(COL_QR), col_spec(COL_KR), col_spec(COL_VR), col_spec(COL_GR),
                  pl.BlockSpec((SEQ, half), lambda b, h: (0, 0)),
                  pl.BlockSpec((SEQ, half), lambda b, h: (0, 0)),
                  pl.BlockSpec((1, w), lambda b, h: (0, h))] + c_in,
        out_specs=[pl.BlockSpec((SEQ, w), lambda b, h: (b, h))] + c_out,
        scratch_shapes=[pltpu.VMEM((RET_HEADS_PER_STEP, RET_KEY_DIM, RET_VAL_DIM), F32)],
        compiler_params=pltpu.CompilerParams(dimension_semantics=("arbitrary", "arbitrary"),
                                             vmem_limit_bytes=VMEM_LIMIT),
    )(log_gamma, proj, proj, proj, proj, cos, sin, gn_w.reshape(1, RET_WIDTH), *[c[0] for c in casts])


def _layer(x, norm1_w, w_in, nsa_q_norm_w, nsa_k_norm_w, cmp_pe_k, cmp_pe_v, w_cmp_k, w_cmp_v,
           rel_bias, ret_gn_w, w_out, norm2_w, w_gate, w_up, w_down):
    batch = x.shape[0]
    rows = batch * SEQ
    xf = x.reshape(rows, D_MODEL)
    h = _rmsnorm(xf, norm1_w)
    w_in_t = w_in.T
    gate0 = W_IN_NSA_COLS
    gate1 = gate0 + W_IN_GATE_COLS
    ret_cols = 4 * RET_WIDTH
    nsa_tn, nsa_tm = 1024, 512
    nsa_steps = (W_IN_NSA_COLS // nsa_tn) * (rows // nsa_tm)
    ret_slabs = 1 << (nsa_steps.bit_length() - 1)
    proj_nsa, w_ret_t = _in_projection(h, w_in_t, 0, W_IN_NSA_COLS, nsa_tn, tm=nsa_tm,
                                       casts=((w_in_t, gate1, ret_cols, ret_slabs),))
    gate_logits, = _in_projection(h, w_in_t, gate0, LANES, LANES)
    biasc, dm = _bias_tables(rel_bias)
    n_steps = batch * N_KV_GROUPS * NQ
    o_nsa, proj_ret, w_gate_b, w_up_b, w_out_b = _nsa_attention(
        proj_nsa, gate_logits, batch, nsa_q_norm_w, nsa_k_norm_w, cmp_pe_k, cmp_pe_v,
        w_cmp_k, w_cmp_v, biasc, dm, h, w_ret_t,
        casts=((w_gate, 0, D_MODEL, n_steps), (w_up, 0, D_MODEL, n_steps), (w_out, 0, D_MODEL, n_steps)))
    o_ret, = _retention(proj_ret, batch, ret_gn_w)
    x1, = _out_projection(o_nsa, o_ret, w_out_b, xf, tn=1024)
    hf = _rmsnorm(x1, norm2_w)
    up_tm, up_tn = 2048, 256
    up_steps = (rows // up_tm) * (D_FF // up_tn)
    mid, w_down_b = _ffn_up(hf, w_gate_b, w_up_b, tm=up_tm, tn=up_tn, casts=((w_down, 0, D_FF, up_steps),))
    out = _ffn_down(mid, w_down_b, x1)
    return out.reshape(batch, SEQ, D_MODEL)


def kernel(x, norm1_w, w_in, nsa_q_norm_w, nsa_k_norm_w, cmp_pe_k, cmp_pe_v, w_cmp_k, w_cmp_v,
           rel_bias, ret_gn_w, w_out, norm2_w, w_gate, w_up, w_down):
    for l in range(norm1_w.shape[0]):
        x = _layer(x, norm1_w[l], w_in[l], nsa_q_norm_w[l], nsa_k_norm_w[l], cmp_pe_k[l], cmp_pe_v[l],
                   w_cmp_k[l], w_cmp_v[l], rel_bias, ret_gn_w[l], w_out[l], norm2_w[l],
                   w_gate[l], w_up[l], w_down[l])
    return x
```

```python
import functools
import math

import numpy as np
import jax
import jax.numpy as jnp
from jax import lax
from jax.experimental import pallas as pl
from jax.experimental.pallas import tpu as pltpu

F32 = jnp.float32
BF16 = jnp.bfloat16

D_MODEL = 4096
SEQ = 2048
HEAD_DIM = 128
N_NSA_HEADS = 16
N_KV_GROUPS = 4
GROUP_SIZE = 4
N_BRANCH = 3
CMP_BLOCK = 32
CMP_STRIDE = 16
SEL_BLOCK = 64
TOP_N = 16
WINDOW = 512
RET_KEY_DIM = 256
RET_VAL_DIM = 256
N_RET_HEADS = 8
RET_CHUNK = 128
ROPE_BASE = 10000.0
NSA_WIDTH = 2048
RET_WIDTH = 2048
D_FF = 11008
NUM_BUCKETS = 32
MAX_DISTANCE = 128
EPS = 1e-6
NEG_INF = -1e30
FORCE_SCORE = 1e9
N_SEL = SEQ // SEL_BLOCK
N_CMP = (SEQ - CMP_BLOCK) // CMP_STRIDE + 1
SCALE = HEAD_DIM ** -0.5
LOG2E = math.log2(math.e)
EXP2_SCALE = SCALE * LOG2E

LANES = 128
VMEM_LIMIT = 56 * 1024 * 1024
NSA_VMEM_LIMIT = 60 * 1024 * 1024

TQ = 256
TK = 256
NQ = SEQ // TQ
RET_HEADS_PER_STEP = 2
FFN_UP_SUBBLOCKS = 4
V_ROWS = HEAD_DIM + 16
assert TQ == TK and WINDOW == 2 * TK and TK % SEL_BLOCK == 0 and N_SEL <= HEAD_DIM and TK >= MAX_DISTANCE

W_IN_NSA_COLS = NSA_WIDTH + N_BRANCH * 2 * N_KV_GROUPS * HEAD_DIM
W_IN_GATE_COLS = N_NSA_HEADS * N_BRANCH
COL_KV = NSA_WIDTH
COL_QR = 0
COL_KR = 2048
COL_VR = 4096
COL_GR = 6144


def _bucket_thresholds():
    rel = np.arange(0, 4 * MAX_DISTANCE)
    max_exact = NUM_BUCKETS // 2
    nf = np.maximum(rel, 1).astype(np.float64)
    large = max_exact + np.floor(np.log(nf / max_exact) / math.log(MAX_DISTANCE / max_exact)
                                 * (NUM_BUCKETS - max_exact)).astype(np.int64)
    large = np.minimum(large, NUM_BUCKETS - 1)
    b = np.where(rel < max_exact, rel, large)
    return [int(np.argmax(b >= k)) for k in range(NUM_BUCKETS)]


_THR = _bucket_thresholds()


def _dot(a, b):
    return jnp.dot(a, b, preferred_element_type=F32)


def _dot_nt(a, b):
    return lax.dot_general(a, b, (((1,), (1,)), ((), ())), preferred_element_type=F32)


def _rms_kernel(x_ref, w_ref, o_ref):
    x = x_ref[...]
    y = x * lax.rsqrt(jnp.mean(x * x, axis=-1, keepdims=True) + EPS)
    o_ref[...] = (y * w_ref[...]).astype(o_ref.dtype)


def _rmsnorm(x, w, tm=512):
    m, d = x.shape
    return pl.pallas_call(
        _rms_kernel,
        out_shape=jax.ShapeDtypeStruct((m, d), BF16),
        grid=(m // tm,),
        in_specs=[pl.BlockSpec((tm, d), lambda i: (i, 0)),
                  pl.BlockSpec((1, d), lambda i: (0, 0))],
        out_specs=pl.BlockSpec((tm, d), lambda i: (i, 0)),
        compiler_params=pltpu.CompilerParams(dimension_semantics=("arbitrary",),
                                             vmem_limit_bytes=VMEM_LIMIT),
    )(x, w.reshape(1, d))


def _rms_proj_kernel(x_ref, w_ref, pt_ref, o_ref, p_ref, pbf_ref):
    @pl.when(pl.program_id(0) == 0)
    def _():
        pbf_ref[...] = pt_ref[...].astype(BF16)

    x = x_ref[...]
    y = x * lax.rsqrt(jnp.mean(x * x, axis=-1, keepdims=True) + EPS)
    h = (y * w_ref[...]).astype(o_ref.dtype)
    o_ref[...] = h
    p_ref[...] = _dot_nt(h, pbf_ref[...])


def _rmsnorm_with_projection(x, w, w_t, row0, n_rows, tm=512):
    m, d = x.shape
    sub = 8
    assert row0 % sub == 0 and n_rows % LANES == 0
    return pl.pallas_call(
        _rms_proj_kernel,
        out_shape=(jax.ShapeDtypeStruct((m, d), BF16), jax.ShapeDtypeStruct((m, n_rows), F32)),
        grid=(m // tm,),
        in_specs=[pl.BlockSpec((tm, d), lambda i: (i, 0)),
                  pl.BlockSpec((1, d), lambda i: (0, 0)),
                  pl.BlockSpec((pl.Element(n_rows), pl.Element(d)), lambda i: ((row0 // sub) * sub, 0))],
        out_specs=(pl.BlockSpec((tm, d), lambda i: (i, 0)),
                   pl.BlockSpec((tm, n_rows), lambda i: (i, 0))),
        scratch_shapes=[pltpu.VMEM((n_rows, d), BF16)],
        compiler_params=pltpu.CompilerParams(dimension_semantics=("arbitrary",),
                                             vmem_limit_bytes=VMEM_LIMIT),
    )(x, w.reshape(1, d), w_t)


def _cast_spec(grid, row0, n_rows, n_cols, n_slabs):
    slab = n_rows // n_slabs
    sub = 8
    n_steps = math.prod(grid)
    assert slab * n_slabs == n_rows and slab % 16 == 0 and n_cols % LANES == 0
    assert row0 % sub == 0 and n_slabs <= n_steps

    def slab_index(ids):
        step = ids[0]
        for dim, idx in zip(grid[1:], ids[1:]):
            step = step * dim + idx
        return step if n_slabs == n_steps else jnp.minimum(step, n_slabs - 1)

    in_spec = pl.BlockSpec((pl.Element(slab), pl.Element(n_cols)),
                           lambda *ids: ((row0 // sub + slab_index(ids) * (slab // sub)) * sub, 0))
    out_spec = pl.BlockSpec((slab, n_cols), lambda *ids: (slab_index(ids), 0))
    return in_spec, out_spec, jax.ShapeDtypeStruct((n_rows, n_cols), BF16)


def _cast_specs(casts, grid):
    specs = [_cast_spec(grid, row0, n_rows, arr.shape[1], n_slabs) for arr, row0, n_rows, n_slabs in casts]
    return [s[0] for s in specs], [s[1] for s in specs], [s[2] for s in specs]


def _cast_slabs(refs, n_casts):
    for src, dst in zip(refs[:n_casts], refs[len(refs) - n_casts:]):
        dst[...] = src[...].astype(dst.dtype)


def _inproj_kernel(a_ref, wt_ref, *rest, n_casts):
    o_ref = rest[n_casts]
    wbf_ref = rest[-1]

    @pl.when(pl.program_id(1) == 0)
    def _():
        wbf_ref[...] = wt_ref[...].astype(BF16)

    o_ref[...] = _dot_nt(a_ref[...], wbf_ref[...])
    _cast_slabs(rest[:n_casts] + rest[n_casts + 1:-1], n_casts)


def _in_projection(a, w_t, row0, n_cols, tn, tm=1024, casts=()):
    m, k = a.shape
    grid = (n_cols // tn, m // tm)
    c_in, c_out, c_shapes = _cast_specs(casts, grid)
    sub = 8
    assert row0 % sub == 0 and tn % sub == 0
    return pl.pallas_call(
        functools.partial(_inproj_kernel, n_casts=len(casts)),
        out_shape=[jax.ShapeDtypeStruct((m, n_cols), F32)] + c_shapes,
        grid=grid,
        in_specs=[pl.BlockSpec((tm, k), lambda j, i: (i, 0)),
                  pl.BlockSpec((pl.Element(tn), pl.Element(k)),
                               lambda j, i: ((row0 // sub + j * (tn // sub)) * sub, 0),
                               pipeline_mode=pl.Buffered(1))] + c_in,
        out_specs=[pl.BlockSpec((tm, tn), lambda j, i: (i, j))] + c_out,
        scratch_shapes=[pltpu.VMEM((tn, k), BF16)],
        compiler_params=pltpu.CompilerParams(dimension_semantics=("arbitrary", "arbitrary"),
                                             vmem_limit_bytes=VMEM_LIMIT),
    )(a, w_t, *[c[0] for c in casts])


def _outproj_kernel(a1_ref, a2_ref, w1_ref, w2_ref, x_ref, *rest, n_casts):
    o_ref = rest[n_casts]
    acc = _dot(a1_ref[...], w1_ref[...]) + _dot(a2_ref[...], w2_ref[...])
    o_ref[...] = x_ref[...] + acc
    _cast_slabs(rest[:n_casts] + rest[n_casts + 1:], n_casts)


def _out_projection(o_nsa, o_ret, w_out, x, tm=1024, tn=512, casts=()):
    m = x.shape[0]
    kh = NSA_WIDTH
    grid = (m // tm, D_MODEL // tn)
    c_in, c_out, c_shapes = _cast_specs(casts, grid)
    return pl.pallas_call(
        functools.partial(_outproj_kernel, n_casts=len(casts)),
        out_shape=[jax.ShapeDtypeStruct((m, D_MODEL), F32)] + c_shapes,
        grid=grid,
        in_specs=[pl.BlockSpec((tm, kh), lambda i, j: (i, 0)),
                  pl.BlockSpec((tm, kh), lambda i, j: (i, 0)),
                  pl.BlockSpec((kh, tn), lambda i, j: (0, j)),
                  pl.BlockSpec((kh, tn), lambda i, j: (1, j)),
                  pl.BlockSpec((tm, tn), lambda i, j: (i, j))] + c_in,
        out_specs=[pl.BlockSpec((tm, tn), lambda i, j: (i, j))] + c_out,
        compiler_params=pltpu.CompilerParams(dimension_semantics=("arbitrary", "arbitrary"),
                                             vmem_limit_bytes=VMEM_LIMIT),
    )(o_nsa, o_ret, w_out, w_out, x, *[c[0] for c in casts])


def _ffn_up_kernel(h_ref, wg_ref, wu_ref, *rest, n_casts):
    o_ref = rest[n_casts]
    sub_rows = h_ref.shape[0] // FFN_UP_SUBBLOCKS
    for s in range(FFN_UP_SUBBLOCKS):
        rows = slice(s * sub_rows, (s + 1) * sub_rows)
        h = h_ref[rows, :]
        g = _dot(h, wg_ref[...])
        u = _dot(h, wu_ref[...])
        o_ref[rows, :] = (g * jax.nn.sigmoid(g) * u).astype(o_ref.dtype)
    _cast_slabs(rest[:n_casts] + rest[n_casts + 1:], n_casts)


def _ffn_up(h, w_gate, w_up, tm=1024, tn=256, casts=()):
    m, k = h.shape
    n = w_gate.shape[1]
    grid = (m // tm, n // tn)
    c_in, c_out, c_shapes = _cast_specs(casts, grid)
    return pl.pallas_call(
        functools.partial(_ffn_up_kernel, n_casts=len(casts)),
        out_shape=[jax.ShapeDtypeStruct((m, n), BF16)] + c_shapes,
        grid=grid,
        in_specs=[pl.BlockSpec((tm, k), lambda i, j: (i, 0)),
                  pl.BlockSpec((k, tn), lambda i, j: (0, j)),
                  pl.BlockSpec((k, tn), lambda i, j: (0, j))] + c_in,
        out_specs=[pl.BlockSpec((tm, tn), lambda i, j: (i, j))] + c_out,
        compiler_params=pltpu.CompilerParams(dimension_semantics=("arbitrary", "arbitrary"),
                                             vmem_limit_bytes=VMEM_LIMIT),
    )(h, w_gate, w_up, *[c[0] for c in casts])


def _ffn_down_kernel(a_ref, b_ref, r_ref, o_ref):
    o_ref[...] = r_ref[...] + _dot(a_ref[...], b_ref[...])


def _ffn_down(a, w_down, x, tm=1024, tn=512, tk=5504):
    m, k = a.shape
    n = w_down.shape[1]
    res = x
    for kk in range(k // tk):
        res = pl.pallas_call(
            _ffn_down_kernel,
            out_shape=jax.ShapeDtypeStruct((m, n), F32),
            grid=(m // tm, n // tn),
            in_specs=[pl.BlockSpec((tm, tk), lambda i, j, kk=kk: (i, kk)),
                      pl.BlockSpec((tk, tn), lambda i, j, kk=kk: (kk, j)),
                      pl.BlockSpec((tm, tn), lambda i, j: (i, j))],
            out_specs=pl.BlockSpec((tm, tn), lambda i, j: (i, j)),
            compiler_params=pltpu.CompilerParams(dimension_semantics=("arbitrary", "arbitrary"),
                                                 vmem_limit_bytes=VMEM_LIMIT),
        )(a, w_down, res)
    return res


def _bias_value(rel, relb_ref, h):
    val = jnp.full(rel.shape, relb_ref[0, h], F32)
    for k in range(1, NUM_BUCKETS):
        val = jnp.where(rel >= _THR[k], relb_ref[k, h], val)
    return val


def _bias_tables_kernel(relb_ref, biasc_ref, dm_ref):
    h = pl.program_id(0)
    sub = 8
    j = lax.broadcasted_iota(jnp.int32, (sub, 2 * SEQ), 1)
    row_c = _bias_value(j - SEQ - (CMP_BLOCK - 1), relb_ref, h)[0:1]
    toe_c = pltpu.roll(jnp.broadcast_to(row_c, (LANES, 2 * SEQ)), 0, 1,
                       stride=CMP_STRIDE, stride_axis=0)
    for qt in range(NQ):
        biasc_ref[0, qt] = toe_c[:, SEQ + qt * TQ:SEQ + (qt + 1) * TQ]
    far = relb_ref[NUM_BUCKETS - 1, h]
    j2 = lax.broadcasted_iota(jnp.int32, (sub, 2 * TK), 1)
    b = lax.broadcasted_iota(jnp.int32, (TK, TQ), 0)
    a = lax.broadcasted_iota(jnp.int32, (TK, TQ), 1)
    for kind in range(2):
        row = ((_bias_value(j2 + (kind - 1) * TK, relb_ref, h) - far) * LOG2E)[0:1]
        toe = pltpu.roll(jnp.broadcast_to(row, (TK, 2 * TK)), 0, 1, stride=1, stride_axis=0)
        tile = toe[:, TK:]
        dm_ref[0, kind] = tile + jnp.where(a >= b, 0.0, NEG_INF) if kind == 0 else tile
    dm_ref[0, 2] = jnp.where(b > a, 0.0, NEG_INF)


def _bias_tables(rel_bias):
    return pl.pallas_call(
        _bias_tables_kernel,
        out_shape=(jax.ShapeDtypeStruct((N_KV_GROUPS, NQ, LANES, GROUP_SIZE * TQ), F32),
                   jax.ShapeDtypeStruct((N_KV_GROUPS, 3, TK, GROUP_SIZE * TQ), F32)),
        grid=(N_NSA_HEADS,),
        in_specs=[pl.BlockSpec(memory_space=pltpu.SMEM)],
        out_specs=(pl.BlockSpec((1, NQ, LANES, TQ), lambda h: (h // GROUP_SIZE, 0, 0, h % GROUP_SIZE)),
                   pl.BlockSpec((1, 3, TK, TQ), lambda h: (h // GROUP_SIZE, 0, 0, h % GROUP_SIZE))),
        compiler_params=pltpu.CompilerParams(dimension_semantics=("arbitrary",),
                                             vmem_limit_bytes=VMEM_LIMIT),
    )(rel_bias)


def _row_rms(x, w):
    return x * lax.rsqrt(jnp.mean(x * x, axis=-1, keepdims=True) + EPS) * w


N_NSA_INPUTS = 17
N_NSA_SCRATCH = 7


def _nsa_kernel(*refs, n_casts):
    (q_ref, k0_ref, v0_ref, k1_ref, v1_ref, k2_ref, v2_ref, gate_ref,
     qnw_ref, knw_ref, pek_ref, pev_ref, wck_ref, wcv_ref, biasc_ref, dm_ref, c2st_ref) = refs[:N_NSA_INPUTS]
    side_a_ref, side_bt_ref = refs[N_NSA_INPUTS:N_NSA_INPUTS + 2]
    n_in = N_NSA_INPUTS + 2 + n_casts
    cast_in = refs[N_NSA_INPUTS + 2:n_in]
    o_ref, side_o_ref = refs[n_in:n_in + 2]
    cast_out = refs[n_in + 2:n_in + 2 + n_casts]
    kc_s, vct_s, kaug_s, kw_s, vst_s, vwt_s, gt_s = refs[len(refs) - N_NSA_SCRATCH:]
    qt = pl.program_id(2)
    cols_all = GROUP_SIZE * TQ
    side_half = side_bt_ref.shape[0] // 2

    @pl.when(qt == 0)
    def _prologue():
        kaug_s[:, :HEAD_DIM] = (_row_rms(k1_ref[...], knw_ref[1:2, :]) * EXP2_SCALE).astype(BF16)
        jblk = lax.broadcasted_iota(jnp.int32, (SEQ, LANES), 0) // SEL_BLOCK
        lane = lax.broadcasted_iota(jnp.int32, (SEQ, LANES), 1)
        kaug_s[:, HEAD_DIM:] = jnp.where(jblk == lane, 1.0, 0.0).astype(BF16)
        kw_s[...] = (_row_rms(k2_ref[...], knw_ref[2:3, :]) * EXP2_SCALE).astype(BF16)
        ones_row = jnp.where(lax.broadcasted_iota(jnp.int32, (V_ROWS - HEAD_DIM, TK), 0) == 0,
                             1.0, 0.0).astype(BF16)
        for t in range(SEQ // TK):
            vst_s[t, :HEAD_DIM, :] = v1_ref[t * TK:(t + 1) * TK, :].T.astype(BF16)
            vst_s[t, HEAD_DIM:, :] = ones_row
            vwt_s[t, :HEAD_DIM, :] = v2_ref[t * TK:(t + 1) * TK, :].T.astype(BF16)
            vwt_s[t, HEAD_DIM:, :] = ones_row

        def compress(x_ref, pe_ref, w_ref):
            acc_lo = jnp.zeros((LANES, HEAD_DIM), F32)
            acc_hi = jnp.zeros((LANES, HEAD_DIM), F32)
            for j in range(CMP_STRIDE):
                y = x_ref[pl.ds(j, LANES, stride=CMP_STRIDE), :]
                acc_lo += _dot((y + pe_ref[j:j + 1, :]).astype(BF16),
                               w_ref[j * HEAD_DIM:(j + 1) * HEAD_DIM, :])
                jh = CMP_STRIDE + j
                acc_hi += _dot((y + pe_ref[jh:jh + 1, :]).astype(BF16),
                               w_ref[jh * HEAD_DIM:(jh + 1) * HEAD_DIM, :])
            return acc_lo + pltpu.roll(acc_hi, LANES - 1, 0)

        kc_s[...] = _row_rms(compress(k0_ref, pek_ref, wck_ref), knw_ref[0:1, :]).astype(BF16)
        vct_s[...] = compress(v0_ref, pev_ref, wcv_ref).T.astype(BF16)

    side_o_ref[:, :side_half] = _dot_nt(side_a_ref[...], side_bt_ref[:side_half, :])
    _cast_slabs(cast_in + cast_out, n_casts)

    t0 = qt * TQ
    q = q_ref[...]
    qts = []
    for r in range(GROUP_SIZE):
        qts.append(_row_rms(q[:, r * HEAD_DIM:(r + 1) * HEAD_DIM], qnw_ref[...]).T)
    qt_stack = jnp.concatenate(qts, axis=1).astype(BF16)

    m0 = jnp.full((1, cols_all), NEG_INF, F32)
    a0 = jnp.zeros((V_ROWS, cols_all), F32)

    def online(carry, tiles):
        m, acc = carry
        m_new = m
        for u, _ in tiles:
            m_new = jnp.maximum(m_new, jnp.max(u, axis=0, keepdims=True))
        acc = jnp.exp2(m - m_new) * acc
        for u, vt in tiles:
            acc = acc + _dot(vt, jnp.exp2(u - m_new).astype(BF16))
        return m_new, acc

    def online_independent(carry, tiles):
        m, acc = carry
        parts = []
        for u, vt in tiles:
            mt = jnp.max(u, axis=0, keepdims=True)
            parts.append((mt, _dot(vt, jnp.exp2(u - mt).astype(BF16))))
        m_new = m
        for mt, _ in parts:
            m_new = jnp.maximum(m_new, mt)
        acc = jnp.exp2(m - m_new) * acc
        for mt, pv in parts:
            acc = acc + jnp.exp2(mt - m_new) * pv
        return m_new, acc

    def normalised(acc):
        return acc[:HEAD_DIM] * (1.0 / acc[HEAD_DIM:HEAD_DIM + 1])

    def key_rows(kt):
        return pl.ds(pl.multiple_of(kt * TK, TK), TK)

    sc = _dot(kc_s[...], qt_stack) * SCALE + biasc_ref[0, 0]
    tpos = t0 + (lax.broadcasted_iota(jnp.int32, (LANES, cols_all), 1) & (TQ - 1))
    nrow = lax.broadcasted_iota(jnp.int32, (LANES, cols_all), 0)
    maskc = tpos >= CMP_STRIDE * nrow + (CMP_BLOCK - 1)
    scm = jnp.where(maskc, sc, NEG_INF)
    mc = jnp.max(scm, axis=0, keepdims=True)
    ec = jnp.where(maskc, jnp.exp(scm - mc), 0.0)
    lc = jnp.sum(ec, axis=0, keepdims=True)
    pc = ec * (1.0 / jnp.where(lc > 0.0, lc, 1.0))
    o_c = _dot(vct_s[...], pc.astype(BF16))

    psum = pc[:, 0:TQ] + pc[:, TQ:2 * TQ] + pc[:, 2 * TQ:3 * TQ] + pc[:, 3 * TQ:4 * TQ]
    p_hi = psum.astype(BF16)
    p_lo = (psum - p_hi.astype(F32)).astype(BF16)
    imp = (_dot(c2st_ref[...], p_hi) + _dot(c2st_ref[...], p_lo))[0:N_SEL]

    tiles = []
    for dist, kind in ((2, 2), (1, 1), (0, 0)):
        kt = jnp.maximum(qt - dist, 0)
        u = _dot(kw_s[key_rows(kt), :], qt_stack) + dm_ref[0, kind]
        if dist > 0:
            u = u + jnp.where(qt >= dist, 0.0, NEG_INF)
        tiles.append((u, vwt_s[kt]))
    _, acc_w = online_independent((m0, a0), tiles)
    o_w = normalised(acc_w)

    tq_pos = t0 + lax.broadcasted_iota(jnp.int32, (N_SEL, TQ), 1)
    sidx = lax.broadcasted_iota(jnp.int32, (N_SEL, TQ), 0)
    cur = tq_pos >> (SEL_BLOCK.bit_length() - 1)
    score = jnp.where(sidx <= cur, imp, NEG_INF)
    for forced_blk in (cur - 1, cur, jnp.zeros_like(cur)):
        score = jnp.where(sidx == forced_blk, FORCE_SCORE, score)
    rank = jnp.zeros((N_SEL, TQ), F32)
    for sp in range(N_SEL):
        row = score[sp:sp + 1, :]
        tie = jnp.where(sidx > sp, 1.0, 0.0)
        rank = rank + jnp.where(row > score, 1.0, jnp.where(row == score, tie, 0.0))
    selb = jnp.where(rank < TOP_N, 0.0, NEG_INF)
    selb = jnp.concatenate([selb, jnp.zeros((HEAD_DIM - N_SEL, TQ), F32)], axis=0).astype(BF16)
    qt_aug = jnp.concatenate([qt_stack, jnp.concatenate([selb] * GROUP_SIZE, axis=1)], axis=0)

    def sel_tile(kt):
        return _dot(kaug_s[key_rows(kt), :], qt_aug), vst_s[kt]

    n_far = jnp.maximum(qt - 1, 0)
    carry = lax.fori_loop(0, n_far >> 1,
                          lambda i, c: online_independent(c, [sel_tile(2 * i), sel_tile(2 * i + 1)]),
                          (m0, a0))
    carry = lax.fori_loop(0, n_far & 1, lambda i, c: online(c, [sel_tile(n_far - 1)]), carry)
    kt1 = jnp.maximum(qt - 1, 0)
    u1, vt1 = sel_tile(kt1)
    u0, vt0 = sel_tile(qt)
    _, acc_s = online_independent(
        carry, [(u1 + dm_ref[0, 1] + jnp.where(qt >= 1, 0.0, NEG_INF), vt1), (u0 + dm_ref[0, 0], vt0)])
    o_s = normalised(acc_s)

    side_o_ref[:, side_half:] = _dot_nt(side_a_ref[...], side_bt_ref[side_half:, :])

    gt_s[...] = jax.nn.sigmoid(gate_ref[...]).T
    head0 = pl.program_id(1) * GROUP_SIZE
    for r in range(GROUP_SIZE):
        cs = slice(r * TQ, (r + 1) * TQ)
        c = (head0 + r) * N_BRANCH
        o = (gt_s[pl.ds(c, 1), :] * o_c[:, cs] + gt_s[pl.ds(c + 1, 1), :] * o_s[:, cs]
             + gt_s[pl.ds(c + 2, 1), :] * o_w[:, cs])
        o_ref[:, r * HEAD_DIM:(r + 1) * HEAD_DIM] = o.T.astype(o_ref.dtype)


def _cmp_to_sel_t():
    ss = np.arange(LANES)[:, None] * SEL_BLOCK
    cs = np.arange(LANES)[None, :] * CMP_STRIDE
    ov = np.clip(np.minimum(cs + CMP_BLOCK, ss + SEL_BLOCK) - np.maximum(cs, ss), 0, None)
    m = ov.astype(np.float32) / np.float32(CMP_BLOCK)
    m[N_SEL:, :] = 0.0
    m[:, N_CMP:] = 0.0
    return jnp.asarray(m, BF16)


def _nsa_attention(proj, gate_logits, batch, q_norm_w, k_norm_w, pe_k, pe_v, w_cmp_k, w_cmp_v, biasc, dm,
                   side_a, side_bt, side_tm=1024, side_tn=512, casts=()):
    kvb = COL_KV // LANES

    def kv_spec(branch, which):
        base = kvb + (branch * 2 + which) * N_KV_GROUPS
        return pl.BlockSpec((SEQ, LANES), lambda b, g, qt: (b, base + g))

    full = lambda shape: pl.BlockSpec(shape, lambda b, g, qt: (0,) * len(shape))
    grid = (batch, N_KV_GROUPS, NQ)
    c_in, c_out, c_shapes = _cast_specs(casts, grid)
    side_m, side_k = side_a.shape
    side_n = side_bt.shape[0]
    n_j = side_n // side_tn
    assert (side_m // side_tm) * n_j == math.prod(grid)
    step = lambda b, g, qt: (b * N_KV_GROUPS + g) * NQ + qt
    side_in = [pl.BlockSpec((side_tm, side_k), lambda b, g, qt: (step(b, g, qt) // n_j, 0),
                            pipeline_mode=pl.Buffered(1)),
               pl.BlockSpec((side_tn, side_k), lambda b, g, qt: (step(b, g, qt) % n_j, 0))]
    side_out = pl.BlockSpec((side_tm, side_tn), lambda b, g, qt: (step(b, g, qt) // n_j, step(b, g, qt) % n_j))
    return pl.pallas_call(
        functools.partial(_nsa_kernel, n_casts=len(casts)),
        out_shape=[jax.ShapeDtypeStruct((batch * SEQ, NSA_WIDTH), BF16),
                   jax.ShapeDtypeStruct((side_m, side_n), F32)] + c_shapes,
        grid=grid,
        in_specs=[pl.BlockSpec((TQ, GROUP_SIZE * HEAD_DIM), lambda b, g, qt: (b * NQ + qt, g)),
                  kv_spec(0, 0), kv_spec(0, 1), kv_spec(1, 0), kv_spec(1, 1),
                  kv_spec(2, 0), kv_spec(2, 1),
                  pl.BlockSpec((TQ, LANES), lambda b, g, qt: (b * NQ + qt, 0)),
                  full((1, HEAD_DIM)), full((N_BRANCH, HEAD_DIM)),
                  full((CMP_BLOCK, HEAD_DIM)), full((CMP_BLOCK, HEAD_DIM)),
                  full((CMP_BLOCK * HEAD_DIM, HEAD_DIM)), full((CMP_BLOCK * HEAD_DIM, HEAD_DIM)),
                  pl.BlockSpec((1, 1, LANES, GROUP_SIZE * TQ), lambda b, g, qt: (g, qt, 0, 0)),
                  pl.BlockSpec((1, 3, TK, GROUP_SIZE * TQ), lambda b, g, qt: (g, 0, 0, 0),
                               pipeline_mode=pl.Buffered(1)),
                  full((LANES, LANES))] + side_in + c_in,
        out_specs=[pl.BlockSpec((TQ, GROUP_SIZE * HEAD_DIM), lambda b, g, qt: (b * NQ + qt, g)),
                   side_out] + c_out,
        scratch_shapes=[pltpu.VMEM((LANES, HEAD_DIM), BF16),
                        pltpu.VMEM((HEAD_DIM, LANES), BF16),
                        pltpu.VMEM((SEQ, 2 * HEAD_DIM), BF16),
                        pltpu.VMEM((SEQ, HEAD_DIM), BF16),
                        pltpu.VMEM((SEQ // TK, V_ROWS, TK), BF16),
                        pltpu.VMEM((SEQ // TK, V_ROWS, TK), BF16),
                        pltpu.VMEM((LANES, TQ), F32)],
        compiler_params=pltpu.CompilerParams(
            dimension_semantics=("arbitrary", "arbitrary", "arbitrary"),
            vmem_limit_bytes=NSA_VMEM_LIMIT),
    )(proj, proj, proj, proj, proj, proj, proj, gate_logits,
      q_norm_w.reshape(1, HEAD_DIM), k_norm_w, pe_k, pe_v,
      w_cmp_k.astype(BF16), w_cmp_v.astype(BF16), biasc, dm, _cmp_to_sel_t(),
      side_a, side_bt, *[c[0] for c in casts])


def _ret_kernel(lg_ref, q_ref, k_ref, v_ref, g_ref, cos_ref, sin_ref, gnw_ref, *rest, n_casts):
    o_ref = rest[n_casts]
    state_ref = rest[-1]
    _cast_slabs(rest[:n_casts] + rest[n_casts + 1:-1], n_casts)
    c_len = RET_CHUNK
    ii = lax.broadcasted_iota(jnp.int32, (c_len, c_len), 0)
    jj = lax.broadcasted_iota(jnp.int32, (c_len, c_len), 1)
    diff = (ii - jj).astype(F32)
    icol = lax.broadcasted_iota(jnp.int32, (c_len, 1), 0).astype(F32)
    half = RET_KEY_DIM // 2
    state_ref[...] = jnp.zeros_like(state_ref)
    tables = []
    for hh in range(RET_HEADS_PER_STEP):
        lg = lg_ref[pl.program_id(1) * RET_HEADS_PER_STEP + hh]
        tables.append((
            jnp.where(diff >= 0.0, jnp.exp(jnp.maximum(diff, 0.0) * lg), 0.0),
            jnp.exp((icol + 1.0) * lg),
            jnp.exp((c_len - 1.0 - icol) * lg),
            jnp.exp(jnp.full((1, RET_VAL_DIM), float(c_len), F32) * lg)))

    def body(c, _):
        rows = pl.ds(pl.multiple_of(c * c_len, c_len), c_len)
        cos = cos_ref[rows, :]
        sin = sin_ref[rows, :]

        def rot(x):
            x1, x2 = x[:, :half], x[:, half:]
            return jnp.concatenate([x1 * cos - x2 * sin, x2 * cos + x1 * sin], axis=-1)

        for hh, (decay_in, xi, zeta, chunk_decay) in enumerate(tables):
            kcols = slice(hh * RET_KEY_DIM, (hh + 1) * RET_KEY_DIM)
            vcols = slice(hh * RET_VAL_DIM, (hh + 1) * RET_VAL_DIM)
            qr = rot(q_ref[rows, kcols])
            kr = rot(k_ref[rows, kcols]) * (RET_KEY_DIM ** -0.5)
            vb = v_ref[rows, vcols].astype(BF16)
            scores = _dot_nt(qr.astype(BF16), kr.astype(BF16)) * decay_in
            inner = _dot(scores.astype(BF16), vb)
            st = state_ref[hh]
            cross = _dot((qr * xi).astype(BF16), st.astype(BF16))
            kv = _dot((kr * zeta).T.astype(BF16), vb)
            state_ref[hh] = chunk_decay * st + kv
            y = inner + cross
            y = y * lax.rsqrt(jnp.mean(y * y, axis=-1, keepdims=True) + EPS) * gnw_ref[:, vcols]
            g = g_ref[rows, vcols]
            o_ref[rows, vcols] = (g * jax.nn.sigmoid(g) * y).astype(o_ref.dtype)
        return 0

    lax.fori_loop(0, SEQ // c_len, body, 0, unroll=4)


def _retention(proj, batch, gn_w, casts=()):
    pos = np.arange(SEQ, dtype=np.float64)
    half = RET_KEY_DIM // 2
    inv = np.exp(-np.linspace(0.0, 1.0, half) * math.log(ROPE_BASE)).astype(np.float32)
    ang = pos.astype(np.float32)[:, None] * inv[None, :]
    cos = jnp.asarray(np.cos(ang.astype(np.float64)), F32)
    sin = jnp.asarray(np.sin(ang.astype(np.float64)), F32)
    log_gamma = jnp.asarray(np.log(1.0 - 2.0 ** (-5.0 - np.arange(N_RET_HEADS))), F32)
    w = RET_KEY_DIM * RET_HEADS_PER_STEP

    def col_spec(col0):
        return pl.BlockSpec((SEQ, w), lambda b, h: (b, col0 // w + h))

    grid = (batch, N_RET_HEADS // RET_HEADS_PER_STEP)
    c_in, c_out, c_shapes = _cast_specs(casts, grid)
    return pl.pallas_call(
        functools.partial(_ret_kernel, n_casts=len(casts)),
        out_shape=[jax.ShapeDtypeStruct((batch * SEQ, RET_WIDTH), BF16)] + c_shapes,
        grid=grid,
        in_specs=[pl.BlockSpec(memory_space=pltpu.SMEM),
                  col_spec(COL_QR), col_spec(COL_KR), col_spec(COL_VR), col_spec(COL_GR),
                  pl.BlockSpec((SEQ, half), lambda b, h: (0, 0)),
                  pl.BlockSpec((SEQ, half), lambda b, h: (0, 0)),
                  pl.BlockSpec((1, w), lambda b, h: (0, h))] + c_in,
        out_specs=[pl.BlockSpec((SEQ, w), lambda b, h: (b, h))] + c_out,
        scratch_shapes=[pltpu.VMEM((RET_HEADS_PER_STEP, RET_KEY_DIM, RET_VAL_DIM), F32)],
        compiler_params=pltpu.CompilerParams(dimension_semantics=("arbitrary", "arbitrary"),
                                             vmem_limit_bytes=VMEM_LIMIT),
    )(log_gamma, proj, proj, proj, proj, cos, sin, gn_w.reshape(1, RET_WIDTH), *[c[0] for c in casts])


def _layer(x, norm1_w, w_in, nsa_q_norm_w, nsa_k_norm_w, cmp_pe_k, cmp_pe_v, w_cmp_k, w_cmp_v,
           rel_bias, ret_gn_w, w_out, norm2_w, w_gate, w_up, w_down):
    batch = x.shape[0]
    rows = batch * SEQ
    xf = x.reshape(rows, D_MODEL)
    w_in_t = w_in.T
    gate0 = W_IN_NSA_COLS
    gate1 = gate0 + W_IN_GATE_COLS
    h, gate_logits = _rmsnorm_with_projection(xf, norm1_w, w_in_t, gate0, LANES)
    ret_cols = 4 * RET_WIDTH
    nsa_tn, nsa_tm = 1024, 512
    nsa_steps = (W_IN_NSA_COLS // nsa_tn) * (rows // nsa_tm)
    ret_slabs = 1 << (nsa_steps.bit_length() - 1)
    proj_nsa, w_ret_t = _in_projection(h, w_in_t, 0, W_IN_NSA_COLS, nsa_tn, tm=nsa_tm,
                                       casts=((w_in_t, gate1, ret_cols, ret_slabs),))
    biasc, dm = _bias_tables(rel_bias)
    n_steps = batch * N_KV_GROUPS * NQ
    o_nsa, proj_ret, w_gate_b, w_up_b, w_out_b = _nsa_attention(
        proj_nsa, gate_logits, batch, nsa_q_norm_w, nsa_k_norm_w, cmp_pe_k, cmp_pe_v,
        w_cmp_k, w_cmp_v, biasc, dm, h, w_ret_t,
        casts=((w_gate, 0, D_MODEL, n_steps), (w_up, 0, D_MODEL, n_steps), (w_out, 0, D_MODEL, n_steps)))
    o_ret, = _retention(proj_ret, batch, ret_gn_w)
    x1, = _out_projection(o_nsa, o_ret, w_out_b, xf, tn=1024)
    hf = _rmsnorm(x1, norm2_w)
    up_tm, up_tn = 2048, 256
    up_steps = (rows // up_tm) * (D_FF // up_tn)
    mid, w_down_b = _ffn_up(hf, w_gate_b, w_up_b, tm=up_tm, tn=up_tn, casts=((w_down, 0, D_FF, up_steps),))
    out = _ffn_down(mid, w_down_b, x1)
    return out.reshape(batch, SEQ, D_MODEL)


def kernel(x, norm1_w, w_in, nsa_q_norm_w, nsa_k_norm_w, cmp_pe_k, cmp_pe_v, w_cmp_k, w_cmp_v,
           rel_bias, ret_gn_w, w_out, norm2_w, w_gate, w_up, w_down):
    for l in range(norm1_w.shape[0]):
        x = _layer(x, norm1_w[l], w_in[l], nsa_q_norm_w[l], nsa_k_norm_w[l], cmp_pe_k[l], cmp_pe_v[l],
                   w_cmp_k[l], w_cmp_v[l], rel_bias, ret_gn_w[l], w_out[l], norm2_w[l],
                   w_gate[l], w_up[l], w_down[l])
    return x
```

```python
import functools
import math

import numpy as np
import jax
import jax.numpy as jnp
from jax import lax
from jax.experimental import pallas as pl
from jax.experimental.pallas import tpu as pltpu

F32 = jnp.float32
BF16 = jnp.bfloat16

D_MODEL = 4096
SEQ = 2048
HEAD_DIM = 128
N_NSA_HEADS = 16
N_KV_GROUPS = 4
GROUP_SIZE = 4
N_BRANCH = 3
CMP_BLOCK = 32
CMP_STRIDE = 16
SEL_BLOCK = 64
TOP_N = 16
WINDOW = 512
RET_KEY_DIM = 256
RET_VAL_DIM = 256
N_RET_HEADS = 8
RET_CHUNK = 128
ROPE_BASE = 10000.0
NSA_WIDTH = 2048
RET_WIDTH = 2048
D_FF = 11008
NUM_BUCKETS = 32
MAX_DISTANCE = 128
EPS = 1e-6
NEG_INF = -1e30
FORCE_SCORE = 1e9
N_SEL = SEQ // SEL_BLOCK
N_CMP = (SEQ - CMP_BLOCK) // CMP_STRIDE + 1
SCALE = HEAD_DIM ** -0.5
LOG2E = math.log2(math.e)
EXP2_SCALE = SCALE * LOG2E

LANES = 128
VMEM_LIMIT = 56 * 1024 * 1024
NSA_VMEM_LIMIT = 60 * 1024 * 1024

TQ = 256
TK = 256
NQ = SEQ // TQ
RET_HEADS_PER_STEP = 2
FFN_UP_SUBBLOCKS = 4
V_ROWS = HEAD_DIM + 16
assert TQ == TK and WINDOW == 2 * TK and TK % SEL_BLOCK == 0 and N_SEL <= HEAD_DIM and TK >= MAX_DISTANCE

W_IN_NSA_COLS = NSA_WIDTH + N_BRANCH * 2 * N_KV_GROUPS * HEAD_DIM
W_IN_GATE_COLS = N_NSA_HEADS * N_BRANCH
COL_KV = NSA_WIDTH
COL_QR = 0
COL_KR = 2048
COL_VR = 4096
COL_GR = 6144


def _bucket_thresholds():
    rel = np.arange(0, 4 * MAX_DISTANCE)
    max_exact = NUM_BUCKETS // 2
    nf = np.maximum(rel, 1).astype(np.float64)
    large = max_exact + np.floor(np.log(nf / max_exact) / math.log(MAX_DISTANCE / max_exact)
                                 * (NUM_BUCKETS - max_exact)).astype(np.int64)
    large = np.minimum(large, NUM_BUCKETS - 1)
    b = np.where(rel < max_exact, rel, large)
    return [int(np.argmax(b >= k)) for k in range(NUM_BUCKETS)]


_THR = _bucket_thresholds()


def _dot(a, b):
    return jnp.dot(a, b, preferred_element_type=F32)


def _dot_nt(a, b):
    return lax.dot_general(a, b, (((1,), (1,)), ((), ())), preferred_element_type=F32)


def _rms_proj_kernel(x_ref, w_ref, pt_ref, o_ref, p_ref, pbf_ref):
    @pl.when(pl.program_id(0) == 0)
    def _():
        pbf_ref[...] = pt_ref[...].astype(BF16)

    x = x_ref[...]
    y = x * lax.rsqrt(jnp.mean(x * x, axis=-1, keepdims=True) + EPS)
    h = (y * w_ref[...]).astype(o_ref.dtype)
    o_ref[...] = h
    p_ref[...] = _dot_nt(h, pbf_ref[...])


def _rmsnorm_with_projection(x, w, w_t, row0, n_rows, tm=512):
    m, d = x.shape
    sub = 8
    assert row0 % sub == 0 and n_rows % LANES == 0
    return pl.pallas_call(
        _rms_proj_kernel,
        out_shape=(jax.ShapeDtypeStruct((m, d), BF16), jax.ShapeDtypeStruct((m, n_rows), F32)),
        grid=(m // tm,),
        in_specs=[pl.BlockSpec((tm, d), lambda i: (i, 0)),
                  pl.BlockSpec((1, d), lambda i: (0, 0)),
                  pl.BlockSpec((pl.Element(n_rows), pl.Element(d)), lambda i: ((row0 // sub) * sub, 0))],
        out_specs=(pl.BlockSpec((tm, d), lambda i: (i, 0)),
                   pl.BlockSpec((tm, n_rows), lambda i: (i, 0))),
        scratch_shapes=[pltpu.VMEM((n_rows, d), BF16)],
        compiler_params=pltpu.CompilerParams(dimension_semantics=("arbitrary",),
                                             vmem_limit_bytes=VMEM_LIMIT),
    )(x, w.reshape(1, d), w_t)


def _cast_spec(grid, row0, n_rows, n_cols, n_slabs):
    slab = n_rows // n_slabs
    sub = 8
    n_steps = math.prod(grid)
    assert slab * n_slabs == n_rows and slab % 16 == 0 and n_cols % LANES == 0
    assert row0 % sub == 0 and n_slabs <= n_steps

    def slab_index(ids):
        step = ids[0]
        for dim, idx in zip(grid[1:], ids[1:]):
            step = step * dim + idx
        return step if n_slabs == n_steps else jnp.minimum(step, n_slabs - 1)

    in_spec = pl.BlockSpec((pl.Element(slab), pl.Element(n_cols)),
                           lambda *ids: ((row0 // sub + slab_index(ids) * (slab // sub)) * sub, 0))
    out_spec = pl.BlockSpec((slab, n_cols), lambda *ids: (slab_index(ids), 0))
    return in_spec, out_spec, jax.ShapeDtypeStruct((n_rows, n_cols), BF16)


def _cast_specs(casts, grid):
    specs = [_cast_spec(grid, row0, n_rows, arr.shape[1], n_slabs) for arr, row0, n_rows, n_slabs in casts]
    return [s[0] for s in specs], [s[1] for s in specs], [s[2] for s in specs]


def _cast_slabs(refs, n_casts):
    for src, dst in zip(refs[:n_casts], refs[len(refs) - n_casts:]):
        dst[...] = src[...].astype(dst.dtype)


def _inproj_kernel(a_ref, wt_ref, *rest, n_casts):
    o_ref = rest[n_casts]
    wbf_ref = rest[-1]

    @pl.when(pl.program_id(1) == 0)
    def _():
        wbf_ref[...] = wt_ref[...].astype(BF16)

    o_ref[...] = _dot_nt(a_ref[...], wbf_ref[...])
    _cast_slabs(rest[:n_casts] + rest[n_casts + 1:-1], n_casts)


def _in_projection(a, w_t, row0, n_cols, tn, tm=1024, casts=()):
    m, k = a.shape
    grid = (n_cols // tn, m // tm)
    c_in, c_out, c_shapes = _cast_specs(casts, grid)
    sub = 8
    assert row0 % sub == 0 and tn % sub == 0
    return pl.pallas_call(
        functools.partial(_inproj_kernel, n_casts=len(casts)),
        out_shape=[jax.ShapeDtypeStruct((m, n_cols), F32)] + c_shapes,
        grid=grid,
        in_specs=[pl.BlockSpec((tm, k), lambda j, i: (i, 0)),
                  pl.BlockSpec((pl.Element(tn), pl.Element(k)),
                               lambda j, i: ((row0 // sub + j * (tn // sub)) * sub, 0),
                               pipeline_mode=pl.Buffered(1))] + c_in,
        out_specs=[pl.BlockSpec((tm, tn), lambda j, i: (i, j))] + c_out,
        scratch_shapes=[pltpu.VMEM((tn, k), BF16)],
        compiler_params=pltpu.CompilerParams(dimension_semantics=("arbitrary", "arbitrary"),
                                             vmem_limit_bytes=VMEM_LIMIT),
    )(a, w_t, *[c[0] for c in casts])


def _outproj_kernel(a1_ref, a2_ref, w1_ref, w2_ref, x_ref, o_ref, ob_ref, ssq_ref):
    acc = _dot(a1_ref[...], w1_ref[...]) + _dot(a2_ref[...], w2_ref[...])
    y = x_ref[...] + acc
    o_ref[...] = y
    ob_ref[...] = y.astype(ob_ref.dtype)
    part = jnp.broadcast_to(jnp.sum(y * y, axis=-1, keepdims=True), ssq_ref.shape)

    @pl.when(pl.program_id(1) == 0)
    def _():
        ssq_ref[...] = part

    @pl.when(pl.program_id(1) > 0)
    def _():
        ssq_ref[...] += part


def _out_projection(o_nsa, o_ret, w_out, x, tm=1024, tn=512):
    m = x.shape[0]
    kh = NSA_WIDTH
    grid = (m // tm, D_MODEL // tn)
    return pl.pallas_call(
        _outproj_kernel,
        out_shape=(jax.ShapeDtypeStruct((m, D_MODEL), F32), jax.ShapeDtypeStruct((m, D_MODEL), BF16),
                   jax.ShapeDtypeStruct((m, LANES), F32)),
        grid=grid,
        in_specs=[pl.BlockSpec((tm, kh), lambda i, j: (i, 0)),
                  pl.BlockSpec((tm, kh), lambda i, j: (i, 0)),
                  pl.BlockSpec((kh, tn), lambda i, j: (0, j)),
                  pl.BlockSpec((kh, tn), lambda i, j: (1, j)),
                  pl.BlockSpec((tm, tn), lambda i, j: (i, j))],
        out_specs=(pl.BlockSpec((tm, tn), lambda i, j: (i, j)),
                   pl.BlockSpec((tm, tn), lambda i, j: (i, j)),
                   pl.BlockSpec((tm, LANES), lambda i, j: (i, 0))),
        compiler_params=pltpu.CompilerParams(dimension_semantics=("arbitrary", "arbitrary"),
                                             vmem_limit_bytes=VMEM_LIMIT),
    )(o_nsa, o_ret, w_out, w_out, x)


def _ffn_up_kernel(x_ref, ssq_ref, wg_ref, wu_ref, *rest, n_casts):
    o_ref = rest[n_casts]
    sub_rows = x_ref.shape[0] // FFN_UP_SUBBLOCKS
    for s in range(FFN_UP_SUBBLOCKS):
        rows = slice(s * sub_rows, (s + 1) * sub_rows)
        inv_rms = lax.rsqrt(ssq_ref[rows, 0:1] * (1.0 / x_ref.shape[1]) + EPS)
        xb = x_ref[rows, :]
        g = _dot(xb, wg_ref[...]) * inv_rms
        u = _dot(xb, wu_ref[...]) * inv_rms
        o_ref[rows, :] = (g * jax.nn.sigmoid(g) * u).astype(o_ref.dtype)
    _cast_slabs(rest[:n_casts] + rest[n_casts + 1:], n_casts)


def _ffn_up(xb, ssq, w_gate, w_up, tm=1024, tn=256, casts=()):
    m, k = xb.shape
    n = w_gate.shape[1]
    grid = (m // tm, n // tn)
    c_in, c_out, c_shapes = _cast_specs(casts, grid)
    return pl.pallas_call(
        functools.partial(_ffn_up_kernel, n_casts=len(casts)),
        out_shape=[jax.ShapeDtypeStruct((m, n), BF16)] + c_shapes,
        grid=grid,
        in_specs=[pl.BlockSpec((tm, k), lambda i, j: (i, 0)),
                  pl.BlockSpec((tm, ssq.shape[1]), lambda i, j: (i, 0)),
                  pl.BlockSpec((k, tn), lambda i, j: (0, j)),
                  pl.BlockSpec((k, tn), lambda i, j: (0, j))] + c_in,
        out_specs=[pl.BlockSpec((tm, tn), lambda i, j: (i, j))] + c_out,
        compiler_params=pltpu.CompilerParams(dimension_semantics=("arbitrary", "arbitrary"),
                                             vmem_limit_bytes=VMEM_LIMIT),
    )(xb, ssq, w_gate, w_up, *[c[0] for c in casts])


def _ffn_down_kernel(a_ref, b_ref, r_ref, o_ref):
    o_ref[...] = r_ref[...] + _dot(a_ref[...], b_ref[...])


def _ffn_down(a, w_down, x, tm=1024, tn=512, tk=5504):
    m, k = a.shape
    n = w_down.shape[1]
    res = x
    for kk in range(k // tk):
        res = pl.pallas_call(
            _ffn_down_kernel,
            out_shape=jax.ShapeDtypeStruct((m, n), F32),
            grid=(m // tm, n // tn),
            in_specs=[pl.BlockSpec((tm, tk), lambda i, j, kk=kk: (i, kk)),
                      pl.BlockSpec((tk, tn), lambda i, j, kk=kk: (kk, j)),
                      pl.BlockSpec((tm, tn), lambda i, j: (i, j))],
            out_specs=pl.BlockSpec((tm, tn), lambda i, j: (i, j)),
            compiler_params=pltpu.CompilerParams(dimension_semantics=("arbitrary", "arbitrary"),
                                                 vmem_limit_bytes=VMEM_LIMIT),
        )(a, w_down, res)
    return res


def _bias_value(rel, relb_ref, h):
    val = jnp.full(rel.shape, relb_ref[0, h], F32)
    for k in range(1, NUM_BUCKETS):
        val = jnp.where(rel >= _THR[k], relb_ref[k, h], val)
    return val


def _bias_tables_kernel(relb_ref, biasc_ref, dm_ref):
    h = pl.program_id(0)
    sub = 8
    j = lax.broadcasted_iota(jnp.int32, (sub, 2 * SEQ), 1)
    row_c = _bias_value(j - SEQ - (CMP_BLOCK - 1), relb_ref, h)[0:1]
    toe_c = pltpu.roll(jnp.broadcast_to(row_c, (LANES, 2 * SEQ)), 0, 1,
                       stride=CMP_STRIDE, stride_axis=0)
    for qt in range(NQ):
        biasc_ref[0, qt] = toe_c[:, SEQ + qt * TQ:SEQ + (qt + 1) * TQ]
    far = relb_ref[NUM_BUCKETS - 1, h]
    j2 = lax.broadcasted_iota(jnp.int32, (sub, 2 * TK), 1)
    b = lax.broadcasted_iota(jnp.int32, (TK, TQ), 0)
    a = lax.broadcasted_iota(jnp.int32, (TK, TQ), 1)
    for kind in range(2):
        row = ((_bias_value(j2 + (kind - 1) * TK, relb_ref, h) - far) * LOG2E)[0:1]
        toe = pltpu.roll(jnp.broadcast_to(row, (TK, 2 * TK)), 0, 1, stride=1, stride_axis=0)
        tile = toe[:, TK:]
        dm_ref[0, kind] = tile + jnp.where(a >= b, 0.0, NEG_INF) if kind == 0 else tile
    dm_ref[0, 2] = jnp.where(b > a, 0.0, NEG_INF)


def _bias_tables(rel_bias):
    return pl.pallas_call(
        _bias_tables_kernel,
        out_shape=(jax.ShapeDtypeStruct((N_KV_GROUPS, NQ, LANES, GROUP_SIZE * TQ), F32),
                   jax.ShapeDtypeStruct((N_KV_GROUPS, 3, TK, GROUP_SIZE * TQ), F32)),
        grid=(N_NSA_HEADS,),
        in_specs=[pl.BlockSpec(memory_space=pltpu.SMEM)],
        out_specs=(pl.BlockSpec((1, NQ, LANES, TQ), lambda h: (h // GROUP_SIZE, 0, 0, h % GROUP_SIZE)),
                   pl.BlockSpec((1, 3, TK, TQ), lambda h: (h // GROUP_SIZE, 0, 0, h % GROUP_SIZE))),
        compiler_params=pltpu.CompilerParams(dimension_semantics=("arbitrary",),
                                             vmem_limit_bytes=VMEM_LIMIT),
    )(rel_bias)


def _row_rms(x, w):
    return x * lax.rsqrt(jnp.mean(x * x, axis=-1, keepdims=True) + EPS) * w


N_NSA_INPUTS = 17
N_NSA_SCRATCH = 7


def _nsa_kernel(*refs, n_casts, n_scaled):
    (q_ref, k0_ref, v0_ref, k1_ref, v1_ref, k2_ref, v2_ref, gate_ref,
     qnw_ref, knw_ref, pek_ref, pev_ref, wck_ref, wcv_ref, biasc_ref, dm_ref, c2st_ref) = refs[:N_NSA_INPUTS]
    side_a_ref, side_bt_ref, rowscale_ref = refs[N_NSA_INPUTS:N_NSA_INPUTS + 3]
    n_in = N_NSA_INPUTS + 3 + n_casts
    cast_in = refs[N_NSA_INPUTS + 3:n_in]
    o_ref, side_o_ref = refs[n_in:n_in + 2]
    cast_out = refs[n_in + 2:n_in + 2 + n_casts]
    kc_s, vct_s, kaug_s, kw_s, vst_s, vwt_s, gt_s = refs[len(refs) - N_NSA_SCRATCH:]
    qt = pl.program_id(2)
    cols_all = GROUP_SIZE * TQ
    side_half = side_bt_ref.shape[0] // 2

    @pl.when(qt == 0)
    def _prologue():
        kaug_s[:, :HEAD_DIM] = (_row_rms(k1_ref[...], knw_ref[1:2, :]) * EXP2_SCALE).astype(BF16)
        jblk = lax.broadcasted_iota(jnp.int32, (SEQ, LANES), 0) // SEL_BLOCK
        lane = lax.broadcasted_iota(jnp.int32, (SEQ, LANES), 1)
        kaug_s[:, HEAD_DIM:] = jnp.where(jblk == lane, 1.0, 0.0).astype(BF16)
        kw_s[...] = (_row_rms(k2_ref[...], knw_ref[2:3, :]) * EXP2_SCALE).astype(BF16)
        ones_row = jnp.where(lax.broadcasted_iota(jnp.int32, (V_ROWS - HEAD_DIM, TK), 0) == 0,
                             1.0, 0.0).astype(BF16)
        for t in range(SEQ // TK):
            vst_s[t, :HEAD_DIM, :] = v1_ref[t * TK:(t + 1) * TK, :].T.astype(BF16)
            vst_s[t, HEAD_DIM:, :] = ones_row
            vwt_s[t, :HEAD_DIM, :] = v2_ref[t * TK:(t + 1) * TK, :].T.astype(BF16)
            vwt_s[t, HEAD_DIM:, :] = ones_row

        def compress(x_ref, pe_ref, w_ref):
            acc_lo = jnp.zeros((LANES, HEAD_DIM), F32)
            acc_hi = jnp.zeros((LANES, HEAD_DIM), F32)
            for j in range(CMP_STRIDE):
                y = x_ref[pl.ds(j, LANES, stride=CMP_STRIDE), :]
                acc_lo += _dot((y + pe_ref[j:j + 1, :]).astype(BF16),
                               w_ref[j * HEAD_DIM:(j + 1) * HEAD_DIM, :])
                jh = CMP_STRIDE + j
                acc_hi += _dot((y + pe_ref[jh:jh + 1, :]).astype(BF16),
                               w_ref[jh * HEAD_DIM:(jh + 1) * HEAD_DIM, :])
            return acc_lo + pltpu.roll(acc_hi, LANES - 1, 0)

        kc_s[...] = _row_rms(compress(k0_ref, pek_ref, wck_ref), knw_ref[0:1, :]).astype(BF16)
        vct_s[...] = compress(v0_ref, pev_ref, wcv_ref).T.astype(BF16)

    side_o_ref[:, :side_half] = _dot_nt(side_a_ref[...], side_bt_ref[:side_half, :])
    for i, (src, dst) in enumerate(zip(cast_in, cast_out)):
        v = src[...] * rowscale_ref[:, 0:1] if i < n_scaled else src[...]
        dst[...] = v.astype(dst.dtype)

    t0 = qt * TQ
    q = q_ref[...]
    qts = []
    for r in range(GROUP_SIZE):
        qts.append(_row_rms(q[:, r * HEAD_DIM:(r + 1) * HEAD_DIM], qnw_ref[...]).T)
    qt_stack = jnp.concatenate(qts, axis=1).astype(BF16)

    m0 = jnp.full((1, cols_all), NEG_INF, F32)
    a0 = jnp.zeros((V_ROWS, cols_all), F32)

    def online(carry, tiles):
        m, acc = carry
        m_new = m
        for u, _ in tiles:
            m_new = jnp.maximum(m_new, jnp.max(u, axis=0, keepdims=True))
        acc = jnp.exp2(m - m_new) * acc
        for u, vt in tiles:
            acc = acc + _dot(vt, jnp.exp2(u - m_new).astype(BF16))
        return m_new, acc

    def online_independent(carry, tiles):
        m, acc = carry
        parts = []
        for u, vt in tiles:
            mt = jnp.max(u, axis=0, keepdims=True)
            parts.append((mt, _dot(vt, jnp.exp2(u - mt).astype(BF16))))
        m_new = m
        for mt, _ in parts:
            m_new = jnp.maximum(m_new, mt)
        acc = jnp.exp2(m - m_new) * acc
        for mt, pv in parts:
            acc = acc + jnp.exp2(mt - m_new) * pv
        return m_new, acc

    def normalised(acc):
        return acc[:HEAD_DIM] * (1.0 / acc[HEAD_DIM:HEAD_DIM + 1])

    def key_rows(kt):
        return pl.ds(pl.multiple_of(kt * TK, TK), TK)

    sc = _dot(kc_s[...], qt_stack) * SCALE + biasc_ref[0, 0]
    tpos = t0 + (lax.broadcasted_iota(jnp.int32, (LANES, cols_all), 1) & (TQ - 1))
    nrow = lax.broadcasted_iota(jnp.int32, (LANES, cols_all), 0)
    maskc = tpos >= CMP_STRIDE * nrow + (CMP_BLOCK - 1)
    scm = jnp.where(maskc, sc, NEG_INF)
    mc = jnp.max(scm, axis=0, keepdims=True)
    ec = jnp.where(maskc, jnp.exp(scm - mc), 0.0)
    lc = jnp.sum(ec, axis=0, keepdims=True)
    pc = ec * (1.0 / jnp.where(lc > 0.0, lc, 1.0))
    o_c = _dot(vct_s[...], pc.astype(BF16))

    psum = pc[:, 0:TQ] + pc[:, TQ:2 * TQ] + pc[:, 2 * TQ:3 * TQ] + pc[:, 3 * TQ:4 * TQ]
    p_hi = psum.astype(BF16)
    p_lo = (psum - p_hi.astype(F32)).astype(BF16)
    imp = (_dot(c2st_ref[...], p_hi) + _dot(c2st_ref[...], p_lo))[0:N_SEL]

    tiles = []
    for dist, kind in ((2, 2), (1, 1), (0, 0)):
        kt = jnp.maximum(qt - dist, 0)
        u = _dot(kw_s[key_rows(kt), :], qt_stack) + dm_ref[0, kind]
        if dist > 0:
            u = u + jnp.where(qt >= dist, 0.0, NEG_INF)
        tiles.append((u, vwt_s[kt]))
    _, acc_w = online_independent((m0, a0), tiles)
    o_w = normalised(acc_w)

    tq_pos = t0 + lax.broadcasted_iota(jnp.int32, (N_SEL, TQ), 1)
    sidx = lax.broadcasted_iota(jnp.int32, (N_SEL, TQ), 0)
    cur = tq_pos >> (SEL_BLOCK.bit_length() - 1)
    score = jnp.where(sidx <= cur, imp, NEG_INF)
    for forced_blk in (cur - 1, cur, jnp.zeros_like(cur)):
        score = jnp.where(sidx == forced_blk, FORCE_SCORE, score)
    rank = jnp.zeros((N_SEL, TQ), F32)
    for sp in range(N_SEL):
        row = score[sp:sp + 1, :]
        tie = jnp.where(sidx > sp, 1.0, 0.0)
        rank = rank + jnp.where(row > score, 1.0, jnp.where(row == score, tie, 0.0))
    selb = jnp.where(rank < TOP_N, 0.0, NEG_INF)
    selb = jnp.concatenate([selb, jnp.zeros((HEAD_DIM - N_SEL, TQ), F32)], axis=0).astype(BF16)
    qt_aug = jnp.concatenate([qt_stack, jnp.concatenate([selb] * GROUP_SIZE, axis=1)], axis=0)

    def sel_tile(kt):
        return _dot(kaug_s[key_rows(kt), :], qt_aug), vst_s[kt]

    n_far = jnp.maximum(qt - 1, 0)
    carry = lax.fori_loop(0, n_far >> 1,
                          lambda i, c: online_independent(c, [sel_tile(2 * i), sel_tile(2 * i + 1)]),
                          (m0, a0))
    carry = lax.fori_loop(0, n_far & 1, lambda i, c: online(c, [sel_tile(n_far - 1)]), carry)
    kt1 = jnp.maximum(qt - 1, 0)
    u1, vt1 = sel_tile(kt1)
    u0, vt0 = sel_tile(qt)
    _, acc_s = online_independent(
        carry, [(u1 + dm_ref[0, 1] + jnp.where(qt >= 1, 0.0, NEG_INF), vt1), (u0 + dm_ref[0, 0], vt0)])
    o_s = normalised(acc_s)

    side_o_ref[:, side_half:] = _dot_nt(side_a_ref[...], side_bt_ref[side_half:, :])

    gt_s[...] = jax.nn.sigmoid(gate_ref[...]).T
    head0 = pl.program_id(1) * GROUP_SIZE
    for r in range(GROUP_SIZE):
        cs = slice(r * TQ, (r + 1) * TQ)
        c = (head0 + r) * N_BRANCH
        o = (gt_s[pl.ds(c, 1), :] * o_c[:, cs] + gt_s[pl.ds(c + 1, 1), :] * o_s[:, cs]
             + gt_s[pl.ds(c + 2, 1), :] * o_w[:, cs])
        o_ref[:, r * HEAD_DIM:(r + 1) * HEAD_DIM] = o.T.astype(o_ref.dtype)


def _cmp_to_sel_t():
    ss = np.arange(LANES)[:, None] * SEL_BLOCK
    cs = np.arange(LANES)[None, :] * CMP_STRIDE
    ov = np.clip(np.minimum(cs + CMP_BLOCK, ss + SEL_BLOCK) - np.maximum(cs, ss), 0, None)
    m = ov.astype(np.float32) / np.float32(CMP_BLOCK)
    m[N_SEL:, :] = 0.0
    m[:, N_CMP:] = 0.0
    return jnp.asarray(m, BF16)


def _nsa_attention(proj, gate_logits, batch, q_norm_w, k_norm_w, pe_k, pe_v, w_cmp_k, w_cmp_v, biasc, dm,
                   side_a, side_bt, row_scale, n_scaled, side_tm=1024, side_tn=512, casts=()):
    kvb = COL_KV // LANES

    def kv_spec(branch, which):
        base = kvb + (branch * 2 + which) * N_KV_GROUPS
        return pl.BlockSpec((SEQ, LANES), lambda b, g, qt: (b, base + g))

    full = lambda shape: pl.BlockSpec(shape, lambda b, g, qt: (0,) * len(shape))
    grid = (batch, N_KV_GROUPS, NQ)
    c_in, c_out, c_shapes = _cast_specs(casts, grid)
    side_m, side_k = side_a.shape
    side_n = side_bt.shape[0]
    n_j = side_n // side_tn
    assert (side_m // side_tm) * n_j == math.prod(grid)
    step = lambda b, g, qt: (b * N_KV_GROUPS + g) * NQ + qt
    side_in = [pl.BlockSpec((side_tm, side_k), lambda b, g, qt: (step(b, g, qt) // n_j, 0),
                            pipeline_mode=pl.Buffered(1)),
               pl.BlockSpec((side_tn, side_k), lambda b, g, qt: (step(b, g, qt) % n_j, 0)),
               pl.BlockSpec((row_scale.shape[0] // math.prod(grid), LANES), lambda b, g, qt: (step(b, g, qt), 0))]
    side_out = pl.BlockSpec((side_tm, side_tn), lambda b, g, qt: (step(b, g, qt) // n_j, step(b, g, qt) % n_j))
    return pl.pallas_call(
        functools.partial(_nsa_kernel, n_casts=len(casts), n_scaled=n_scaled),
        out_shape=[jax.ShapeDtypeStruct((batch * SEQ, NSA_WIDTH), BF16),
                   jax.ShapeDtypeStruct((side_m, side_n), F32)] + c_shapes,
        grid=grid,
        in_specs=[pl.BlockSpec((TQ, GROUP_SIZE * HEAD_DIM), lambda b, g, qt: (b * NQ + qt, g)),
                  kv_spec(0, 0), kv_spec(0, 1), kv_spec(1, 0), kv_spec(1, 1),
                  kv_spec(2, 0), kv_spec(2, 1),
                  pl.BlockSpec((TQ, LANES), lambda b, g, qt: (b * NQ + qt, 0)),
                  full((1, HEAD_DIM)), full((N_BRANCH, HEAD_DIM)),
                  full((CMP_BLOCK, HEAD_DIM)), full((CMP_BLOCK, HEAD_DIM)),
                  full((CMP_BLOCK * HEAD_DIM, HEAD_DIM)), full((CMP_BLOCK * HEAD_DIM, HEAD_DIM)),
                  pl.BlockSpec((1, 1, LANES, GROUP_SIZE * TQ), lambda b, g, qt: (g, qt, 0, 0)),
                  pl.BlockSpec((1, 3, TK, GROUP_SIZE * TQ), lambda b, g, qt: (g, 0, 0, 0),
                               pipeline_mode=pl.Buffered(1)),
                  full((LANES, LANES))] + side_in + c_in,
        out_specs=[pl.BlockSpec((TQ, GROUP_SIZE * HEAD_DIM), lambda b, g, qt: (b * NQ + qt, g)),
                   side_out] + c_out,
        scratch_shapes=[pltpu.VMEM((LANES, HEAD_DIM), BF16),
                        pltpu.VMEM((HEAD_DIM, LANES), BF16),
                        pltpu.VMEM((SEQ, 2 * HEAD_DIM), BF16),
                        pltpu.VMEM((SEQ, HEAD_DIM), BF16),
                        pltpu.VMEM((SEQ // TK, V_ROWS, TK), BF16),
                        pltpu.VMEM((SEQ // TK, V_ROWS, TK), BF16),
                        pltpu.VMEM((LANES, TQ), F32)],
        compiler_params=pltpu.CompilerParams(
            dimension_semantics=("arbitrary", "arbitrary", "arbitrary"),
            vmem_limit_bytes=NSA_VMEM_LIMIT),
    )(proj, proj, proj, proj, proj, proj, proj, gate_logits,
      q_norm_w.reshape(1, HEAD_DIM), k_norm_w, pe_k, pe_v,
      w_cmp_k.astype(BF16), w_cmp_v.astype(BF16), biasc, dm, _cmp_to_sel_t(),
      side_a, side_bt, row_scale, *[c[0] for c in casts])


def _ret_kernel(lg_ref, q_ref, k_ref, v_ref, g_ref, cos_ref, sin_ref, gnw_ref, *rest, n_casts):
    o_ref = rest[n_casts]
    state_ref = rest[-1]
    _cast_slabs(rest[:n_casts] + rest[n_casts + 1:-1], n_casts)
    c_len = RET_CHUNK
    ii = lax.broadcasted_iota(jnp.int32, (c_len, c_len), 0)
    jj = lax.broadcasted_iota(jnp.int32, (c_len, c_len), 1)
    diff = (ii - jj).astype(F32)
    icol = lax.broadcasted_iota(jnp.int32, (c_len, 1), 0).astype(F32)
    half = RET_KEY_DIM // 2
    state_ref[...] = jnp.zeros_like(state_ref)
    tables = []
    for hh in range(RET_HEADS_PER_STEP):
        lg = lg_ref[pl.program_id(1) * RET_HEADS_PER_STEP + hh]
        tables.append((
            jnp.where(diff >= 0.0, jnp.exp(jnp.maximum(diff, 0.0) * lg), 0.0),
            jnp.exp((icol + 1.0) * lg),
            jnp.exp((c_len - 1.0 - icol) * lg),
            jnp.exp(jnp.full((1, RET_VAL_DIM), float(c_len), F32) * lg)))

    def body(c, _):
        rows = pl.ds(pl.multiple_of(c * c_len, c_len), c_len)
        cos = cos_ref[rows, :]
        sin = sin_ref[rows, :]

        def rot(x):
            x1, x2 = x[:, :half], x[:, half:]
            return jnp.concatenate([x1 * cos - x2 * sin, x2 * cos + x1 * sin], axis=-1)

        for hh, (decay_in, xi, zeta, chunk_decay) in enumerate(tables):
            kcols = slice(hh * RET_KEY_DIM, (hh + 1) * RET_KEY_DIM)
            vcols = slice(hh * RET_VAL_DIM, (hh + 1) * RET_VAL_DIM)
            qr = rot(q_ref[rows, kcols])
            kr = rot(k_ref[rows, kcols]) * (RET_KEY_DIM ** -0.5)
            vb = v_ref[rows, vcols].astype(BF16)
            scores = _dot_nt(qr.astype(BF16), kr.astype(BF16)) * decay_in
            inner = _dot(scores.astype(BF16), vb)
            st = state_ref[hh]
            cross = _dot((qr * xi).astype(BF16), st.astype(BF16))
            kv = _dot((kr * zeta).T.astype(BF16), vb)
            state_ref[hh] = chunk_decay * st + kv
            y = inner + cross
            y = y * lax.rsqrt(jnp.mean(y * y, axis=-1, keepdims=True) + EPS) * gnw_ref[:, vcols]
            g = g_ref[rows, vcols]
            o_ref[rows, vcols] = (g * jax.nn.sigmoid(g) * y).astype(o_ref.dtype)
        return 0

    lax.fori_loop(0, SEQ // c_len, body, 0, unroll=4)


def _retention(proj, batch, gn_w, casts=()):
    pos = np.arange(SEQ, dtype=np.float64)
    half = RET_KEY_DIM // 2
    inv = np.exp(-np.linspace(0.0, 1.0, half) * math.log(ROPE_BASE)).astype(np.float32)
    ang = pos.astype(np.float32)[:, None] * inv[None, :]
    cos = jnp.asarray(np.cos(ang.astype(np.float64)), F32)
    sin = jnp.asarray(np.sin(ang.astype(np.float64)), F32)
    log_gamma = jnp.asarray(np.log(1.0 - 2.0 ** (-5.0 - np.arange(N_RET_HEADS))), F32)
    w = RET_KEY_DIM * RET_HEADS_PER_STEP

    def col_spec(col0):
        return pl.BlockSpec((SEQ, w), lambda b, h: (b, col0 // w + h))

    grid = (batch, N_RET_HEADS // RET_HEADS_PER_STEP)
    c_in, c_out, c_shapes = _cast_specs(casts, grid)
    return pl.pallas_call(
        functools.partial(_ret_kernel, n_casts=len(casts)),
        out_shape=[jax.ShapeDtypeStruct((batch * SEQ, RET_WIDTH), BF16)] + c_shapes,
        grid=grid,
        in_specs=[pl.BlockSpec(memory_space=pltpu.SMEM),
                  col_spec(COL_QR), col_spec(COL_KR), col_spec(COL_VR), col_spec(COL_GR),
                  pl.BlockSpec((SEQ, half), lambda b, h: (0, 0)),
                  pl.BlockSpec((SEQ, half), lambda b, h: (0, 0)),
                  pl.BlockSpec((1, w), lambda b, h: (0, h))] + c_in,
        out_specs=[pl.BlockSpec((SEQ, w), lambda b, h: (b, h))] + c_out,
        scratch_shapes=[pltpu.VMEM((RET_HEADS_PER_STEP, RET_KEY_DIM, RET_VAL_DIM), F32)],
        compiler_params=pltpu.CompilerParams(dimension_semantics=("arbitrary", "arbitrary"),
                                             vmem_limit_bytes=VMEM_LIMIT),
    )(log_gamma, proj, proj, proj, proj, cos, sin, gn_w.reshape(1, RET_WIDTH), *[c[0] for c in casts])


def _layer(x, norm1_w, w_in, nsa_q_norm_w, nsa_k_norm_w, cmp_pe_k, cmp_pe_v, w_cmp_k, w_cmp_v,
           rel_bias, ret_gn_w, w_out, norm2_w, w_gate, w_up, w_down):
    batch = x.shape[0]
    rows = batch * SEQ
    xf = x.reshape(rows, D_MODEL)
    w_in_t = w_in.T
    gate0 = W_IN_NSA_COLS
    gate1 = gate0 + W_IN_GATE_COLS
    h, gate_logits = _rmsnorm_with_projection(xf, norm1_w, w_in_t, gate0, LANES)
    ret_cols = 4 * RET_WIDTH
    nsa_tn, nsa_tm = 1024, 512
    nsa_steps = (W_IN_NSA_COLS // nsa_tn) * (rows // nsa_tm)
    ret_slabs = 1 << (nsa_steps.bit_length() - 1)
    proj_nsa, w_ret_t = _in_projection(h, w_in_t, 0, W_IN_NSA_COLS, nsa_tn, tm=nsa_tm,
                                       casts=((w_in_t, gate1, ret_cols, ret_slabs),))
    biasc, dm = _bias_tables(rel_bias)
    n_steps = batch * N_KV_GROUPS * NQ
    norm2_rows = jnp.broadcast_to(norm2_w[:, None], (D_MODEL, LANES))
    o_nsa, proj_ret, w_gate_b, w_up_b, w_out_b = _nsa_attention(
        proj_nsa, gate_logits, batch, nsa_q_norm_w, nsa_k_norm_w, cmp_pe_k, cmp_pe_v,
        w_cmp_k, w_cmp_v, biasc, dm, h, w_ret_t, norm2_rows, 2,
        casts=((w_gate, 0, D_MODEL, n_steps), (w_up, 0, D_MODEL, n_steps), (w_out, 0, D_MODEL, n_steps)))
    o_ret, = _retention(proj_ret, batch, ret_gn_w)
    x1, x1_b, x1_ssq = _out_projection(o_nsa, o_ret, w_out_b, xf)
    up_tm, up_tn = 2048, 256
    up_steps = (rows // up_tm) * (D_FF // up_tn)
    mid, w_down_b = _ffn_up(x1_b, x1_ssq, w_gate_b, w_up_b, tm=up_tm, tn=up_tn,
                            casts=((w_down, 0, D_FF, up_steps),))
    out = _ffn_down(mid, w_down_b, x1)
    return out.reshape(batch, SEQ, D_MODEL)


def kernel(x, norm1_w, w_in, nsa_q_norm_w, nsa_k_norm_w, cmp_pe_k, cmp_pe_v, w_cmp_k, w_cmp_v,
           rel_bias, ret_gn_w, w_out, norm2_w, w_gate, w_up, w_down):
    for l in range(norm1_w.shape[0]):
        x = _layer(x, norm1_w[l], w_in[l], nsa_q_norm_w[l], nsa_k_norm_w[l], cmp_pe_k[l], cmp_pe_v[l],
                   w_cmp_k[l], w_cmp_v[l], rel_bias, ret_gn_w[l], w_out[l], norm2_w[l],
                   w_gate[l], w_up[l], w_down[l])
    return x
```

```python
import functools
import math

import numpy as np
import jax
import jax.numpy as jnp
from jax import lax
from jax.experimental import pallas as pl
from jax.experimental.pallas import tpu as pltpu

F32 = jnp.float32
BF16 = jnp.bfloat16

D_MODEL = 4096
SEQ = 2048
HEAD_DIM = 128
N_NSA_HEADS = 16
N_KV_GROUPS = 4
GROUP_SIZE = 4
N_BRANCH = 3
CMP_BLOCK = 32
CMP_STRIDE = 16
SEL_BLOCK = 64
TOP_N = 16
WINDOW = 512
RET_KEY_DIM = 256
RET_VAL_DIM = 256
N_RET_HEADS = 8
RET_CHUNK = 128
ROPE_BASE = 10000.0
NSA_WIDTH = 2048
RET_WIDTH = 2048
D_FF = 11008
NUM_BUCKETS = 32
MAX_DISTANCE = 128
EPS = 1e-6
NEG_INF = -1e30
FORCE_SCORE = 1e9
N_SEL = SEQ // SEL_BLOCK
N_CMP = (SEQ - CMP_BLOCK) // CMP_STRIDE + 1
SCALE = HEAD_DIM ** -0.5
LOG2E = math.log2(math.e)
EXP2_SCALE = SCALE * LOG2E

LANES = 128
VMEM_LIMIT = 56 * 1024 * 1024
NSA_VMEM_LIMIT = 60 * 1024 * 1024

TQ = 256
TK = 256
NQ = SEQ // TQ
RET_HEADS_PER_STEP = 2
FFN_UP_SUBBLOCKS = 4
V_ROWS = HEAD_DIM + 16
assert TQ == TK and WINDOW == 2 * TK and TK % SEL_BLOCK == 0 and N_SEL <= HEAD_DIM and TK >= MAX_DISTANCE

W_IN_NSA_COLS = NSA_WIDTH + N_BRANCH * 2 * N_KV_GROUPS * HEAD_DIM
W_IN_GATE_COLS = N_NSA_HEADS * N_BRANCH
COL_KV = NSA_WIDTH
COL_QR = 0
COL_KR = 2048
COL_VR = 4096
COL_GR = 6144


def _bucket_thresholds():
    rel = np.arange(0, 4 * MAX_DISTANCE)
    max_exact = NUM_BUCKETS // 2
    nf = np.maximum(rel, 1).astype(np.float64)
    large = max_exact + np.floor(np.log(nf / max_exact) / math.log(MAX_DISTANCE / max_exact)
                                 * (NUM_BUCKETS - max_exact)).astype(np.int64)
    large = np.minimum(large, NUM_BUCKETS - 1)
    b = np.where(rel < max_exact, rel, large)
    return [int(np.argmax(b >= k)) for k in range(NUM_BUCKETS)]


_THR = _bucket_thresholds()


def _dot(a, b):
    return jnp.dot(a, b, preferred_element_type=F32)


def _dot_nt(a, b):
    return lax.dot_general(a, b, (((1,), (1,)), ((), ())), preferred_element_type=F32)


def _rms_kernel(x_ref, w_ref, o_ref):
    x = x_ref[...]
    y = x * lax.rsqrt(jnp.mean(x * x, axis=-1, keepdims=True) + EPS)
    o_ref[...] = (y * w_ref[...]).astype(o_ref.dtype)


def _rmsnorm(x, w, tm=512):
    m, d = x.shape
    return pl.pallas_call(
        _rms_kernel,
        out_shape=jax.ShapeDtypeStruct((m, d), BF16),
        grid=(m // tm,),
        in_specs=[pl.BlockSpec((tm, d), lambda i: (i, 0)),
                  pl.BlockSpec((1, d), lambda i: (0, 0))],
        out_specs=pl.BlockSpec((tm, d), lambda i: (i, 0)),
        compiler_params=pltpu.CompilerParams(dimension_semantics=("arbitrary",),
                                             vmem_limit_bytes=VMEM_LIMIT),
    )(x, w.reshape(1, d))


def _rms_proj_kernel(x_ref, w_ref, pt_ref, o_ref, p_ref, pbf_ref):
    @pl.when(pl.program_id(0) == 0)
    def _():
        pbf_ref[...] = pt_ref[...].astype(BF16)

    x = x_ref[...]
    y = x * lax.rsqrt(jnp.mean(x * x, axis=-1, keepdims=True) + EPS)
    h = (y * w_ref[...]).astype(o_ref.dtype)
    o_ref[...] = h
    p_ref[...] = _dot_nt(h, pbf_ref[...])


def _rmsnorm_with_projection(x, w, w_t, row0, n_rows, tm=512):
    m, d = x.shape
    sub = 8
    assert row0 % sub == 0 and n_rows % LANES == 0
    return pl.pallas_call(
        _rms_proj_kernel,
        out_shape=(jax.ShapeDtypeStruct((m, d), BF16), jax.ShapeDtypeStruct((m, n_rows), F32)),
        grid=(m // tm,),
        in_specs=[pl.BlockSpec((tm, d), lambda i: (i, 0)),
                  pl.BlockSpec((1, d), lambda i: (0, 0)),
                  pl.BlockSpec((pl.Element(n_rows), pl.Element(d)), lambda i: ((row0 // sub) * sub, 0))],
        out_specs=(pl.BlockSpec((tm, d), lambda i: (i, 0)),
                   pl.BlockSpec((tm, n_rows), lambda i: (i, 0))),
        scratch_shapes=[pltpu.VMEM((n_rows, d), BF16)],
        compiler_params=pltpu.CompilerParams(dimension_semantics=("arbitrary",),
                                             vmem_limit_bytes=VMEM_LIMIT),
    )(x, w.reshape(1, d), w_t)


def _cast_spec(grid, row0, n_rows, n_cols, n_slabs):
    slab = n_rows // n_slabs
    sub = 8
    n_steps = math.prod(grid)
    assert slab * n_slabs == n_rows and slab % 16 == 0 and n_cols % LANES == 0
    assert row0 % sub == 0 and n_slabs <= n_steps

    def slab_index(ids):
        step = ids[0]
        for dim, idx in zip(grid[1:], ids[1:]):
            step = step * dim + idx
        return step if n_slabs == n_steps else jnp.minimum(step, n_slabs - 1)

    in_spec = pl.BlockSpec((pl.Element(slab), pl.Element(n_cols)),
                           lambda *ids: ((row0 // sub + slab_index(ids) * (slab // sub)) * sub, 0))
    out_spec = pl.BlockSpec((slab, n_cols), lambda *ids: (slab_index(ids), 0))
    return in_spec, out_spec, jax.ShapeDtypeStruct((n_rows, n_cols), BF16)


def _cast_specs(casts, grid):
    specs = [_cast_spec(grid, row0, n_rows, arr.shape[1], n_slabs) for arr, row0, n_rows, n_slabs in casts]
    return [s[0] for s in specs], [s[1] for s in specs], [s[2] for s in specs]


def _cast_slabs(refs, n_casts):
    for src, dst in zip(refs[:n_casts], refs[len(refs) - n_casts:]):
        dst[...] = src[...].astype(dst.dtype)


def _inproj_kernel(a_ref, wslab_ref, *rest, n_casts):
    o_ref = rest[n_casts]
    wbf_ref = rest[-1]
    j = pl.program_id(0)
    slab = wslab_ref.shape[0]
    rows = pl.ds(pl.multiple_of(pl.program_id(1) * slab, slab), slab)
    wbf_ref[j % 2, rows, :] = wslab_ref[...].astype(BF16)

    @pl.when(j > 0)
    def _():
        o_ref[...] = _dot_nt(a_ref[...], wbf_ref[(j + 1) % 2])

    _cast_slabs(rest[:n_casts] + rest[n_casts + 1:-1], n_casts)


def _in_projection(a, w_t, row0, n_cols, tn, tm=1024, casts=()):
    m, k = a.shape
    n_tiles, n_i = n_cols // tn, m // tm
    grid = (n_tiles + 1, n_i)
    c_in, c_out, c_shapes = _cast_specs(casts, grid)
    sub = 8
    slab = tn // n_i
    assert row0 % sub == 0 and slab % 16 == 0 and slab * n_i == tn

    def slab_row(j, i):
        return (row0 // sub + jnp.minimum(j, n_tiles - 1) * (tn // sub) + i * (slab // sub)) * sub

    return pl.pallas_call(
        functools.partial(_inproj_kernel, n_casts=len(casts)),
        out_shape=[jax.ShapeDtypeStruct((m, n_cols), F32)] + c_shapes,
        grid=grid,
        in_specs=[pl.BlockSpec((tm, k), lambda j, i: (jnp.where(j > 0, i, 0), 0)),
                  pl.BlockSpec((pl.Element(slab), pl.Element(k)), lambda j, i: (slab_row(j, i), 0))] + c_in,
        out_specs=[pl.BlockSpec((tm, tn), lambda j, i: (jnp.where(j > 0, i, 0), jnp.maximum(j - 1, 0)))] + c_out,
        scratch_shapes=[pltpu.VMEM((2, tn, k), BF16)],
        compiler_params=pltpu.CompilerParams(dimension_semantics=("arbitrary", "arbitrary"),
                                             vmem_limit_bytes=VMEM_LIMIT),
    )(a, w_t, *[c[0] for c in casts])


def _outproj_kernel(a1_ref, a2_ref, w1_ref, w2_ref, x_ref, *rest, n_casts):
    o_ref = rest[n_casts]
    acc = _dot(a1_ref[...], w1_ref[...]) + _dot(a2_ref[...], w2_ref[...])
    o_ref[...] = x_ref[...] + acc
    _cast_slabs(rest[:n_casts] + rest[n_casts + 1:], n_casts)


def _out_projection(o_nsa, o_ret, w_out, x, tm=1024, tn=512, casts=()):
    m = x.shape[0]
    kh = NSA_WIDTH
    grid = (m // tm, D_MODEL // tn)
    c_in, c_out, c_shapes = _cast_specs(casts, grid)
    return pl.pallas_call(
        functools.partial(_outproj_kernel, n_casts=len(casts)),
        out_shape=[jax.ShapeDtypeStruct((m, D_MODEL), F32)] + c_shapes,
        grid=grid,
        in_specs=[pl.BlockSpec((tm, kh), lambda i, j: (i, 0)),
                  pl.BlockSpec((tm, kh), lambda i, j: (i, 0)),
                  pl.BlockSpec((kh, tn), lambda i, j: (0, j)),
                  pl.BlockSpec((kh, tn), lambda i, j: (1, j)),
                  pl.BlockSpec((tm, tn), lambda i, j: (i, j))] + c_in,
        out_specs=[pl.BlockSpec((tm, tn), lambda i, j: (i, j))] + c_out,
        compiler_params=pltpu.CompilerParams(dimension_semantics=("arbitrary", "arbitrary"),
                                             vmem_limit_bytes=VMEM_LIMIT),
    )(o_nsa, o_ret, w_out, w_out, x, *[c[0] for c in casts])


def _ffn_up_kernel(h_ref, wg_ref, wu_ref, *rest, n_casts):
    o_ref = rest[n_casts]
    sub_rows = h_ref.shape[0] // FFN_UP_SUBBLOCKS
    for s in range(FFN_UP_SUBBLOCKS):
        rows = slice(s * sub_rows, (s + 1) * sub_rows)
        h = h_ref[rows, :]
        g = _dot(h, wg_ref[...])
        u = _dot(h, wu_ref[...])
        o_ref[rows, :] = (g * jax.nn.sigmoid(g) * u).astype(o_ref.dtype)
    _cast_slabs(rest[:n_casts] + rest[n_casts + 1:], n_casts)


def _ffn_up(h, w_gate, w_up, tm=1024, tn=256, casts=()):
    m, k = h.shape
    n = w_gate.shape[1]
    grid = (m // tm, n // tn)
    c_in, c_out, c_shapes = _cast_specs(casts, grid)
    return pl.pallas_call(
        functools.partial(_ffn_up_kernel, n_casts=len(casts)),
        out_shape=[jax.ShapeDtypeStruct((m, n), BF16)] + c_shapes,
        grid=grid,
        in_specs=[pl.BlockSpec((tm, k), lambda i, j: (i, 0)),
                  pl.BlockSpec((k, tn), lambda i, j: (0, j)),
                  pl.BlockSpec((k, tn), lambda i, j: (0, j))] + c_in,
        out_specs=[pl.BlockSpec((tm, tn), lambda i, j: (i, j))] + c_out,
        compiler_params=pltpu.CompilerParams(dimension_semantics=("arbitrary", "arbitrary"),
                                             vmem_limit_bytes=VMEM_LIMIT),
    )(h, w_gate, w_up, *[c[0] for c in casts])


def _ffn_down_kernel(a_ref, b_ref, r_ref, o_ref):
    o_ref[...] = r_ref[...] + _dot(a_ref[...], b_ref[...])


def _ffn_down(a, w_down, x, tm=1024, tn=512, tk=5504):
    m, k = a.shape
    n = w_down.shape[1]
    res = x
    for kk in range(k // tk):
        res = pl.pallas_call(
            _ffn_down_kernel,
            out_shape=jax.ShapeDtypeStruct((m, n), F32),
            grid=(m // tm, n // tn),
            in_specs=[pl.BlockSpec((tm, tk), lambda i, j, kk=kk: (i, kk)),
                      pl.BlockSpec((tk, tn), lambda i, j, kk=kk: (kk, j)),
                      pl.BlockSpec((tm, tn), lambda i, j: (i, j))],
            out_specs=pl.BlockSpec((tm, tn), lambda i, j: (i, j)),
            compiler_params=pltpu.CompilerParams(dimension_semantics=("arbitrary", "arbitrary"),
                                                 vmem_limit_bytes=VMEM_LIMIT),
        )(a, w_down, res)
    return res


def _bias_value(rel, relb_ref, h):
    val = jnp.full(rel.shape, relb_ref[0, h], F32)
    for k in range(1, NUM_BUCKETS):
        val = jnp.where(rel >= _THR[k], relb_ref[k, h], val)
    return val


def _bias_tables_kernel(relb_ref, biasc_ref, dm_ref):
    h = pl.program_id(0)
    sub = 8
    j = lax.broadcasted_iota(jnp.int32, (sub, 2 * SEQ), 1)
    row_c = _bias_value(j - SEQ - (CMP_BLOCK - 1), relb_ref, h)[0:1]
    toe_c = pltpu.roll(jnp.broadcast_to(row_c, (LANES, 2 * SEQ)), 0, 1,
                       stride=CMP_STRIDE, stride_axis=0)
    for qt in range(NQ):
        biasc_ref[0, qt] = toe_c[:, SEQ + qt * TQ:SEQ + (qt + 1) * TQ]
    far = relb_ref[NUM_BUCKETS - 1, h]
    j2 = lax.broadcasted_iota(jnp.int32, (sub, 2 * TK), 1)
    b = lax.broadcasted_iota(jnp.int32, (TK, TQ), 0)
    a = lax.broadcasted_iota(jnp.int32, (TK, TQ), 1)
    for kind in range(2):
        row = ((_bias_value(j2 + (kind - 1) * TK, relb_ref, h) - far) * LOG2E)[0:1]
        toe = pltpu.roll(jnp.broadcast_to(row, (TK, 2 * TK)), 0, 1, stride=1, stride_axis=0)
        tile = toe[:, TK:]
        dm_ref[0, kind] = tile + jnp.where(a >= b, 0.0, NEG_INF) if kind == 0 else tile
    dm_ref[0, 2] = jnp.where(b > a, 0.0, NEG_INF)


def _bias_tables(rel_bias):
    return pl.pallas_call(
        _bias_tables_kernel,
        out_shape=(jax.ShapeDtypeStruct((N_KV_GROUPS, NQ, LANES, GROUP_SIZE * TQ), F32),
                   jax.ShapeDtypeStruct((N_KV_GROUPS, 3, TK, GROUP_SIZE * TQ), F32)),
        grid=(N_NSA_HEADS,),
        in_specs=[pl.BlockSpec(memory_space=pltpu.SMEM)],
        out_specs=(pl.BlockSpec((1, NQ, LANES, TQ), lambda h: (h // GROUP_SIZE, 0, 0, h % GROUP_SIZE)),
                   pl.BlockSpec((1, 3, TK, TQ), lambda h: (h // GROUP_SIZE, 0, 0, h % GROUP_SIZE))),
        compiler_params=pltpu.CompilerParams(dimension_semantics=("arbitrary",),
                                             vmem_limit_bytes=VMEM_LIMIT),
    )(rel_bias)


def _row_rms(x, w):
    return x * lax.rsqrt(jnp.mean(x * x, axis=-1, keepdims=True) + EPS) * w


N_NSA_INPUTS = 17
N_NSA_SCRATCH = 7


def _nsa_kernel(*refs, n_casts):
    (q_ref, k0_ref, v0_ref, k1_ref, v1_ref, k2_ref, v2_ref, gate_ref,
     qnw_ref, knw_ref, pek_ref, pev_ref, wck_ref, wcv_ref, biasc_ref, dm_ref, c2st_ref) = refs[:N_NSA_INPUTS]
    side_a_ref, side_bt_ref = refs[N_NSA_INPUTS:N_NSA_INPUTS + 2]
    n_in = N_NSA_INPUTS + 2 + n_casts
    cast_in = refs[N_NSA_INPUTS + 2:n_in]
    o_ref, side_o_ref = refs[n_in:n_in + 2]
    cast_out = refs[n_in + 2:n_in + 2 + n_casts]
    kc_s, vct_s, kaug_s, kw_s, vst_s, vwt_s, gt_s = refs[len(refs) - N_NSA_SCRATCH:]
    qt = pl.program_id(2)
    cols_all = GROUP_SIZE * TQ
    side_half = side_bt_ref.shape[0] // 2

    @pl.when(qt == 0)
    def _prologue():
        kaug_s[:, :HEAD_DIM] = (_row_rms(k1_ref[...], knw_ref[1:2, :]) * EXP2_SCALE).astype(BF16)
        jblk = lax.broadcasted_iota(jnp.int32, (SEQ, LANES), 0) // SEL_BLOCK
        lane = lax.broadcasted_iota(jnp.int32, (SEQ, LANES), 1)
        kaug_s[:, HEAD_DIM:] = jnp.where(jblk == lane, 1.0, 0.0).astype(BF16)
        kw_s[...] = (_row_rms(k2_ref[...], knw_ref[2:3, :]) * EXP2_SCALE).astype(BF16)
        ones_row = jnp.where(lax.broadcasted_iota(jnp.int32, (V_ROWS - HEAD_DIM, TK), 0) == 0,
                             1.0, 0.0).astype(BF16)
        for t in range(SEQ // TK):
            vst_s[t, :HEAD_DIM, :] = v1_ref[t * TK:(t + 1) * TK, :].T.astype(BF16)
            vst_s[t, HEAD_DIM:, :] = ones_row
            vwt_s[t, :HEAD_DIM, :] = v2_ref[t * TK:(t + 1) * TK, :].T.astype(BF16)
            vwt_s[t, HEAD_DIM:, :] = ones_row

        def compress(x_ref, pe_ref, w_ref):
            acc_lo = jnp.zeros((LANES, HEAD_DIM), F32)
            acc_hi = jnp.zeros((LANES, HEAD_DIM), F32)
            for j in range(CMP_STRIDE):
                y = x_ref[pl.ds(j, LANES, stride=CMP_STRIDE), :]
                acc_lo += _dot((y + pe_ref[j:j + 1, :]).astype(BF16),
                               w_ref[j * HEAD_DIM:(j + 1) * HEAD_DIM, :])
                jh = CMP_STRIDE + j
                acc_hi += _dot((y + pe_ref[jh:jh + 1, :]).astype(BF16),
                               w_ref[jh * HEAD_DIM:(jh + 1) * HEAD_DIM, :])
            return acc_lo + pltpu.roll(acc_hi, LANES - 1, 0)

        kc_s[...] = _row_rms(compress(k0_ref, pek_ref, wck_ref), knw_ref[0:1, :]).astype(BF16)
        vct_s[...] = compress(v0_ref, pev_ref, wcv_ref).T.astype(BF16)

    side_o_ref[:, :side_half] = _dot_nt(side_a_ref[...], side_bt_ref[:side_half, :])
    _cast_slabs(cast_in + cast_out, n_casts)

    t0 = qt * TQ
    q = q_ref[...]
    qts = []
    for r in range(GROUP_SIZE):
        qts.append(_row_rms(q[:, r * HEAD_DIM:(r + 1) * HEAD_DIM], qnw_ref[...]).T)
    qt_stack = jnp.concatenate(qts, axis=1).astype(BF16)

    m0 = jnp.full((1, cols_all), NEG_INF, F32)
    a0 = jnp.zeros((V_ROWS, cols_all), F32)

    def online(carry, tiles):
        m, acc = carry
        m_new = m
        for u, _ in tiles:
            m_new = jnp.maximum(m_new, jnp.max(u, axis=0, keepdims=True))
        acc = jnp.exp2(m - m_new) * acc
        for u, vt in tiles:
            acc = acc + _dot(vt, jnp.exp2(u - m_new).astype(BF16))
        return m_new, acc

    def online_independent(carry, tiles):
        m, acc = carry
        parts = []
        for u, vt in tiles:
            mt = jnp.max(u, axis=0, keepdims=True)
            parts.append((mt, _dot(vt, jnp.exp2(u - mt).astype(BF16))))
        m_new = m
        for mt, _ in parts:
            m_new = jnp.maximum(m_new, mt)
        acc = jnp.exp2(m - m_new) * acc
        for mt, pv in parts:
            acc = acc + jnp.exp2(mt - m_new) * pv
        return m_new, acc

    def normalised(acc):
        return acc[:HEAD_DIM] * (1.0 / acc[HEAD_DIM:HEAD_DIM + 1])

    def key_rows(kt):
        return pl.ds(pl.multiple_of(kt * TK, TK), TK)

    sc = _dot(kc_s[...], qt_stack) * SCALE + biasc_ref[0, 0]
    tpos = t0 + (lax.broadcasted_iota(jnp.int32, (LANES, cols_all), 1) & (TQ - 1))
    nrow = lax.broadcasted_iota(jnp.int32, (LANES, cols_all), 0)
    maskc = tpos >= CMP_STRIDE * nrow + (CMP_BLOCK - 1)
    scm = jnp.where(maskc, sc, NEG_INF)
    mc = jnp.max(scm, axis=0, keepdims=True)
    ec = jnp.where(maskc, jnp.exp(scm - mc), 0.0)
    lc = jnp.sum(ec, axis=0, keepdims=True)
    pc = ec * (1.0 / jnp.where(lc > 0.0, lc, 1.0))
    o_c = _dot(vct_s[...], pc.astype(BF16))

    psum = pc[:, 0:TQ] + pc[:, TQ:2 * TQ] + pc[:, 2 * TQ:3 * TQ] + pc[:, 3 * TQ:4 * TQ]
    p_hi = psum.astype(BF16)
    p_lo = (psum - p_hi.astype(F32)).astype(BF16)
    imp = (_dot(c2st_ref[...], p_hi) + _dot(c2st_ref[...], p_lo))[0:N_SEL]

    tiles = []
    for dist, kind in ((2, 2), (1, 1), (0, 0)):
        kt = jnp.maximum(qt - dist, 0)
        u = _dot(kw_s[key_rows(kt), :], qt_stack) + dm_ref[0, kind]
        if dist > 0:
            u = u + jnp.where(qt >= dist, 0.0, NEG_INF)
        tiles.append((u, vwt_s[kt]))
    _, acc_w = online_independent((m0, a0), tiles)
    o_w = normalised(acc_w)

    tq_pos = t0 + lax.broadcasted_iota(jnp.int32, (N_SEL, TQ), 1)
    sidx = lax.broadcasted_iota(jnp.int32, (N_SEL, TQ), 0)
    cur = tq_pos >> (SEL_BLOCK.bit_length() - 1)
    score = jnp.where(sidx <= cur, imp, NEG_INF)
    for forced_blk in (cur - 1, cur, jnp.zeros_like(cur)):
        score = jnp.where(sidx == forced_blk, FORCE_SCORE, score)
    rank = jnp.zeros((N_SEL, TQ), F32)
    for sp in range(N_SEL):
        row = score[sp:sp + 1, :]
        tie = jnp.where(sidx > sp, 1.0, 0.0)
        rank = rank + jnp.where(row > score, 1.0, jnp.where(row == score, tie, 0.0))
    selb = jnp.where(rank < TOP_N, 0.0, NEG_INF)
    selb = jnp.concatenate([selb, jnp.zeros((HEAD_DIM - N_SEL, TQ), F32)], axis=0).astype(BF16)
    qt_aug = jnp.concatenate([qt_stack, jnp.concatenate([selb] * GROUP_SIZE, axis=1)], axis=0)

    def sel_tile(kt):
        return _dot(kaug_s[key_rows(kt), :], qt_aug), vst_s[kt]

    n_far = jnp.maximum(qt - 1, 0)
    carry = lax.fori_loop(0, n_far >> 1,
                          lambda i, c: online_independent(c, [sel_tile(2 * i), sel_tile(2 * i + 1)]),
                          (m0, a0))
    carry = lax.fori_loop(0, n_far & 1, lambda i, c: online(c, [sel_tile(n_far - 1)]), carry)
    kt1 = jnp.maximum(qt - 1, 0)
    u1, vt1 = sel_tile(kt1)
    u0, vt0 = sel_tile(qt)
    _, acc_s = online_independent(
        carry, [(u1 + dm_ref[0, 1] + jnp.where(qt >= 1, 0.0, NEG_INF), vt1), (u0 + dm_ref[0, 0], vt0)])
    o_s = normalised(acc_s)

    side_o_ref[:, side_half:] = _dot_nt(side_a_ref[...], side_bt_ref[side_half:, :])

    gt_s[...] = jax.nn.sigmoid(gate_ref[...]).T
    head0 = pl.program_id(1) * GROUP_SIZE
    for r in range(GROUP_SIZE):
        cs = slice(r * TQ, (r + 1) * TQ)
        c = (head0 + r) * N_BRANCH
        o = (gt_s[pl.ds(c, 1), :] * o_c[:, cs] + gt_s[pl.ds(c + 1, 1), :] * o_s[:, cs]
             + gt_s[pl.ds(c + 2, 1), :] * o_w[:, cs])
        o_ref[:, r * HEAD_DIM:(r + 1) * HEAD_DIM] = o.T.astype(o_ref.dtype)


def _cmp_to_sel_t():
    ss = np.arange(LANES)[:, None] * SEL_BLOCK
    cs = np.arange(LANES)[None, :] * CMP_STRIDE
    ov = np.clip(np.minimum(cs + CMP_BLOCK, ss + SEL_BLOCK) - np.maximum(cs, ss), 0, None)
    m = ov.astype(np.float32) / np.float32(CMP_BLOCK)
    m[N_SEL:, :] = 0.0
    m[:, N_CMP:] = 0.0
    return jnp.asarray(m, BF16)


def _nsa_attention(proj, gate_logits, batch, q_norm_w, k_norm_w, pe_k, pe_v, w_cmp_k, w_cmp_v, biasc, dm,
                   side_a, side_bt, side_tm=1024, side_tn=512, casts=()):
    kvb = COL_KV // LANES

    def kv_spec(branch, which):
        base = kvb + (branch * 2 + which) * N_KV_GROUPS
        return pl.BlockSpec((SEQ, LANES), lambda b, g, qt: (b, base + g))

    full = lambda shape: pl.BlockSpec(shape, lambda b, g, qt: (0,) * len(shape))
    grid = (batch, N_KV_GROUPS, NQ)
    c_in, c_out, c_shapes = _cast_specs(casts, grid)
    side_m, side_k = side_a.shape
    side_n = side_bt.shape[0]
    n_j = side_n // side_tn
    assert (side_m // side_tm) * n_j == math.prod(grid)
    step = lambda b, g, qt: (b * N_KV_GROUPS + g) * NQ + qt
    side_in = [pl.BlockSpec((side_tm, side_k), lambda b, g, qt: (step(b, g, qt) // n_j, 0),
                            pipeline_mode=pl.Buffered(1)),
               pl.BlockSpec((side_tn, side_k), lambda b, g, qt: (step(b, g, qt) % n_j, 0))]
    side_out = pl.BlockSpec((side_tm, side_tn), lambda b, g, qt: (step(b, g, qt) // n_j, step(b, g, qt) % n_j))
    return pl.pallas_call(
        functools.partial(_nsa_kernel, n_casts=len(casts)),
        out_shape=[jax.ShapeDtypeStruct((batch * SEQ, NSA_WIDTH), BF16),
                   jax.ShapeDtypeStruct((side_m, side_n), F32)] + c_shapes,
        grid=grid,
        in_specs=[pl.BlockSpec((TQ, GROUP_SIZE * HEAD_DIM), lambda b, g, qt: (b * NQ + qt, g)),
                  kv_spec(0, 0), kv_spec(0, 1), kv_spec(1, 0), kv_spec(1, 1),
                  kv_spec(2, 0), kv_spec(2, 1),
                  pl.BlockSpec((TQ, LANES), lambda b, g, qt: (b * NQ + qt, 0)),
                  full((1, HEAD_DIM)), full((N_BRANCH, HEAD_DIM)),
                  full((CMP_BLOCK, HEAD_DIM)), full((CMP_BLOCK, HEAD_DIM)),
                  full((CMP_BLOCK * HEAD_DIM, HEAD_DIM)), full((CMP_BLOCK * HEAD_DIM, HEAD_DIM)),
                  pl.BlockSpec((1, 1, LANES, GROUP_SIZE * TQ), lambda b, g, qt: (g, qt, 0, 0)),
                  pl.BlockSpec((1, 3, TK, GROUP_SIZE * TQ), lambda b, g, qt: (g, 0, 0, 0),
                               pipeline_mode=pl.Buffered(1)),
                  full((LANES, LANES))] + side_in + c_in,
        out_specs=[pl.BlockSpec((TQ, GROUP_SIZE * HEAD_DIM), lambda b, g, qt: (b * NQ + qt, g)),
                   side_out] + c_out,
        scratch_shapes=[pltpu.VMEM((LANES, HEAD_DIM), BF16),
                        pltpu.VMEM((HEAD_DIM, LANES), BF16),
                        pltpu.VMEM((SEQ, 2 * HEAD_DIM), BF16),
                        pltpu.VMEM((SEQ, HEAD_DIM), BF16),
                        pltpu.VMEM((SEQ // TK, V_ROWS, TK), BF16),
                        pltpu.VMEM((SEQ // TK, V_ROWS, TK), BF16),
                        pltpu.VMEM((LANES, TQ), F32)],
        compiler_params=pltpu.CompilerParams(
            dimension_semantics=("arbitrary", "arbitrary", "arbitrary"),
            vmem_limit_bytes=NSA_VMEM_LIMIT),
    )(proj, proj, proj, proj, proj, proj, proj, gate_logits,
      q_norm_w.reshape(1, HEAD_DIM), k_norm_w, pe_k, pe_v,
      w_cmp_k.astype(BF16), w_cmp_v.astype(BF16), biasc, dm, _cmp_to_sel_t(),
      side_a, side_bt, *[c[0] for c in casts])


def _ret_kernel(lg_ref, q_ref, k_ref, v_ref, g_ref, cos_ref, sin_ref, gnw_ref, *rest, n_casts):
    o_ref = rest[n_casts]
    state_ref = rest[-1]
    _cast_slabs(rest[:n_casts] + rest[n_casts + 1:-1], n_casts)
    c_len = RET_CHUNK
    ii = lax.broadcasted_iota(jnp.int32, (c_len, c_len), 0)
    jj = lax.broadcasted_iota(jnp.int32, (c_len, c_len), 1)
    diff = (ii - jj).astype(F32)
    icol = lax.broadcasted_iota(jnp.int32, (c_len, 1), 0).astype(F32)
    half = RET_KEY_DIM // 2
    state_ref[...] = jnp.zeros_like(state_ref)
    tables = []
    for hh in range(RET_HEADS_PER_STEP):
        lg = lg_ref[pl.program_id(1) * RET_HEADS_PER_STEP + hh]
        tables.append((
            jnp.where(diff >= 0.0, jnp.exp(jnp.maximum(diff, 0.0) * lg), 0.0),
            jnp.exp((icol + 1.0) * lg),
            jnp.exp((c_len - 1.0 - icol) * lg),
            jnp.exp(jnp.full((1, RET_VAL_DIM), float(c_len), F32) * lg)))

    def body(c, _):
        rows = pl.ds(pl.multiple_of(c * c_len, c_len), c_len)
        cos = cos_ref[rows, :]
        sin = sin_ref[rows, :]

        def rot(x):
            x1, x2 = x[:, :half], x[:, half:]
            return jnp.concatenate([x1 * cos - x2 * sin, x2 * cos + x1 * sin], axis=-1)

        for hh, (decay_in, xi, zeta, chunk_decay) in enumerate(tables):
            kcols = slice(hh * RET_KEY_DIM, (hh + 1) * RET_KEY_DIM)
            vcols = slice(hh * RET_VAL_DIM, (hh + 1) * RET_VAL_DIM)
            qr = rot(q_ref[rows, kcols])
            kr = rot(k_ref[rows, kcols]) * (RET_KEY_DIM ** -0.5)
            vb = v_ref[rows, vcols].astype(BF16)
            scores = _dot_nt(qr.astype(BF16), kr.astype(BF16)) * decay_in
            inner = _dot(scores.astype(BF16), vb)
            st = state_ref[hh]
            cross = _dot((qr * xi).astype(BF16), st.astype(BF16))
            kv = _dot((kr * zeta).T.astype(BF16), vb)
            state_ref[hh] = chunk_decay * st + kv
            y = inner + cross
            y = y * lax.rsqrt(jnp.mean(y * y, axis=-1, keepdims=True) + EPS) * gnw_ref[:, vcols]
            g = g_ref[rows, vcols]
            o_ref[rows, vcols] = (g * jax.nn.sigmoid(g) * y).astype(o_ref.dtype)
        return 0

    lax.fori_loop(0, SEQ // c_len, body, 0, unroll=4)


def _retention(proj, batch, gn_w, casts=()):
    pos = np.arange(SEQ, dtype=np.float64)
    half = RET_KEY_DIM // 2
    inv = np.exp(-np.linspace(0.0, 1.0, half) * math.log(ROPE_BASE)).astype(np.float32)
    ang = pos.astype(np.float32)[:, None] * inv[None, :]
    cos = jnp.asarray(np.cos(ang.astype(np.float64)), F32)
    sin = jnp.asarray(np.sin(ang.astype(np.float64)), F32)
    log_gamma = jnp.asarray(np.log(1.0 - 2.0 ** (-5.0 - np.arange(N_RET_HEADS))), F32)
    w = RET_KEY_DIM * RET_HEADS_PER_STEP

    def col_spec(col0):
        return pl.BlockSpec((SEQ, w), lambda b, h: (b, col0 // w + h))

    grid = (batch, N_RET_HEADS // RET_HEADS_PER_STEP)
    c_in, c_out, c_shapes = _cast_specs(casts, grid)
    return pl.pallas_call(
        functools.partial(_ret_kernel, n_casts=len(casts)),
        out_shape=[jax.ShapeDtypeStruct((batch * SEQ, RET_WIDTH), BF16)] + c_shapes,
        grid=grid,
        in_specs=[pl.BlockSpec(memory_space=pltpu.SMEM),
                  col_spec(COL_QR), col_spec(COL_KR), col_spec(COL_VR), col_spec(COL_GR),
                  pl.BlockSpec((SEQ, half), lambda b, h: (0, 0)),
                  pl.BlockSpec((SEQ, half), lambda b, h: (0, 0)),
                  pl.BlockSpec((1, w), lambda b, h: (0, h))] + c_in,
        out_specs=[pl.BlockSpec((SEQ, w), lambda b, h: (b, h))] + c_out,
        scratch_shapes=[pltpu.VMEM((RET_HEADS_PER_STEP, RET_KEY_DIM, RET_VAL_DIM), F32)],
        compiler_params=pltpu.CompilerParams(dimension_semantics=("arbitrary", "arbitrary"),
                                             vmem_limit_bytes=VMEM_LIMIT),
    )(log_gamma, proj, proj, proj, proj, cos, sin, gn_w.reshape(1, RET_WIDTH), *[c[0] for c in casts])


def _layer(x, norm1_w, w_in, nsa_q_norm_w, nsa_k_norm_w, cmp_pe_k, cmp_pe_v, w_cmp_k, w_cmp_v,
           rel_bias, ret_gn_w, w_out, norm2_w, w_gate, w_up, w_down):
    batch = x.shape[0]
    rows = batch * SEQ
    xf = x.reshape(rows, D_MODEL)
    w_in_t = w_in.T
    gate0 = W_IN_NSA_COLS
    gate1 = gate0 + W_IN_GATE_COLS
    h, gate_logits = _rmsnorm_with_projection(xf, norm1_w, w_in_t, gate0, LANES)
    ret_cols = 4 * RET_WIDTH
    nsa_tn, nsa_tm = 1024, 512
    nsa_steps = (W_IN_NSA_COLS // nsa_tn + 1) * (rows // nsa_tm)
    ret_slabs = 1 << (nsa_steps.bit_length() - 1)
    proj_nsa, w_ret_t = _in_projection(h, w_in_t, 0, W_IN_NSA_COLS, nsa_tn, tm=nsa_tm,
                                       casts=((w_in_t, gate1, ret_cols, ret_slabs),))
    biasc, dm = _bias_tables(rel_bias)
    n_steps = batch * N_KV_GROUPS * NQ
    o_nsa, proj_ret, w_gate_b, w_up_b, w_out_b = _nsa_attention(
        proj_nsa, gate_logits, batch, nsa_q_norm_w, nsa_k_norm_w, cmp_pe_k, cmp_pe_v,
        w_cmp_k, w_cmp_v, biasc, dm, h, w_ret_t,
        casts=((w_gate, 0, D_MODEL, n_steps), (w_up, 0, D_MODEL, n_steps), (w_out, 0, D_MODEL, n_steps)))
    o_ret, = _retention(proj_ret, batch, ret_gn_w)
    x1, = _out_projection(o_nsa, o_ret, w_out_b, xf, tn=1024)
    hf = _rmsnorm(x1, norm2_w)
    up_tm, up_tn = 2048, 256
    up_steps = (rows // up_tm) * (D_FF // up_tn)
    mid, w_down_b = _ffn_up(hf, w_gate_b, w_up_b, tm=up_tm, tn=up_tn, casts=((w_down, 0, D_FF, up_steps),))
    out = _ffn_down(mid, w_down_b, x1)
    return out.reshape(batch, SEQ, D_MODEL)


def kernel(x, norm1_w, w_in, nsa_q_norm_w, nsa_k_norm_w, cmp_pe_k, cmp_pe_v, w_cmp_k, w_cmp_v,
           rel_bias, ret_gn_w, w_out, norm2_w, w_gate, w_up, w_down):
    for l in range(norm1_w.shape[0]):
        x = _layer(x, norm1_w[l], w_in[l], nsa_q_norm_w[l], nsa_k_norm_w[l], cmp_pe_k[l], cmp_pe_v[l],
                   w_cmp_k[l], w_cmp_v[l], rel_bias, ret_gn_w[l], w_out[l], norm2_w[l],
                   w_gate[l], w_up[l], w_down[l])
    return x
```

```python
import functools
import math

import numpy as np
import jax
import jax.numpy as jnp
from jax import lax
from jax.experimental import pallas as pl
from jax.experimental.pallas import tpu as pltpu

F32 = jnp.float32
BF16 = jnp.bfloat16

D_MODEL = 4096
SEQ = 2048
HEAD_DIM = 128
N_NSA_HEADS = 16
N_KV_GROUPS = 4
GROUP_SIZE = 4
N_BRANCH = 3
CMP_BLOCK = 32
CMP_STRIDE = 16
SEL_BLOCK = 64
TOP_N = 16
WINDOW = 512
RET_KEY_DIM = 256
RET_VAL_DIM = 256
N_RET_HEADS = 8
RET_CHUNK = 128
ROPE_BASE = 10000.0
NSA_WIDTH = 2048
RET_WIDTH = 2048
D_FF = 11008
NUM_BUCKETS = 32
MAX_DISTANCE = 128
EPS = 1e-6
NEG_INF = -1e30
FORCE_SCORE = 1e9
N_SEL = SEQ // SEL_BLOCK
N_CMP = (SEQ - CMP_BLOCK) // CMP_STRIDE + 1
SCALE = HEAD_DIM ** -0.5
LOG2E = math.log2(math.e)
EXP2_SCALE = SCALE * LOG2E

LANES = 128
VMEM_LIMIT = 56 * 1024 * 1024
NSA_VMEM_LIMIT = 60 * 1024 * 1024

TQ = 256
TK = 256
NQ = SEQ // TQ
RET_HEADS_PER_STEP = 2
FFN_UP_SUBBLOCKS = 4
V_ROWS = HEAD_DIM + 16
assert TQ == TK and WINDOW == 2 * TK and TK % SEL_BLOCK == 0 and N_SEL <= HEAD_DIM and TK >= MAX_DISTANCE

W_IN_NSA_COLS = NSA_WIDTH + N_BRANCH * 2 * N_KV_GROUPS * HEAD_DIM
W_IN_GATE_COLS = N_NSA_HEADS * N_BRANCH
COL_KV = NSA_WIDTH
COL_QR = 0
COL_KR = 2048
COL_VR = 4096
COL_GR = 6144


def _bucket_thresholds():
    rel = np.arange(0, 4 * MAX_DISTANCE)
    max_exact = NUM_BUCKETS // 2
    nf = np.maximum(rel, 1).astype(np.float64)
    large = max_exact + np.floor(np.log(nf / max_exact) / math.log(MAX_DISTANCE / max_exact)
                                 * (NUM_BUCKETS - max_exact)).astype(np.int64)
    large = np.minimum(large, NUM_BUCKETS - 1)
    b = np.where(rel < max_exact, rel, large)
    return [int(np.argmax(b >= k)) for k in range(NUM_BUCKETS)]


_THR = _bucket_thresholds()


def _dot(a, b):
    return jnp.dot(a, b, preferred_element_type=F32)


def _dot_nt(a, b):
    return lax.dot_general(a, b, (((1,), (1,)), ((), ())), preferred_element_type=F32)


def _rms_kernel(x_ref, w_ref, o_ref):
    x = x_ref[...]
    y = x * lax.rsqrt(jnp.mean(x * x, axis=-1, keepdims=True) + EPS)
    o_ref[...] = (y * w_ref[...]).astype(o_ref.dtype)


def _rmsnorm(x, w, tm=512):
    m, d = x.shape
    return pl.pallas_call(
        _rms_kernel,
        out_shape=jax.ShapeDtypeStruct((m, d), BF16),
        grid=(m // tm,),
        in_specs=[pl.BlockSpec((tm, d), lambda i: (i, 0)),
                  pl.BlockSpec((1, d), lambda i: (0, 0))],
        out_specs=pl.BlockSpec((tm, d), lambda i: (i, 0)),
        compiler_params=pltpu.CompilerParams(dimension_semantics=("arbitrary",),
                                             vmem_limit_bytes=VMEM_LIMIT),
    )(x, w.reshape(1, d))


def _rms_proj_kernel(x_ref, w_ref, pt_ref, o_ref, p_ref, pbf_ref):
    @pl.when(pl.program_id(0) == 0)
    def _():
        pbf_ref[...] = pt_ref[...].astype(BF16)

    x = x_ref[...]
    y = x * lax.rsqrt(jnp.mean(x * x, axis=-1, keepdims=True) + EPS)
    h = (y * w_ref[...]).astype(o_ref.dtype)
    o_ref[...] = h
    p_ref[...] = _dot_nt(h, pbf_ref[...])


def _rmsnorm_with_projection(x, w, w_t, row0, n_rows, tm=512):
    m, d = x.shape
    sub = 8
    assert row0 % sub == 0 and n_rows % LANES == 0
    return pl.pallas_call(
        _rms_proj_kernel,
        out_shape=(jax.ShapeDtypeStruct((m, d), BF16), jax.ShapeDtypeStruct((m, n_rows), F32)),
        grid=(m // tm,),
        in_specs=[pl.BlockSpec((tm, d), lambda i: (i, 0)),
                  pl.BlockSpec((1, d), lambda i: (0, 0)),
                  pl.BlockSpec((pl.Element(n_rows), pl.Element(d)), lambda i: ((row0 // sub) * sub, 0))],
        out_specs=(pl.BlockSpec((tm, d), lambda i: (i, 0)),
                   pl.BlockSpec((tm, n_rows), lambda i: (i, 0))),
        scratch_shapes=[pltpu.VMEM((n_rows, d), BF16)],
        compiler_params=pltpu.CompilerParams(dimension_semantics=("arbitrary",),
                                             vmem_limit_bytes=VMEM_LIMIT),
    )(x, w.reshape(1, d), w_t)


def _cast_spec(grid, row0, n_rows, n_cols, n_slabs):
    slab = n_rows // n_slabs
    sub = 8
    n_steps = math.prod(grid)
    assert slab * n_slabs == n_rows and slab % 16 == 0 and n_cols % LANES == 0
    assert row0 % sub == 0 and n_slabs <= n_steps

    def slab_index(ids):
        step = ids[0]
        for dim, idx in zip(grid[1:], ids[1:]):
            step = step * dim + idx
        return step if n_slabs == n_steps else jnp.minimum(step, n_slabs - 1)

    in_spec = pl.BlockSpec((pl.Element(slab), pl.Element(n_cols)),
                           lambda *ids: ((row0 // sub + slab_index(ids) * (slab // sub)) * sub, 0))
    out_spec = pl.BlockSpec((slab, n_cols), lambda *ids: (slab_index(ids), 0))
    return in_spec, out_spec, jax.ShapeDtypeStruct((n_rows, n_cols), BF16)


def _cast_specs(casts, grid):
    specs = [_cast_spec(grid, row0, n_rows, arr.shape[1], n_slabs) for arr, row0, n_rows, n_slabs in casts]
    return [s[0] for s in specs], [s[1] for s in specs], [s[2] for s in specs]


def _cast_slabs(refs, n_casts):
    for src, dst in zip(refs[:n_casts], refs[len(refs) - n_casts:]):
        dst[...] = src[...].astype(dst.dtype)


def _inproj_kernel(a_ref, wt_ref, *rest, n_casts):
    o_ref = rest[n_casts]
    wbf_ref = rest[-1]

    @pl.when(pl.program_id(1) == 0)
    def _():
        wbf_ref[...] = wt_ref[...].astype(BF16)

    o_ref[...] = _dot_nt(a_ref[...], wbf_ref[...])
    _cast_slabs(rest[:n_casts] + rest[n_casts + 1:-1], n_casts)


def _in_projection(a, w_t, row0, n_cols, tn, tm=1024, casts=()):
    m, k = a.shape
    grid = (n_cols // tn, m // tm)
    c_in, c_out, c_shapes = _cast_specs(casts, grid)
    sub = 8
    assert row0 % sub == 0 and tn % sub == 0
    return pl.pallas_call(
        functools.partial(_inproj_kernel, n_casts=len(casts)),
        out_shape=[jax.ShapeDtypeStruct((m, n_cols), F32)] + c_shapes,
        grid=grid,
        in_specs=[pl.BlockSpec((tm, k), lambda j, i: (i, 0)),
                  pl.BlockSpec((pl.Element(tn), pl.Element(k)),
                               lambda j, i: ((row0 // sub + j * (tn // sub)) * sub, 0),
                               pipeline_mode=pl.Buffered(1))] + c_in,
        out_specs=[pl.BlockSpec((tm, tn), lambda j, i: (i, j))] + c_out,
        scratch_shapes=[pltpu.VMEM((tn, k), BF16)],
        compiler_params=pltpu.CompilerParams(dimension_semantics=("arbitrary", "arbitrary"),
                                             vmem_limit_bytes=VMEM_LIMIT),
    )(a, w_t, *[c[0] for c in casts])


def _outproj_kernel(a1_ref, a2_ref, w1_ref, w2_ref, x_ref, *rest, n_casts):
    o_ref = rest[n_casts]
    acc = _dot(a1_ref[...], w1_ref[...]) + _dot(a2_ref[...], w2_ref[...])
    o_ref[...] = x_ref[...] + acc
    _cast_slabs(rest[:n_casts] + rest[n_casts + 1:], n_casts)


def _out_projection(o_nsa, o_ret, w_out, x, tm=1024, tn=512, casts=()):
    m = x.shape[0]
    kh = NSA_WIDTH
    grid = (m // tm, D_MODEL // tn)
    c_in, c_out, c_shapes = _cast_specs(casts, grid)
    return pl.pallas_call(
        functools.partial(_outproj_kernel, n_casts=len(casts)),
        out_shape=[jax.ShapeDtypeStruct((m, D_MODEL), F32)] + c_shapes,
        grid=grid,
        in_specs=[pl.BlockSpec((tm, kh), lambda i, j: (i, 0)),
                  pl.BlockSpec((tm, kh), lambda i, j: (i, 0)),
                  pl.BlockSpec((kh, tn), lambda i, j: (0, j)),
                  pl.BlockSpec((kh, tn), lambda i, j: (1, j)),
                  pl.BlockSpec((tm, tn), lambda i, j: (i, j))] + c_in,
        out_specs=[pl.BlockSpec((tm, tn), lambda i, j: (i, j))] + c_out,
        compiler_params=pltpu.CompilerParams(dimension_semantics=("arbitrary", "arbitrary"),
                                             vmem_limit_bytes=VMEM_LIMIT),
    )(o_nsa, o_ret, w_out, w_out, x, *[c[0] for c in casts])


def _ffn_up_kernel(h_ref, wg_ref, wu_ref, *rest, n_casts):
    o_ref = rest[n_casts]
    sub_rows = h_ref.shape[0] // FFN_UP_SUBBLOCKS
    for s in range(FFN_UP_SUBBLOCKS):
        rows = slice(s * sub_rows, (s + 1) * sub_rows)
        h = h_ref[rows, :]
        g = _dot(h, wg_ref[...])
        u = _dot(h, wu_ref[...])
        o_ref[rows, :] = (g * jax.nn.sigmoid(g) * u).astype(o_ref.dtype)
    _cast_slabs(rest[:n_casts] + rest[n_casts + 1:], n_casts)


def _ffn_up(h, w_gate, w_up, tm=1024, tn=256, casts=()):
    m, k = h.shape
    n = w_gate.shape[1]
    grid = (m // tm, n // tn)
    c_in, c_out, c_shapes = _cast_specs(casts, grid)
    return pl.pallas_call(
        functools.partial(_ffn_up_kernel, n_casts=len(casts)),
        out_shape=[jax.ShapeDtypeStruct((m, n), BF16)] + c_shapes,
        grid=grid,
        in_specs=[pl.BlockSpec((tm, k), lambda i, j: (i, 0)),
                  pl.BlockSpec((k, tn), lambda i, j: (0, j)),
                  pl.BlockSpec((k, tn), lambda i, j: (0, j))] + c_in,
        out_specs=[pl.BlockSpec((tm, tn), lambda i, j: (i, j))] + c_out,
        compiler_params=pltpu.CompilerParams(dimension_semantics=("arbitrary", "arbitrary"),
                                             vmem_limit_bytes=VMEM_LIMIT),
    )(h, w_gate, w_up, *[c[0] for c in casts])


def _ffn_down_kernel(a_ref, b_ref, r_ref, o_ref):
    o_ref[...] = r_ref[...] + _dot(a_ref[...], b_ref[...])


def _ffn_down(a, w_down, x, tm=1024, tn=512, tk=5504):
    m, k = a.shape
    n = w_down.shape[1]
    res = x
    for kk in range(k // tk):
        res = pl.pallas_call(
            _ffn_down_kernel,
            out_shape=jax.ShapeDtypeStruct((m, n), F32),
            grid=(m // tm, n // tn),
            in_specs=[pl.BlockSpec((tm, tk), lambda i, j, kk=kk: (i, kk)),
                      pl.BlockSpec((tk, tn), lambda i, j, kk=kk: (kk, j)),
                      pl.BlockSpec((tm, tn), lambda i, j: (i, j))],
            out_specs=pl.BlockSpec((tm, tn), lambda i, j: (i, j)),
            compiler_params=pltpu.CompilerParams(dimension_semantics=("arbitrary", "arbitrary"),
                                                 vmem_limit_bytes=VMEM_LIMIT),
        )(a, w_down, res)
    return res


def _bias_value(rel, relb_ref, h):
    val = jnp.full(rel.shape, relb_ref[0, h], F32)
    for k in range(1, NUM_BUCKETS):
        val = jnp.where(rel >= _THR[k], relb_ref[k, h], val)
    return val


def _bias_tables_kernel(relb_ref, biasc_ref, dm_ref):
    h = pl.program_id(0)
    sub = 8
    j = lax.broadcasted_iota(jnp.int32, (sub, 2 * SEQ), 1)
    row_c = _bias_value(j - SEQ - (CMP_BLOCK - 1), relb_ref, h)[0:1]
    toe_c = pltpu.roll(jnp.broadcast_to(row_c, (LANES, 2 * SEQ)), 0, 1,
                       stride=CMP_STRIDE, stride_axis=0)
    for qt in range(NQ):
        biasc_ref[0, qt] = toe_c[:, SEQ + qt * TQ:SEQ + (qt + 1) * TQ]
    far = relb_ref[NUM_BUCKETS - 1, h]
    j2 = lax.broadcasted_iota(jnp.int32, (sub, 2 * TK), 1)
    b = lax.broadcasted_iota(jnp.int32, (TK, TQ), 0)
    a = lax.broadcasted_iota(jnp.int32, (TK, TQ), 1)
    for kind in range(2):
        row = ((_bias_value(j2 + (kind - 1) * TK, relb_ref, h) - far) * LOG2E)[0:1]
        toe = pltpu.roll(jnp.broadcast_to(row, (TK, 2 * TK)), 0, 1, stride=1, stride_axis=0)
        tile = toe[:, TK:]
        dm_ref[0, kind] = tile + jnp.where(a >= b, 0.0, NEG_INF) if kind == 0 else tile
    dm_ref[0, 2] = jnp.where(b > a, 0.0, NEG_INF)


def _bias_tables(rel_bias):
    return pl.pallas_call(
        _bias_tables_kernel,
        out_shape=(jax.ShapeDtypeStruct((N_KV_GROUPS, NQ, LANES, GROUP_SIZE * TQ), F32),
                   jax.ShapeDtypeStruct((N_KV_GROUPS, 3, TK, GROUP_SIZE * TQ), F32)),
        grid=(N_NSA_HEADS,),
        in_specs=[pl.BlockSpec(memory_space=pltpu.SMEM)],
        out_specs=(pl.BlockSpec((1, NQ, LANES, TQ), lambda h: (h // GROUP_SIZE, 0, 0, h % GROUP_SIZE)),
                   pl.BlockSpec((1, 3, TK, TQ), lambda h: (h // GROUP_SIZE, 0, 0, h % GROUP_SIZE))),
        compiler_params=pltpu.CompilerParams(dimension_semantics=("arbitrary",),
                                             vmem_limit_bytes=VMEM_LIMIT),
    )(rel_bias)


def _row_rms(x, w):
    return x * lax.rsqrt(jnp.mean(x * x, axis=-1, keepdims=True) + EPS) * w


N_NSA_INPUTS = 17
N_NSA_SCRATCH = 7


def _nsa_kernel(*refs, n_casts):
    (q_ref, k0_ref, v0_ref, k1_ref, v1_ref, k2_ref, v2_ref, gate_ref,
     qnw_ref, knw_ref, pek_ref, pev_ref, wck_ref, wcv_ref, biasc_ref, dm_ref, c2st_ref) = refs[:N_NSA_INPUTS]
    side_a_ref, side_bt_ref = refs[N_NSA_INPUTS:N_NSA_INPUTS + 2]
    n_in = N_NSA_INPUTS + 2 + n_casts
    cast_in = refs[N_NSA_INPUTS + 2:n_in]
    o_ref, side_o_ref = refs[n_in:n_in + 2]
    cast_out = refs[n_in + 2:n_in + 2 + n_casts]
    kc_s, vct_s, kaug_s, kw_s, vst_s, vwt_s, gt_s = refs[len(refs) - N_NSA_SCRATCH:]
    qt = pl.program_id(2)
    cols_all = GROUP_SIZE * TQ
    side_half = side_bt_ref.shape[0] // 2

    @pl.when(qt == 0)
    def _prologue():
        kaug_s[:, :HEAD_DIM] = (_row_rms(k1_ref[...], knw_ref[1:2, :]) * EXP2_SCALE).astype(BF16)
        jblk = lax.broadcasted_iota(jnp.int32, (SEQ, LANES), 0) // SEL_BLOCK
        lane = lax.broadcasted_iota(jnp.int32, (SEQ, LANES), 1)
        kaug_s[:, HEAD_DIM:] = jnp.where(jblk == lane, 1.0, 0.0).astype(BF16)
        kw_s[...] = (_row_rms(k2_ref[...], knw_ref[2:3, :]) * EXP2_SCALE).astype(BF16)
        ones_row = jnp.where(lax.broadcasted_iota(jnp.int32, (V_ROWS - HEAD_DIM, TK), 0) == 0,
                             1.0, 0.0).astype(BF16)
        for t in range(SEQ // TK):
            vst_s[t, :HEAD_DIM, :] = v1_ref[t * TK:(t + 1) * TK, :].T.astype(BF16)
            vst_s[t, HEAD_DIM:, :] = ones_row
            vwt_s[t, :HEAD_DIM, :] = v2_ref[t * TK:(t + 1) * TK, :].T.astype(BF16)
            vwt_s[t, HEAD_DIM:, :] = ones_row

        def compress(x_ref, pe_ref, w_ref):
            acc_lo = jnp.zeros((LANES, HEAD_DIM), F32)
            acc_hi = jnp.zeros((LANES, HEAD_DIM), F32)
            for j in range(CMP_STRIDE):
                y = x_ref[pl.ds(j, LANES, stride=CMP_STRIDE), :]
                acc_lo += _dot((y + pe_ref[j:j + 1, :]).astype(BF16),
                               w_ref[j * HEAD_DIM:(j + 1) * HEAD_DIM, :])
                jh = CMP_STRIDE + j
                acc_hi += _dot((y + pe_ref[jh:jh + 1, :]).astype(BF16),
                               w_ref[jh * HEAD_DIM:(jh + 1) * HEAD_DIM, :])
            return acc_lo + pltpu.roll(acc_hi, LANES - 1, 0)

        kc_s[...] = _row_rms(compress(k0_ref, pek_ref, wck_ref), knw_ref[0:1, :]).astype(BF16)
        vct_s[...] = compress(v0_ref, pev_ref, wcv_ref).T.astype(BF16)

    side_o_ref[:, :side_half] = _dot_nt(side_a_ref[...], side_bt_ref[:side_half, :])
    _cast_slabs(cast_in + cast_out, n_casts)

    t0 = qt * TQ
    q = q_ref[...]
    qts = []
    for r in range(GROUP_SIZE):
        qts.append(_row_rms(q[:, r * HEAD_DIM:(r + 1) * HEAD_DIM], qnw_ref[...]).T)
    qt_stack = jnp.concatenate(qts, axis=1).astype(BF16)

    m0 = jnp.full((1, cols_all), NEG_INF, F32)
    a0 = jnp.zeros((V_ROWS, cols_all), F32)

    def online(carry, tiles):
        m, acc = carry
        m_new = m
        for u, _ in tiles:
            m_new = jnp.maximum(m_new, jnp.max(u, axis=0, keepdims=True))
        acc = jnp.exp2(m - m_new) * acc
        for u, vt in tiles:
            acc = acc + _dot(vt, jnp.exp2(u - m_new).astype(BF16))
        return m_new, acc

    def online_independent(carry, tiles):
        m, acc = carry
        parts = []
        for u, vt in tiles:
            mt = jnp.max(u, axis=0, keepdims=True)
            parts.append((mt, _dot(vt, jnp.exp2(u - mt).astype(BF16))))
        m_new = m
        for mt, _ in parts:
            m_new = jnp.maximum(m_new, mt)
        acc = jnp.exp2(m - m_new) * acc
        for mt, pv in parts:
            acc = acc + jnp.exp2(mt - m_new) * pv
        return m_new, acc

    def normalised(acc):
        return acc[:HEAD_DIM] * (1.0 / acc[HEAD_DIM:HEAD_DIM + 1])

    def key_rows(kt):
        return pl.ds(pl.multiple_of(kt * TK, TK), TK)

    sc = _dot(kc_s[...], qt_stack) * SCALE + biasc_ref[0, 0]
    tpos = t0 + (lax.broadcasted_iota(jnp.int32, (LANES, cols_all), 1) & (TQ - 1))
    nrow = lax.broadcasted_iota(jnp.int32, (LANES, cols_all), 0)
    maskc = tpos >= CMP_STRIDE * nrow + (CMP_BLOCK - 1)
    scm = jnp.where(maskc, sc, NEG_INF)
    mc = jnp.max(scm, axis=0, keepdims=True)
    ec = jnp.where(maskc, jnp.exp(scm - mc), 0.0)
    lc = jnp.sum(ec, axis=0, keepdims=True)
    pc = ec * (1.0 / jnp.where(lc > 0.0, lc, 1.0))
    o_c = _dot(vct_s[...], pc.astype(BF16))

    psum = pc[:, 0:TQ] + pc[:, TQ:2 * TQ] + pc[:, 2 * TQ:3 * TQ] + pc[:, 3 * TQ:4 * TQ]
    p_hi = psum.astype(BF16)
    p_lo = (psum - p_hi.astype(F32)).astype(BF16)
    imp = (_dot(c2st_ref[...], p_hi) + _dot(c2st_ref[...], p_lo))[0:N_SEL]

    tiles = []
    for dist, kind in ((2, 2), (1, 1), (0, 0)):
        kt = jnp.maximum(qt - dist, 0)
        u = _dot(kw_s[key_rows(kt), :], qt_stack) + dm_ref[0, kind]
        if dist > 0:
            u = u + jnp.where(qt >= dist, 0.0, NEG_INF)
        tiles.append((u, vwt_s[kt]))
    _, acc_w = online_independent((m0, a0), tiles)
    o_w = normalised(acc_w)

    tq_pos = t0 + lax.broadcasted_iota(jnp.int32, (N_SEL, TQ), 1)
    sidx = lax.broadcasted_iota(jnp.int32, (N_SEL, TQ), 0)
    cur = tq_pos >> (SEL_BLOCK.bit_length() - 1)
    score = jnp.where(sidx <= cur, imp, NEG_INF)
    for forced_blk in (cur - 1, cur, jnp.zeros_like(cur)):
        score = jnp.where(sidx == forced_blk, FORCE_SCORE, score)
    rank = jnp.zeros((N_SEL, TQ), F32)
    for sp in range(N_SEL):
        row = score[sp:sp + 1, :]
        tie = jnp.where(sidx > sp, 1.0, 0.0)
        rank = rank + jnp.where(row > score, 1.0, jnp.where(row == score, tie, 0.0))
    selb = jnp.where(rank < TOP_N, 0.0, NEG_INF)
    selb = jnp.concatenate([selb, jnp.zeros((HEAD_DIM - N_SEL, TQ), F32)], axis=0).astype(BF16)
    qt_aug = jnp.concatenate([qt_stack, jnp.concatenate([selb] * GROUP_SIZE, axis=1)], axis=0)

    def sel_tile(kt):
        return _dot(kaug_s[key_rows(kt), :], qt_aug), vst_s[kt]

    n_far = jnp.maximum(qt - 1, 0)
    carry = lax.fori_loop(0, n_far >> 1,
                          lambda i, c: online_independent(c, [sel_tile(2 * i), sel_tile(2 * i + 1)]),
                          (m0, a0))
    carry = lax.fori_loop(0, n_far & 1, lambda i, c: online(c, [sel_tile(n_far - 1)]), carry)
    kt1 = jnp.maximum(qt - 1, 0)
    u1, vt1 = sel_tile(kt1)
    u0, vt0 = sel_tile(qt)
    _, acc_s = online_independent(
        carry, [(u1 + dm_ref[0, 1] + jnp.where(qt >= 1, 0.0, NEG_INF), vt1), (u0 + dm_ref[0, 0], vt0)])
    o_s = normalised(acc_s)

    side_o_ref[:, side_half:] = _dot_nt(side_a_ref[...], side_bt_ref[side_half:, :])

    gt_s[...] = jax.nn.sigmoid(gate_ref[...]).T
    head0 = pl.program_id(1) * GROUP_SIZE
    for r in range(GROUP_SIZE):
        cs = slice(r * TQ, (r + 1) * TQ)
        c = (head0 + r) * N_BRANCH
        o = (gt_s[pl.ds(c, 1), :] * o_c[:, cs] + gt_s[pl.ds(c + 1, 1), :] * o_s[:, cs]
             + gt_s[pl.ds(c + 2, 1), :] * o_w[:, cs])
        o_ref[:, r * HEAD_DIM:(r + 1) * HEAD_DIM] = o.T.astype(o_ref.dtype)


def _cmp_to_sel_t():
    ss = np.arange(LANES)[:, None] * SEL_BLOCK
    cs = np.arange(LANES)[None, :] * CMP_STRIDE
    ov = np.clip(np.minimum(cs + CMP_BLOCK, ss + SEL_BLOCK) - np.maximum(cs, ss), 0, None)
    m = ov.astype(np.float32) / np.float32(CMP_BLOCK)
    m[N_SEL:, :] = 0.0
    m[:, N_CMP:] = 0.0
    return jnp.asarray(m, BF16)


def _nsa_attention(proj, gate_logits, batch, q_norm_w, k_norm_w, pe_k, pe_v, w_cmp_k, w_cmp_v, biasc, dm,
                   side_a, side_bt, side_tm=1024, side_tn=512, casts=()):
    kvb = COL_KV // LANES

    def kv_spec(branch, which):
        base = kvb + (branch * 2 + which) * N_KV_GROUPS
        return pl.BlockSpec((SEQ, LANES), lambda b, g, qt: (b, base + g))

    full = lambda shape: pl.BlockSpec(shape, lambda b, g, qt: (0,) * len(shape))
    grid = (batch, N_KV_GROUPS, NQ)
    c_in, c_out, c_shapes = _cast_specs(casts, grid)
    side_m, side_k = side_a.shape
    side_n = side_bt.shape[0]
    n_j = side_n // side_tn
    assert (side_m // side_tm) * n_j == math.prod(grid)
    step = lambda b, g, qt: (b * N_KV_GROUPS + g) * NQ + qt
    side_in = [pl.BlockSpec((side_tm, side_k), lambda b, g, qt: (step(b, g, qt) // n_j, 0),
                            pipeline_mode=pl.Buffered(1)),
               pl.BlockSpec((side_tn, side_k), lambda b, g, qt: (step(b, g, qt) % n_j, 0))]
    side_out = pl.BlockSpec((side_tm, side_tn), lambda b, g, qt: (step(b, g, qt) // n_j, step(b, g, qt) % n_j))
    return pl.pallas_call(
        functools.partial(_nsa_kernel, n_casts=len(casts)),
        out_shape=[jax.ShapeDtypeStruct((batch * SEQ, NSA_WIDTH), BF16),
                   jax.ShapeDtypeStruct((side_m, side_n), F32)] + c_shapes,
        grid=grid,
        in_specs=[pl.BlockSpec((TQ, GROUP_SIZE * HEAD_DIM), lambda b, g, qt: (b * NQ + qt, g)),
                  kv_spec(0, 0), kv_spec(0, 1), kv_spec(1, 0), kv_spec(1, 1),
                  kv_spec(2, 0), kv_spec(2, 1),
                  pl.BlockSpec((TQ, LANES), lambda b, g, qt: (b * NQ + qt, 0)),
                  full((1, HEAD_DIM)), full((N_BRANCH, HEAD_DIM)),
                  full((CMP_BLOCK, HEAD_DIM)), full((CMP_BLOCK, HEAD_DIM)),
                  full((CMP_BLOCK * HEAD_DIM, HEAD_DIM)), full((CMP_BLOCK * HEAD_DIM, HEAD_DIM)),
                  pl.BlockSpec((1, 1, LANES, GROUP_SIZE * TQ), lambda b, g, qt: (g, qt, 0, 0)),
                  pl.BlockSpec((1, 3, TK, GROUP_SIZE * TQ), lambda b, g, qt: (g, 0, 0, 0),
                               pipeline_mode=pl.Buffered(1)),
                  full((LANES, LANES))] + side_in + c_in,
        out_specs=[pl.BlockSpec((TQ, GROUP_SIZE * HEAD_DIM), lambda b, g, qt: (b * NQ + qt, g)),
                   side_out] + c_out,
        scratch_shapes=[pltpu.VMEM((LANES, HEAD_DIM), BF16),
                        pltpu.VMEM((HEAD_DIM, LANES), BF16),
                        pltpu.VMEM((SEQ, 2 * HEAD_DIM), BF16),
                        pltpu.VMEM((SEQ, HEAD_DIM), BF16),
                        pltpu.VMEM((SEQ // TK, V_ROWS, TK), BF16),
                        pltpu.VMEM((SEQ // TK, V_ROWS, TK), BF16),
                        pltpu.VMEM((LANES, TQ), F32)],
        compiler_params=pltpu.CompilerParams(
            dimension_semantics=("arbitrary", "arbitrary", "arbitrary"),
            vmem_limit_bytes=NSA_VMEM_LIMIT),
    )(proj, proj, proj, proj, proj, proj, proj, gate_logits,
      q_norm_w.reshape(1, HEAD_DIM), k_norm_w, pe_k, pe_v,
      w_cmp_k.astype(BF16), w_cmp_v.astype(BF16), biasc, dm, _cmp_to_sel_t(),
      side_a, side_bt, *[c[0] for c in casts])


def _ret_kernel(lg_ref, q_ref, k_ref, v_ref, g_ref, cos_ref, sin_ref, gnw_ref, *rest, n_casts):
    o_ref = rest[n_casts]
    state_ref = rest[-1]
    _cast_slabs(rest[:n_casts] + rest[n_casts + 1:-1], n_casts)
    c_len = RET_CHUNK
    ii = lax.broadcasted_iota(jnp.int32, (c_len, c_len), 0)
    jj = lax.broadcasted_iota(jnp.int32, (c_len, c_len), 1)
    diff = (ii - jj).astype(F32)
    icol = lax.broadcasted_iota(jnp.int32, (c_len, 1), 0).astype(F32)
    half = RET_KEY_DIM // 2
    state_ref[...] = jnp.zeros_like(state_ref)
    tables = []
    for hh in range(RET_HEADS_PER_STEP):
        lg = lg_ref[pl.program_id(1) * RET_HEADS_PER_STEP + hh]
        tables.append((
            jnp.where(diff >= 0.0, jnp.exp(jnp.maximum(diff, 0.0) * lg), 0.0),
            jnp.exp((icol + 1.0) * lg),
            jnp.exp((c_len - 1.0 - icol) * lg),
            jnp.exp(jnp.full((1, RET_VAL_DIM), float(c_len), F32) * lg)))

    def body(c, _):
        rows = pl.ds(pl.multiple_of(c * c_len, c_len), c_len)
        cos = cos_ref[rows, :]
        sin = sin_ref[rows, :]

        def rot(x):
            x1, x2 = x[:, :half], x[:, half:]
            return jnp.concatenate([x1 * cos - x2 * sin, x2 * cos + x1 * sin], axis=-1)

        for hh, (decay_in, xi, zeta, chunk_decay) in enumerate(tables):
            kcols = slice(hh * RET_KEY_DIM, (hh + 1) * RET_KEY_DIM)
            vcols = slice(hh * RET_VAL_DIM, (hh + 1) * RET_VAL_DIM)
            qr = rot(q_ref[rows, kcols])
            kr = rot(k_ref[rows, kcols]) * (RET_KEY_DIM ** -0.5)
            vb = v_ref[rows, vcols].astype(BF16)
            scores = _dot_nt(qr.astype(BF16), kr.astype(BF16)) * decay_in
            inner = _dot(scores.astype(BF16), vb)
            st = state_ref[hh]
            cross = _dot((qr * xi).astype(BF16), st.astype(BF16))
            kv = _dot((kr * zeta).T.astype(BF16), vb)
            state_ref[hh] = chunk_decay * st + kv
            y = inner + cross
            y = y * lax.rsqrt(jnp.mean(y * y, axis=-1, keepdims=True) + EPS) * gnw_ref[:, vcols]
            g = g_ref[rows, vcols]
            o_ref[rows, vcols] = (g * jax.nn.sigmoid(g) * y).astype(o_ref.dtype)
        return 0

    lax.fori_loop(0, SEQ // c_len, body, 0, unroll=4)


def _retention(proj, batch, gn_w, casts=()):
    pos = np.arange(SEQ, dtype=np.float64)
    half = RET_KEY_DIM // 2
    inv = np.exp(-np.linspace(0.0, 1.0, half) * math.log(ROPE_BASE)).astype(np.float32)
    ang = pos.astype(np.float32)[:, None] * inv[None, :]
    cos = jnp.asarray(np.cos(ang.astype(np.float64)), F32)
    sin = jnp.asarray(np.sin(ang.astype(np.float64)), F32)
    log_gamma = jnp.asarray(np.log(1.0 - 2.0 ** (-5.0 - np.arange(N_RET_HEADS))), F32)
    w = RET_KEY_DIM * RET_HEADS_PER_STEP

    def col_spec(col0):
        return pl.BlockSpec((SEQ, w), lambda b, h: (b, col0 // w + h))

    grid = (batch, N_RET_HEADS // RET_HEADS_PER_STEP)
    c_in, c_out, c_shapes = _cast_specs(casts, grid)
    return pl.pallas_call(
        functools.partial(_ret_kernel, n_casts=len(casts)),
        out_shape=[jax.ShapeDtypeStruct((batch * SEQ, RET_WIDTH), BF16)] + c_shapes,
        grid=grid,
        in_specs=[pl.BlockSpec(memory_space=pltpu.SMEM),
                  col_spec(COL_QR), col_spec(COL_KR), col_spec(COL_VR), col_spec(COL_GR),
                  pl.BlockSpec((SEQ, half), lambda b, h: (0, 0)),
                  pl.BlockSpec((SEQ, half), lambda b, h: (0, 0)),
                  pl.BlockSpec((1, w), lambda b, h: (0, h))] + c_in,
        out_specs=[pl.BlockSpec((SEQ, w), lambda b, h: (b, h))] + c_out,
        scratch_shapes=[pltpu.VMEM((RET_HEADS_PER_STEP, RET_KEY_DIM, RET_VAL_DIM), F32)],
        compiler_params=pltpu.CompilerParams(dimension_semantics=("arbitrary", "arbitrary"),
                                             vmem_limit_bytes=VMEM_LIMIT),
    )(log_gamma, proj, proj, proj, proj, cos, sin, gn_w.reshape(1, RET_WIDTH), *[c[0] for c in casts])


def _layer(x, norm1_w, w_in, nsa_q_norm_w, nsa_k_norm_w, cmp_pe_k, cmp_pe_v, w_cmp_k, w_cmp_v,
           rel_bias, ret_gn_w, w_out, norm2_w, w_gate, w_up, w_down):
    batch = x.shape[0]
    rows = batch * SEQ
    xf = x.reshape(rows, D_MODEL)
    w_in_t = w_in.T
    gate0 = W_IN_NSA_COLS
    gate1 = gate0 + W_IN_GATE_COLS
    h, gate_logits = _rmsnorm_with_projection(xf, norm1_w, w_in_t, gate0, LANES)
    ret_cols = 4 * RET_WIDTH
    nsa_tn, nsa_tm = 1024, 512
    nsa_steps = (W_IN_NSA_COLS // nsa_tn) * (rows // nsa_tm)
    ret_slabs = 1 << (nsa_steps.bit_length() - 1)
    proj_nsa, w_ret_t = _in_projection(h, w_in_t, 0, W_IN_NSA_COLS, nsa_tn, tm=nsa_tm,
                                       casts=((w_in_t, gate1, ret_cols, ret_slabs),))
    biasc, dm = _bias_tables(rel_bias)
    n_steps = batch * N_KV_GROUPS * NQ
    o_nsa, proj_ret, w_gate_b, w_up_b, w_out_b = _nsa_attention(
        proj_nsa, gate_logits, batch, nsa_q_norm_w, nsa_k_norm_w, cmp_pe_k, cmp_pe_v,
        w_cmp_k, w_cmp_v, biasc, dm, h, w_ret_t,
        casts=((w_gate, 0, D_MODEL, n_steps), (w_up, 0, D_MODEL, n_steps), (w_out, 0, D_MODEL, n_steps)))
    o_ret, = _retention(proj_ret, batch, ret_gn_w)
    x1, = _out_projection(o_nsa, o_ret, w_out_b, xf, tn=1024)
    hf = _rmsnorm(x1, norm2_w)
    up_tm, up_tn = 2048, 256
    up_steps = (rows // up_tm) * (D_FF // up_tn)
    mid, w_down_b = _ffn_up(hf, w_gate_b, w_up_b, tm=up_tm, tn=up_tn, casts=((w_down, 0, D_FF, up_steps),))
    out = _ffn_down(mid, w_down_b, x1, tm=512, tn=512, tk=D_FF)
    return out.reshape(batch, SEQ, D_MODEL)


def kernel(x, norm1_w, w_in, nsa_q_norm_w, nsa_k_norm_w, cmp_pe_k, cmp_pe_v, w_cmp_k, w_cmp_v,
           rel_bias, ret_gn_w, w_out, norm2_w, w_gate, w_up, w_down):
    for l in range(norm1_w.shape[0]):
        x = _layer(x, norm1_w[l], w_in[l], nsa_q_norm_w[l], nsa_k_norm_w[l], cmp_pe_k[l], cmp_pe_v[l],
                   w_cmp_k[l], w_cmp_v[l], rel_bias, ret_gn_w[l], w_out[l], norm2_w[l],
                   w_gate[l], w_up[l], w_down[l])
    return x
```

```python
import functools
import math

import numpy as np
import jax
import jax.numpy as jnp
from jax import lax
from jax.experimental import pallas as pl
from jax.experimental.pallas import tpu as pltpu

F32 = jnp.float32
BF16 = jnp.bfloat16

D_MODEL = 4096
SEQ = 2048
HEAD_DIM = 128
N_NSA_HEADS = 16
N_KV_GROUPS = 4
GROUP_SIZE = 4
N_BRANCH = 3
CMP_BLOCK = 32
CMP_STRIDE = 16
SEL_BLOCK = 64
TOP_N = 16
WINDOW = 512
RET_KEY_DIM = 256
RET_VAL_DIM = 256
N_RET_HEADS = 8
RET_CHUNK = 128
ROPE_BASE = 10000.0
NSA_WIDTH = 2048
RET_WIDTH = 2048
D_FF = 11008
NUM_BUCKETS = 32
MAX_DISTANCE = 128
EPS = 1e-6
NEG_INF = -1e30
FORCE_SCORE = 1e9
N_SEL = SEQ // SEL_BLOCK
N_CMP = (SEQ - CMP_BLOCK) // CMP_STRIDE + 1
SCALE = HEAD_DIM ** -0.5
LOG2E = math.log2(math.e)
EXP2_SCALE = SCALE * LOG2E

LANES = 128
VMEM_LIMIT = 56 * 1024 * 1024
NSA_VMEM_LIMIT = 60 * 1024 * 1024

TQ = 256
TK = 256
NQ = SEQ // TQ
RET_HEADS_PER_STEP = 2
FFN_UP_SUBBLOCKS = 4
V_ROWS = HEAD_DIM + 16
assert TQ == TK and WINDOW == 2 * TK and TK % SEL_BLOCK == 0 and N_SEL <= HEAD_DIM and TK >= MAX_DISTANCE

W_IN_NSA_COLS = NSA_WIDTH + N_BRANCH * 2 * N_KV_GROUPS * HEAD_DIM
W_IN_GATE_COLS = N_NSA_HEADS * N_BRANCH
COL_KV = NSA_WIDTH
COL_QR = 0
COL_KR = 2048
COL_VR = 4096
COL_GR = 6144


def _bucket_thresholds():
    rel = np.arange(0, 4 * MAX_DISTANCE)
    max_exact = NUM_BUCKETS // 2
    nf = np.maximum(rel, 1).astype(np.float64)
    large = max_exact + np.floor(np.log(nf / max_exact) / math.log(MAX_DISTANCE / max_exact)
                                 * (NUM_BUCKETS - max_exact)).astype(np.int64)
    large = np.minimum(large, NUM_BUCKETS - 1)
    b = np.where(rel < max_exact, rel, large)
    return [int(np.argmax(b >= k)) for k in range(NUM_BUCKETS)]


_THR = _bucket_thresholds()


def _dot(a, b):
    return jnp.dot(a, b, preferred_element_type=F32)


def _dot_nt(a, b):
    return lax.dot_general(a, b, (((1,), (1,)), ((), ())), preferred_element_type=F32)


def _rms_kernel(x_ref, w_ref, o_ref):
    x = x_ref[...]
    y = x * lax.rsqrt(jnp.mean(x * x, axis=-1, keepdims=True) + EPS)
    o_ref[...] = (y * w_ref[...]).astype(o_ref.dtype)


def _rmsnorm(x, w, tm=512):
    m, d = x.shape
    return pl.pallas_call(
        _rms_kernel,
        out_shape=jax.ShapeDtypeStruct((m, d), BF16),
        grid=(m // tm,),
        in_specs=[pl.BlockSpec((tm, d), lambda i: (i, 0)),
                  pl.BlockSpec((1, d), lambda i: (0, 0))],
        out_specs=pl.BlockSpec((tm, d), lambda i: (i, 0)),
        compiler_params=pltpu.CompilerParams(dimension_semantics=("arbitrary",),
                                             vmem_limit_bytes=VMEM_LIMIT),
    )(x, w.reshape(1, d))


def _rms_proj_kernel(x_ref, w_ref, pt_ref, o_ref, p_ref, pbf_ref):
    @pl.when(pl.program_id(0) == 0)
    def _():
        pbf_ref[...] = pt_ref[...].astype(BF16)

    x = x_ref[...]
    y = x * lax.rsqrt(jnp.mean(x * x, axis=-1, keepdims=True) + EPS)
    h = (y * w_ref[...]).astype(o_ref.dtype)
    o_ref[...] = h
    p_ref[...] = _dot_nt(h, pbf_ref[...])


def _rmsnorm_with_projection(x, w, w_t, row0, n_rows, tm=512):
    m, d = x.shape
    sub = 8
    assert row0 % sub == 0 and n_rows % LANES == 0
    return pl.pallas_call(
        _rms_proj_kernel,
        out_shape=(jax.ShapeDtypeStruct((m, d), BF16), jax.ShapeDtypeStruct((m, n_rows), F32)),
        grid=(m // tm,),
        in_specs=[pl.BlockSpec((tm, d), lambda i: (i, 0)),
                  pl.BlockSpec((1, d), lambda i: (0, 0)),
                  pl.BlockSpec((pl.Element(n_rows), pl.Element(d)), lambda i: ((row0 // sub) * sub, 0))],
        out_specs=(pl.BlockSpec((tm, d), lambda i: (i, 0)),
                   pl.BlockSpec((tm, n_rows), lambda i: (i, 0))),
        scratch_shapes=[pltpu.VMEM((n_rows, d), BF16)],
        compiler_params=pltpu.CompilerParams(dimension_semantics=("arbitrary",),
                                             vmem_limit_bytes=VMEM_LIMIT),
    )(x, w.reshape(1, d), w_t)


def _cast_spec(grid, row0, n_rows, n_cols, n_slabs):
    slab = n_rows // n_slabs
    sub = 8
    n_steps = math.prod(grid)
    assert slab * n_slabs == n_rows and slab % 16 == 0 and n_cols % LANES == 0
    assert row0 % sub == 0 and n_slabs <= n_steps

    def slab_index(ids):
        step = ids[0]
        for dim, idx in zip(grid[1:], ids[1:]):
            step = step * dim + idx
        return step if n_slabs == n_steps else jnp.minimum(step, n_slabs - 1)

    in_spec = pl.BlockSpec((pl.Element(slab), pl.Element(n_cols)),
                           lambda *ids: ((row0 // sub + slab_index(ids) * (slab // sub)) * sub, 0))
    out_spec = pl.BlockSpec((slab, n_cols), lambda *ids: (slab_index(ids), 0))
    return in_spec, out_spec, jax.ShapeDtypeStruct((n_rows, n_cols), BF16)


def _cast_specs(casts, grid):
    specs = [_cast_spec(grid, row0, n_rows, arr.shape[1], n_slabs) for arr, row0, n_rows, n_slabs in casts]
    return [s[0] for s in specs], [s[1] for s in specs], [s[2] for s in specs]


def _cast_slabs(refs, n_casts):
    for src, dst in zip(refs[:n_casts], refs[len(refs) - n_casts:]):
        dst[...] = src[...].astype(dst.dtype)


def _inproj_kernel(a_ref, wslab_ref, *rest, n_casts):
    o_ref = rest[n_casts]
    wbf_ref = rest[-1]
    j = pl.program_id(0)
    slab = wslab_ref.shape[0]
    rows = pl.ds(pl.multiple_of(pl.program_id(1) * slab, slab), slab)
    wbf_ref[j % 2, rows, :] = wslab_ref[...].astype(BF16)

    @pl.when(j > 0)
    def _():
        o_ref[...] = _dot_nt(a_ref[...], wbf_ref[(j + 1) % 2])

    _cast_slabs(rest[:n_casts] + rest[n_casts + 1:-1], n_casts)


def _in_projection(a, w_t, row0, n_cols, tn, tm=1024, casts=()):
    m, k = a.shape
    n_tiles, n_i = n_cols // tn, m // tm
    grid = (n_tiles + 1, n_i)
    c_in, c_out, c_shapes = _cast_specs(casts, grid)
    sub = 8
    slab = tn // n_i
    assert row0 % sub == 0 and slab % 16 == 0 and slab * n_i == tn

    def slab_row(j, i):
        return (row0 // sub + jnp.minimum(j, n_tiles - 1) * (tn // sub) + i * (slab // sub)) * sub

    return pl.pallas_call(
        functools.partial(_inproj_kernel, n_casts=len(casts)),
        out_shape=[jax.ShapeDtypeStruct((m, n_cols), F32)] + c_shapes,
        grid=grid,
        in_specs=[pl.BlockSpec((tm, k), lambda j, i: (jnp.where(j > 0, i, 0), 0)),
                  pl.BlockSpec((pl.Element(slab), pl.Element(k)), lambda j, i: (slab_row(j, i), 0))] + c_in,
        out_specs=[pl.BlockSpec((tm, tn), lambda j, i: (jnp.where(j > 0, i, 0), jnp.maximum(j - 1, 0)))] + c_out,
        scratch_shapes=[pltpu.VMEM((2, tn, k), BF16)],
        compiler_params=pltpu.CompilerParams(dimension_semantics=("arbitrary", "arbitrary"),
                                             vmem_limit_bytes=VMEM_LIMIT),
    )(a, w_t, *[c[0] for c in casts])


def _outproj_kernel(a1_ref, a2_ref, w1_ref, w2_ref, x_ref, *rest, n_casts):
    o_ref = rest[n_casts]
    acc = _dot(a1_ref[...], w1_ref[...]) + _dot(a2_ref[...], w2_ref[...])
    o_ref[...] = x_ref[...] + acc
    _cast_slabs(rest[:n_casts] + rest[n_casts + 1:], n_casts)


def _out_projection(o_nsa, o_ret, w_out, x, tm=1024, tn=512, casts=()):
    m = x.shape[0]
    kh = NSA_WIDTH
    grid = (m // tm, D_MODEL // tn)
    c_in, c_out, c_shapes = _cast_specs(casts, grid)
    return pl.pallas_call(
        functools.partial(_outproj_kernel, n_casts=len(casts)),
        out_shape=[jax.ShapeDtypeStruct((m, D_MODEL), F32)] + c_shapes,
        grid=grid,
        in_specs=[pl.BlockSpec((tm, kh), lambda i, j: (i, 0)),
                  pl.BlockSpec((tm, kh), lambda i, j: (i, 0)),
                  pl.BlockSpec((kh, tn), lambda i, j: (0, j)),
                  pl.BlockSpec((kh, tn), lambda i, j: (1, j)),
                  pl.BlockSpec((tm, tn), lambda i, j: (i, j))] + c_in,
        out_specs=[pl.BlockSpec((tm, tn), lambda i, j: (i, j))] + c_out,
        compiler_params=pltpu.CompilerParams(dimension_semantics=("arbitrary", "arbitrary"),
                                             vmem_limit_bytes=VMEM_LIMIT),
    )(o_nsa, o_ret, w_out, w_out, x, *[c[0] for c in casts])


def _ffn_up_kernel(h_ref, wg_ref, wu_ref, *rest, n_casts):
    o_ref = rest[n_casts]
    sub_rows = h_ref.shape[0] // FFN_UP_SUBBLOCKS
    for s in range(FFN_UP_SUBBLOCKS):
        rows = slice(s * sub_rows, (s + 1) * sub_rows)
        h = h_ref[rows, :]
        g = _dot(h, wg_ref[...])
        u = _dot(h, wu_ref[...])
        o_ref[rows, :] = (g * jax.nn.sigmoid(g) * u).astype(o_ref.dtype)
    _cast_slabs(rest[:n_casts] + rest[n_casts + 1:], n_casts)


def _ffn_up(h, w_gate, w_up, tm=1024, tn=256, casts=()):
    m, k = h.shape
    n = w_gate.shape[1]
    grid = (m // tm, n // tn)
    c_in, c_out, c_shapes = _cast_specs(casts, grid)
    return pl.pallas_call(
        functools.partial(_ffn_up_kernel, n_casts=len(casts)),
        out_shape=[jax.ShapeDtypeStruct((m, n), BF16)] + c_shapes,
        grid=grid,
        in_specs=[pl.BlockSpec((tm, k), lambda i, j: (i, 0)),
                  pl.BlockSpec((k, tn), lambda i, j: (0, j)),
                  pl.BlockSpec((k, tn), lambda i, j: (0, j))] + c_in,
        out_specs=[pl.BlockSpec((tm, tn), lambda i, j: (i, j))] + c_out,
        compiler_params=pltpu.CompilerParams(dimension_semantics=("arbitrary", "arbitrary"),
                                             vmem_limit_bytes=VMEM_LIMIT),
    )(h, w_gate, w_up, *[c[0] for c in casts])


def _ffn_down_kernel(a_ref, b_ref, r_ref, o_ref):
    o_ref[...] = r_ref[...] + _dot(a_ref[...], b_ref[...])


def _ffn_down(a, w_down, x, tm=1024, tn=512, tk=5504):
    m, k = a.shape
    n = w_down.shape[1]
    res = x
    for kk in range(k // tk):
        res = pl.pallas_call(
            _ffn_down_kernel,
            out_shape=jax.ShapeDtypeStruct((m, n), F32),
            grid=(m // tm, n // tn),
            in_specs=[pl.BlockSpec((tm, tk), lambda i, j, kk=kk: (i, kk)),
                      pl.BlockSpec((tk, tn), lambda i, j, kk=kk: (kk, j)),
                      pl.BlockSpec((tm, tn), lambda i, j: (i, j))],
            out_specs=pl.BlockSpec((tm, tn), lambda i, j: (i, j)),
            compiler_params=pltpu.CompilerParams(dimension_semantics=("arbitrary", "arbitrary"),
                                                 vmem_limit_bytes=VMEM_LIMIT),
        )(a, w_down, res)
    return res


def _bias_value(rel, relb_ref, h):
    val = jnp.full(rel.shape, relb_ref[0, h], F32)
    for k in range(1, NUM_BUCKETS):
        val = jnp.where(rel >= _THR[k], relb_ref[k, h], val)
    return val


def _bias_tables_kernel(relb_ref, biasc_ref, dm_ref):
    h = pl.program_id(0)
    sub = 8
    j = lax.broadcasted_iota(jnp.int32, (sub, 2 * SEQ), 1)
    row_c = _bias_value(j - SEQ - (CMP_BLOCK - 1), relb_ref, h)[0:1]
    toe_c = pltpu.roll(jnp.broadcast_to(row_c, (LANES, 2 * SEQ)), 0, 1,
                       stride=CMP_STRIDE, stride_axis=0)
    for qt in range(NQ):
        biasc_ref[0, qt] = toe_c[:, SEQ + qt * TQ:SEQ + (qt + 1) * TQ]
    far = relb_ref[NUM_BUCKETS - 1, h]
    j2 = lax.broadcasted_iota(jnp.int32, (sub, 2 * TK), 1)
    b = lax.broadcasted_iota(jnp.int32, (TK, TQ), 0)
    a = lax.broadcasted_iota(jnp.int32, (TK, TQ), 1)
    for kind in range(2):
        row = ((_bias_value(j2 + (kind - 1) * TK, relb_ref, h) - far) * LOG2E)[0:1]
        toe = pltpu.roll(jnp.broadcast_to(row, (TK, 2 * TK)), 0, 1, stride=1, stride_axis=0)
        tile = toe[:, TK:]
        dm_ref[0, kind] = tile + jnp.where(a >= b, 0.0, NEG_INF) if kind == 0 else tile
    dm_ref[0, 2] = jnp.where(b > a, 0.0, NEG_INF)


def _bias_tables(rel_bias):
    return pl.pallas_call(
        _bias_tables_kernel,
        out_shape=(jax.ShapeDtypeStruct((N_KV_GROUPS, NQ, LANES, GROUP_SIZE * TQ), F32),
                   jax.ShapeDtypeStruct((N_KV_GROUPS, 3, TK, GROUP_SIZE * TQ), F32)),
        grid=(N_NSA_HEADS,),
        in_specs=[pl.BlockSpec(memory_space=pltpu.SMEM)],
        out_specs=(pl.BlockSpec((1, NQ, LANES, TQ), lambda h: (h // GROUP_SIZE, 0, 0, h % GROUP_SIZE)),
                   pl.BlockSpec((1, 3, TK, TQ), lambda h: (h // GROUP_SIZE, 0, 0, h % GROUP_SIZE))),
        compiler_params=pltpu.CompilerParams(dimension_semantics=("arbitrary",),
                                             vmem_limit_bytes=VMEM_LIMIT),
    )(rel_bias)


def _row_rms(x, w):
    return x * lax.rsqrt(jnp.mean(x * x, axis=-1, keepdims=True) + EPS) * w


N_NSA_INPUTS = 17
N_NSA_SCRATCH = 7


def _nsa_kernel(*refs, n_casts):
    (q_ref, k0_ref, v0_ref, k1_ref, v1_ref, k2_ref, v2_ref, gate_ref,
     qnw_ref, knw_ref, pek_ref, pev_ref, wck_ref, wcv_ref, biasc_ref, dm_ref, c2st_ref) = refs[:N_NSA_INPUTS]
    side_a_ref, side_bt_ref = refs[N_NSA_INPUTS:N_NSA_INPUTS + 2]
    n_in = N_NSA_INPUTS + 2 + n_casts
    cast_in = refs[N_NSA_INPUTS + 2:n_in]
    o_ref, side_o_ref = refs[n_in:n_in + 2]
    cast_out = refs[n_in + 2:n_in + 2 + n_casts]
    kc_s, vct_s, kaug_s, kw_s, vst_s, vwt_s, gt_s = refs[len(refs) - N_NSA_SCRATCH:]
    qt = pl.program_id(2)
    cols_all = GROUP_SIZE * TQ
    side_half = side_bt_ref.shape[0] // 2

    @pl.when(qt == 0)
    def _prologue():
        kaug_s[:, :HEAD_DIM] = (_row_rms(k1_ref[...], knw_ref[1:2, :]) * EXP2_SCALE).astype(BF16)
        jblk = lax.broadcasted_iota(jnp.int32, (SEQ, LANES), 0) // SEL_BLOCK
        lane = lax.broadcasted_iota(jnp.int32, (SEQ, LANES), 1)
        kaug_s[:, HEAD_DIM:] = jnp.where(jblk == lane, 1.0, 0.0).astype(BF16)
        kw_s[...] = (_row_rms(k2_ref[...], knw_ref[2:3, :]) * EXP2_SCALE).astype(BF16)
        ones_row = jnp.where(lax.broadcasted_iota(jnp.int32, (V_ROWS - HEAD_DIM, TK), 0) == 0,
                             1.0, 0.0).astype(BF16)
        for t in range(SEQ // TK):
            vst_s[t, :HEAD_DIM, :] = v1_ref[t * TK:(t + 1) * TK, :].T.astype(BF16)
            vst_s[t, HEAD_DIM:, :] = ones_row
            vwt_s[t, :HEAD_DIM, :] = v2_ref[t * TK:(t + 1) * TK, :].T.astype(BF16)
            vwt_s[t, HEAD_DIM:, :] = ones_row

        def compress(x_ref, pe_ref, w_ref):
            acc_lo = jnp.zeros((LANES, HEAD_DIM), F32)
            acc_hi = jnp.zeros((LANES, HEAD_DIM), F32)
            for j in range(CMP_STRIDE):
                y = x_ref[pl.ds(j, LANES, stride=CMP_STRIDE), :]
                acc_lo += _dot((y + pe_ref[j:j + 1, :]).astype(BF16),
                               w_ref[j * HEAD_DIM:(j + 1) * HEAD_DIM, :])
                jh = CMP_STRIDE + j
                acc_hi += _dot((y + pe_ref[jh:jh + 1, :]).astype(BF16),
                               w_ref[jh * HEAD_DIM:(jh + 1) * HEAD_DIM, :])
            return acc_lo + pltpu.roll(acc_hi, LANES - 1, 0)

        kc_s[...] = _row_rms(compress(k0_ref, pek_ref, wck_ref), knw_ref[0:1, :]).astype(BF16)
        vct_s[...] = compress(v0_ref, pev_ref, wcv_ref).T.astype(BF16)

    side_o_ref[:, :side_half] = _dot_nt(side_a_ref[...], side_bt_ref[:side_half, :])
    _cast_slabs(cast_in + cast_out, n_casts)

    t0 = qt * TQ
    q = q_ref[...]
    qts = []
    for r in range(GROUP_SIZE):
        qts.append(_row_rms(q[:, r * HEAD_DIM:(r + 1) * HEAD_DIM], qnw_ref[...]).T)
    qt_stack = jnp.concatenate(qts, axis=1).astype(BF16)

    m0 = jnp.full((1, cols_all), NEG_INF, F32)
    a0 = jnp.zeros((V_ROWS, cols_all), F32)

    def online(carry, tiles):
        m, acc = carry
        m_new = m
        for u, _ in tiles:
            m_new = jnp.maximum(m_new, jnp.max(u, axis=0, keepdims=True))
        acc = jnp.exp2(m - m_new) * acc
        for u, vt in tiles:
            acc = acc + _dot(vt, jnp.exp2(u - m_new).astype(BF16))
        return m_new, acc

    def online_independent(carry, tiles):
        m, acc = carry
        parts = []
        for u, vt in tiles:
            mt = jnp.max(u, axis=0, keepdims=True)
            parts.append((mt, _dot(vt, jnp.exp2(u - mt).astype(BF16))))
        m_new = m
        for mt, _ in parts:
            m_new = jnp.maximum(m_new, mt)
        acc = jnp.exp2(m - m_new) * acc
        for mt, pv in parts:
            acc = acc + jnp.exp2(mt - m_new) * pv
        return m_new, acc

    def normalised(acc):
        return acc[:HEAD_DIM] * (1.0 / acc[HEAD_DIM:HEAD_DIM + 1])

    def key_rows(kt):
        return pl.ds(pl.multiple_of(kt * TK, TK), TK)

    sc = _dot(kc_s[...], qt_stack) * SCALE + biasc_ref[0, 0]
    tpos = t0 + (lax.broadcasted_iota(jnp.int32, (LANES, cols_all), 1) & (TQ - 1))
    nrow = lax.broadcasted_iota(jnp.int32, (LANES, cols_all), 0)
    maskc = tpos >= CMP_STRIDE * nrow + (CMP_BLOCK - 1)
    scm = jnp.where(maskc, sc, NEG_INF)
    mc = jnp.max(scm, axis=0, keepdims=True)
    ec = jnp.where(maskc, jnp.exp(scm - mc), 0.0)
    lc = jnp.sum(ec, axis=0, keepdims=True)
    pc = ec * (1.0 / jnp.where(lc > 0.0, lc, 1.0))
    o_c = _dot(vct_s[...], pc.astype(BF16))

    psum = pc[:, 0:TQ] + pc[:, TQ:2 * TQ] + pc[:, 2 * TQ:3 * TQ] + pc[:, 3 * TQ:4 * TQ]
    p_hi = psum.astype(BF16)
    p_lo = (psum - p_hi.astype(F32)).astype(BF16)
    imp = (_dot(c2st_ref[...], p_hi) + _dot(c2st_ref[...], p_lo))[0:N_SEL]

    tiles = []
    for dist, kind in ((2, 2), (1, 1), (0, 0)):
        kt = jnp.maximum(qt - dist, 0)
        u = _dot(kw_s[key_rows(kt), :], qt_stack) + dm_ref[0, kind]
        if dist > 0:
            u = u + jnp.where(qt >= dist, 0.0, NEG_INF)
        tiles.append((u, vwt_s[kt]))
    _, acc_w = online_independent((m0, a0), tiles)
    o_w = normalised(acc_w)

    tq_pos = t0 + lax.broadcasted_iota(jnp.int32, (N_SEL, TQ), 1)
    sidx = lax.broadcasted_iota(jnp.int32, (N_SEL, TQ), 0)
    cur = tq_pos >> (SEL_BLOCK.bit_length() - 1)
    score = jnp.where(sidx <= cur, imp, NEG_INF)
    for forced_blk in (cur - 1, cur, jnp.zeros_like(cur)):
        score = jnp.where(sidx == forced_blk, FORCE_SCORE, score)
    rank = jnp.zeros((N_SEL, TQ), F32)
    for sp in range(N_SEL):
        row = score[sp:sp + 1, :]
        tie = jnp.where(sidx > sp, 1.0, 0.0)
        rank = rank + jnp.where(row > score, 1.0, jnp.where(row == score, tie, 0.0))
    selb = jnp.where(rank < TOP_N, 0.0, NEG_INF)
    selb = jnp.concatenate([selb, jnp.zeros((HEAD_DIM - N_SEL, TQ), F32)], axis=0).astype(BF16)
    qt_aug = jnp.concatenate([qt_stack, jnp.concatenate([selb] * GROUP_SIZE, axis=1)], axis=0)

    def sel_tile(kt):
        return _dot(kaug_s[key_rows(kt), :], qt_aug), vst_s[kt]

    n_far = jnp.maximum(qt - 1, 0)
    carry = lax.fori_loop(0, n_far >> 1,
                          lambda i, c: online_independent(c, [sel_tile(2 * i), sel_tile(2 * i + 1)]),
                          (m0, a0))
    carry = lax.fori_loop(0, n_far & 1, lambda i, c: online(c, [sel_tile(n_far - 1)]), carry)
    kt1 = jnp.maximum(qt - 1, 0)
    u1, vt1 = sel_tile(kt1)
    u0, vt0 = sel_tile(qt)
    _, acc_s = online_independent(
        carry, [(u1 + dm_ref[0, 1] + jnp.where(qt >= 1, 0.0, NEG_INF), vt1), (u0 + dm_ref[0, 0], vt0)])
    o_s = normalised(acc_s)

    side_o_ref[:, side_half:] = _dot_nt(side_a_ref[...], side_bt_ref[side_half:, :])

    gt_s[...] = jax.nn.sigmoid(gate_ref[...]).T
    head0 = pl.program_id(1) * GROUP_SIZE
    for r in range(GROUP_SIZE):
        cs = slice(r * TQ, (r + 1) * TQ)
        c = (head0 + r) * N_BRANCH
        o = (gt_s[pl.ds(c, 1), :] * o_c[:, cs] + gt_s[pl.ds(c + 1, 1), :] * o_s[:, cs]
             + gt_s[pl.ds(c + 2, 1), :] * o_w[:, cs])
        o_ref[:, r * HEAD_DIM:(r + 1) * HEAD_DIM] = o.T.astype(o_ref.dtype)


def _cmp_to_sel_t():
    ss = np.arange(LANES)[:, None] * SEL_BLOCK
    cs = np.arange(LANES)[None, :] * CMP_STRIDE
    ov = np.clip(np.minimum(cs + CMP_BLOCK, ss + SEL_BLOCK) - np.maximum(cs, ss), 0, None)
    m = ov.astype(np.float32) / np.float32(CMP_BLOCK)
    m[N_SEL:, :] = 0.0
    m[:, N_CMP:] = 0.0
    return jnp.asarray(m, BF16)


def _nsa_attention(proj, gate_logits, batch, q_norm_w, k_norm_w, pe_k, pe_v, w_cmp_k, w_cmp_v, biasc, dm,
                   side_a, side_bt, side_tm=1024, side_tn=512, casts=()):
    kvb = COL_KV // LANES

    def kv_spec(branch, which):
        base = kvb + (branch * 2 + which) * N_KV_GROUPS
        return pl.BlockSpec((SEQ, LANES), lambda b, g, qt: (b, base + g))

    full = lambda shape: pl.BlockSpec(shape, lambda b, g, qt: (0,) * len(shape))
    grid = (batch, N_KV_GROUPS, NQ)
    c_in, c_out, c_shapes = _cast_specs(casts, grid)
    side_m, side_k = side_a.shape
    side_n = side_bt.shape[0]
    n_j = side_n // side_tn
    assert (side_m // side_tm) * n_j == math.prod(grid)
    step = lambda b, g, qt: (b * N_KV_GROUPS + g) * NQ + qt
    side_in = [pl.BlockSpec((side_tm, side_k), lambda b, g, qt: (step(b, g, qt) // n_j, 0),
                            pipeline_mode=pl.Buffered(1)),
               pl.BlockSpec((side_tn, side_k), lambda b, g, qt: (step(b, g, qt) % n_j, 0))]
    side_out = pl.BlockSpec((side_tm, side_tn), lambda b, g, qt: (step(b, g, qt) // n_j, step(b, g, qt) % n_j))
    return pl.pallas_call(
        functools.partial(_nsa_kernel, n_casts=len(casts)),
        out_shape=[jax.ShapeDtypeStruct((batch * SEQ, NSA_WIDTH), BF16),
                   jax.ShapeDtypeStruct((side_m, side_n), F32)] + c_shapes,
        grid=grid,
        in_specs=[pl.BlockSpec((TQ, GROUP_SIZE * HEAD_DIM), lambda b, g, qt: (b * NQ + qt, g)),
                  kv_spec(0, 0), kv_spec(0, 1), kv_spec(1, 0), kv_spec(1, 1),
                  kv_spec(2, 0), kv_spec(2, 1),
                  pl.BlockSpec((TQ, LANES), lambda b, g, qt: (b * NQ + qt, 0)),
                  full((1, HEAD_DIM)), full((N_BRANCH, HEAD_DIM)),
                  full((CMP_BLOCK, HEAD_DIM)), full((CMP_BLOCK, HEAD_DIM)),
                  full((CMP_BLOCK * HEAD_DIM, HEAD_DIM)), full((CMP_BLOCK * HEAD_DIM, HEAD_DIM)),
                  pl.BlockSpec((1, 1, LANES, GROUP_SIZE * TQ), lambda b, g, qt: (g, qt, 0, 0)),
                  pl.BlockSpec((1, 3, TK, GROUP_SIZE * TQ), lambda b, g, qt: (g, 0, 0, 0),
                               pipeline_mode=pl.Buffered(1)),
                  full((LANES, LANES))] + side_in + c_in,
        out_specs=[pl.BlockSpec((TQ, GROUP_SIZE * HEAD_DIM), lambda b, g, qt: (b * NQ + qt, g)),
                   side_out] + c_out,
        scratch_shapes=[pltpu.VMEM((LANES, HEAD_DIM), BF16),
                        pltpu.VMEM((HEAD_DIM, LANES), BF16),
                        pltpu.VMEM((SEQ, 2 * HEAD_DIM), BF16),
                        pltpu.VMEM((SEQ, HEAD_DIM), BF16),
                        pltpu.VMEM((SEQ // TK, V_ROWS, TK), BF16),
                        pltpu.VMEM((SEQ // TK, V_ROWS, TK), BF16),
                        pltpu.VMEM((LANES, TQ), F32)],
        compiler_params=pltpu.CompilerParams(
            dimension_semantics=("arbitrary", "arbitrary", "arbitrary"),
            vmem_limit_bytes=NSA_VMEM_LIMIT),
    )(proj, proj, proj, proj, proj, proj, proj, gate_logits,
      q_norm_w.reshape(1, HEAD_DIM), k_norm_w, pe_k, pe_v,
      w_cmp_k.astype(BF16), w_cmp_v.astype(BF16), biasc, dm, _cmp_to_sel_t(),
      side_a, side_bt, *[c[0] for c in casts])


def _ret_kernel(lg_ref, q_ref, k_ref, v_ref, g_ref, cos_ref, sin_ref, gnw_ref, *rest, n_casts):
    o_ref = rest[n_casts]
    state_ref = rest[-1]
    _cast_slabs(rest[:n_casts] + rest[n_casts + 1:-1], n_casts)
    c_len = RET_CHUNK
    ii = lax.broadcasted_iota(jnp.int32, (c_len, c_len), 0)
    jj = lax.broadcasted_iota(jnp.int32, (c_len, c_len), 1)
    diff = (ii - jj).astype(F32)
    icol = lax.broadcasted_iota(jnp.int32, (c_len, 1), 0).astype(F32)
    half = RET_KEY_DIM // 2
    state_ref[...] = jnp.zeros_like(state_ref)
    tables = []
    for hh in range(RET_HEADS_PER_STEP):
        lg = lg_ref[pl.program_id(1) * RET_HEADS_PER_STEP + hh]
        tables.append((
            jnp.where(diff >= 0.0, jnp.exp(jnp.maximum(diff, 0.0) * lg), 0.0),
            jnp.exp((icol + 1.0) * lg),
            jnp.exp((c_len - 1.0 - icol) * lg),
            jnp.exp(jnp.full((1, RET_VAL_DIM), float(c_len), F32) * lg)))

    def body(c, _):
        rows = pl.ds(pl.multiple_of(c * c_len, c_len), c_len)
        cos = cos_ref[rows, :]
        sin = sin_ref[rows, :]

        def rot(x):
            x1, x2 = x[:, :half], x[:, half:]
            return jnp.concatenate([x1 * cos - x2 * sin, x2 * cos + x1 * sin], axis=-1)

        for hh, (decay_in, xi, zeta, chunk_decay) in enumerate(tables):
            kcols = slice(hh * RET_KEY_DIM, (hh + 1) * RET_KEY_DIM)
            vcols = slice(hh * RET_VAL_DIM, (hh + 1) * RET_VAL_DIM)
            qr = rot(q_ref[rows, kcols])
            kr = rot(k_ref[rows, kcols]) * (RET_KEY_DIM ** -0.5)
            vb = v_ref[rows, vcols].astype(BF16)
            scores = _dot_nt(qr.astype(BF16), kr.astype(BF16)) * decay_in
            inner = _dot(scores.astype(BF16), vb)
            st = state_ref[hh]
            cross = _dot((qr * xi).astype(BF16), st.astype(BF16))
            kv = _dot((kr * zeta).T.astype(BF16), vb)
            state_ref[hh] = chunk_decay * st + kv
            y = inner + cross
            y = y * lax.rsqrt(jnp.mean(y * y, axis=-1, keepdims=True) + EPS) * gnw_ref[:, vcols]
            g = g_ref[rows, vcols]
            o_ref[rows, vcols] = (g * jax.nn.sigmoid(g) * y).astype(o_ref.dtype)
        return 0

    lax.fori_loop(0, SEQ // c_len, body, 0, unroll=4)


def _retention(proj, batch, gn_w, casts=()):
    pos = np.arange(SEQ, dtype=np.float64)
    half = RET_KEY_DIM // 2
    inv = np.exp(-np.linspace(0.0, 1.0, half) * math.log(ROPE_BASE)).astype(np.float32)
    ang = pos.astype(np.float32)[:, None] * inv[None, :]
    cos = jnp.asarray(np.cos(ang.astype(np.float64)), F32)
    sin = jnp.asarray(np.sin(ang.astype(np.float64)), F32)
    log_gamma = jnp.asarray(np.log(1.0 - 2.0 ** (-5.0 - np.arange(N_RET_HEADS))), F32)
    w = RET_KEY_DIM * RET_HEADS_PER_STEP

    def col_spec(col0):
        return pl.BlockSpec((SEQ, w), lambda b, h: (b, col0 // w + h))

    grid = (batch, N_RET_HEADS // RET_HEADS_PER_STEP)
    c_in, c_out, c_shapes = _cast_specs(casts, grid)
    return pl.pallas_call(
        functools.partial(_ret_kernel, n_casts=len(casts)),
        out_shape=[jax.ShapeDtypeStruct((batch * SEQ, RET_WIDTH), BF16)] + c_shapes,
        grid=grid,
        in_specs=[pl.BlockSpec(memory_space=pltpu.SMEM),
                  col_spec(COL_QR), col_spec(COL_KR), col_spec(COL_VR), col_spec(COL_GR),
                  pl.BlockSpec((SEQ, half), lambda b, h: (0, 0)),
                  pl.BlockSpec((SEQ, half), lambda b, h: (0, 0)),
                  pl.BlockSpec((1, w), lambda b, h: (0, h))] + c_in,
        out_specs=[pl.BlockSpec((SEQ, w), lambda b, h: (b, h))] + c_out,
        scratch_shapes=[pltpu.VMEM((RET_HEADS_PER_STEP, RET_KEY_DIM, RET_VAL_DIM), F32)],
        compiler_params=pltpu.CompilerParams(dimension_semantics=("arbitrary", "arbitrary"),
                                             vmem_limit_bytes=VMEM_LIMIT),
    )(log_gamma, proj, proj, proj, proj, cos, sin, gn_w.reshape(1, RET_WIDTH), *[c[0] for c in casts])


def _layer(x, norm1_w, w_in, nsa_q_norm_w, nsa_k_norm_w, cmp_pe_k, cmp_pe_v, w_cmp_k, w_cmp_v,
           rel_bias, ret_gn_w, w_out, norm2_w, w_gate, w_up, w_down):
    batch = x.shape[0]
    rows = batch * SEQ
    xf = x.reshape(rows, D_MODEL)
    w_in_t = w_in.T
    gate0 = W_IN_NSA_COLS
    gate1 = gate0 + W_IN_GATE_COLS
    h, gate_logits = _rmsnorm_with_projection(xf, norm1_w, w_in_t, gate0, LANES)
    ret_cols = 4 * RET_WIDTH
    nsa_tn, nsa_tm = 1024, 512
    nsa_steps = (W_IN_NSA_COLS // nsa_tn + 1) * (rows // nsa_tm)
    ret_slabs = 1 << (nsa_steps.bit_length() - 1)
    proj_nsa, w_ret_t = _in_projection(h, w_in_t, 0, W_IN_NSA_COLS, nsa_tn, tm=nsa_tm,
                                       casts=((w_in_t, gate1, ret_cols, ret_slabs),))
    biasc, dm = _bias_tables(rel_bias)
    n_steps = batch * N_KV_GROUPS * NQ
    o_nsa, proj_ret, w_gate_b, w_up_b, w_out_b = _nsa_attention(
        proj_nsa, gate_logits, batch, nsa_q_norm_w, nsa_k_norm_w, cmp_pe_k, cmp_pe_v,
        w_cmp_k, w_cmp_v, biasc, dm, h, w_ret_t,
        casts=((w_gate, 0, D_MODEL, n_steps), (w_up, 0, D_MODEL, n_steps), (w_out, 0, D_MODEL, n_steps)))
    o_ret, = _retention(proj_ret, batch, ret_gn_w)
    x1, = _out_projection(o_nsa, o_ret, w_out_b, xf, tn=1024)
    hf = _rmsnorm(x1, norm2_w)
    up_tm, up_tn = 2048, 256
    up_steps = (rows // up_tm) * (D_FF // up_tn)
    mid, w_down_b = _ffn_up(hf, w_gate_b, w_up_b, tm=up_tm, tn=up_tn, casts=((w_down, 0, D_FF, up_steps),))
    out = _ffn_down(mid, w_down_b, x1, tm=512, tn=512, tk=D_FF)
    return out.reshape(batch, SEQ, D_MODEL)


def kernel(x, norm1_w, w_in, nsa_q_norm_w, nsa_k_norm_w, cmp_pe_k, cmp_pe_v, w_cmp_k, w_cmp_v,
           rel_bias, ret_gn_w, w_out, norm2_w, w_gate, w_up, w_down):
    for l in range(norm1_w.shape[0]):
        x = _layer(x, norm1_w[l], w_in[l], nsa_q_norm_w[l], nsa_k_norm_w[l], cmp_pe_k[l], cmp_pe_v[l],
                   w_cmp_k[l], w_cmp_v[l], rel_bias, ret_gn_w[l], w_out[l], norm2_w[l],
                   w_gate[l], w_up[l], w_down[l])
    return x
```

```python
import functools
import math

import numpy as np
import jax
import jax.numpy as jnp
from jax import lax
from jax.experimental import pallas as pl
from jax.experimental.pallas import tpu as pltpu

F32 = jnp.float32
BF16 = jnp.bfloat16

D_MODEL = 4096
SEQ = 2048
HEAD_DIM = 128
N_NSA_HEADS = 16
N_KV_GROUPS = 4
GROUP_SIZE = 4
N_BRANCH = 3
CMP_BLOCK = 32
CMP_STRIDE = 16
SEL_BLOCK = 64
TOP_N = 16
WINDOW = 512
RET_KEY_DIM = 256
RET_VAL_DIM = 256
N_RET_HEADS = 8
RET_CHUNK = 128
ROPE_BASE = 10000.0
NSA_WIDTH = 2048
RET_WIDTH = 2048
D_FF = 11008
NUM_BUCKETS = 32
MAX_DISTANCE = 128
EPS = 1e-6
NEG_INF = -1e30
FORCE_SCORE = 1e9
N_SEL = SEQ // SEL_BLOCK
N_CMP = (SEQ - CMP_BLOCK) // CMP_STRIDE + 1
SCALE = HEAD_DIM ** -0.5
LOG2E = math.log2(math.e)
EXP2_SCALE = SCALE * LOG2E

LANES = 128
VMEM_LIMIT = 56 * 1024 * 1024
NSA_VMEM_LIMIT = 60 * 1024 * 1024

TQ = 256
TK = 256
NQ = SEQ // TQ
RET_HEADS_PER_STEP = 2
FFN_UP_SUBBLOCKS = 4
V_ROWS = HEAD_DIM + 16
assert TQ == TK and WINDOW == 2 * TK and TK % SEL_BLOCK == 0 and N_SEL <= HEAD_DIM and TK >= MAX_DISTANCE

W_IN_NSA_COLS = NSA_WIDTH + N_BRANCH * 2 * N_KV_GROUPS * HEAD_DIM
W_IN_GATE_COLS = N_NSA_HEADS * N_BRANCH
COL_KV = NSA_WIDTH
COL_QR = 0
COL_KR = 2048
COL_VR = 4096
COL_GR = 6144


def _bucket_thresholds():
    rel = np.arange(0, 4 * MAX_DISTANCE)
    max_exact = NUM_BUCKETS // 2
    nf = np.maximum(rel, 1).astype(np.float64)
    large = max_exact + np.floor(np.log(nf / max_exact) / math.log(MAX_DISTANCE / max_exact)
                                 * (NUM_BUCKETS - max_exact)).astype(np.int64)
    large = np.minimum(large, NUM_BUCKETS - 1)
    b = np.where(rel < max_exact, rel, large)
    return [int(np.argmax(b >= k)) for k in range(NUM_BUCKETS)]


_THR = _bucket_thresholds()


def _dot(a, b):
    return jnp.dot(a, b, preferred_element_type=F32)


def _dot_nt(a, b):
    return lax.dot_general(a, b, (((1,), (1,)), ((), ())), preferred_element_type=F32)


def _rms_kernel(x_ref, w_ref, o_ref):
    x = x_ref[...]
    y = x * lax.rsqrt(jnp.mean(x * x, axis=-1, keepdims=True) + EPS)
    o_ref[...] = (y * w_ref[...]).astype(o_ref.dtype)


def _rmsnorm(x, w, tm=512):
    m, d = x.shape
    return pl.pallas_call(
        _rms_kernel,
        out_shape=jax.ShapeDtypeStruct((m, d), BF16),
        grid=(m // tm,),
        in_specs=[pl.BlockSpec((tm, d), lambda i: (i, 0)),
                  pl.BlockSpec((1, d), lambda i: (0, 0))],
        out_specs=pl.BlockSpec((tm, d), lambda i: (i, 0)),
        compiler_params=pltpu.CompilerParams(dimension_semantics=("arbitrary",),
                                             vmem_limit_bytes=VMEM_LIMIT),
    )(x, w.reshape(1, d))


def _rms_proj_kernel(x_ref, w_ref, pt_ref, o_ref, p_ref, pbf_ref):
    @pl.when(pl.program_id(0) == 0)
    def _():
        pbf_ref[...] = pt_ref[...].astype(BF16)

    x = x_ref[...]
    y = x * lax.rsqrt(jnp.mean(x * x, axis=-1, keepdims=True) + EPS)
    h = (y * w_ref[...]).astype(o_ref.dtype)
    o_ref[...] = h
    p_ref[...] = _dot_nt(h, pbf_ref[...])


def _rmsnorm_with_projection(x, w, w_t, row0, n_rows, tm=512):
    m, d = x.shape
    sub = 8
    assert row0 % sub == 0 and n_rows % LANES == 0
    return pl.pallas_call(
        _rms_proj_kernel,
        out_shape=(jax.ShapeDtypeStruct((m, d), BF16), jax.ShapeDtypeStruct((m, n_rows), F32)),
        grid=(m // tm,),
        in_specs=[pl.BlockSpec((tm, d), lambda i: (i, 0)),
                  pl.BlockSpec((1, d), lambda i: (0, 0)),
                  pl.BlockSpec((pl.Element(n_rows), pl.Element(d)), lambda i: ((row0 // sub) * sub, 0))],
        out_specs=(pl.BlockSpec((tm, d), lambda i: (i, 0)),
                   pl.BlockSpec((tm, n_rows), lambda i: (i, 0))),
        scratch_shapes=[pltpu.VMEM((n_rows, d), BF16)],
        compiler_params=pltpu.CompilerParams(dimension_semantics=("arbitrary",),
                                             vmem_limit_bytes=VMEM_LIMIT),
    )(x, w.reshape(1, d), w_t)


def _cast_spec(grid, row0, n_rows, n_cols, n_slabs):
    slab = n_rows // n_slabs
    sub = 8
    n_steps = math.prod(grid)
    assert slab * n_slabs == n_rows and slab % 16 == 0 and n_cols % LANES == 0
    assert row0 % sub == 0 and n_slabs <= n_steps

    def slab_index(ids):
        step = ids[0]
        for dim, idx in zip(grid[1:], ids[1:]):
            step = step * dim + idx
        return step if n_slabs == n_steps else jnp.minimum(step, n_slabs - 1)

    in_spec = pl.BlockSpec((pl.Element(slab), pl.Element(n_cols)),
                           lambda *ids: ((row0 // sub + slab_index(ids) * (slab // sub)) * sub, 0))
    out_spec = pl.BlockSpec((slab, n_cols), lambda *ids: (slab_index(ids), 0))
    return in_spec, out_spec, jax.ShapeDtypeStruct((n_rows, n_cols), BF16)


def _cast_specs(casts, grid):
    specs = [_cast_spec(grid, row0, n_rows, arr.shape[1], n_slabs) for arr, row0, n_rows, n_slabs in casts]
    return [s[0] for s in specs], [s[1] for s in specs], [s[2] for s in specs]


def _cast_slabs(refs, n_casts):
    for src, dst in zip(refs[:n_casts], refs[len(refs) - n_casts:]):
        dst[...] = src[...].astype(dst.dtype)


def _inproj_kernel(a_ref, wslab_ref, *rest, n_casts):
    o_ref = rest[n_casts]
    wbf_ref = rest[-1]
    j = pl.program_id(0)
    slab = wslab_ref.shape[0]
    rows = pl.ds(pl.multiple_of(pl.program_id(1) * slab, slab), slab)
    wbf_ref[j % 2, rows, :] = wslab_ref[...].astype(BF16)

    @pl.when(j > 0)
    def _():
        o_ref[...] = _dot_nt(a_ref[...], wbf_ref[(j + 1) % 2])

    _cast_slabs(rest[:n_casts] + rest[n_casts + 1:-1], n_casts)


def _in_projection(a, w_t, row0, n_cols, tn, tm=1024, casts=()):
    m, k = a.shape
    n_tiles, n_i = n_cols // tn, m // tm
    grid = (n_tiles + 1, n_i)
    c_in, c_out, c_shapes = _cast_specs(casts, grid)
    sub = 8
    slab = tn // n_i
    assert row0 % sub == 0 and slab % 16 == 0 and slab * n_i == tn

    def slab_row(j, i):
        return (row0 // sub + jnp.minimum(j, n_tiles - 1) * (tn // sub) + i * (slab // sub)) * sub

    return pl.pallas_call(
        functools.partial(_inproj_kernel, n_casts=len(casts)),
        out_shape=[jax.ShapeDtypeStruct((m, n_cols), F32)] + c_shapes,
        grid=grid,
        in_specs=[pl.BlockSpec((tm, k), lambda j, i: (jnp.where(j > 0, i, 0), 0)),
                  pl.BlockSpec((pl.Element(slab), pl.Element(k)), lambda j, i: (slab_row(j, i), 0))] + c_in,
        out_specs=[pl.BlockSpec((tm, tn), lambda j, i: (jnp.where(j > 0, i, 0), jnp.maximum(j - 1, 0)))] + c_out,
        scratch_shapes=[pltpu.VMEM((2, tn, k), BF16)],
        compiler_params=pltpu.CompilerParams(dimension_semantics=("arbitrary", "arbitrary"),
                                             vmem_limit_bytes=VMEM_LIMIT),
    )(a, w_t, *[c[0] for c in casts])


def _outproj_kernel(a1_ref, a2_ref, w1_ref, w2_ref, x_ref, *rest, n_casts):
    o_ref = rest[n_casts]
    acc = _dot(a1_ref[...], w1_ref[...]) + _dot(a2_ref[...], w2_ref[...])
    o_ref[...] = x_ref[...] + acc
    _cast_slabs(rest[:n_casts] + rest[n_casts + 1:], n_casts)


def _out_projection(o_nsa, o_ret, w_out, x, tm=1024, tn=512, casts=()):
    m = x.shape[0]
    kh = NSA_WIDTH
    grid = (m // tm, D_MODEL // tn)
    c_in, c_out, c_shapes = _cast_specs(casts, grid)
    return pl.pallas_call(
        functools.partial(_outproj_kernel, n_casts=len(casts)),
        out_shape=[jax.ShapeDtypeStruct((m, D_MODEL), F32)] + c_shapes,
        grid=grid,
        in_specs=[pl.BlockSpec((tm, kh), lambda i, j: (i, 0)),
                  pl.BlockSpec((tm, kh), lambda i, j: (i, 0)),
                  pl.BlockSpec((kh, tn), lambda i, j: (0, j)),
                  pl.BlockSpec((kh, tn), lambda i, j: (1, j)),
                  pl.BlockSpec((tm, tn), lambda i, j: (i, j))] + c_in,
        out_specs=[pl.BlockSpec((tm, tn), lambda i, j: (i, j))] + c_out,
        compiler_params=pltpu.CompilerParams(dimension_semantics=("arbitrary", "arbitrary"),
                                             vmem_limit_bytes=VMEM_LIMIT),
    )(o_nsa, o_ret, w_out, w_out, x, *[c[0] for c in casts])


def _ffn_up_kernel(h_ref, wg_ref, wu_ref, *rest, n_casts):
    o_ref = rest[n_casts]
    sub_rows = h_ref.shape[0] // FFN_UP_SUBBLOCKS
    for s in range(FFN_UP_SUBBLOCKS):
        rows = slice(s * sub_rows, (s + 1) * sub_rows)
        h = h_ref[rows, :]
        g = _dot(h, wg_ref[...])
        u = _dot(h, wu_ref[...])
        o_ref[rows, :] = (g * jax.nn.sigmoid(g) * u).astype(o_ref.dtype)
    _cast_slabs(rest[:n_casts] + rest[n_casts + 1:], n_casts)


def _ffn_up(h, w_gate, w_up, tm=1024, tn=256, casts=()):
    m, k = h.shape
    n = w_gate.shape[1]
    grid = (m // tm, n // tn)
    c_in, c_out, c_shapes = _cast_specs(casts, grid)
    return pl.pallas_call(
        functools.partial(_ffn_up_kernel, n_casts=len(casts)),
        out_shape=[jax.ShapeDtypeStruct((m, n), BF16)] + c_shapes,
        grid=grid,
        in_specs=[pl.BlockSpec((tm, k), lambda i, j: (i, 0)),
                  pl.BlockSpec((k, tn), lambda i, j: (0, j)),
                  pl.BlockSpec((k, tn), lambda i, j: (0, j))] + c_in,
        out_specs=[pl.BlockSpec((tm, tn), lambda i, j: (i, j))] + c_out,
        compiler_params=pltpu.CompilerParams(dimension_semantics=("arbitrary", "arbitrary"),
                                             vmem_limit_bytes=VMEM_LIMIT),
    )(h, w_gate, w_up, *[c[0] for c in casts])


def _ffn_down_kernel(a_ref, b_ref, r_ref, o_ref):
    o_ref[...] = r_ref[...] + _dot(a_ref[...], b_ref[...])


def _ffn_down(a, w_down, x, tm=1024, tn=512, tk=5504):
    m, k = a.shape
    n = w_down.shape[1]
    res = x
    for kk in range(k // tk):
        res = pl.pallas_call(
            _ffn_down_kernel,
            out_shape=jax.ShapeDtypeStruct((m, n), F32),
            grid=(m // tm, n // tn),
            in_specs=[pl.BlockSpec((tm, tk), lambda i, j, kk=kk: (i, kk)),
                      pl.BlockSpec((tk, tn), lambda i, j, kk=kk: (kk, j)),
                      pl.BlockSpec((tm, tn), lambda i, j: (i, j))],
            out_specs=pl.BlockSpec((tm, tn), lambda i, j: (i, j)),
            compiler_params=pltpu.CompilerParams(dimension_semantics=("arbitrary", "arbitrary"),
                                                 vmem_limit_bytes=VMEM_LIMIT),
        )(a, w_down, res)
    return res


def _bias_value(rel, relb_ref, h):
    val = jnp.full(rel.shape, relb_ref[0, h], F32)
    for k in range(1, NUM_BUCKETS):
        val = jnp.where(rel >= _THR[k], relb_ref[k, h], val)
    return val


def _bias_tables_kernel(relb_ref, biasc_ref, dm_ref):
    h = pl.program_id(0)
    sub = 8
    j = lax.broadcasted_iota(jnp.int32, (sub, 2 * SEQ), 1)
    row_c = _bias_value(j - SEQ - (CMP_BLOCK - 1), relb_ref, h)[0:1]
    toe_c = pltpu.roll(jnp.broadcast_to(row_c, (LANES, 2 * SEQ)), 0, 1,
                       stride=CMP_STRIDE, stride_axis=0)
    for qt in range(NQ):
        biasc_ref[0, qt] = toe_c[:, SEQ + qt * TQ:SEQ + (qt + 1) * TQ]
    far = relb_ref[NUM_BUCKETS - 1, h]
    j2 = lax.broadcasted_iota(jnp.int32, (sub, 2 * TK), 1)
    b = lax.broadcasted_iota(jnp.int32, (TK, TQ), 0)
    a = lax.broadcasted_iota(jnp.int32, (TK, TQ), 1)
    for kind in range(2):
        row = ((_bias_value(j2 + (kind - 1) * TK, relb_ref, h) - far) * LOG2E)[0:1]
        toe = pltpu.roll(jnp.broadcast_to(row, (TK, 2 * TK)), 0, 1, stride=1, stride_axis=0)
        tile = toe[:, TK:]
        dm_ref[0, kind] = tile + jnp.where(a >= b, 0.0, NEG_INF) if kind == 0 else tile
    dm_ref[0, 2] = jnp.where(b > a, 0.0, NEG_INF)


def _bias_tables(rel_bias):
    return pl.pallas_call(
        _bias_tables_kernel,
        out_shape=(jax.ShapeDtypeStruct((N_KV_GROUPS, NQ, LANES, GROUP_SIZE * TQ), F32),
                   jax.ShapeDtypeStruct((N_KV_GROUPS, 3, TK, GROUP_SIZE * TQ), F32)),
        grid=(N_NSA_HEADS,),
        in_specs=[pl.BlockSpec(memory_space=pltpu.SMEM)],
        out_specs=(pl.BlockSpec((1, NQ, LANES, TQ), lambda h: (h // GROUP_SIZE, 0, 0, h % GROUP_SIZE)),
                   pl.BlockSpec((1, 3, TK, TQ), lambda h: (h // GROUP_SIZE, 0, 0, h % GROUP_SIZE))),
        compiler_params=pltpu.CompilerParams(dimension_semantics=("arbitrary",),
                                             vmem_limit_bytes=VMEM_LIMIT),
    )(rel_bias)


def _row_rms(x, w):
    return x * lax.rsqrt(jnp.mean(x * x, axis=-1, keepdims=True) + EPS) * w


N_NSA_INPUTS = 17
N_NSA_SCRATCH = 7


def _nsa_kernel(*refs, n_casts):
    (q_ref, k0_ref, v0_ref, k1_ref, v1_ref, k2_ref, v2_ref, gate_ref,
     qnw_ref, knw_ref, pek_ref, pev_ref, wck_ref, wcv_ref, biasc_ref, dm_ref, c2st_ref) = refs[:N_NSA_INPUTS]
    side_a_ref, side_bt_ref = refs[N_NSA_INPUTS:N_NSA_INPUTS + 2]
    n_in = N_NSA_INPUTS + 2 + n_casts
    cast_in = refs[N_NSA_INPUTS + 2:n_in]
    o_ref, side_o_ref = refs[n_in:n_in + 2]
    cast_out = refs[n_in + 2:n_in + 2 + n_casts]
    kc_s, vct_s, kaug_s, kw_s, vst_s, vwt_s, gt_s = refs[len(refs) - N_NSA_SCRATCH:]
    qt = pl.program_id(2)
    cols_all = GROUP_SIZE * TQ
    side_half = side_bt_ref.shape[0] // 2

    @pl.when(qt == 0)
    def _prologue():
        kaug_s[:, :HEAD_DIM] = (_row_rms(k1_ref[...], knw_ref[1:2, :]) * EXP2_SCALE).astype(BF16)
        jblk = lax.broadcasted_iota(jnp.int32, (SEQ, LANES), 0) // SEL_BLOCK
        lane = lax.broadcasted_iota(jnp.int32, (SEQ, LANES), 1)
        kaug_s[:, HEAD_DIM:] = jnp.where(jblk == lane, 1.0, 0.0).astype(BF16)
        kw_s[...] = (_row_rms(k2_ref[...], knw_ref[2:3, :]) * EXP2_SCALE).astype(BF16)
        ones_row = jnp.where(lax.broadcasted_iota(jnp.int32, (V_ROWS - HEAD_DIM, TK), 0) == 0,
                             1.0, 0.0).astype(BF16)
        for t in range(SEQ // TK):
            vst_s[t, :HEAD_DIM, :] = v1_ref[t * TK:(t + 1) * TK, :].T.astype(BF16)
            vst_s[t, HEAD_DIM:, :] = ones_row
            vwt_s[t, :HEAD_DIM, :] = v2_ref[t * TK:(t + 1) * TK, :].T.astype(BF16)
            vwt_s[t, HEAD_DIM:, :] = ones_row

        def compress(x_ref, pe_ref, w_ref):
            acc_lo = jnp.zeros((LANES, HEAD_DIM), F32)
            acc_hi = jnp.zeros((LANES, HEAD_DIM), F32)
            for j in range(CMP_STRIDE):
                y = x_ref[pl.ds(j, LANES, stride=CMP_STRIDE), :]
                acc_lo += _dot((y + pe_ref[j:j + 1, :]).astype(BF16),
                               w_ref[j * HEAD_DIM:(j + 1) * HEAD_DIM, :])
                jh = CMP_STRIDE + j
                acc_hi += _dot((y + pe_ref[jh:jh + 1, :]).astype(BF16),
                               w_ref[jh * HEAD_DIM:(jh + 1) * HEAD_DIM, :])
            return acc_lo + pltpu.roll(acc_hi, LANES - 1, 0)

        kc_s[...] = _row_rms(compress(k0_ref, pek_ref, wck_ref), knw_ref[0:1, :]).astype(BF16)
        vct_s[...] = compress(v0_ref, pev_ref, wcv_ref).T.astype(BF16)

    side_o_ref[:, :side_half] = _dot_nt(side_a_ref[...], side_bt_ref[:side_half, :])
    _cast_slabs(cast_in + cast_out, n_casts)

    t0 = qt * TQ
    q = q_ref[...]
    qts = []
    for r in range(GROUP_SIZE):
        qts.append(_row_rms(q[:, r * HEAD_DIM:(r + 1) * HEAD_DIM], qnw_ref[...]).T)
    qt_stack = jnp.concatenate(qts, axis=1).astype(BF16)

    m0 = jnp.full((1, cols_all), NEG_INF, F32)
    a0 = jnp.zeros((V_ROWS, cols_all), F32)

    def online(carry, tiles):
        m, acc = carry
        m_new = m
        for u, _ in tiles:
            m_new = jnp.maximum(m_new, jnp.max(u, axis=0, keepdims=True))
        acc = jnp.exp2(m - m_new) * acc
        for u, vt in tiles:
            acc = acc + _dot(vt, jnp.exp2(u - m_new).astype(BF16))
        return m_new, acc

    def online_independent(carry, tiles):
        m, acc = carry
        parts = []
        for u, vt in tiles:
            mt = jnp.max(u, axis=0, keepdims=True)
            parts.append((mt, _dot(vt, jnp.exp2(u - mt).astype(BF16))))
        m_new = m
        for mt, _ in parts:
            m_new = jnp.maximum(m_new, mt)
        acc = jnp.exp2(m - m_new) * acc
        for mt, pv in parts:
            acc = acc + jnp.exp2(mt - m_new) * pv
        return m_new, acc

    def normalised(acc):
        return acc[:HEAD_DIM] * (1.0 / acc[HEAD_DIM:HEAD_DIM + 1])

    def key_rows(kt):
        return pl.ds(pl.multiple_of(kt * TK, TK), TK)

    sc = _dot(kc_s[...], qt_stack) * SCALE + biasc_ref[0, 0]
    tpos = t0 + (lax.broadcasted_iota(jnp.int32, (LANES, cols_all), 1) & (TQ - 1))
    nrow = lax.broadcasted_iota(jnp.int32, (LANES, cols_all), 0)
    maskc = tpos >= CMP_STRIDE * nrow + (CMP_BLOCK - 1)
    scm = jnp.where(maskc, sc, NEG_INF)
    mc = jnp.max(scm, axis=0, keepdims=True)
    ec = jnp.where(maskc, jnp.exp(scm - mc), 0.0)
    lc = jnp.sum(ec, axis=0, keepdims=True)
    pc = ec * (1.0 / jnp.where(lc > 0.0, lc, 1.0))
    o_c = _dot(vct_s[...], pc.astype(BF16))

    psum = pc[:, 0:TQ] + pc[:, TQ:2 * TQ] + pc[:, 2 * TQ:3 * TQ] + pc[:, 3 * TQ:4 * TQ]
    p_hi = psum.astype(BF16)
    p_lo = (psum - p_hi.astype(F32)).astype(BF16)
    imp = (_dot(c2st_ref[...], p_hi) + _dot(c2st_ref[...], p_lo))[0:N_SEL]

    tiles = []
    for dist, kind in ((2, 2), (1, 1), (0, 0)):
        kt = jnp.maximum(qt - dist, 0)
        u = _dot(kw_s[key_rows(kt), :], qt_stack) + dm_ref[0, kind]
        if dist > 0:
            u = u + jnp.where(qt >= dist, 0.0, NEG_INF)
        tiles.append((u, vwt_s[kt]))
    _, acc_w = online_independent((m0, a0), tiles)
    o_w = normalised(acc_w)

    tq_pos = t0 + lax.broadcasted_iota(jnp.int32, (N_SEL, TQ), 1)
    sidx = lax.broadcasted_iota(jnp.int32, (N_SEL, TQ), 0)
    cur = tq_pos >> (SEL_BLOCK.bit_length() - 1)
    score = jnp.where(sidx <= cur, imp, NEG_INF)
    for forced_blk in (cur - 1, cur, jnp.zeros_like(cur)):
        score = jnp.where(sidx == forced_blk, FORCE_SCORE, score)
    rank = jnp.zeros((N_SEL, TQ), F32)
    for sp in range(N_SEL):
        row = score[sp:sp + 1, :]
        tie = jnp.where(sidx > sp, 1.0, 0.0)
        rank = rank + jnp.where(row > score, 1.0, jnp.where(row == score, tie, 0.0))
    selb = jnp.where(rank < TOP_N, 0.0, NEG_INF)
    selb = jnp.concatenate([selb, jnp.zeros((HEAD_DIM - N_SEL, TQ), F32)], axis=0).astype(BF16)
    qt_aug = jnp.concatenate([qt_stack, jnp.concatenate([selb] * GROUP_SIZE, axis=1)], axis=0)

    def sel_tile(kt):
        return _dot(kaug_s[key_rows(kt), :], qt_aug), vst_s[kt]

    n_far = jnp.maximum(qt - 1, 0)
    carry = lax.fori_loop(0, n_far >> 1,
                          lambda i, c: online_independent(c, [sel_tile(2 * i), sel_tile(2 * i + 1)]),
                          (m0, a0))
    carry = lax.fori_loop(0, n_far & 1, lambda i, c: online(c, [sel_tile(n_far - 1)]), carry)
    kt1 = jnp.maximum(qt - 1, 0)
    u1, vt1 = sel_tile(kt1)
    u0, vt0 = sel_tile(qt)
    _, acc_s = online_independent(
        carry, [(u1 + dm_ref[0, 1] + jnp.where(qt >= 1, 0.0, NEG_INF), vt1), (u0 + dm_ref[0, 0], vt0)])
    o_s = normalised(acc_s)

    side_o_ref[:, side_half:] = _dot_nt(side_a_ref[...], side_bt_ref[side_half:, :])

    gt_s[...] = jax.nn.sigmoid(gate_ref[...]).T
    head0 = pl.program_id(0) * GROUP_SIZE
    for r in range(GROUP_SIZE):
        cs = slice(r * TQ, (r + 1) * TQ)
        c = (head0 + r) * N_BRANCH
        o = (gt_s[pl.ds(c, 1), :] * o_c[:, cs] + gt_s[pl.ds(c + 1, 1), :] * o_s[:, cs]
             + gt_s[pl.ds(c + 2, 1), :] * o_w[:, cs])
        o_ref[:, r * HEAD_DIM:(r + 1) * HEAD_DIM] = o.T.astype(o_ref.dtype)


def _cmp_to_sel_t():
    ss = np.arange(LANES)[:, None] * SEL_BLOCK
    cs = np.arange(LANES)[None, :] * CMP_STRIDE
    ov = np.clip(np.minimum(cs + CMP_BLOCK, ss + SEL_BLOCK) - np.maximum(cs, ss), 0, None)
    m = ov.astype(np.float32) / np.float32(CMP_BLOCK)
    m[N_SEL:, :] = 0.0
    m[:, N_CMP:] = 0.0
    return jnp.asarray(m, BF16)


def _nsa_attention(proj, gate_logits, batch, q_norm_w, k_norm_w, pe_k, pe_v, w_cmp_k, w_cmp_v, biasc, dm,
                   side_a, side_bt, side_tm=1024, side_tn=512, casts=()):
    kvb = COL_KV // LANES

    def kv_spec(branch, which):
        base = kvb + (branch * 2 + which) * N_KV_GROUPS
        return pl.BlockSpec((SEQ, LANES), lambda g, b, qt: (b, base + g))

    full = lambda shape: pl.BlockSpec(shape, lambda g, b, qt: (0,) * len(shape))
    grid = (N_KV_GROUPS, batch, NQ)
    c_in, c_out, c_shapes = _cast_specs(casts, grid)
    side_m, side_k = side_a.shape
    side_n = side_bt.shape[0]
    n_j = side_n // side_tn
    assert (side_m // side_tm) * n_j == math.prod(grid)
    step = lambda b, g, qt: (g * batch + b) * NQ + qt
    side_in = [pl.BlockSpec((side_tm, side_k), lambda g, b, qt: (step(b, g, qt) // n_j, 0),
                            pipeline_mode=pl.Buffered(1)),
               pl.BlockSpec((side_tn, side_k), lambda g, b, qt: (step(b, g, qt) % n_j, 0))]
    side_out = pl.BlockSpec((side_tm, side_tn), lambda g, b, qt: (step(b, g, qt) // n_j, step(b, g, qt) % n_j))
    return pl.pallas_call(
        functools.partial(_nsa_kernel, n_casts=len(casts)),
        out_shape=[jax.ShapeDtypeStruct((batch * SEQ, NSA_WIDTH), BF16),
                   jax.ShapeDtypeStruct((side_m, side_n), F32)] + c_shapes,
        grid=grid,
        in_specs=[pl.BlockSpec((TQ, GROUP_SIZE * HEAD_DIM), lambda g, b, qt: (b * NQ + qt, g)),
                  kv_spec(0, 0), kv_spec(0, 1), kv_spec(1, 0), kv_spec(1, 1),
                  kv_spec(2, 0), kv_spec(2, 1),
                  pl.BlockSpec((TQ, LANES), lambda g, b, qt: (b * NQ + qt, 0)),
                  full((1, HEAD_DIM)), full((N_BRANCH, HEAD_DIM)),
                  full((CMP_BLOCK, HEAD_DIM)), full((CMP_BLOCK, HEAD_DIM)),
                  full((CMP_BLOCK * HEAD_DIM, HEAD_DIM)), full((CMP_BLOCK * HEAD_DIM, HEAD_DIM)),
                  pl.BlockSpec((1, 1, LANES, GROUP_SIZE * TQ), lambda g, b, qt: (g, qt, 0, 0)),
                  pl.BlockSpec((1, 3, TK, GROUP_SIZE * TQ), lambda g, b, qt: (g, 0, 0, 0),
                               pipeline_mode=pl.Buffered(1)),
                  full((LANES, LANES))] + side_in + c_in,
        out_specs=[pl.BlockSpec((TQ, GROUP_SIZE * HEAD_DIM), lambda g, b, qt: (b * NQ + qt, g)),
                   side_out] + c_out,
        scratch_shapes=[pltpu.VMEM((LANES, HEAD_DIM), BF16),
                        pltpu.VMEM((HEAD_DIM, LANES), BF16),
                        pltpu.VMEM((SEQ, 2 * HEAD_DIM), BF16),
                        pltpu.VMEM((SEQ, HEAD_DIM), BF16),
                        pltpu.VMEM((SEQ // TK, V_ROWS, TK), BF16),
                        pltpu.VMEM((SEQ // TK, V_ROWS, TK), BF16),
                        pltpu.VMEM((LANES, TQ), F32)],
        compiler_params=pltpu.CompilerParams(
            dimension_semantics=("arbitrary", "arbitrary", "arbitrary"),
            vmem_limit_bytes=NSA_VMEM_LIMIT),
    )(proj, proj, proj, proj, proj, proj, proj, gate_logits,
      q_norm_w.reshape(1, HEAD_DIM), k_norm_w, pe_k, pe_v,
      w_cmp_k.astype(BF16), w_cmp_v.astype(BF16), biasc, dm, _cmp_to_sel_t(),
      side_a, side_bt, *[c[0] for c in casts])


def _ret_kernel(lg_ref, q_ref, k_ref, v_ref, g_ref, cos_ref, sin_ref, gnw_ref, *rest, n_casts):
    o_ref = rest[n_casts]
    state_ref = rest[-1]
    _cast_slabs(rest[:n_casts] + rest[n_casts + 1:-1], n_casts)
    c_len = RET_CHUNK
    ii = lax.broadcasted_iota(jnp.int32, (c_len, c_len), 0)
    jj = lax.broadcasted_iota(jnp.int32, (c_len, c_len), 1)
    diff = (ii - jj).astype(F32)
    icol = lax.broadcasted_iota(jnp.int32, (c_len, 1), 0).astype(F32)
    half = RET_KEY_DIM // 2
    state_ref[...] = jnp.zeros_like(state_ref)
    tables = []
    for hh in range(RET_HEADS_PER_STEP):
        lg = lg_ref[pl.program_id(1) * RET_HEADS_PER_STEP + hh]
        tables.append((
            jnp.where(diff >= 0.0, jnp.exp(jnp.maximum(diff, 0.0) * lg), 0.0),
            jnp.exp((icol + 1.0) * lg),
            jnp.exp((c_len - 1.0 - icol) * lg),
            jnp.exp(jnp.full((1, RET_VAL_DIM), float(c_len), F32) * lg)))

    def body(c, _):
        rows = pl.ds(pl.multiple_of(c * c_len, c_len), c_len)
        cos = cos_ref[rows, :]
        sin = sin_ref[rows, :]

        def rot(x):
            x1, x2 = x[:, :half], x[:, half:]
            return jnp.concatenate([x1 * cos - x2 * sin, x2 * cos + x1 * sin], axis=-1)

        for hh, (decay_in, xi, zeta, chunk_decay) in enumerate(tables):
            kcols = slice(hh * RET_KEY_DIM, (hh + 1) * RET_KEY_DIM)
            vcols = slice(hh * RET_VAL_DIM, (hh + 1) * RET_VAL_DIM)
            qr = rot(q_ref[rows, kcols])
            kr = rot(k_ref[rows, kcols]) * (RET_KEY_DIM ** -0.5)
            vb = v_ref[rows, vcols].astype(BF16)
            scores = _dot_nt(qr.astype(BF16), kr.astype(BF16)) * decay_in
            inner = _dot(scores.astype(BF16), vb)
            st = state_ref[hh]
            cross = _dot((qr * xi).astype(BF16), st.astype(BF16))
            kv = _dot((kr * zeta).T.astype(BF16), vb)
            state_ref[hh] = chunk_decay * st + kv
            y = inner + cross
            y = y * lax.rsqrt(jnp.mean(y * y, axis=-1, keepdims=True) + EPS) * gnw_ref[:, vcols]
            g = g_ref[rows, vcols]
            o_ref[rows, vcols] = (g * jax.nn.sigmoid(g) * y).astype(o_ref.dtype)
        return 0

    lax.fori_loop(0, SEQ // c_len, body, 0, unroll=4)


def _retention(proj, batch, gn_w, casts=()):
    pos = np.arange(SEQ, dtype=np.float64)
    half = RET_KEY_DIM // 2
    inv = np.exp(-np.linspace(0.0, 1.0, half) * math.log(ROPE_BASE)).astype(np.float32)
    ang = pos.astype(np.float32)[:, None] * inv[None, :]
    cos = jnp.asarray(np.cos(ang.astype(np.float64)), F32)
    sin = jnp.asarray(np.sin(ang.astype(np.float64)), F32)
    log_gamma = jnp.asarray(np.log(1.0 - 2.0 ** (-5.0 - np.arange(N_RET_HEADS))), F32)
    w = RET_KEY_DIM * RET_HEADS_PER_STEP

    def col_spec(col0):
        return pl.BlockSpec((SEQ, w), lambda b, h: (b, col0 // w + h))

    grid = (batch, N_RET_HEADS // RET_HEADS_PER_STEP)
    c_in, c_out, c_shapes = _cast_specs(casts, grid)
    return pl.pallas_call(
        functools.partial(_ret_kernel, n_casts=len(casts)),
        out_shape=[jax.ShapeDtypeStruct((batch * SEQ, RET_WIDTH), BF16)] + c_shapes,
        grid=grid,
        in_specs=[pl.BlockSpec(memory_space=pltpu.SMEM),
                  col_spec(COL_QR), col_spec(COL_KR), col_spec(COL_VR), col_spec(COL_GR),
                  pl.BlockSpec((SEQ, half), lambda b, h: (0, 0)),
                  pl.BlockSpec((SEQ, half), lambda b, h: (0, 0)),
                  pl.BlockSpec((1, w), lambda b, h: (0, h))] + c_in,
        out_specs=[pl.BlockSpec((SEQ, w), lambda b, h: (b, h))] + c_out,
        scratch_shapes=[pltpu.VMEM((RET_HEADS_PER_STEP, RET_KEY_DIM, RET_VAL_DIM), F32)],
        compiler_params=pltpu.CompilerParams(dimension_semantics=("arbitrary", "arbitrary"),
                                             vmem_limit_bytes=VMEM_LIMIT),
    )(log_gamma, proj, proj, proj, proj, cos, sin, gn_w.reshape(1, RET_WIDTH), *[c[0] for c in casts])


def _layer(x, norm1_w, w_in, nsa_q_norm_w, nsa_k_norm_w, cmp_pe_k, cmp_pe_v, w_cmp_k, w_cmp_v,
           rel_bias, ret_gn_w, w_out, norm2_w, w_gate, w_up, w_down):
    batch = x.shape[0]
    rows = batch * SEQ
    xf = x.reshape(rows, D_MODEL)
    w_in_t = w_in.T
    gate0 = W_IN_NSA_COLS
    gate1 = gate0 + W_IN_GATE_COLS
    h, gate_logits = _rmsnorm_with_projection(xf, norm1_w, w_in_t, gate0, LANES)
    ret_cols = 4 * RET_WIDTH
    nsa_tn, nsa_tm = 1024, 512
    nsa_steps = (W_IN_NSA_COLS // nsa_tn + 1) * (rows // nsa_tm)
    ret_slabs = 1 << (nsa_steps.bit_length() - 1)
    proj_nsa, w_ret_t = _in_projection(h, w_in_t, 0, W_IN_NSA_COLS, nsa_tn, tm=nsa_tm,
                                       casts=((w_in_t, gate1, ret_cols, ret_slabs),))
    biasc, dm = _bias_tables(rel_bias)
    n_steps = batch * N_KV_GROUPS * NQ
    o_nsa, proj_ret, w_gate_b, w_up_b, w_out_b = _nsa_attention(
        proj_nsa, gate_logits, batch, nsa_q_norm_w, nsa_k_norm_w, cmp_pe_k, cmp_pe_v,
        w_cmp_k, w_cmp_v, biasc, dm, h, w_ret_t,
        casts=((w_gate, 0, D_MODEL, n_steps), (w_up, 0, D_MODEL, n_steps), (w_out, 0, D_MODEL, n_steps)))
    o_ret, = _retention(proj_ret, batch, ret_gn_w)
    x1, = _out_projection(o_nsa, o_ret, w_out_b, xf, tn=1024)
    hf = _rmsnorm(x1, norm2_w)
    up_tm, up_tn = 2048, 256
    up_steps = (rows // up_tm) * (D_FF // up_tn)
    mid, w_down_b = _ffn_up(hf, w_gate_b, w_up_b, tm=up_tm, tn=up_tn, casts=((w_down, 0, D_FF, up_steps),))
    out = _ffn_down(mid, w_down_b, x1, tm=512, tn=512, tk=D_FF)
    return out.reshape(batch, SEQ, D_MODEL)


def kernel(x, norm1_w, w_in, nsa_q_norm_w, nsa_k_norm_w, cmp_pe_k, cmp_pe_v, w_cmp_k, w_cmp_v,
           rel_bias, ret_gn_w, w_out, norm2_w, w_gate, w_up, w_down):
    for l in range(norm1_w.shape[0]):
        x = _layer(x, norm1_w[l], w_in[l], nsa_q_norm_w[l], nsa_k_norm_w[l], cmp_pe_k[l], cmp_pe_v[l],
                   w_cmp_k[l], w_cmp_v[l], rel_bias, ret_gn_w[l], w_out[l], norm2_w[l],
                   w_gate[l], w_up[l], w_down[l])
    return x
```
